```python
import jax
import jax.numpy as jnp
from jax import lax
import numpy as np

D_MODEL = 1024
BATCH = 2
SEQ = 8192
DEPTH = 2

GRID_W = 64
CTX_LEN = 256
CHUNK = 64
EPS = 1e-6
M_INIT = -1e30

GLA_HEADS = 4
GLA_DK = D_MODEL // 16
GLA_DV = D_MODEL // 8
GLA_RANK = 16
GLA_TAU = 16.0
ML_HEADS = 4
ML_DK = D_MODEL // 8
ML_DV = D_MODEL // 8
CONV_K = 3
ML_F_BIAS_LO = 3.0
ML_F_BIAS_HI = 6.0
RET_HEADS = 4
RET_DK = D_MODEL // 16
RET_DV = D_MODEL // 8
ROPE_BASE = 10000.0

BRANCH_W = GLA_HEADS * GLA_DV
N_BRANCH = 3
D_FF = 4 * D_MODEL
ADA_SCALE = 0.5

GLA_QK = GLA_HEADS * GLA_DK
ML_QK = ML_HEADS * ML_DK
RET_QK = RET_HEADS * RET_DK

IN_SIZES = (GLA_QK, GLA_QK, BRANCH_W, BRANCH_W, 2 * GLA_RANK,
            ML_QK, ML_QK, BRANCH_W, BRANCH_W, 2 * ML_HEADS, 2 * ML_HEADS,
            RET_QK, RET_QK, BRANCH_W, BRANCH_W,
            N_BRANCH * D_MODEL)
IN_WIDTH = sum(IN_SIZES)
IN_SPLITS = tuple(int(s) for s in np.cumsum(IN_SIZES)[:-1])

kernel_name = "hybrid_gla_mlstm_retention_prefix_dit"

F32 = jnp.float32


def rms_norm(x, g):
    xf = x.astype(F32)
    y = xf * lax.rsqrt(jnp.mean(xf * xf, axis=-1, keepdims=True) + EPS)
    return (y * g.astype(F32)).astype(x.dtype)


def modulate(h, shift, scale):
    return h * (1.0 + scale) + shift


def head_norm(o, g, center, dtype):
    B, H, T, dv = o.shape
    of = jnp.swapaxes(o, 1, 2).astype(F32)
    if center:
        of = of - jnp.mean(of, axis=-1, keepdims=True)
    y = of * lax.rsqrt(jnp.mean(of * of, axis=-1, keepdims=True) + EPS)
    return (y.reshape(B, T, H * dv) * g.astype(F32)).astype(dtype)


def _heads(t, n_heads):
    B, T, W = t.shape
    return jnp.transpose(t.reshape(B, T, n_heads, W // n_heads), (0, 2, 1, 3))


def _chunk(t):
    B, H, T = t.shape[:3]
    return t.reshape(B, H, T // CHUNK, CHUNK, *t.shape[3:])


def _flip(t):
    return jnp.flip(t, axis=2)


def _ident(t):
    return t


def dwconv_centred(x, w):
    C = x.shape[-1]
    return lax.conv_general_dilated(
        x, w[:, None, :].astype(x.dtype), window_strides=(1,),
        padding=[(CONV_K // 2, CONV_K // 2)],
        dimension_numbers=("NWC", "WIO", "NWC"), feature_group_count=C)


def axial_rope(x, row, col):
    half = x.shape[-1] // 2
    nf = half // 2
    inv = ROPE_BASE ** (-jnp.arange(nf, dtype=F32) / nf)
    ang = jnp.concatenate([row[:, None] * inv, col[:, None] * inv], axis=-1)
    cos, sin = jnp.cos(ang), jnp.sin(ang)
    xf = x.astype(F32)
    x1, x2 = xf[..., :half], xf[..., half:]
    return jnp.concatenate([x1 * cos - x2 * sin, x1 * sin + x2 * cos], axis=-1).astype(x.dtype)


def decay_states(k, v, log_a, s0):
    kc, vc, la = _chunk(k.astype(F32)), _chunk(v.astype(F32)), _chunk(log_a.astype(F32))
    b = jnp.cumsum(la, axis=3)
    b_last = b[:, :, :, -1]
    k_end = kc * jnp.exp(b_last[:, :, :, None] - b)
    u = jnp.einsum("bhncd,bhnce->bhnde", k_end, vc)

    def step(s, inp):
        dec, u_n = inp
        return dec[..., None] * s + u_n, s

    s_fin, s_prev = lax.scan(step, s0, (jnp.moveaxis(jnp.exp(b_last), 2, 0), jnp.moveaxis(u, 2, 0)))
    return jnp.moveaxis(s_prev, 0, 2), s_fin, b


def decay_outputs(q, k, v, b, s_prev):
    B, H, T, _ = q.shape
    qc, kc, vc = _chunk(q.astype(F32)), _chunk(k.astype(F32)), _chunk(v.astype(F32))
    q_in = qc * jnp.exp(b)
    k_in = kc * jnp.exp(-b)
    mask = jnp.tril(jnp.ones((CHUNK, CHUNK), dtype=bool))
    att = jnp.where(mask, jnp.einsum("bhncd,bhnsd->bhncs", q_in, k_in), 0.0)
    o = jnp.einsum("bhncs,bhnse->bhnce", att, vc) + jnp.einsum("bhncd,bhnde->bhnce", q_in, s_prev)
    return o.reshape(B, H, T, v.shape[-1])


def bidir_decay(ql, kl, vl, lal, qc, kc, vc, lac, need_ctx):
    B, H, _, dk = kl.shape
    s0 = jnp.zeros((B, H, dk, vl.shape[-1]), F32)
    o_l, o_c = [], []
    for d in range(2):
        f = _flip if d else _ident
        sp_c, s_fin_c, b_c = decay_states(f(kc), f(vc), f(lac[d]), s0)
        sp_l, _, b_l = decay_states(f(kl), f(vl), f(lal[d]), s_fin_c)
        o_l.append(f(decay_outputs(f(ql), f(kl), f(vl), b_l, sp_l)))
        if need_ctx:
            o_c.append(f(decay_outputs(f(qc), f(kc), f(vc), b_c, sp_c)))
    return o_l[0] + o_l[1], (o_c[0] + o_c[1] if need_ctx else None)


def mlstm_states(k, v, log_i, log_f, state0):
    kc, vc = _chunk(k.astype(F32)), _chunk(v.astype(F32))
    li, lf = _chunk(log_i), _chunk(log_f)
    b = jnp.cumsum(lf, axis=3)
    b_last = b[..., -1]
    g = b_last[..., None] - b + li
    m_loc = jnp.max(g, axis=-1)
    w = jnp.exp(g - m_loc[..., None])
    u_c = jnp.einsum("bhnc,bhncd,bhnce->bhnde", w, kc, vc)
    u_n = jnp.einsum("bhnc,bhncd->bhnd", w, kc)

    def step(carry, inp):
        c, n, m = carry
        bl, ml, uc, un = inp
        m_new = jnp.maximum(bl + m, ml)
        a = jnp.exp(bl + m - m_new)
        e = jnp.exp(ml - m_new)
        return (a[..., None, None] * c + e[..., None, None] * uc, a[..., None] * n + e[..., None] * un, m_new), (c, n, m)

    mv = lambda t: jnp.moveaxis(t, 2, 0)
    fin, prev = lax.scan(step, state0, (mv(b_last), mv(m_loc), mv(u_c), mv(u_n)))
    prev = tuple(jnp.moveaxis(p, 0, 2) for p in prev)
    return prev, fin, b


def mlstm_outputs(q, k, v, log_i, b, prev):
    B, H, T, _ = q.shape
    c_prev, n_prev, m_prev = prev
    qc, kc, vc = _chunk(q.astype(F32)), _chunk(k.astype(F32)), _chunk(v.astype(F32))
    li = _chunk(log_i)
    mask = jnp.tril(jnp.ones((CHUNK, CHUNK), dtype=bool))
    a = jnp.where(mask, b[..., :, None] - b[..., None, :] + li[..., None, :], -jnp.inf)
    inter = b + m_prev[..., None]
    m = jnp.maximum(jnp.max(a, axis=-1), inter)
    s = jnp.einsum("bhncd,bhnsd->bhncs", qc, kc) * jnp.exp(a - m[..., None])
    e_inter = jnp.exp(inter - m)
    num = jnp.einsum("bhncs,bhnse->bhnce", s, vc) + e_inter[..., None] * jnp.einsum("bhncd,bhnde->bhnce", qc, c_prev)
    den = jnp.sum(s, axis=-1) + e_inter * jnp.einsum("bhncd,bhnd->bhnc", qc, n_prev)
    h = num / jnp.maximum(jnp.abs(den), jnp.exp(-m))[..., None]
    return h.reshape(B, H, T, v.shape[-1])


def bidir_mlstm(ql, kl, vl, lil, lfl, qc, kc, vc, lic, lfc, need_ctx):
    B, H, _, dk = kl.shape
    st0 = (jnp.zeros((B, H, dk, vl.shape[-1]), F32), jnp.zeros((B, H, dk), F32), jnp.full((B, H), M_INIT, F32))
    o_l, o_c = [], []
    for d in range(2):
        f = _flip if d else _ident
        prev_c, fin_c, b_c = mlstm_states(f(kc), f(vc), f(lic[d]), f(lfc[d]), st0)
        prev_l, _, b_l = mlstm_states(f(kl), f(vl), f(lil[d]), f(lfl[d]), fin_c)
        o_l.append(f(mlstm_outputs(f(ql), f(kl), f(vl), f(lil[d]), b_l, prev_l)))
        if need_ctx:
            o_c.append(f(mlstm_outputs(f(qc), f(kc), f(vc), f(lic[d]), b_c, prev_c)))
    return o_l[0] + o_l[1], (o_c[0] + o_c[1] if need_ctx else None)


def gla_branch(pl, pc, a2, a_bias, norm_g, need_ctx, dtype):
    def prep(p):
        q, k, v, g, a = p
        B, T = q.shape[:2]
        q = _heads(q, GLA_HEADS) * GLA_DK ** -0.5
        k, v = _heads(k, GLA_HEADS), _heads(v, GLA_HEADS)
        z = jnp.einsum("btjr,jrk->jbtk", a.reshape(B, T, 2, GLA_RANK).astype(F32), a2.astype(F32)) + a_bias.astype(F32)[:, None, None, :]
        la = jax.nn.log_sigmoid(z) / GLA_TAU
        la = jnp.moveaxis(la.reshape(2, B, T, GLA_HEADS, GLA_DK), 3, 2)
        return q, k, v, la, g

    ql, kl, vl, lal, gl = prep(pl)
    qc, kc, vc, lac, gc = prep(pc)
    o_l, o_c = bidir_decay(ql, kl, vl, lal, qc, kc, vc, lac, need_ctx)
    out_l = head_norm(o_l, norm_g, False, dtype) * jax.nn.silu(gl)
    out_c = head_norm(o_c, norm_g, False, dtype) * jax.nn.silu(gc) if need_ctx else None
    return out_l, out_c


def mlstm_branch(pl, pc, conv_w, i_bias, f_bias, norm_g, need_ctx, dtype):
    def prep(p):
        q, k, v, z, ig, fg = p
        B, T = q.shape[:2]
        qk = jax.nn.silu(dwconv_centred(jnp.concatenate([q, k], axis=-1), conv_w))
        q, k = jnp.split(qk, 2, axis=-1)
        q = _heads(q, ML_HEADS)
        k = _heads(k, ML_HEADS) * ML_DK ** -0.5
        v = _heads(v, ML_HEADS)
        li = ig.reshape(B, T, 2, ML_HEADS).astype(F32) + i_bias.astype(F32)
        lf = jax.nn.log_sigmoid(fg.reshape(B, T, 2, ML_HEADS).astype(F32) + f_bias.astype(F32))
        return q, k, v, jnp.transpose(li, (2, 0, 3, 1)), jnp.transpose(lf, (2, 0, 3, 1)), z

    ql, kl, vl, lil, lfl, zl = prep(pl)
    qc, kc, vc, lic, lfc, zc = prep(pc)
    o_l, o_c = bidir_mlstm(ql, kl, vl, lil, lfl, qc, kc, vc, lic, lfc, need_ctx)
    out_l = head_norm(o_l, norm_g, True, dtype) * jax.nn.silu(zl)
    out_c = head_norm(o_c, norm_g, True, dtype) * jax.nn.silu(zc) if need_ctx else None
    return out_l, out_c


def retention_branch(pl, pc, row, col, decay_logit, norm_g, need_ctx, dtype):
    lg = jax.nn.log_sigmoid(decay_logit.astype(F32))

    def prep(p, rope):
        q, k, v, g = p
        q = _heads(q, RET_HEADS)
        k = _heads(k, RET_HEADS) * RET_DK ** -0.5
        if rope:
            q, k = axial_rope(q, row, col), axial_rope(k, row, col)
        la = [jnp.broadcast_to(lg[d][None, :, None, None], k.shape) for d in range(2)]
        return q, k, _heads(v, RET_HEADS), la, g

    ql, kl, vl, lal, gl = prep(pl, True)
    qc, kc, vc, lac, gc = prep(pc, False)
    o_l, o_c = bidir_decay(ql, kl, vl, lal, qc, kc, vc, lac, need_ctx)
    out_l = head_norm(o_l, norm_g, True, dtype) * jax.nn.silu(gl)
    out_c = head_norm(o_c, norm_g, True, dtype) * jax.nn.silu(gc) if need_ctx else None
    return out_l, out_c


def merge(branches, gate_pre, w_branch, w_o):
    B, T = gate_pre.shape[:2]
    gates = jax.nn.sigmoid(gate_pre.reshape(B, T, N_BRANCH, D_MODEL).astype(F32)).astype(gate_pre.dtype)
    br = jnp.stack(branches, axis=2)
    y = jnp.einsum("btjc,jcd->btjd", br, w_branch)
    return jnp.sum(gates * y, axis=2) @ w_o


def mixer_layer(h_l, h_c, row, col, w_in, gla_a2, gla_a_bias, gla_norm_g, ml_conv, ml_i_bias,
                ml_f_bias, ml_norm_g, ret_decay_logit, ret_norm_g, w_branch, w_o, need_ctx):
    dt = h_l.dtype
    pl = jnp.split(h_l @ w_in, IN_SPLITS, axis=-1)
    pc = jnp.split(h_c @ w_in, IN_SPLITS, axis=-1)
    g_l, g_c = gla_branch(pl[0:5], pc[0:5], gla_a2, gla_a_bias, gla_norm_g, need_ctx, dt)
    m_l, m_c = mlstm_branch(pl[5:11], pc[5:11], ml_conv, ml_i_bias, ml_f_bias, ml_norm_g, need_ctx, dt)
    r_l, r_c = retention_branch(pl[11:15], pc[11:15], row, col, ret_decay_logit, ret_norm_g, need_ctx, dt)
    out_l = merge([g_l, m_l, r_l], pl[15], w_branch, w_o)
    out_c = merge([g_c, m_c, r_c], pc[15], w_branch, w_o) if need_ctx else None
    return out_l, out_c


def squared_relu_mlp(h, w_up, w_down):
    return jnp.square(jax.nn.relu(h @ w_up)) @ w_down


def setup_inputs(seed: int = 0) -> dict:
    key = jax.random.key(seed)
    ks = jax.random.split(key, 24)
    L, D = DEPTH, D_MODEL

    def nrm(k, shape, scale):
        return jax.random.normal(k, shape, F32) * scale

    gamma = 1.0 - 2.0 ** (-5.0 - jnp.arange(RET_HEADS, dtype=F32))
    return {
        "x": nrm(ks[0], (BATCH, SEQ, D), 1.0),
        "c": nrm(ks[1], (BATCH, D), 1.0),
        "ctx": nrm(ks[2], (BATCH, CTX_LEN, D), 1.0),
        "c_ctx": nrm(ks[3], (D,), 1.0),
        "norm1_g": 1.0 + nrm(ks[4], (L, D), 0.01),
        "norm2_g": 1.0 + nrm(ks[5], (L, D), 0.01),
        "w_ada": nrm(ks[6], (L, D, 6 * D), ADA_SCALE * D ** -0.5),
        "b_ada": nrm(ks[7], (L, 6 * D), 0.01),
        "w_in": nrm(ks[8], (L, D, IN_WIDTH), D ** -0.5),
        "gla_a2": nrm(ks[9], (L, 2, GLA_RANK, GLA_QK), GLA_RANK ** -0.5),
        "gla_a_bias": nrm(ks[10], (L, 2, GLA_QK), 0.01),
        "gla_norm_g": 1.0 + nrm(ks[11], (L, BRANCH_W), 0.01),
        "ml_conv": nrm(ks[12], (L, CONV_K, 2 * ML_QK), CONV_K ** -0.5),
        "ml_i_bias": nrm(ks[13], (L, 2, ML_HEADS), 0.01),
        "ml_f_bias": jnp.linspace(ML_F_BIAS_LO, ML_F_BIAS_HI, ML_HEADS, dtype=F32)[None, None, :] + nrm(ks[14], (L, 2, ML_HEADS), 0.01),
        "ml_norm_g": 1.0 + nrm(ks[15], (L, BRANCH_W), 0.01),
        "ret_decay_logit": jnp.log(gamma / (1.0 - gamma))[None, None, :] + nrm(ks[16], (L, 2, RET_HEADS), 0.01),
        "ret_norm_g": 1.0 + nrm(ks[17], (L, BRANCH_W), 0.01),
        "w_branch": nrm(ks[18], (L, N_BRANCH, BRANCH_W, D), BRANCH_W ** -0.5),
        "w_o": nrm(ks[19], (L, D, D), D ** -0.5),
        "w_up": nrm(ks[20], (L, D, D_FF), D ** -0.5),
        "w_down": nrm(ks[21], (L, D_FF, D), D_FF ** -0.5),
        "final_g": 1.0 + nrm(ks[22], (D,), 0.01),
    }


def reference(x, c, ctx, c_ctx, norm1_g, norm2_g, w_ada, b_ada, w_in, gla_a2, gla_a_bias, gla_norm_g,
              ml_conv, ml_i_bias, ml_f_bias, ml_norm_g, ret_decay_logit, ret_norm_g, w_branch, w_o,
              w_up, w_down, final_g):
    n = x.shape[1]
    rows = n // GRID_W
    row = jnp.repeat(jnp.arange(rows, dtype=F32), GRID_W)
    col = jnp.tile(jnp.arange(GRID_W, dtype=F32), rows)
    s_l = jax.nn.silu(c)
    s_c = jax.nn.silu(c_ctx)
    for l in range(DEPTH):
        need_ctx = l < DEPTH - 1
        mod_l = (s_l @ w_ada[l] + b_ada[l]).reshape(-1, 6, D_MODEL)[:, :, None, :]
        mod_c = (s_c @ w_ada[l] + b_ada[l]).reshape(6, D_MODEL)[:, None, None, :]
        h_l = modulate(rms_norm(x, norm1_g[l]), mod_l[:, 0], mod_l[:, 1])
        h_c = modulate(rms_norm(ctx, norm1_g[l]), mod_c[0], mod_c[1])
        o_l, o_c = mixer_layer(h_l, h_c, row, col, w_in[l], gla_a2[l], gla_a_bias[l], gla_norm_g[l],
                               ml_conv[l], ml_i_bias[l], ml_f_bias[l], ml_norm_g[l], ret_decay_logit[l],
                               ret_norm_g[l], w_branch[l], w_o[l], need_ctx)
        x = x + mod_l[:, 2] * o_l
        h2 = modulate(rms_norm(x, norm2_g[l]), mod_l[:, 3], mod_l[:, 4])
        x = x + mod_l[:, 5] * squared_relu_mlp(h2, w_up[l], w_down[l])
        if need_ctx:
            ctx = ctx + mod_c[2] * o_c
            h2c = modulate(rms_norm(ctx, norm2_g[l]), mod_c[3], mod_c[4])
            ctx = ctx + mod_c[5] * squared_relu_mlp(h2c, w_up[l], w_down[l])
    return rms_norm(x, final_g)
```

```python
import functools

import jax
import jax.numpy as jnp
import numpy as np
from jax import lax
from jax.experimental import pallas as pl
from jax.experimental.pallas import tpu as pltpu

F32 = jnp.float32
BF16 = jnp.bfloat16

D_MODEL = 1024
DEPTH = 2
GRID_W = 64
CHUNK = 64
EPS = 1e-6
M_INIT = -1e30
HEADS = 4
DK_SMALL = 64
DK_ML = 128
DV = 128
BRANCH_W = HEADS * DV
GLA_RANK = 16
GLA_TAU = 16.0
ROPE_BASE = 10000.0
D_FF = 4 * D_MODEL

TILE = 256
CHUNKS_PER_TILE = TILE // CHUNK
HALO = 8
VMEM_LIMIT = 56 * 1024 * 1024

W_GLA = 2 * HEADS * DK_SMALL + 2 * BRANCH_W + 2 * GLA_RANK
W_ML = 2 * HEADS * DK_ML + 2 * BRANCH_W + 4 * HEADS
W_RET = 2 * HEADS * DK_SMALL + 2 * BRANCH_W
W_GATE = 3 * D_MODEL


def _cparams(n_axes):
    return pltpu.CompilerParams(dimension_semantics=("arbitrary",) * n_axes,
                                vmem_limit_bytes=VMEM_LIMIT)


def _log_sigmoid(z):
    return jnp.minimum(z, 0.0) - jnp.log(1.0 + jnp.exp(-jnp.abs(z)))


def _sigmoid(z):
    return 1.0 / (1.0 + jnp.exp(-z))


def _silu(z):
    return z * _sigmoid(z)


def _norm_mod(x, g, shift, scale):
    y = x * lax.rsqrt(jnp.mean(x * x, axis=-1, keepdims=True) + EPS) * g
    return y * (1.0 + scale) + shift


def _dot(a, b):
    return jnp.dot(a, b, preferred_element_type=F32)


def _dot_nt(a, b):
    return lax.dot_general(a, b, (((1,), (1,)), ((), ())), preferred_element_type=F32)


def _dot_tn(a, b):
    return lax.dot_general(a, b, (((0,), (0,)), ((), ())), preferred_element_type=F32)


def _dot_exact_lhs(a_bf16, x):
    hi = x.astype(BF16)
    r1 = x - hi.astype(F32)
    mid = r1.astype(BF16)
    lo = (r1 - mid.astype(F32)).astype(BF16)
    return _dot(a_bf16, hi) + _dot(a_bf16, mid) + _dot(a_bf16, lo)


def _chunk_tri(forward):
    t = lax.broadcasted_iota(jnp.int32, (TILE, TILE), 0)
    s = lax.broadcasted_iota(jnp.int32, (TILE, TILE), 1)
    same = (t // CHUNK) == (s // CHUNK)
    tri = (s <= t) if forward else (s >= t)
    return jnp.where(same & tri, 1.0, 0.0).astype(BF16)


def _chunk_total(x):
    w = x.shape[-1]
    tot = jnp.sum(x.reshape(CHUNKS_PER_TILE, CHUNK, w), axis=1, keepdims=True)
    return jnp.broadcast_to(tot, (CHUNKS_PER_TILE, CHUNK, w)).reshape(TILE, w)


def _chunk_max(x):
    w = x.shape[-1]
    mx = jnp.max(x.reshape(CHUNKS_PER_TILE, CHUNK, w), axis=1, keepdims=True)
    return jnp.broadcast_to(mx, (CHUNKS_PER_TILE, CHUNK, w)).reshape(TILE, w)


def _mod_row_index(b, i):
    return jnp.where(i == 0, 2, b)


def _ada_kernel(c_ref, w_ref, b_ref, o_ref):
    s = _silu(c_ref[...])
    o_ref[0] = jnp.dot(s, w_ref[0], precision=lax.Precision.HIGHEST,
                       preferred_element_type=F32) + b_ref[0]


def _ada(c_rows, w_ada, b_ada):
    tn = 1536
    return pl.pallas_call(
        _ada_kernel,
        out_shape=jax.ShapeDtypeStruct((DEPTH, 8, 6 * D_MODEL), F32),
        grid=(DEPTH, 6 * D_MODEL // tn),
        in_specs=[pl.BlockSpec((8, D_MODEL), lambda l, n: (0, 0)),
                  pl.BlockSpec((1, D_MODEL, tn), lambda l, n: (l, 0, n)),
                  pl.BlockSpec((1, 1, tn), lambda l, n: (l, 0, n))],
        out_specs=pl.BlockSpec((1, 8, tn), lambda l, n: (l, 0, n)),
        compiler_params=_cparams(2), name="ada",
    )(c_rows, w_ada, b_ada.reshape(DEPTH, 1, 6 * D_MODEL))


def _rope_kernel(ang_ref, sign_ref, cos_ref, sin_ref):
    a = ang_ref[...]
    cos_ref[...] = jnp.cos(a)
    sin_ref[...] = jnp.sin(a) * sign_ref[...]


def _rope_tables(seq):
    rows = seq // GRID_W
    row = jnp.repeat(jnp.arange(rows, dtype=F32), GRID_W)
    col = jnp.tile(jnp.arange(GRID_W, dtype=F32), rows)
    nf = DK_SMALL // 4
    inv = ROPE_BASE ** (-jnp.arange(nf, dtype=F32) / nf)
    ang = jnp.concatenate([row[:, None] * inv, col[:, None] * inv], axis=-1)
    ang = jnp.concatenate([ang, ang], axis=-1)
    sign = jnp.concatenate([-jnp.ones((1, 32), F32), jnp.ones((1, 32), F32)], axis=-1)
    tt = 1024
    cos, sin = pl.pallas_call(
        _rope_kernel,
        out_shape=(jax.ShapeDtypeStruct((seq, DK_SMALL), F32),) * 2,
        grid=(seq // tt,),
        in_specs=[pl.BlockSpec((tt, DK_SMALL), lambda i: (i, 0)),
                  pl.BlockSpec((1, DK_SMALL), lambda i: (0, 0))],
        out_specs=(pl.BlockSpec((tt, DK_SMALL), lambda i: (i, 0)),) * 2,
        compiler_params=_cparams(1), name="rope_tables",
    )(ang, sign)
    return cos, sin


def _proj_gla_kernel(x_ref, mod_ref, g_ref, w_ref, a2_ref, ab_ref,
                     qin_ref, kin_ref, kend_ref, v_ref, sg_ref, decb_ref):
    h = _norm_mod(x_ref[0], g_ref[...], mod_ref[0, 0:1, :], mod_ref[0, 1:2, :]).astype(BF16)
    pr = _dot(h, w_ref[...])
    nqk = HEADS * DK_SMALL
    q = pr[:, 0:nqk] * (DK_SMALL ** -0.5)
    k = pr[:, nqk:2 * nqk]
    v_ref[0] = pr[:, 2 * nqk:2 * nqk + BRANCH_W].astype(BF16)
    sg_ref[0] = _silu(pr[:, 2 * nqk + BRANCH_W:2 * nqk + 2 * BRANCH_W])
    a = pr[:, 2 * nqk + 2 * BRANCH_W:]
    z = jnp.dot(a, a2_ref[...], precision=lax.Precision.HIGHEST,
                preferred_element_type=F32) + ab_ref[...]
    la = _log_sigmoid(z) * (1.0 / GLA_TAU)
    half = jnp.sum(la.reshape(2 * CHUNKS_PER_TILE, CHUNK // 2, 2 * nqk), axis=1)
    half_t = half.T
    for d in range(2):
        la_d = la[:, d * nqk:(d + 1) * nqk]
        b = _dot_exact_lhs(_chunk_tri(d == 0), la_d)
        tot = _chunk_total(la_d)
        qin_ref[0, :, d * nqk:(d + 1) * nqk] = (q * jnp.exp(b)).astype(BF16)
        kin_ref[0, :, d * nqk:(d + 1) * nqk] = (k * jnp.exp(-b)).astype(BF16)
        kend_ref[0, :, d * nqk:(d + 1) * nqk] = (k * jnp.exp(tot - b)).astype(BF16)
        ht = half_t[d * nqk:(d + 1) * nqk, :]
        for c in range(CHUNKS_PER_TILE):
            tot_col = ht[:, 2 * c:2 * c + 1] + ht[:, 2 * c + 1:2 * c + 2]
            decb_ref[0, c, d * nqk:(d + 1) * nqk, :] = jnp.broadcast_to(jnp.exp(tot_col), (nqk, 128))


def _proj_gla(tok, mods, g1, w, a2blk, abias):
    B, T, _ = tok.shape
    nt = T // TILE
    nqk = HEADS * DK_SMALL
    tile = lambda width: pl.BlockSpec((1, TILE, width), lambda b, i: (b, i, 0))
    const = lambda shape: pl.BlockSpec(shape, lambda b, i: (0,) * len(shape))
    return pl.pallas_call(
        _proj_gla_kernel,
        out_shape=(jax.ShapeDtypeStruct((B, T, 2 * nqk), BF16),
                   jax.ShapeDtypeStruct((B, T, 2 * nqk), BF16),
                   jax.ShapeDtypeStruct((B, T, 2 * nqk), BF16),
                   jax.ShapeDtypeStruct((B, T, BRANCH_W), BF16),
                   jax.ShapeDtypeStruct((B, T, BRANCH_W), F32),
                   jax.ShapeDtypeStruct((B, T // CHUNK, 2 * nqk, 128), F32)),
        grid=(B, nt),
        in_specs=[tile(D_MODEL),
                  pl.BlockSpec((1, 6, D_MODEL), lambda b, i: (_mod_row_index(b, i), 0, 0)),
                  const((1, D_MODEL)), const((D_MODEL, W_GLA)),
                  const((2 * GLA_RANK, 2 * nqk)), const((1, 2 * nqk))],
        out_specs=(tile(2 * nqk), tile(2 * nqk), tile(2 * nqk), tile(BRANCH_W), tile(BRANCH_W),
                   pl.BlockSpec((1, CHUNKS_PER_TILE, 2 * nqk, 128), lambda b, i: (b, i, 0, 0))),
        compiler_params=_cparams(2), name="proj_gla",
    )(tok, mods, g1, w, a2blk, abias)


def _proj_ret_kernel(x_ref, mod_ref, g_ref, w_ref, cos_ref, sin_ref, dl_ref,
                     qin_ref, kin_ref, kend_ref, v_ref, sg_ref):
    h = _norm_mod(x_ref[0], g_ref[...], mod_ref[0, 0:1, :], mod_ref[0, 1:2, :]).astype(BF16)
    pr = _dot(h, w_ref[...])
    nqk = HEADS * DK_SMALL
    cos = cos_ref[...]
    sin = sin_ref[...]
    lane = lax.broadcasted_iota(jnp.int32, (TILE, nqk), 1)
    first_half = (lane % DK_SMALL) < (DK_SMALL // 2)

    def rope(x):
        swapped = jnp.where(first_half, pltpu.roll(x, nqk - DK_SMALL // 2, 1),
                            pltpu.roll(x, DK_SMALL // 2, 1))
        return x * cos + swapped * sin

    q = rope(pr[:, 0:nqk])
    k = rope(pr[:, nqk:2 * nqk] * (DK_SMALL ** -0.5))
    v_ref[0] = pr[:, 2 * nqk:2 * nqk + BRANCH_W].astype(BF16)
    sg_ref[0] = _silu(pr[:, 2 * nqk + BRANCH_W:])
    lg = _log_sigmoid(dl_ref[...])
    t_in_chunk = (lax.broadcasted_iota(jnp.int32, (TILE, nqk), 0) % CHUNK).astype(F32)
    for d in range(2):
        lg_d = lg[:, d * nqk:(d + 1) * nqk]
        steps = (t_in_chunk + 1.0) if d == 0 else (float(CHUNK) - t_in_chunk)
        b = steps * lg_d
        tot = float(CHUNK) * lg_d
        qin_ref[0, :, d * nqk:(d + 1) * nqk] = (q * jnp.exp(b)).astype(BF16)
        kin_ref[0, :, d * nqk:(d + 1) * nqk] = (k * jnp.exp(-b)).astype(BF16)
        kend_ref[0, :, d * nqk:(d + 1) * nqk] = (k * jnp.exp(tot - b)).astype(BF16)


def _proj_ret(tok, mods, g1, w, cos_t, sin_t, dl_row):
    B, T, _ = tok.shape
    nt = T // TILE
    nqk = HEADS * DK_SMALL
    tile = lambda width: pl.BlockSpec((1, TILE, width), lambda b, i: (b, i, 0))
    const = lambda shape: pl.BlockSpec(shape, lambda b, i: (0,) * len(shape))
    return pl.pallas_call(
        _proj_ret_kernel,
        out_shape=(jax.ShapeDtypeStruct((B, T, 2 * nqk), BF16),
                   jax.ShapeDtypeStruct((B, T, 2 * nqk), BF16),
                   jax.ShapeDtypeStruct((B, T, 2 * nqk), BF16),
                   jax.ShapeDtypeStruct((B, T, BRANCH_W), BF16),
                   jax.ShapeDtypeStruct((B, T, BRANCH_W), F32)),
        grid=(B, nt),
        in_specs=[tile(D_MODEL),
                  pl.BlockSpec((1, 6, D_MODEL), lambda b, i: (_mod_row_index(b, i), 0, 0)),
                  const((1, D_MODEL)), const((D_MODEL, W_RET)),
                  pl.BlockSpec((TILE, nqk), lambda b, i: (i, 0)),
                  pl.BlockSpec((TILE, nqk), lambda b, i: (i, 0)),
                  const((1, 2 * nqk))],
        out_specs=(tile(2 * nqk), tile(2 * nqk), tile(2 * nqk), tile(BRANCH_W), tile(BRANCH_W)),
        compiler_params=_cparams(2), name="proj_ret",
    )(tok, mods, g1, w, cos_t, sin_t, dl_row)


def _proj_ml_kernel(nt, x_ref, xp_ref, xn_ref, mod_ref, g_ref, w_ref, cw_ref, gb_ref,
                    q_ref, k_ref, kw_ref, v_ref, sz_ref, p0_ref, gp_ref, ext_ref):
    i = pl.program_id(1)
    nqk = HEADS * DK_ML
    x_all = jnp.concatenate([xp_ref[0], x_ref[0], xn_ref[0]], axis=0)
    h_all = _norm_mod(x_all, g_ref[...], mod_ref[0, 0:1, :], mod_ref[0, 1:2, :])
    ext = _dot(h_all.astype(BF16), w_ref[:, 0:2 * nqk])
    row = lax.broadcasted_iota(jnp.int32, (TILE + 2 * HALO, 1), 0)
    prev_ok = i >= 2
    next_ok = jnp.logical_and(i >= 1, i <= nt - 2)
    dead = jnp.logical_or(jnp.logical_and(row < HALO, jnp.logical_not(prev_ok)),
                          jnp.logical_and(row >= TILE + HALO, jnp.logical_not(next_ok)))
    ext_ref[...] = jnp.where(dead, 0.0, ext)
    cw = cw_ref[...]
    qk = (cw[0:1, :] * ext_ref[HALO - 1:HALO - 1 + TILE, :]
          + cw[1:2, :] * ext_ref[HALO:HALO + TILE, :]
          + cw[2:3, :] * ext_ref[HALO + 1:HALO + 1 + TILE, :])
    qk = _silu(qk)
    q_ref[0] = qk[:, 0:nqk].astype(BF16)
    k = qk[:, nqk:] * (DK_ML ** -0.5)
    k_ref[0] = k.astype(BF16)

    h = h_all[HALO:HALO + TILE, :].astype(BF16)
    pr = _dot(h, w_ref[:, 2 * nqk:])
    v_ref[0] = pr[:, 0:BRANCH_W].astype(BF16)
    sz_ref[0] = _silu(pr[:, BRANCH_W:2 * BRANCH_W])
    gates = pr[:, 2 * BRANCH_W:] + gb_ref[...]
    nhd = 2 * HEADS
    li = gates[:, 0:nhd]
    lf = _log_sigmoid(gates[:, nhd:])
    colid = lax.broadcasted_iota(jnp.int32, (TILE, nhd), 1)
    b = jnp.where(colid < HEADS, _dot_exact_lhs(_chunk_tri(True), lf),
                  _dot_exact_lhs(_chunk_tri(False), lf))
    tot = _chunk_total(lf)
    g = tot - b + li
    mloc = _chunk_max(g)
    w = jnp.exp(g - mloc)
    r = li - b
    r_t = r.T
    ti = lax.broadcasted_iota(jnp.int32, (CHUNK, CHUNK), 0)
    si = lax.broadcasted_iota(jnp.int32, (CHUNK, CHUNK), 1)
    gp_ref[0, :, 0:nhd] = b
    gp_ref[0, :, 2 * nhd:3 * nhd] = tot
    gp_ref[0, :, 3 * nhd:4 * nhd] = mloc
    for j in range(nhd):
        d, hd = j // HEADS, j % HEADS
        mask = (si <= ti) if d == 0 else (si >= ti)
        kw_ref[0, :, d * nqk + hd * DK_ML:d * nqk + (hd + 1) * DK_ML] = (
            k[:, hd * DK_ML:(hd + 1) * DK_ML] * w[:, j:j + 1]).astype(BF16)
        for c in range(CHUNKS_PER_TILE):
            rows = slice(c * CHUNK, (c + 1) * CHUNK)
            a = jnp.where(mask, b[rows, j:j + 1] + r_t[j:j + 1, rows], -jnp.inf)
            mrow = jnp.max(a, axis=-1, keepdims=True)
            p0_ref[0, j, rows, :] = jnp.exp(a - mrow)
            gp_ref[0, rows, nhd + j:nhd + j + 1] = mrow


def _proj_ml(tok, mods, g1, w, conv_w, gate_bias):
    B, T, _ = tok.shape
    nt = T // TILE
    nqk = HEADS * DK_ML
    tile = lambda width: pl.BlockSpec((1, TILE, width), lambda b, i: (b, i, 0))
    const = lambda shape: pl.BlockSpec(shape, lambda b, i: (0,) * len(shape))
    per_halo = TILE // HALO
    n_halo = T // HALO
    return pl.pallas_call(
        functools.partial(_proj_ml_kernel, nt),
        out_shape=(jax.ShapeDtypeStruct((B, T, nqk), BF16),
                   jax.ShapeDtypeStruct((B, T, nqk), BF16),
                   jax.ShapeDtypeStruct((B, T, 2 * nqk), BF16),
                   jax.ShapeDtypeStruct((B, T, BRANCH_W), BF16),
                   jax.ShapeDtypeStruct((B, T, BRANCH_W), F32),
                   jax.ShapeDtypeStruct((B, 2 * HEADS, T, CHUNK), F32),
                   jax.ShapeDtypeStruct((B, T, 8 * HEADS), F32)),
        grid=(B, nt),
        in_specs=[tile(D_MODEL),
                  pl.BlockSpec((1, HALO, D_MODEL),
                               lambda b, i: (b, jnp.maximum(i * per_halo - 1, 0), 0)),
                  pl.BlockSpec((1, HALO, D_MODEL),
                               lambda b, i: (b, jnp.minimum((i + 1) * per_halo, n_halo - 1), 0)),
                  pl.BlockSpec((1, 6, D_MODEL), lambda b, i: (_mod_row_index(b, i), 0, 0)),
                  const((1, D_MODEL)), const((D_MODEL, W_ML)),
                  const((3, 2 * nqk)), const((1, 4 * HEADS))],
        out_specs=(tile(nqk), tile(nqk), tile(2 * nqk), tile(BRANCH_W), tile(BRANCH_W),
                   pl.BlockSpec((1, 2 * HEADS, TILE, CHUNK), lambda b, i: (b, 0, i, 0)),
                   tile(8 * HEADS)),
        scratch_shapes=[pltpu.VMEM((TILE + 2 * HALO, 2 * nqk), F32)],
        compiler_params=_cparams(2), name="proj_ml",
    )(tok, tok, tok, mods, g1, w, conv_w, gate_bias)


def _mix_kernel(*refs):
    (gq_f, gq_b, gk_f, gk_b, ge_f, ge_b, gv_f, gv_b, gd_f, gd_b,
     rq_f, rq_b, rk_f, rk_b, re_f, re_b, rv_f, rv_b, rdl,
     mq_f, mq_b, mk_f, mk_b, mw_f, mw_b, mv_f, mv_b, mp_f, mp_b, mg_f, mg_b,
     of_ref, ob_ref, sg_ref, sr_ref, sm_ref, m_ref) = refs
    j = pl.program_id(1)

    @pl.when(j == 0)
    def _():
        sg_ref[...] = jnp.zeros_like(sg_ref)
        sr_ref[...] = jnp.zeros_like(sr_ref)
        sm_ref[...] = jnp.zeros_like(sm_ref)
        m_ref[...] = jnp.full_like(m_ref, M_INIT)

    ti = lax.broadcasted_iota(jnp.int32, (CHUNK, CHUNK), 0)
    si = lax.broadcasted_iota(jnp.int32, (CHUNK, CHUNK), 1)
    masks = (si <= ti, si >= ti)
    ones_col = jnp.where(lax.broadcasted_iota(jnp.int32, (CHUNK, DV), 1) == 0, 1.0, 0.0).astype(BF16)
    ret_dec = jnp.exp(float(CHUNK) * _log_sigmoid(rdl[...]))

    decay_sets = (
        ((gq_f, gk_f, ge_f, gv_f), (gq_b, gk_b, ge_b, gv_b), sg_ref, 0),
        ((rq_f, rk_f, re_f, rv_f), (rq_b, rk_b, re_b, rv_b), sr_ref, 2 * BRANCH_W),
    )
    ml_sets = ((mq_f, mk_f, mw_f, mv_f, mp_f, mg_f), (mq_b, mk_b, mw_b, mv_b, mp_b, mg_b))
    outs = (of_ref, ob_ref)

    def chunk_step(step, carry):
        for d in range(2):
            c = step if d == 0 else CHUNKS_PER_TILE - 1 - step
            r0 = pl.multiple_of(c * CHUNK, CHUNK)
            rows = pl.ds(r0, CHUNK)
            o_ref = outs[d]
            for bi, (fset, bset, s_ref, base) in enumerate(decay_sets):
                q_r, k_r, e_r, v_r = fset if d == 0 else bset
                for hd in range(HEADS):
                    ks = slice(hd * DK_SMALL, (hd + 1) * DK_SMALL)
                    vs = slice(hd * DV, (hd + 1) * DV)
                    q = q_r[0, rows, ks]
                    v = v_r[0, rows, vs]
                    att = jnp.where(masks[d], _dot_nt(q, k_r[0, rows, ks]), 0.0).astype(BF16)
                    s_prev = s_ref[d, hd]
                    o_ref[0, rows, base + hd * DV:base + (hd + 1) * DV] = (
                        _dot(att, v) + _dot(q, s_prev.astype(BF16)))
                    if bi == 0:
                        dec = (gd_f if d == 0 else gd_b)[0, c, ks, :]
                    else:
                        dec = ret_dec[d, ks, :]
                    s_ref[d, hd] = dec * s_prev + _dot_tn(e_r[0, rows, ks], v)
            q_r, k_r, w_r, v_r, p_r, g_r = ml_sets[d]
            gp = g_r[0, rows, :]
            nhd = 2 * HEADS
            for hd in range(HEADS):
                jj = d * HEADS + hd
                ks = slice(hd * DK_ML, (hd + 1) * DK_ML)
                vs = slice(hd * DV, (hd + 1) * DV)
                q = q_r[0, rows, ks]
                vext = jnp.concatenate([v_r[0, rows, vs], ones_col], axis=1)
                b_col = gp[:, jj:jj + 1]
                mrow = gp[:, nhd + jj:nhd + jj + 1]
                tot = gp[0:1, 2 * nhd + jj:2 * nhd + jj + 1]
                mloc = gp[0:1, 3 * nhd + jj:3 * nhd + jj + 1]
                m_prev = m_ref[jj:jj + 1, 0:1]
                sc = (_dot_nt(q, k_r[0, rows, ks]) * p_r[0, hd, rows, :]).astype(BF16)
                inter = b_col + m_prev
                m = jnp.maximum(mrow, inter)
                st = sm_ref[d, hd]
                nd = (jnp.exp(mrow - m) * _dot(sc, vext)
                      + jnp.exp(inter - m) * _dot(q, st.astype(BF16)))
                den = jnp.maximum(jnp.abs(nd[:, DV:DV + 1]), jnp.exp(-m))
                o_ref[0, rows, BRANCH_W + hd * DV:BRANCH_W + (hd + 1) * DV] = nd[:, 0:DV] / den
                m_new = jnp.maximum(tot + m_prev, mloc)
                sm_ref[d, hd] = (jnp.exp(tot + m_prev - m_new) * st
                                 + jnp.exp(mloc - m_new) * _dot_tn(w_r[0, rows, ks], vext))
                m_ref[jj:jj + 1, :] = jnp.broadcast_to(m_new, (1, 128))
        return carry

    lax.fori_loop(0, CHUNKS_PER_TILE, chunk_step, 0)


def _mix(gla, ret, ml, ret_dl_col):
    gq, gk, ge, gv, gdec = gla
    rq, rk, re, rv = ret
    mq, mk, mw, mv, mp, mg = ml
    B, T, _ = gv.shape
    nt = T // TILE
    nqk_s = HEADS * DK_SMALL
    nqk_m = HEADS * DK_ML

    def tf(j):
        return j

    def tb(j):
        return jnp.where(j == 0, 0, nt - j)

    def pair(width, lane_block_b=0):
        return [pl.BlockSpec((1, TILE, width), lambda b, j: (b, tf(j), 0)),
                pl.BlockSpec((1, TILE, width), lambda b, j: (b, tb(j), lane_block_b))]

    dec_pair = [pl.BlockSpec((1, CHUNKS_PER_TILE, nqk_s, 128), lambda b, j: (b, tf(j), 0, 0)),
                pl.BlockSpec((1, CHUNKS_PER_TILE, nqk_s, 128), lambda b, j: (b, tb(j), 1, 0))]
    p0_pair = [pl.BlockSpec((1, HEADS, TILE, CHUNK), lambda b, j: (b, 0, tf(j), 0)),
               pl.BlockSpec((1, HEADS, TILE, CHUNK), lambda b, j: (b, 1, tb(j), 0))]
    decay_specs = pair(nqk_s, 1) + pair(nqk_s, 1) + pair(nqk_s, 1) + pair(BRANCH_W)
    in_specs = (decay_specs + dec_pair + decay_specs
                + [pl.BlockSpec((2, nqk_s, 128), lambda b, j: (0, 0, 0))]
                + pair(nqk_m) + pair(nqk_m) + pair(nqk_m, 1) + pair(BRANCH_W) + p0_pair
                + pair(8 * HEADS))
    args = (gq, gq, gk, gk, ge, ge, gv, gv, gdec, gdec,
            rq, rq, rk, rk, re, re, rv, rv, ret_dl_col,
            mq, mq, mk, mk, mw, mw, mv, mv, mp, mp, mg, mg)
    return pl.pallas_call(
        _mix_kernel,
        out_shape=(jax.ShapeDtypeStruct((B, T, 3 * BRANCH_W), F32),) * 2,
        grid=(B, nt),
        in_specs=in_specs,
        out_specs=(pl.BlockSpec((1, TILE, 3 * BRANCH_W), lambda b, j: (b, tf(j), 0)),
                   pl.BlockSpec((1, TILE, 3 * BRANCH_W), lambda b, j: (b, tb(j), 0))),
        scratch_shapes=[pltpu.VMEM((2, HEADS, DK_SMALL, DV), F32),
                        pltpu.VMEM((2, HEADS, DK_SMALL, DV), F32),
                        pltpu.VMEM((2, HEADS, DK_ML, 2 * DV), F32),
                        pltpu.VMEM((2 * HEADS, 128), F32)],
        compiler_params=_cparams(2), name="mix",
    )(*args)


def _merge_kernel(x_ref, mod_ref, g_ref, wg_ref, of_ref, ob_ref, sgg_ref, sgm_ref, sgr_ref,
                  ng_ref, wb_ref, wo_ref, out_ref):
    x = x_ref[0]
    h = _norm_mod(x, g_ref[...], mod_ref[0, 0:1, :], mod_ref[0, 1:2, :]).astype(BF16)
    o = of_ref[0] + ob_ref[0]
    side = (sgg_ref, sgm_ref, sgr_ref)
    acc = jnp.zeros((TILE, D_MODEL), F32)
    for br in range(3):
        parts = []
        for hd in range(HEADS):
            oh = o[:, br * BRANCH_W + hd * DV:br * BRANCH_W + (hd + 1) * DV]
            if br > 0:
                oh = oh - jnp.mean(oh, axis=-1, keepdims=True)
            parts.append(oh * lax.rsqrt(jnp.mean(oh * oh, axis=-1, keepdims=True) + EPS))
        y = jnp.concatenate(parts, axis=1) * ng_ref[:, br * BRANCH_W:(br + 1) * BRANCH_W]
        y = (y * side[br][0]).astype(BF16)
        gate = _sigmoid(_dot(h, wg_ref[:, br * D_MODEL:(br + 1) * D_MODEL]))
        acc = acc + gate * _dot(y, wb_ref[br])
    out = _dot(acc.astype(BF16), wo_ref[...])
    out_ref[0] = x + mod_ref[0, 2:3, :] * out


def _merge(tok, mods, g1, wgate, o_f, o_b, sg_gla, sz_ml, sg_ret, norm_g, wb, wo, skip):
    B, T, _ = tok.shape
    nt = T // TILE - skip
    tile = lambda width: pl.BlockSpec((1, TILE, width), lambda b, i: (b, i + skip, 0))
    const = lambda shape: pl.BlockSpec(shape, lambda b, i: (0,) * len(shape))
    return pl.pallas_call(
        _merge_kernel,
        out_shape=jax.ShapeDtypeStruct((B, T, D_MODEL), F32),
        grid=(B, nt),
        in_specs=[tile(D_MODEL),
                  pl.BlockSpec((1, 6, D_MODEL), lambda b, i: (_mod_row_index(b, i + skip), 0, 0)),
                  const((1, D_MODEL)), const((D_MODEL, W_GATE)),
                  tile(3 * BRANCH_W), tile(3 * BRANCH_W),
                  tile(BRANCH_W), tile(BRANCH_W), tile(BRANCH_W),
                  const((1, 3 * BRANCH_W)), const((3, BRANCH_W, D_MODEL)), const((D_MODEL, D_MODEL))],
        out_specs=tile(D_MODEL),
        compiler_params=_cparams(2), name="merge",
    )(tok, mods, g1, wgate, o_f, o_b, sg_gla, sz_ml, sg_ret, norm_g, wb, wo)


def _mlp_kernel(final, x_ref, mod_ref, g_ref, wu_ref, wd_ref, fg_ref, out_ref):
    x = x_ref[0]
    h = _norm_mod(x, g_ref[...], mod_ref[0, 3:4, :], mod_ref[0, 4:5, :]).astype(BF16)
    u = jnp.maximum(_dot(h, wu_ref[...]), 0.0)
    y = x + mod_ref[0, 5:6, :] * _dot((u * u).astype(BF16), wd_ref[...])
    if final:
        y = y * lax.rsqrt(jnp.mean(y * y, axis=-1, keepdims=True) + EPS) * fg_ref[...]
    out_ref[0] = y


def _mlp(tok, mods, g2, wu, wd, final_g, skip, final):
    B, T, _ = tok.shape
    nt = T // TILE - skip
    out_t = T - skip * TILE if final else T
    out_off = 0 if final else skip
    const = lambda shape: pl.BlockSpec(shape, lambda b, i: (0,) * len(shape))
    return pl.pallas_call(
        functools.partial(_mlp_kernel, final),
        out_shape=jax.ShapeDtypeStruct((B, out_t, D_MODEL), F32),
        grid=(B, nt),
        in_specs=[pl.BlockSpec((1, TILE, D_MODEL), lambda b, i: (b, i + skip, 0)),
                  pl.BlockSpec((1, 6, D_MODEL), lambda b, i: (_mod_row_index(b, i + skip), 0, 0)),
                  const((1, D_MODEL)), const((D_MODEL, D_FF)), const((D_FF, D_MODEL)),
                  const((1, D_MODEL))],
        out_specs=pl.BlockSpec((1, TILE, D_MODEL), lambda b, i: (b, i + out_off, 0)),
        compiler_params=_cparams(2), name="mlp",
    )(tok, mods, g2, wu, wd, final_g)


def kernel(x, c, ctx, c_ctx, norm1_g, norm2_g, w_ada, b_ada, w_in, gla_a2, gla_a_bias, gla_norm_g,
           ml_conv, ml_i_bias, ml_f_bias, ml_norm_g, ret_decay_logit, ret_norm_g, w_branch, w_o,
           w_up, w_down, final_g):
    B, seq, _ = x.shape
    ctx_len = ctx.shape[1]
    assert B == 2 and ctx_len == TILE and seq % TILE == 0 and seq % GRID_W == 0
    tok = jnp.concatenate([ctx, x], axis=1)

    c_rows = jnp.concatenate([c, c_ctx[None, :], jnp.zeros((8 - B - 1, D_MODEL), F32)], axis=0)
    mods_all = _ada(c_rows, w_ada, b_ada)[:, 0:3].reshape(DEPTH, 3, 6, D_MODEL)

    cos64, sin64 = _rope_tables(seq)
    cos_t = jnp.concatenate([jnp.ones((ctx_len, HEADS * DK_SMALL), F32), jnp.tile(cos64, (1, HEADS))], axis=0)
    sin_t = jnp.concatenate([jnp.zeros((ctx_len, HEADS * DK_SMALL), F32), jnp.tile(sin64, (1, HEADS))], axis=0)

    o0 = W_GLA
    o1 = o0 + W_ML
    o2 = o1 + W_RET
    nqk_s = HEADS * DK_SMALL
    for l in range(DEPTH):
        last = l == DEPTH - 1
        skip = 1 if last else 0
        mods = mods_all[l]
        g1 = norm1_g[l][None, :]
        g2 = norm2_g[l][None, :]
        w_l = w_in[l]
        w_gla = w_l[:, 0:o0].astype(BF16)
        w_ml = w_l[:, o0:o1].astype(BF16)
        w_ret = w_l[:, o1:o2].astype(BF16)
        w_gate = w_l[:, o2:].astype(BF16)
        zeros = jnp.zeros((GLA_RANK, nqk_s), F32)
        a2blk = jnp.concatenate([jnp.concatenate([gla_a2[l, 0], zeros], axis=1),
                                 jnp.concatenate([zeros, gla_a2[l, 1]], axis=1)], axis=0)
        abias = gla_a_bias[l].reshape(1, 2 * nqk_s)
        gate_bias = jnp.concatenate([ml_i_bias[l].reshape(1, 2 * HEADS),
                                     ml_f_bias[l].reshape(1, 2 * HEADS)], axis=1)
        dl = ret_decay_logit[l]
        dl_row = jnp.repeat(dl, DK_SMALL, axis=1).reshape(1, 2 * nqk_s)
        dl_col = jnp.broadcast_to(jnp.repeat(dl, DK_SMALL, axis=1)[:, :, None], (2, nqk_s, 128))

        gla = _proj_gla(tok, mods, g1, w_gla, a2blk, abias)
        ml = _proj_ml(tok, mods, g1, w_ml, ml_conv[l], gate_bias)
        ret = _proj_ret(tok, mods, g1, w_ret, cos_t, sin_t, dl_row)
        gq, gk, ge, gv, sg_gla, gdec = gla
        mq, mk, mw, mv, sz_ml, mp, mg = ml
        rq, rk, re, rv, sg_ret = ret
        o_f, o_b = _mix((gq, gk, ge, gv, gdec), (rq, rk, re, rv), (mq, mk, mw, mv, mp, mg), dl_col)
        norm_g = jnp.concatenate([gla_norm_g[l], ml_norm_g[l], ret_norm_g[l]])[None, :]
        tok = _merge(tok, mods, g1, w_gate, o_f, o_b, sg_gla, sz_ml, sg_ret, norm_g,
                     w_branch[l].astype(BF16), w_o[l].astype(BF16), skip)
        tok = _mlp(tok, mods, g2, w_up[l].astype(BF16), w_down[l].astype(BF16),
                   final_g[None, :], skip, last)
    return tok
```

```python
import functools

import jax
import jax.numpy as jnp
import numpy as np
from jax import lax
from jax.experimental import pallas as pl
from jax.experimental.pallas import tpu as pltpu

F32 = jnp.float32
BF16 = jnp.bfloat16

D_MODEL = 1024
DEPTH = 2
GRID_W = 64
CHUNK = 64
EPS = 1e-6
M_INIT = -1e30
HEADS = 4
DK_SMALL = 64
DK_ML = 128
DV = 128
BRANCH_W = HEADS * DV
GLA_RANK = 16
GLA_TAU = 16.0
ROPE_BASE = 10000.0
D_FF = 4 * D_MODEL

TILE = 256
CHUNKS_PER_TILE = TILE // CHUNK
HALO = 8
VMEM_LIMIT = 56 * 1024 * 1024

W_GLA = 2 * HEADS * DK_SMALL + 2 * BRANCH_W + 2 * GLA_RANK
W_ML = 2 * HEADS * DK_ML + 2 * BRANCH_W + 4 * HEADS
W_RET = 2 * HEADS * DK_SMALL + 2 * BRANCH_W
W_GATE = 3 * D_MODEL


def _cparams(n_axes):
    return pltpu.CompilerParams(dimension_semantics=("arbitrary",) * n_axes,
                                vmem_limit_bytes=VMEM_LIMIT)


def _log_sigmoid(z):
    return jnp.minimum(z, 0.0) - jnp.log(1.0 + jnp.exp(-jnp.abs(z)))


def _sigmoid(z):
    return 1.0 / (1.0 + jnp.exp(-z))


def _silu(z):
    return z * _sigmoid(z)


def _norm_mod(x, g, shift, scale):
    y = x * lax.rsqrt(jnp.mean(x * x, axis=-1, keepdims=True) + EPS) * g
    return y * (1.0 + scale) + shift


def _dot(a, b):
    return jnp.dot(a, b, preferred_element_type=F32)


def _dot_nt(a, b):
    return lax.dot_general(a, b, (((1,), (1,)), ((), ())), preferred_element_type=F32)


def _dot_tn(a, b):
    return lax.dot_general(a, b, (((0,), (0,)), ((), ())), preferred_element_type=F32)


def _dot_exact_lhs(a_bf16, x):
    hi = x.astype(BF16)
    r1 = x - hi.astype(F32)
    mid = r1.astype(BF16)
    lo = (r1 - mid.astype(F32)).astype(BF16)
    return _dot(a_bf16, hi) + _dot(a_bf16, mid) + _dot(a_bf16, lo)


def _chunk_tri(forward):
    t = lax.broadcasted_iota(jnp.int32, (TILE, TILE), 0)
    s = lax.broadcasted_iota(jnp.int32, (TILE, TILE), 1)
    same = (t // CHUNK) == (s // CHUNK)
    tri = (s <= t) if forward else (s >= t)
    return jnp.where(same & tri, 1.0, 0.0).astype(BF16)


def _chunk_total(x):
    w = x.shape[-1]
    tot = jnp.sum(x.reshape(CHUNKS_PER_TILE, CHUNK, w), axis=1, keepdims=True)
    return jnp.broadcast_to(tot, (CHUNKS_PER_TILE, CHUNK, w)).reshape(TILE, w)


def _chunk_max(x):
    w = x.shape[-1]
    mx = jnp.max(x.reshape(CHUNKS_PER_TILE, CHUNK, w), axis=1, keepdims=True)
    return jnp.broadcast_to(mx, (CHUNKS_PER_TILE, CHUNK, w)).reshape(TILE, w)


def _mod_row_index(b, i):
    return jnp.where(i == 0, 2, b)


def _ada_kernel(c_ref, w_ref, b_ref, o_ref):
    s = _silu(c_ref[...])
    o_ref[0] = jnp.dot(s, w_ref[0], precision=lax.Precision.HIGHEST,
                       preferred_element_type=F32) + b_ref[0]


def _ada(c_rows, w_ada, b_ada):
    tn = 1536
    return pl.pallas_call(
        _ada_kernel,
        out_shape=jax.ShapeDtypeStruct((DEPTH, 8, 6 * D_MODEL), F32),
        grid=(DEPTH, 6 * D_MODEL // tn),
        in_specs=[pl.BlockSpec((8, D_MODEL), lambda l, n: (0, 0)),
                  pl.BlockSpec((1, D_MODEL, tn), lambda l, n: (l, 0, n)),
                  pl.BlockSpec((1, 1, tn), lambda l, n: (l, 0, n))],
        out_specs=pl.BlockSpec((1, 8, tn), lambda l, n: (l, 0, n)),
        compiler_params=_cparams(2), name="ada",
    )(c_rows, w_ada, b_ada.reshape(DEPTH, 1, 6 * D_MODEL))


def _rope_kernel(ang_ref, sign_ref, cos_ref, sin_ref):
    a = ang_ref[...]
    cos_ref[...] = jnp.cos(a)
    sin_ref[...] = jnp.sin(a) * sign_ref[...]


def _rope_tables(seq):
    rows = seq // GRID_W
    row = jnp.repeat(jnp.arange(rows, dtype=F32), GRID_W)
    col = jnp.tile(jnp.arange(GRID_W, dtype=F32), rows)
    nf = DK_SMALL // 4
    inv = ROPE_BASE ** (-jnp.arange(nf, dtype=F32) / nf)
    ang = jnp.concatenate([row[:, None] * inv, col[:, None] * inv], axis=-1)
    ang = jnp.concatenate([ang, ang], axis=-1)
    sign = jnp.concatenate([-jnp.ones((1, 32), F32), jnp.ones((1, 32), F32)], axis=-1)
    tt = 1024
    cos, sin = pl.pallas_call(
        _rope_kernel,
        out_shape=(jax.ShapeDtypeStruct((seq, DK_SMALL), F32),) * 2,
        grid=(seq // tt,),
        in_specs=[pl.BlockSpec((tt, DK_SMALL), lambda i: (i, 0)),
                  pl.BlockSpec((1, DK_SMALL), lambda i: (0, 0))],
        out_specs=(pl.BlockSpec((tt, DK_SMALL), lambda i: (i, 0)),) * 2,
        compiler_params=_cparams(1), name="rope_tables",
    )(ang, sign)
    return cos, sin


def _proj_gla_kernel(x_ref, mod_ref, g_ref, w_ref, a2_ref, ab_ref,
                     qin_ref, kin_ref, kend_ref, v_ref, sg_ref, decb_ref):
    h = _norm_mod(x_ref[0], g_ref[...], mod_ref[0, 0:1, :], mod_ref[0, 1:2, :]).astype(BF16)
    pr = _dot(h, w_ref[...])
    nqk = HEADS * DK_SMALL
    q = pr[:, 0:nqk] * (DK_SMALL ** -0.5)
    k = pr[:, nqk:2 * nqk]
    v_ref[0] = pr[:, 2 * nqk:2 * nqk + BRANCH_W].astype(BF16)
    sg_ref[0] = _silu(pr[:, 2 * nqk + BRANCH_W:2 * nqk + 2 * BRANCH_W])
    a = pr[:, 2 * nqk + 2 * BRANCH_W:]
    z = jnp.dot(a, a2_ref[...], precision=lax.Precision.HIGHEST,
                preferred_element_type=F32) + ab_ref[...]
    la = _log_sigmoid(z) * (1.0 / GLA_TAU)
    half = jnp.sum(la.reshape(2 * CHUNKS_PER_TILE, CHUNK // 2, 2 * nqk), axis=1)
    half_t = half.T
    for d in range(2):
        la_d = la[:, d * nqk:(d + 1) * nqk]
        b = _dot_exact_lhs(_chunk_tri(d == 0), la_d)
        tot = _chunk_total(la_d)
        qin_ref[0, :, d * nqk:(d + 1) * nqk] = (q * jnp.exp(b)).astype(BF16)
        kin_ref[0, :, d * nqk:(d + 1) * nqk] = (k * jnp.exp(-b)).astype(BF16)
        kend_ref[0, :, d * nqk:(d + 1) * nqk] = (k * jnp.exp(tot - b)).astype(BF16)
        ht = half_t[d * nqk:(d + 1) * nqk, :]
        for c in range(CHUNKS_PER_TILE):
            tot_col = ht[:, 2 * c:2 * c + 1] + ht[:, 2 * c + 1:2 * c + 2]
            decb_ref[0, c, d * nqk:(d + 1) * nqk, :] = jnp.broadcast_to(jnp.exp(tot_col), (nqk, 128))


def _proj_gla(tok, mods, g1, w, a2blk, abias):
    B, T, _ = tok.shape
    nt = T // TILE
    nqk = HEADS * DK_SMALL
    tile = lambda width: pl.BlockSpec((1, TILE, width), lambda b, i: (b, i, 0))
    const = lambda shape: pl.BlockSpec(shape, lambda b, i: (0,) * len(shape))
    return pl.pallas_call(
        _proj_gla_kernel,
        out_shape=(jax.ShapeDtypeStruct((B, T, 2 * nqk), BF16),
                   jax.ShapeDtypeStruct((B, T, 2 * nqk), BF16),
                   jax.ShapeDtypeStruct((B, T, 2 * nqk), BF16),
                   jax.ShapeDtypeStruct((B, T, BRANCH_W), BF16),
                   jax.ShapeDtypeStruct((B, T, BRANCH_W), F32),
                   jax.ShapeDtypeStruct((B, T // CHUNK, 2 * nqk, 128), F32)),
        grid=(B, nt),
        in_specs=[tile(D_MODEL),
                  pl.BlockSpec((1, 6, D_MODEL), lambda b, i: (_mod_row_index(b, i), 0, 0)),
                  const((1, D_MODEL)), const((D_MODEL, W_GLA)),
                  const((2 * GLA_RANK, 2 * nqk)), const((1, 2 * nqk))],
        out_specs=(tile(2 * nqk), tile(2 * nqk), tile(2 * nqk), tile(BRANCH_W), tile(BRANCH_W),
                   pl.BlockSpec((1, CHUNKS_PER_TILE, 2 * nqk, 128), lambda b, i: (b, i, 0, 0))),
        compiler_params=_cparams(2), name="proj_gla",
    )(tok, mods, g1, w, a2blk, abias)


def _proj_ret_kernel(x_ref, mod_ref, g_ref, w_ref, cos_ref, sin_ref, dl_ref,
                     qin_ref, kin_ref, kend_ref, v_ref, sg_ref):
    h = _norm_mod(x_ref[0], g_ref[...], mod_ref[0, 0:1, :], mod_ref[0, 1:2, :]).astype(BF16)
    pr = _dot(h, w_ref[...])
    nqk = HEADS * DK_SMALL
    cos = cos_ref[...]
    sin = sin_ref[...]
    lane = lax.broadcasted_iota(jnp.int32, (TILE, nqk), 1)
    first_half = (lane % DK_SMALL) < (DK_SMALL // 2)

    def rope(x):
        swapped = jnp.where(first_half, pltpu.roll(x, nqk - DK_SMALL // 2, 1),
                            pltpu.roll(x, DK_SMALL // 2, 1))
        return x * cos + swapped * sin

    q = rope(pr[:, 0:nqk])
    k = rope(pr[:, nqk:2 * nqk] * (DK_SMALL ** -0.5))
    v_ref[0] = pr[:, 2 * nqk:2 * nqk + BRANCH_W].astype(BF16)
    sg_ref[0] = _silu(pr[:, 2 * nqk + BRANCH_W:])
    lg = _log_sigmoid(dl_ref[...])
    t_in_chunk = (lax.broadcasted_iota(jnp.int32, (TILE, nqk), 0) % CHUNK).astype(F32)
    for d in range(2):
        lg_d = lg[:, d * nqk:(d + 1) * nqk]
        steps = (t_in_chunk + 1.0) if d == 0 else (float(CHUNK) - t_in_chunk)
        b = steps * lg_d
        tot = float(CHUNK) * lg_d
        qin_ref[0, :, d * nqk:(d + 1) * nqk] = (q * jnp.exp(b)).astype(BF16)
        kin_ref[0, :, d * nqk:(d + 1) * nqk] = (k * jnp.exp(-b)).astype(BF16)
        kend_ref[0, :, d * nqk:(d + 1) * nqk] = (k * jnp.exp(tot - b)).astype(BF16)


def _proj_ret(tok, mods, g1, w, cos_t, sin_t, dl_row):
    B, T, _ = tok.shape
    nt = T // TILE
    nqk = HEADS * DK_SMALL
    tile = lambda width: pl.BlockSpec((1, TILE, width), lambda b, i: (b, i, 0))
    const = lambda shape: pl.BlockSpec(shape, lambda b, i: (0,) * len(shape))
    return pl.pallas_call(
        _proj_ret_kernel,
        out_shape=(jax.ShapeDtypeStruct((B, T, 2 * nqk), BF16),
                   jax.ShapeDtypeStruct((B, T, 2 * nqk), BF16),
                   jax.ShapeDtypeStruct((B, T, 2 * nqk), BF16),
                   jax.ShapeDtypeStruct((B, T, BRANCH_W), BF16),
                   jax.ShapeDtypeStruct((B, T, BRANCH_W), F32)),
        grid=(B, nt),
        in_specs=[tile(D_MODEL),
                  pl.BlockSpec((1, 6, D_MODEL), lambda b, i: (_mod_row_index(b, i), 0, 0)),
                  const((1, D_MODEL)), const((D_MODEL, W_RET)),
                  pl.BlockSpec((TILE, nqk), lambda b, i: (i, 0)),
                  pl.BlockSpec((TILE, nqk), lambda b, i: (i, 0)),
                  const((1, 2 * nqk))],
        out_specs=(tile(2 * nqk), tile(2 * nqk), tile(2 * nqk), tile(BRANCH_W), tile(BRANCH_W)),
        compiler_params=_cparams(2), name="proj_ret",
    )(tok, mods, g1, w, cos_t, sin_t, dl_row)


def _proj_ml_kernel(nt, x_ref, xp_ref, xn_ref, mod_ref, g_ref, w_ref, cw_ref, gb_ref,
                    q_ref, k_ref, kw_ref, v_ref, sz_ref, p0_ref, ub_ref, bb_ref, cs_ref, ext_ref):
    i = pl.program_id(1)
    nqk = HEADS * DK_ML
    x_all = jnp.concatenate([xp_ref[0], x_ref[0], xn_ref[0]], axis=0)
    h_all = _norm_mod(x_all, g_ref[...], mod_ref[0, 0:1, :], mod_ref[0, 1:2, :])
    ext = _dot(h_all.astype(BF16), w_ref[:, 0:2 * nqk])
    row = lax.broadcasted_iota(jnp.int32, (TILE + 2 * HALO, 1), 0)
    prev_ok = i >= 2
    next_ok = jnp.logical_and(i >= 1, i <= nt - 2)
    dead = jnp.logical_or(jnp.logical_and(row < HALO, jnp.logical_not(prev_ok)),
                          jnp.logical_and(row >= TILE + HALO, jnp.logical_not(next_ok)))
    ext_ref[...] = jnp.where(dead, 0.0, ext)
    cw = cw_ref[...]
    qk = (cw[0:1, :] * ext_ref[HALO - 1:HALO - 1 + TILE, :]
          + cw[1:2, :] * ext_ref[HALO:HALO + TILE, :]
          + cw[2:3, :] * ext_ref[HALO + 1:HALO + 1 + TILE, :])
    qk = _silu(qk)
    q_ref[0] = qk[:, 0:nqk].astype(BF16)
    k = qk[:, nqk:] * (DK_ML ** -0.5)
    k_ref[0] = k.astype(BF16)

    h = h_all[HALO:HALO + TILE, :].astype(BF16)
    pr = _dot(h, w_ref[:, 2 * nqk:])
    v_ref[0] = pr[:, 0:BRANCH_W].astype(BF16)
    sz_ref[0] = _silu(pr[:, BRANCH_W:2 * BRANCH_W])
    gates = pr[:, 2 * BRANCH_W:] + gb_ref[...]
    nhd = 2 * HEADS
    li = gates[:, 0:nhd]
    lf = _log_sigmoid(gates[:, nhd:])
    colid = lax.broadcasted_iota(jnp.int32, (TILE, nhd), 1)
    b = jnp.where(colid < HEADS, _dot_exact_lhs(_chunk_tri(True), lf),
                  _dot_exact_lhs(_chunk_tri(False), lf))
    tot = _chunk_total(lf)
    g = tot - b + li
    mloc = _chunk_max(g)
    w = jnp.exp(g - mloc)
    r_t = (li - b).T
    tot_t = tot.T
    mloc_t = mloc.T
    ti = lax.broadcasted_iota(jnp.int32, (CHUNK, CHUNK), 0)
    si = lax.broadcasted_iota(jnp.int32, (CHUNK, CHUNK), 1)
    for c in range(CHUNKS_PER_TILE):
        cs_ref[0, c, 0:nhd, :] = jnp.broadcast_to(tot_t[:, c * CHUNK:c * CHUNK + 1], (nhd, 128))
        cs_ref[0, c, nhd:2 * nhd, :] = jnp.broadcast_to(mloc_t[:, c * CHUNK:c * CHUNK + 1], (nhd, 128))
    for j in range(nhd):
        d, hd = j // HEADS, j % HEADS
        mask = (si <= ti) if d == 0 else (si >= ti)
        kw_ref[0, :, d * nqk + hd * DK_ML:d * nqk + (hd + 1) * DK_ML] = (
            k[:, hd * DK_ML:(hd + 1) * DK_ML] * w[:, j:j + 1]).astype(BF16)
        bb_ref[0, :, j * 128:(j + 1) * 128] = jnp.broadcast_to(b[:, j:j + 1], (TILE, 128))
        for c in range(CHUNKS_PER_TILE):
            rows = slice(c * CHUNK, (c + 1) * CHUNK)
            a = jnp.where(mask, jnp.broadcast_to(r_t[j:j + 1, rows], (CHUNK, CHUNK)), -jnp.inf)
            cmax = jnp.max(a, axis=-1, keepdims=True)
            p0_ref[0, j // 2, rows, (j % 2) * CHUNK:(j % 2 + 1) * CHUNK] = jnp.exp(a - cmax)
            ub_ref[0, rows, j * 128:(j + 1) * 128] = jnp.broadcast_to(cmax, (CHUNK, 128))


def _proj_ml(tok, mods, g1, w, conv_w, gate_bias):
    B, T, _ = tok.shape
    nt = T // TILE
    nqk = HEADS * DK_ML
    tile = lambda width: pl.BlockSpec((1, TILE, width), lambda b, i: (b, i, 0))
    const = lambda shape: pl.BlockSpec(shape, lambda b, i: (0,) * len(shape))
    per_halo = TILE // HALO
    n_halo = T // HALO
    return pl.pallas_call(
        functools.partial(_proj_ml_kernel, nt),
        out_shape=(jax.ShapeDtypeStruct((B, T, nqk), BF16),
                   jax.ShapeDtypeStruct((B, T, nqk), BF16),
                   jax.ShapeDtypeStruct((B, T, 2 * nqk), BF16),
                   jax.ShapeDtypeStruct((B, T, BRANCH_W), BF16),
                   jax.ShapeDtypeStruct((B, T, BRANCH_W), F32),
                   jax.ShapeDtypeStruct((B, HEADS, T, 2 * CHUNK), F32),
                   jax.ShapeDtypeStruct((B, T, 2 * HEADS * 128), F32),
                   jax.ShapeDtypeStruct((B, T, 2 * HEADS * 128), F32),
                   jax.ShapeDtypeStruct((B, T // CHUNK, 4 * HEADS, 128), F32)),
        grid=(B, nt),
        in_specs=[tile(D_MODEL),
                  pl.BlockSpec((1, HALO, D_MODEL),
                               lambda b, i: (b, jnp.maximum(i * per_halo - 1, 0), 0)),
                  pl.BlockSpec((1, HALO, D_MODEL),
                               lambda b, i: (b, jnp.minimum((i + 1) * per_halo, n_halo - 1), 0)),
                  pl.BlockSpec((1, 6, D_MODEL), lambda b, i: (_mod_row_index(b, i), 0, 0)),
                  const((1, D_MODEL)), const((D_MODEL, W_ML)),
                  const((3, 2 * nqk)), const((1, 4 * HEADS))],
        out_specs=(tile(nqk), tile(nqk), tile(2 * nqk), tile(BRANCH_W), tile(BRANCH_W),
                   pl.BlockSpec((1, HEADS, TILE, 2 * CHUNK), lambda b, i: (b, 0, i, 0)),
                   tile(2 * HEADS * 128), tile(2 * HEADS * 128),
                   pl.BlockSpec((1, CHUNKS_PER_TILE, 4 * HEADS, 128), lambda b, i: (b, i, 0, 0))),
        scratch_shapes=[pltpu.VMEM((TILE + 2 * HALO, 2 * nqk), F32)],
        compiler_params=_cparams(2), name="proj_ml",
    )(tok, tok, tok, mods, g1, w, conv_w, gate_bias)


PAIRS = HEADS // 2
PAIR_K = 2 * DK_SMALL
PAIR_V = 2 * DV


def _mix_kernel(*refs):
    (gq_f, gq_b, gk_f, gk_b, ge_f, ge_b, gv_f, gv_b, gd_f, gd_b,
     rq_f, rq_b, rk_f, rk_b, re_f, re_b, rv_f, rv_b, rdl,
     mq_f, mq_b, mk_f, mk_b, mw_f, mw_b, mv_f, mv_b, mp_f, mp_b, mu_f, mu_b, mb_f, mb_b, mc_f, mc_b,
     of_ref, ob_ref, sg_ref, sr_ref, sm_ref, m_ref) = refs
    j = pl.program_id(1)

    @pl.when(j == 0)
    def _():
        sg_ref[...] = jnp.zeros_like(sg_ref)
        sr_ref[...] = jnp.zeros_like(sr_ref)
        sm_ref[...] = jnp.zeros_like(sm_ref)
        m_ref[...] = jnp.full_like(m_ref, M_INIT)

    row_head = lax.broadcasted_iota(jnp.int32, (2 * CHUNK, PAIR_V), 0) // CHUNK
    diag_kv = row_head == lax.broadcasted_iota(jnp.int32, (2 * CHUNK, PAIR_V), 1) // DV
    diag_kk = (lax.broadcasted_iota(jnp.int32, (2 * CHUNK, PAIR_K), 0) // CHUNK
               == lax.broadcasted_iota(jnp.int32, (2 * CHUNK, PAIR_K), 1) // DK_SMALL)
    t_idx = lax.broadcasted_iota(jnp.int32, (CHUNK, 2 * CHUNK), 0)
    s_idx = lax.broadcasted_iota(jnp.int32, (CHUNK, 2 * CHUNK), 1) % CHUNK
    causal = (s_idx <= t_idx, s_idx >= t_idx)
    ones_v = jnp.ones((CHUNK, DV), BF16)
    zeros_vext = jnp.zeros((CHUNK, PAIR_V), BF16)
    ret_dec = jnp.exp(float(CHUNK) * _log_sigmoid(rdl[...]))

    decay_sets = (
        ((gq_f, gk_f, ge_f, gv_f), (gq_b, gk_b, ge_b, gv_b), sg_ref, 0),
        ((rq_f, rk_f, re_f, rv_f), (rq_b, rk_b, re_b, rv_b), sr_ref, 2 * BRANCH_W),
    )
    ml_sets = ((mq_f, mk_f, mw_f, mv_f, mp_f, mu_f, mb_f, mc_f),
               (mq_b, mk_b, mw_b, mv_b, mp_b, mu_b, mb_b, mc_b))
    outs = (of_ref, ob_ref)

    def chunk_step(step, carry):
        work = []
        for d in range(2):
            c = step if d == 0 else CHUNKS_PER_TILE - 1 - step
            r0 = pl.multiple_of(c * CHUNK, CHUNK)
            rows = pl.ds(r0, CHUNK)
            o_ref = outs[d]
            for bi, (fset, bset, s_ref, base) in enumerate(decay_sets):
                q_r, k_r, e_r, v_r = fset if d == 0 else bset
                for p in range(PAIRS):
                    ks = slice(p * PAIR_K, (p + 1) * PAIR_K)
                    q2 = q_r[0, rows, ks]
                    k2 = k_r[0, rows, ks]
                    v2 = v_r[0, rows, p * PAIR_V:(p + 1) * PAIR_V]
                    k_bd = jnp.where(diag_kk, jnp.concatenate([k2, k2], axis=0), 0.0)
                    if bi == 0:
                        dec = (gd_f if d == 0 else gd_b)[0, c, ks, :]
                    else:
                        dec = ret_dec[d, ks, :]
                    upd = jnp.where(diag_kv, _dot_tn(e_r[0, rows, ks], v2), 0.0)
                    work.append(dict(
                        kind="decay", d=d, p=p, rows=rows, o_ref=o_ref, s_ref=s_ref, base=base,
                        q2=q2, v2=v2, dec=jnp.concatenate([dec, dec], axis=1), upd=upd,
                        att=_dot_nt(q2, k_bd)))
            q_r, k_r, w_r, v_r, p_r, u_r, b_r, c_r = ml_sets[d]
            cs = c_r[0, c]
            for p in range(PAIRS):
                q2 = q_r[0, rows, p * 2 * DK_ML:(p + 1) * 2 * DK_ML]
                k2 = k_r[0, rows, p * 2 * DK_ML:(p + 1) * 2 * DK_ML]
                k_bd = jnp.where(diag_kv, jnp.concatenate([k2, k2], axis=0), 0.0)
                att = _dot_nt(q2, k_bd)
                heads = []
                for hl in range(2):
                    hd = 2 * p + hl
                    vext = jnp.concatenate([v_r[0, rows, hd * DV:(hd + 1) * DV], ones_v], axis=1)
                    st = sm_ref[d, hd]
                    heads.append(dict(
                        hd=hd, st=st, vext=vext,
                        inter=_dot(q2[:, hl * DK_ML:(hl + 1) * DK_ML], st.astype(BF16)),
                        upd=_dot_tn(w_r[0, rows, hd * DK_ML:(hd + 1) * DK_ML], vext),
                        u=u_r[0, rows, hd * 128:(hd + 1) * 128],
                        b=b_r[0, rows, hd * 128:(hd + 1) * 128]))
                work.append(dict(kind="ml", d=d, p=p, rows=rows, o_ref=o_ref, att=att, heads=heads,
                                 p0=p_r[0, p, rows, :], cs=cs))
        nhd = 2 * HEADS
        for w in work:
            d, p, rows, o_ref = w["d"], w["p"], w["rows"], w["o_ref"]
            if w["kind"] == "decay":
                s_ref = w["s_ref"]
                s_prev = s_ref[d, p]
                att = jnp.where(causal[d], w["att"], 0.0).astype(BF16)
                v_bd = jnp.where(diag_kv, jnp.concatenate([w["v2"], w["v2"]], axis=0), 0.0)
                lhs = jnp.concatenate([att, w["q2"]], axis=1)
                rhs = jnp.concatenate([v_bd, s_prev.astype(BF16)], axis=0)
                o_ref[0, rows, w["base"] + p * PAIR_V:w["base"] + (p + 1) * PAIR_V] = _dot(lhs, rhs)
                s_ref[d, p] = w["dec"] * s_prev + w["upd"]
            else:
                sc = (w["att"] * w["p0"]).astype(BF16)
                for hl, hw in enumerate(w["heads"]):
                    hd = hw["hd"]
                    jj = d * HEADS + hd
                    vz = (jnp.concatenate([hw["vext"], zeros_vext], axis=0) if hl == 0
                          else jnp.concatenate([zeros_vext, hw["vext"]], axis=0))
                    intra = _dot(sc, vz)
                    m_prev = m_ref[jj:jj + 1, :]
                    u = hw["u"]
                    mx = jnp.maximum(u, m_prev)
                    f = jnp.exp(u - mx)
                    e = jnp.exp(m_prev - mx)
                    num = f * intra[:, 0:DV] + e * hw["inter"][:, 0:DV]
                    den = f * intra[:, DV:] + e * hw["inter"][:, DV:]
                    den = jnp.maximum(jnp.abs(den), jnp.exp(-(hw["b"] + mx)))
                    o_ref[0, rows, BRANCH_W + hd * DV:BRANCH_W + (hd + 1) * DV] = num / den
                    tot = w["cs"][jj:jj + 1, :]
                    mloc = w["cs"][nhd + jj:nhd + jj + 1, :]
                    m_new = jnp.maximum(tot + m_prev, mloc)
                    keep = jnp.exp(tot + m_prev - m_new)
                    gain = jnp.exp(mloc - m_new)
                    sm_ref[d, hd] = (jnp.concatenate([keep, keep], axis=1) * hw["st"]
                                     + jnp.concatenate([gain, gain], axis=1) * hw["upd"])
                    m_ref[jj:jj + 1, :] = m_new
        return carry

    lax.fori_loop(0, CHUNKS_PER_TILE, chunk_step, 0)


def _mix(gla, ret, ml, ret_dl_col):
    gq, gk, ge, gv, gdec = gla
    rq, rk, re, rv = ret
    mq, mk, mw, mv, mp, mu, mb, mc = ml
    B, T, _ = gv.shape
    nt = T // TILE
    nqk_s = HEADS * DK_SMALL
    nqk_m = HEADS * DK_ML

    def tf(j):
        return j

    def tb(j):
        return jnp.where(j == 0, 0, nt - j)

    def pair(width, lane_block_b=0):
        return [pl.BlockSpec((1, TILE, width), lambda b, j: (b, tf(j), 0)),
                pl.BlockSpec((1, TILE, width), lambda b, j: (b, tb(j), lane_block_b))]

    dec_pair = [pl.BlockSpec((1, CHUNKS_PER_TILE, nqk_s, 128), lambda b, j: (b, tf(j), 0, 0)),
                pl.BlockSpec((1, CHUNKS_PER_TILE, nqk_s, 128), lambda b, j: (b, tb(j), 1, 0))]
    p0_pair = [pl.BlockSpec((1, PAIRS, TILE, 2 * CHUNK), lambda b, j: (b, 0, tf(j), 0)),
               pl.BlockSpec((1, PAIRS, TILE, 2 * CHUNK), lambda b, j: (b, 1, tb(j), 0))]
    cs_pair = [pl.BlockSpec((1, CHUNKS_PER_TILE, 4 * HEADS, 128), lambda b, j: (b, tf(j), 0, 0)),
               pl.BlockSpec((1, CHUNKS_PER_TILE, 4 * HEADS, 128), lambda b, j: (b, tb(j), 0, 0))]
    decay_specs = pair(nqk_s, 1) + pair(nqk_s, 1) + pair(nqk_s, 1) + pair(BRANCH_W)
    in_specs = (decay_specs + dec_pair + decay_specs
                + [pl.BlockSpec((2, nqk_s, 128), lambda b, j: (0, 0, 0))]
                + pair(nqk_m) + pair(nqk_m) + pair(nqk_m, 1) + pair(BRANCH_W) + p0_pair
                + pair(HEADS * 128, 1) + pair(HEADS * 128, 1) + cs_pair)
    args = (gq, gq, gk, gk, ge, ge, gv, gv, gdec, gdec,
            rq, rq, rk, rk, re, re, rv, rv, ret_dl_col,
            mq, mq, mk, mk, mw, mw, mv, mv, mp, mp, mu, mu, mb, mb, mc, mc)
    return pl.pallas_call(
        _mix_kernel,
        out_shape=(jax.ShapeDtypeStruct((B, T, 3 * BRANCH_W), F32),) * 2,
        grid=(B, nt),
        in_specs=in_specs,
        out_specs=(pl.BlockSpec((1, TILE, 3 * BRANCH_W), lambda b, j: (b, tf(j), 0)),
                   pl.BlockSpec((1, TILE, 3 * BRANCH_W), lambda b, j: (b, tb(j), 0))),
        scratch_shapes=[pltpu.VMEM((2, PAIRS, PAIR_K, PAIR_V), F32),
                        pltpu.VMEM((2, PAIRS, PAIR_K, PAIR_V), F32),
                        pltpu.VMEM((2, HEADS, DK_ML, 2 * DV), F32),
                        pltpu.VMEM((2 * HEADS, 128), F32)],
        compiler_params=_cparams(2), name="mix",
    )(*args)


def _merge_kernel(x_ref, mod_ref, g_ref, wg_ref, of_ref, ob_ref, sgg_ref, sgm_ref, sgr_ref,
                  ng_ref, wb_ref, wo_ref, out_ref):
    x = x_ref[0]
    h = _norm_mod(x, g_ref[...], mod_ref[0, 0:1, :], mod_ref[0, 1:2, :]).astype(BF16)
    o = of_ref[0] + ob_ref[0]
    side = (sgg_ref, sgm_ref, sgr_ref)
    acc = jnp.zeros((TILE, D_MODEL), F32)
    for br in range(3):
        parts = []
        for hd in range(HEADS):
            oh = o[:, br * BRANCH_W + hd * DV:br * BRANCH_W + (hd + 1) * DV]
            if br > 0:
                oh = oh - jnp.mean(oh, axis=-1, keepdims=True)
            parts.append(oh * lax.rsqrt(jnp.mean(oh * oh, axis=-1, keepdims=True) + EPS))
        y = jnp.concatenate(parts, axis=1) * ng_ref[:, br * BRANCH_W:(br + 1) * BRANCH_W]
        y = (y * side[br][0]).astype(BF16)
        gate = _sigmoid(_dot(h, wg_ref[:, br * D_MODEL:(br + 1) * D_MODEL]))
        acc = acc + gate * _dot(y, wb_ref[br])
    out = _dot(acc.astype(BF16), wo_ref[...])
    out_ref[0] = x + mod_ref[0, 2:3, :] * out


def _merge(tok, mods, g1, wgate, o_f, o_b, sg_gla, sz_ml, sg_ret, norm_g, wb, wo, skip):
    B, T, _ = tok.shape
    nt = T // TILE - skip
    tile = lambda width: pl.BlockSpec((1, TILE, width), lambda b, i: (b, i + skip, 0))
    const = lambda shape: pl.BlockSpec(shape, lambda b, i: (0,) * len(shape))
    return pl.pallas_call(
        _merge_kernel,
        out_shape=jax.ShapeDtypeStruct((B, T, D_MODEL), F32),
        grid=(B, nt),
        in_specs=[tile(D_MODEL),
                  pl.BlockSpec((1, 6, D_MODEL), lambda b, i: (_mod_row_index(b, i + skip), 0, 0)),
                  const((1, D_MODEL)), const((D_MODEL, W_GATE)),
                  tile(3 * BRANCH_W), tile(3 * BRANCH_W),
                  tile(BRANCH_W), tile(BRANCH_W), tile(BRANCH_W),
                  const((1, 3 * BRANCH_W)), const((3, BRANCH_W, D_MODEL)), const((D_MODEL, D_MODEL))],
        out_specs=tile(D_MODEL),
        compiler_params=_cparams(2), name="merge",
    )(tok, mods, g1, wgate, o_f, o_b, sg_gla, sz_ml, sg_ret, norm_g, wb, wo)


def _mlp_kernel(final, x_ref, mod_ref, g_ref, wu_ref, wd_ref, fg_ref, out_ref):
    x = x_ref[0]
    h = _norm_mod(x, g_ref[...], mod_ref[0, 3:4, :], mod_ref[0, 4:5, :]).astype(BF16)
    u = jnp.maximum(_dot(h, wu_ref[...]), 0.0)
    y = x + mod_ref[0, 5:6, :] * _dot((u * u).astype(BF16), wd_ref[...])
    if final:
        y = y * lax.rsqrt(jnp.mean(y * y, axis=-1, keepdims=True) + EPS) * fg_ref[...]
    out_ref[0] = y


def _mlp(tok, mods, g2, wu, wd, final_g, skip, final):
    B, T, _ = tok.shape
    nt = T // TILE - skip
    out_t = T - skip * TILE if final else T
    out_off = 0 if final else skip
    const = lambda shape: pl.BlockSpec(shape, lambda b, i: (0,) * len(shape))
    return pl.pallas_call(
        functools.partial(_mlp_kernel, final),
        out_shape=jax.ShapeDtypeStruct((B, out_t, D_MODEL), F32),
        grid=(B, nt),
        in_specs=[pl.BlockSpec((1, TILE, D_MODEL), lambda b, i: (b, i + skip, 0)),
                  pl.BlockSpec((1, 6, D_MODEL), lambda b, i: (_mod_row_index(b, i + skip), 0, 0)),
                  const((1, D_MODEL)), const((D_MODEL, D_FF)), const((D_FF, D_MODEL)),
                  const((1, D_MODEL))],
        out_specs=pl.BlockSpec((1, TILE, D_MODEL), lambda b, i: (b, i + out_off, 0)),
        compiler_params=_cparams(2), name="mlp",
    )(tok, mods, g2, wu, wd, final_g)


def kernel(x, c, ctx, c_ctx, norm1_g, norm2_g, w_ada, b_ada, w_in, gla_a2, gla_a_bias, gla_norm_g,
           ml_conv, ml_i_bias, ml_f_bias, ml_norm_g, ret_decay_logit, ret_norm_g, w_branch, w_o,
           w_up, w_down, final_g):
    B, seq, _ = x.shape
    ctx_len = ctx.shape[1]
    assert B == 2 and ctx_len == TILE and seq % TILE == 0 and seq % GRID_W == 0
    tok = jnp.concatenate([ctx, x], axis=1)

    c_rows = jnp.concatenate([c, c_ctx[None, :], jnp.zeros((8 - B - 1, D_MODEL), F32)], axis=0)
    mods_all = _ada(c_rows, w_ada, b_ada)[:, 0:3].reshape(DEPTH, 3, 6, D_MODEL)

    cos64, sin64 = _rope_tables(seq)
    cos_t = jnp.concatenate([jnp.ones((ctx_len, HEADS * DK_SMALL), F32), jnp.tile(cos64, (1, HEADS))], axis=0)
    sin_t = jnp.concatenate([jnp.zeros((ctx_len, HEADS * DK_SMALL), F32), jnp.tile(sin64, (1, HEADS))], axis=0)

    o0 = W_GLA
    o1 = o0 + W_ML
    o2 = o1 + W_RET
    nqk_s = HEADS * DK_SMALL
    for l in range(DEPTH):
        last = l == DEPTH - 1
        skip = 1 if last else 0
        mods = mods_all[l]
        g1 = norm1_g[l][None, :]
        g2 = norm2_g[l][None, :]
        w_l = w_in[l]
        w_gla = w_l[:, 0:o0].astype(BF16)
        w_ml = w_l[:, o0:o1].astype(BF16)
        w_ret = w_l[:, o1:o2].astype(BF16)
        w_gate = w_l[:, o2:].astype(BF16)
        zeros = jnp.zeros((GLA_RANK, nqk_s), F32)
        a2blk = jnp.concatenate([jnp.concatenate([gla_a2[l, 0], zeros], axis=1),
                                 jnp.concatenate([zeros, gla_a2[l, 1]], axis=1)], axis=0)
        abias = gla_a_bias[l].reshape(1, 2 * nqk_s)
        gate_bias = jnp.concatenate([ml_i_bias[l].reshape(1, 2 * HEADS),
                                     ml_f_bias[l].reshape(1, 2 * HEADS)], axis=1)
        dl = ret_decay_logit[l]
        dl_row = jnp.repeat(dl, DK_SMALL, axis=1).reshape(1, 2 * nqk_s)
        dl_col = jnp.broadcast_to(jnp.repeat(dl, DK_SMALL, axis=1)[:, :, None], (2, nqk_s, 128))

        gla = _proj_gla(tok, mods, g1, w_gla, a2blk, abias)
        ml = _proj_ml(tok, mods, g1, w_ml, ml_conv[l], gate_bias)
        ret = _proj_ret(tok, mods, g1, w_ret, cos_t, sin_t, dl_row)
        gq, gk, ge, gv, sg_gla, gdec = gla
        mq, mk, mw, mv, sz_ml, mp, mu, mb, mc = ml
        rq, rk, re, rv, sg_ret = ret
        o_f, o_b = _mix((gq, gk, ge, gv, gdec), (rq, rk, re, rv), (mq, mk, mw, mv, mp, mu, mb, mc), dl_col)
        norm_g = jnp.concatenate([gla_norm_g[l], ml_norm_g[l], ret_norm_g[l]])[None, :]
        tok = _merge(tok, mods, g1, w_gate, o_f, o_b, sg_gla, sz_ml, sg_ret, norm_g,
                     w_branch[l].astype(BF16), w_o[l].astype(BF16), skip)
        tok = _mlp(tok, mods, g2, w_up[l].astype(BF16), w_down[l].astype(BF16),
                   final_g[None, :], skip, last)
    return tok
```

```python
import functools

import jax
import jax.numpy as jnp
import numpy as np
from jax import lax
from jax.experimental import pallas as pl
from jax.experimental.pallas import tpu as pltpu

F32 = jnp.float32
BF16 = jnp.bfloat16

D_MODEL = 1024
DEPTH = 2
GRID_W = 64
CHUNK = 64
EPS = 1e-6
M_INIT = -1e30
HEADS = 4
DK_SMALL = 64
DK_ML = 128
DV = 128
BRANCH_W = HEADS * DV
GLA_RANK = 16
GLA_TAU = 16.0
ROPE_BASE = 10000.0
D_FF = 4 * D_MODEL

TILE = 256
CHUNKS_PER_TILE = TILE // CHUNK
HALO = 8
VMEM_LIMIT = 56 * 1024 * 1024

W_GLA = 2 * HEADS * DK_SMALL + 2 * BRANCH_W + 2 * GLA_RANK
W_ML = 2 * HEADS * DK_ML + 2 * BRANCH_W + 4 * HEADS
W_RET = 2 * HEADS * DK_SMALL + 2 * BRANCH_W
W_GATE = 3 * D_MODEL


def _cparams(n_axes):
    return pltpu.CompilerParams(dimension_semantics=("arbitrary",) * n_axes,
                                vmem_limit_bytes=VMEM_LIMIT)


def _log_sigmoid(z):
    return jnp.minimum(z, 0.0) - jnp.log(1.0 + jnp.exp(-jnp.abs(z)))


def _sigmoid(z):
    return 1.0 / (1.0 + jnp.exp(-z))


def _silu(z):
    return z * _sigmoid(z)


def _norm_mod(x, g, shift, scale):
    y = x * lax.rsqrt(jnp.mean(x * x, axis=-1, keepdims=True) + EPS) * g
    return y * (1.0 + scale) + shift


def _dot(a, b):
    return jnp.dot(a, b, preferred_element_type=F32)


def _dot_nt(a, b):
    return lax.dot_general(a, b, (((1,), (1,)), ((), ())), preferred_element_type=F32)


def _dot_tn(a, b):
    return lax.dot_general(a, b, (((0,), (0,)), ((), ())), preferred_element_type=F32)


def _dot_exact_lhs(a_bf16, x):
    hi = x.astype(BF16)
    r1 = x - hi.astype(F32)
    mid = r1.astype(BF16)
    lo = (r1 - mid.astype(F32)).astype(BF16)
    return _dot(a_bf16, hi) + _dot(a_bf16, mid) + _dot(a_bf16, lo)


def _chunk_tri(forward):
    t = lax.broadcasted_iota(jnp.int32, (TILE, TILE), 0)
    s = lax.broadcasted_iota(jnp.int32, (TILE, TILE), 1)
    same = (t // CHUNK) == (s // CHUNK)
    tri = (s <= t) if forward else (s >= t)
    return jnp.where(same & tri, 1.0, 0.0).astype(BF16)


def _chunk_total(x):
    w = x.shape[-1]
    tot = jnp.sum(x.reshape(CHUNKS_PER_TILE, CHUNK, w), axis=1, keepdims=True)
    return jnp.broadcast_to(tot, (CHUNKS_PER_TILE, CHUNK, w)).reshape(TILE, w)


def _chunk_max(x):
    w = x.shape[-1]
    mx = jnp.max(x.reshape(CHUNKS_PER_TILE, CHUNK, w), axis=1, keepdims=True)
    return jnp.broadcast_to(mx, (CHUNKS_PER_TILE, CHUNK, w)).reshape(TILE, w)


def _mod_row_index(b, i):
    return jnp.where(i == 0, 2, b)


def _ada_kernel(c_ref, w_ref, b_ref, o_ref):
    s = _silu(c_ref[...])
    o_ref[0] = jnp.dot(s, w_ref[0], precision=lax.Precision.HIGHEST,
                       preferred_element_type=F32) + b_ref[0]


def _ada(c_rows, w_ada, b_ada):
    tn = 1536
    return pl.pallas_call(
        _ada_kernel,
        out_shape=jax.ShapeDtypeStruct((DEPTH, 8, 6 * D_MODEL), F32),
        grid=(DEPTH, 6 * D_MODEL // tn),
        in_specs=[pl.BlockSpec((8, D_MODEL), lambda l, n: (0, 0)),
                  pl.BlockSpec((1, D_MODEL, tn), lambda l, n: (l, 0, n)),
                  pl.BlockSpec((1, 1, tn), lambda l, n: (l, 0, n))],
        out_specs=pl.BlockSpec((1, 8, tn), lambda l, n: (l, 0, n)),
        compiler_params=_cparams(2), name="ada",
    )(c_rows, w_ada, b_ada.reshape(DEPTH, 1, 6 * D_MODEL))


def _rope_kernel(ang_ref, sign_ref, cos_ref, sin_ref):
    a = ang_ref[...]
    cos_ref[...] = jnp.cos(a)
    sin_ref[...] = jnp.sin(a) * sign_ref[...]


def _rope_tables(seq):
    rows = seq // GRID_W
    row = jnp.repeat(jnp.arange(rows, dtype=F32), GRID_W)
    col = jnp.tile(jnp.arange(GRID_W, dtype=F32), rows)
    nf = DK_SMALL // 4
    inv = ROPE_BASE ** (-jnp.arange(nf, dtype=F32) / nf)
    ang = jnp.concatenate([row[:, None] * inv, col[:, None] * inv], axis=-1)
    ang = jnp.concatenate([ang, ang], axis=-1)
    sign = jnp.concatenate([-jnp.ones((1, 32), F32), jnp.ones((1, 32), F32)], axis=-1)
    tt = 1024
    cos, sin = pl.pallas_call(
        _rope_kernel,
        out_shape=(jax.ShapeDtypeStruct((seq, DK_SMALL), F32),) * 2,
        grid=(seq // tt,),
        in_specs=[pl.BlockSpec((tt, DK_SMALL), lambda i: (i, 0)),
                  pl.BlockSpec((1, DK_SMALL), lambda i: (0, 0))],
        out_specs=(pl.BlockSpec((tt, DK_SMALL), lambda i: (i, 0)),) * 2,
        compiler_params=_cparams(1), name="rope_tables",
    )(ang, sign)
    return cos, sin


def _proj_gla_kernel(x_ref, mod_ref, g_ref, w_ref, a2_ref, ab_ref,
                     qin_ref, kin_ref, kend_ref, v_ref, sg_ref, decb_ref):
    h = _norm_mod(x_ref[0], g_ref[...], mod_ref[0, 0:1, :], mod_ref[0, 1:2, :]).astype(BF16)
    pr = _dot(h, w_ref[...])
    nqk = HEADS * DK_SMALL
    q = pr[:, 0:nqk] * (DK_SMALL ** -0.5)
    k = pr[:, nqk:2 * nqk]
    v_ref[0] = pr[:, 2 * nqk:2 * nqk + BRANCH_W].astype(BF16)
    sg_ref[0] = _silu(pr[:, 2 * nqk + BRANCH_W:2 * nqk + 2 * BRANCH_W])
    a = pr[:, 2 * nqk + 2 * BRANCH_W:]
    z = jnp.dot(a, a2_ref[...], precision=lax.Precision.HIGHEST,
                preferred_element_type=F32) + ab_ref[...]
    la = _log_sigmoid(z) * (1.0 / GLA_TAU)
    half = jnp.sum(la.reshape(2 * CHUNKS_PER_TILE, CHUNK // 2, 2 * nqk), axis=1)
    half_t = half.T
    for d in range(2):
        la_d = la[:, d * nqk:(d + 1) * nqk]
        b = _dot_exact_lhs(_chunk_tri(d == 0), la_d)
        tot = _chunk_total(la_d)
        qin_ref[0, :, d * nqk:(d + 1) * nqk] = (q * jnp.exp(b)).astype(BF16)
        kin_ref[0, :, d * nqk:(d + 1) * nqk] = (k * jnp.exp(-b)).astype(BF16)
        kend_ref[0, :, d * nqk:(d + 1) * nqk] = (k * jnp.exp(tot - b)).astype(BF16)
        ht = half_t[d * nqk:(d + 1) * nqk, :]
        for c in range(CHUNKS_PER_TILE):
            tot_col = ht[:, 2 * c:2 * c + 1] + ht[:, 2 * c + 1:2 * c + 2]
            decb_ref[0, c, d * nqk:(d + 1) * nqk, :] = jnp.broadcast_to(jnp.exp(tot_col), (nqk, 128))


def _proj_gla(tok, mods, g1, w, a2blk, abias):
    B, T, _ = tok.shape
    nt = T // TILE
    nqk = HEADS * DK_SMALL
    tile = lambda width: pl.BlockSpec((1, TILE, width), lambda b, i: (b, i, 0))
    const = lambda shape: pl.BlockSpec(shape, lambda b, i: (0,) * len(shape))
    return pl.pallas_call(
        _proj_gla_kernel,
        out_shape=(jax.ShapeDtypeStruct((B, T, 2 * nqk), BF16),
                   jax.ShapeDtypeStruct((B, T, 2 * nqk), BF16),
                   jax.ShapeDtypeStruct((B, T, 2 * nqk), BF16),
                   jax.ShapeDtypeStruct((B, T, BRANCH_W), BF16),
                   jax.ShapeDtypeStruct((B, T, BRANCH_W), F32),
                   jax.ShapeDtypeStruct((B, T // CHUNK, 2 * nqk, 128), F32)),
        grid=(B, nt),
        in_specs=[tile(D_MODEL),
                  pl.BlockSpec((1, 6, D_MODEL), lambda b, i: (_mod_row_index(b, i), 0, 0)),
                  const((1, D_MODEL)), const((D_MODEL, W_GLA)),
                  const((2 * GLA_RANK, 2 * nqk)), const((1, 2 * nqk))],
        out_specs=(tile(2 * nqk), tile(2 * nqk), tile(2 * nqk), tile(BRANCH_W), tile(BRANCH_W),
                   pl.BlockSpec((1, CHUNKS_PER_TILE, 2 * nqk, 128), lambda b, i: (b, i, 0, 0))),
        compiler_params=_cparams(2), name="proj_gla",
    )(tok, mods, g1, w, a2blk, abias)


def _proj_ret_kernel(x_ref, mod_ref, g_ref, w_ref, cos_ref, sin_ref, dl_ref,
                     qin_ref, kin_ref, kend_ref, v_ref, sg_ref):
    h = _norm_mod(x_ref[0], g_ref[...], mod_ref[0, 0:1, :], mod_ref[0, 1:2, :]).astype(BF16)
    pr = _dot(h, w_ref[...])
    nqk = HEADS * DK_SMALL
    cos = cos_ref[...]
    sin = sin_ref[...]
    lane = lax.broadcasted_iota(jnp.int32, (TILE, nqk), 1)
    first_half = (lane % DK_SMALL) < (DK_SMALL // 2)

    def rope(x):
        swapped = jnp.where(first_half, pltpu.roll(x, nqk - DK_SMALL // 2, 1),
                            pltpu.roll(x, DK_SMALL // 2, 1))
        return x * cos + swapped * sin

    q = rope(pr[:, 0:nqk])
    k = rope(pr[:, nqk:2 * nqk] * (DK_SMALL ** -0.5))
    v_ref[0] = pr[:, 2 * nqk:2 * nqk + BRANCH_W].astype(BF16)
    sg_ref[0] = _silu(pr[:, 2 * nqk + BRANCH_W:])
    lg = _log_sigmoid(dl_ref[...])
    t_in_chunk = (lax.broadcasted_iota(jnp.int32, (TILE, nqk), 0) % CHUNK).astype(F32)
    for d in range(2):
        lg_d = lg[:, d * nqk:(d + 1) * nqk]
        steps = (t_in_chunk + 1.0) if d == 0 else (float(CHUNK) - t_in_chunk)
        b = steps * lg_d
        tot = float(CHUNK) * lg_d
        qin_ref[0, :, d * nqk:(d + 1) * nqk] = (q * jnp.exp(b)).astype(BF16)
        kin_ref[0, :, d * nqk:(d + 1) * nqk] = (k * jnp.exp(-b)).astype(BF16)
        kend_ref[0, :, d * nqk:(d + 1) * nqk] = (k * jnp.exp(tot - b)).astype(BF16)


def _proj_ret(tok, mods, g1, w, cos_t, sin_t, dl_row):
    B, T, _ = tok.shape
    nt = T // TILE
    nqk = HEADS * DK_SMALL
    tile = lambda width: pl.BlockSpec((1, TILE, width), lambda b, i: (b, i, 0))
    const = lambda shape: pl.BlockSpec(shape, lambda b, i: (0,) * len(shape))
    return pl.pallas_call(
        _proj_ret_kernel,
        out_shape=(jax.ShapeDtypeStruct((B, T, 2 * nqk), BF16),
                   jax.ShapeDtypeStruct((B, T, 2 * nqk), BF16),
                   jax.ShapeDtypeStruct((B, T, 2 * nqk), BF16),
                   jax.ShapeDtypeStruct((B, T, BRANCH_W), BF16),
                   jax.ShapeDtypeStruct((B, T, BRANCH_W), F32)),
        grid=(B, nt),
        in_specs=[tile(D_MODEL),
                  pl.BlockSpec((1, 6, D_MODEL), lambda b, i: (_mod_row_index(b, i), 0, 0)),
                  const((1, D_MODEL)), const((D_MODEL, W_RET)),
                  pl.BlockSpec((TILE, nqk), lambda b, i: (i, 0)),
                  pl.BlockSpec((TILE, nqk), lambda b, i: (i, 0)),
                  const((1, 2 * nqk))],
        out_specs=(tile(2 * nqk), tile(2 * nqk), tile(2 * nqk), tile(BRANCH_W), tile(BRANCH_W)),
        compiler_params=_cparams(2), name="proj_ret",
    )(tok, mods, g1, w, cos_t, sin_t, dl_row)


def _proj_ml_kernel(nt, x_ref, xp_ref, xn_ref, mod_ref, g_ref, w_ref, cw_ref, gb_ref,
                    q_ref, k_ref, kw_ref, v_ref, sz_ref, p0_ref, ub_ref, bb_ref, cs_ref, ext_ref):
    i = pl.program_id(1)
    nqk = HEADS * DK_ML
    x_all = jnp.concatenate([xp_ref[0], x_ref[0], xn_ref[0]], axis=0)
    h_all = _norm_mod(x_all, g_ref[...], mod_ref[0, 0:1, :], mod_ref[0, 1:2, :])
    ext = _dot(h_all.astype(BF16), w_ref[:, 0:2 * nqk])
    row = lax.broadcasted_iota(jnp.int32, (TILE + 2 * HALO, 1), 0)
    prev_ok = i >= 2
    next_ok = jnp.logical_and(i >= 1, i <= nt - 2)
    dead = jnp.logical_or(jnp.logical_and(row < HALO, jnp.logical_not(prev_ok)),
                          jnp.logical_and(row >= TILE + HALO, jnp.logical_not(next_ok)))
    ext_ref[...] = jnp.where(dead, 0.0, ext)
    cw = cw_ref[...]
    qk = (cw[0:1, :] * ext_ref[HALO - 1:HALO - 1 + TILE, :]
          + cw[1:2, :] * ext_ref[HALO:HALO + TILE, :]
          + cw[2:3, :] * ext_ref[HALO + 1:HALO + 1 + TILE, :])
    qk = _silu(qk)
    q_ref[0] = qk[:, 0:nqk].astype(BF16)
    k = qk[:, nqk:] * (DK_ML ** -0.5)
    k_ref[0] = k.astype(BF16)

    h = h_all[HALO:HALO + TILE, :].astype(BF16)
    pr = _dot(h, w_ref[:, 2 * nqk:])
    v_ref[0] = pr[:, 0:BRANCH_W].astype(BF16)
    sz_ref[0] = _silu(pr[:, BRANCH_W:2 * BRANCH_W])
    gates = pr[:, 2 * BRANCH_W:] + gb_ref[...]
    nhd = 2 * HEADS
    li = gates[:, 0:nhd]
    lf = _log_sigmoid(gates[:, nhd:])
    colid = lax.broadcasted_iota(jnp.int32, (TILE, nhd), 1)
    b = jnp.where(colid < HEADS, _dot_exact_lhs(_chunk_tri(True), lf),
                  _dot_exact_lhs(_chunk_tri(False), lf))
    tot = _chunk_total(lf)
    g = tot - b + li
    mloc = _chunk_max(g)
    w = jnp.exp(g - mloc)
    r_t = (li - b).T
    tot_t = tot.T
    mloc_t = mloc.T
    ti = lax.broadcasted_iota(jnp.int32, (CHUNK, CHUNK), 0)
    si = lax.broadcasted_iota(jnp.int32, (CHUNK, CHUNK), 1)
    for c in range(CHUNKS_PER_TILE):
        cs_ref[0, c, 0:nhd, :] = jnp.broadcast_to(tot_t[:, c * CHUNK:c * CHUNK + 1], (nhd, 128))
        cs_ref[0, c, nhd:2 * nhd, :] = jnp.broadcast_to(mloc_t[:, c * CHUNK:c * CHUNK + 1], (nhd, 128))
    for j in range(nhd):
        d, hd = j // HEADS, j % HEADS
        mask = (si <= ti) if d == 0 else (si >= ti)
        kw_ref[0, :, d * nqk + hd * DK_ML:d * nqk + (hd + 1) * DK_ML] = (
            k[:, hd * DK_ML:(hd + 1) * DK_ML] * w[:, j:j + 1]).astype(BF16)
        bb_ref[0, :, j * 128:(j + 1) * 128] = jnp.broadcast_to(b[:, j:j + 1], (TILE, 128))
        for c in range(CHUNKS_PER_TILE):
            rows = slice(c * CHUNK, (c + 1) * CHUNK)
            a = jnp.where(mask, jnp.broadcast_to(r_t[j:j + 1, rows], (CHUNK, CHUNK)), -jnp.inf)
            cmax = jnp.max(a, axis=-1, keepdims=True)
            p0_ref[0, j // 2, rows, (j % 2) * CHUNK:(j % 2 + 1) * CHUNK] = jnp.exp(a - cmax)
            ub_ref[0, rows, j * 128:(j + 1) * 128] = jnp.broadcast_to(cmax, (CHUNK, 128))


def _proj_ml(tok, mods, g1, w, conv_w, gate_bias):
    B, T, _ = tok.shape
    nt = T // TILE
    nqk = HEADS * DK_ML
    tile = lambda width: pl.BlockSpec((1, TILE, width), lambda b, i: (b, i, 0))
    const = lambda shape: pl.BlockSpec(shape, lambda b, i: (0,) * len(shape))
    per_halo = TILE // HALO
    n_halo = T // HALO
    return pl.pallas_call(
        functools.partial(_proj_ml_kernel, nt),
        out_shape=(jax.ShapeDtypeStruct((B, T, nqk), BF16),
                   jax.ShapeDtypeStruct((B, T, nqk), BF16),
                   jax.ShapeDtypeStruct((B, T, 2 * nqk), BF16),
                   jax.ShapeDtypeStruct((B, T, BRANCH_W), BF16),
                   jax.ShapeDtypeStruct((B, T, BRANCH_W), F32),
                   jax.ShapeDtypeStruct((B, HEADS, T, 2 * CHUNK), F32),
                   jax.ShapeDtypeStruct((B, T, 2 * HEADS * 128), F32),
                   jax.ShapeDtypeStruct((B, T, 2 * HEADS * 128), F32),
                   jax.ShapeDtypeStruct((B, T // CHUNK, 4 * HEADS, 128), F32)),
        grid=(B, nt),
        in_specs=[tile(D_MODEL),
                  pl.BlockSpec((1, HALO, D_MODEL),
                               lambda b, i: (b, jnp.maximum(i * per_halo - 1, 0), 0)),
                  pl.BlockSpec((1, HALO, D_MODEL),
                               lambda b, i: (b, jnp.minimum((i + 1) * per_halo, n_halo - 1), 0)),
                  pl.BlockSpec((1, 6, D_MODEL), lambda b, i: (_mod_row_index(b, i), 0, 0)),
                  const((1, D_MODEL)), const((D_MODEL, W_ML)),
                  const((3, 2 * nqk)), const((1, 4 * HEADS))],
        out_specs=(tile(nqk), tile(nqk), tile(2 * nqk), tile(BRANCH_W), tile(BRANCH_W),
                   pl.BlockSpec((1, HEADS, TILE, 2 * CHUNK), lambda b, i: (b, 0, i, 0)),
                   tile(2 * HEADS * 128), tile(2 * HEADS * 128),
                   pl.BlockSpec((1, CHUNKS_PER_TILE, 4 * HEADS, 128), lambda b, i: (b, i, 0, 0))),
        scratch_shapes=[pltpu.VMEM((TILE + 2 * HALO, 2 * nqk), F32)],
        compiler_params=_cparams(2), name="proj_ml",
    )(tok, tok, tok, mods, g1, w, conv_w, gate_bias)


PC_MLQK = 0
PC_REST = 2 * HEADS * DK_ML
PR_GLA_Q = 0
PR_GLA_K = PR_GLA_Q + HEADS * DK_SMALL
PR_GLA_V = PR_GLA_K + HEADS * DK_SMALL
PR_GLA_G = PR_GLA_V + BRANCH_W
PR_ML_V = PR_GLA_G + BRANCH_W
PR_ML_Z = PR_ML_V + BRANCH_W
PR_RET_Q = PR_ML_Z + BRANCH_W
PR_RET_K = PR_RET_Q + HEADS * DK_SMALL
PR_RET_V = PR_RET_K + HEADS * DK_SMALL
PR_RET_G = PR_RET_V + BRANCH_W
PR_GLA_A = PR_RET_G + BRANCH_W
PR_ML_GATES = PR_GLA_A + 2 * GLA_RANK
PR_WIDTH = PR_ML_GATES + 4 * HEADS
W_PROJ = PC_REST + PR_WIDTH


def _split2(x):
    hi = x.astype(BF16)
    return hi, (x - hi.astype(F32)).astype(BF16)


def _proj_kernel(nt, batch, x_ref, xp_ref, xn_ref, mod_ref, g_ref, w_ref, a2_ref, ab_ref, cw_ref, gb_ref,
                 cos_ref, sin_ref, dl_ref,
                 gq_ref, gk_ref, ge_ref, gv_ref, gsg_ref, gdec_ref,
                 mq_ref, mk_ref, mkw_ref, mv_ref, msz_ref, mp0_ref, mub_ref, mbb_ref, mcs_ref,
                 rq_ref, rk_ref, re_ref, rv_ref, rsg_ref, ext_ref):
    i = pl.program_id(0)
    nqk_s = HEADS * DK_SMALL
    nqk_m = HEADS * DK_ML
    nhd = 2 * HEADS
    tri_ge = _chunk_tri(True)
    tri_le = _chunk_tri(False)
    row8 = lax.broadcasted_iota(jnp.int32, (nhd, TILE), 0)
    lane_chunk = lax.broadcasted_iota(jnp.int32, (nhd, TILE), 1) // CHUNK
    ti = lax.broadcasted_iota(jnp.int32, (CHUNK, CHUNK), 0)
    si = lax.broadcasted_iota(jnp.int32, (CHUNK, CHUNK), 1)
    lane = lax.broadcasted_iota(jnp.int32, (TILE, nqk_s), 1)
    first_half = (lane % DK_SMALL) < (DK_SMALL // 2)
    t_in_chunk = (lax.broadcasted_iota(jnp.int32, (TILE, nqk_s), 0) % CHUNK).astype(F32)
    ext_row = lax.broadcasted_iota(jnp.int32, (TILE + 2 * HALO, 1), 0)
    prev_ok = i >= 2
    next_ok = jnp.logical_and(i >= 1, i <= nt - 2)
    dead = jnp.logical_or(jnp.logical_and(ext_row < HALO, jnp.logical_not(prev_ok)),
                          jnp.logical_and(ext_row >= TILE + HALO, jnp.logical_not(next_ok)))
    cos = cos_ref[...]
    sin = sin_ref[...]
    lg = _log_sigmoid(dl_ref[...])
    cw = cw_ref[...]

    def rope(x):
        swapped = jnp.where(first_half, pltpu.roll(x, nqk_s - DK_SMALL // 2, 1),
                            pltpu.roll(x, DK_SMALL // 2, 1))
        return x * cos + swapped * sin

    for bb in range(batch):
        mod = mod_ref[jnp.where(i == 0, batch, bb)]
        x_all = jnp.concatenate([xp_ref[bb], x_ref[bb], xn_ref[bb]], axis=0)
        h_all = _norm_mod(x_all, g_ref[...], mod[0:1, :], mod[1:2, :])
        ext = _dot(h_all.astype(BF16), w_ref[:, PC_MLQK:PC_REST])
        ext_ref[...] = jnp.where(dead, 0.0, ext)
        pr = _dot(h_all[HALO:HALO + TILE, :].astype(BF16), w_ref[:, PC_REST:])

        qk = _silu(cw[0:1, :] * ext_ref[HALO - 1:HALO - 1 + TILE, :]
                   + cw[1:2, :] * ext_ref[HALO:HALO + TILE, :]
                   + cw[2:3, :] * ext_ref[HALO + 1:HALO + 1 + TILE, :])
        mq_ref[bb] = qk[:, 0:nqk_m].astype(BF16)
        k_ml = qk[:, nqk_m:] * (DK_ML ** -0.5)
        mk_ref[bb] = k_ml.astype(BF16)
        mv_ref[bb] = pr[:, PR_ML_V:PR_ML_V + BRANCH_W].astype(BF16)
        msz_ref[bb] = _silu(pr[:, PR_ML_Z:PR_ML_Z + BRANCH_W]).astype(BF16)
        gates_t = (pr[:, PR_ML_GATES:] + gb_ref[...]).T
        li_t = gates_t[0:nhd]
        lf_t = _log_sigmoid(gates_t[nhd:])
        lf_hi, lf_lo = _split2(lf_t)
        lf2 = jnp.concatenate([lf_hi, lf_lo], axis=0)
        cum_f = _dot(lf2, tri_le)
        cum_b = _dot(lf2, tri_ge)
        b_t = jnp.where(row8 < HEADS, cum_f[0:nhd] + cum_f[nhd:], cum_b[0:nhd] + cum_b[nhd:])
        tot_t = jnp.zeros((nhd, TILE), F32)
        mloc_t = jnp.zeros((nhd, TILE), F32)
        for c in range(CHUNKS_PER_TILE):
            tot_c = jnp.sum(lf_t[:, c * CHUNK:(c + 1) * CHUNK], axis=-1, keepdims=True)
            tot_t = jnp.where(lane_chunk == c, tot_c, tot_t)
            mcs_ref[bb, c, 0:nhd, :] = jnp.broadcast_to(tot_c, (nhd, 128))
        g_t = tot_t - b_t + li_t
        for c in range(CHUNKS_PER_TILE):
            mloc_c = jnp.max(g_t[:, c * CHUNK:(c + 1) * CHUNK], axis=-1, keepdims=True)
            mloc_t = jnp.where(lane_chunk == c, mloc_c, mloc_t)
            mcs_ref[bb, c, nhd:2 * nhd, :] = jnp.broadcast_to(mloc_c, (nhd, 128))
        w_t = jnp.exp(g_t - mloc_t)
        r_t = li_t - b_t
        cols = jnp.concatenate([b_t, w_t], axis=0).T
        for j in range(nhd):
            d, hd = j // HEADS, j % HEADS
            mask = (si <= ti) if d == 0 else (si >= ti)
            mkw_ref[bb, :, d * nqk_m + hd * DK_ML:d * nqk_m + (hd + 1) * DK_ML] = (
                k_ml[:, hd * DK_ML:(hd + 1) * DK_ML] * cols[:, nhd + j:nhd + j + 1]).astype(BF16)
            mbb_ref[bb, :, j * 128:(j + 1) * 128] = jnp.broadcast_to(cols[:, j:j + 1], (TILE, 128))
            for c in range(CHUNKS_PER_TILE):
                rows = slice(c * CHUNK, (c + 1) * CHUNK)
                a = jnp.where(mask, jnp.broadcast_to(r_t[j:j + 1, rows], (CHUNK, CHUNK)), -jnp.inf)
                cmax = jnp.max(a, axis=-1, keepdims=True)
                mp0_ref[bb, j // 2, rows, (j % 2) * CHUNK:(j % 2 + 1) * CHUNK] = jnp.exp(a - cmax)
                mub_ref[bb, rows, j * 128:(j + 1) * 128] = jnp.broadcast_to(cmax, (CHUNK, 128))

        gv_ref[bb] = pr[:, PR_GLA_V:PR_GLA_V + BRANCH_W].astype(BF16)
        gsg_ref[bb] = _silu(pr[:, PR_GLA_G:PR_GLA_G + BRANCH_W]).astype(BF16)
        a_hi, a_lo = _split2(pr[:, PR_GLA_A:PR_GLA_A + 2 * GLA_RANK])
        z = _dot(jnp.concatenate([a_hi, a_lo], axis=1), a2_ref[...]) + ab_ref[...]
        la = _log_sigmoid(z) * (1.0 / GLA_TAU)
        half_t = jnp.sum(la.reshape(2 * CHUNKS_PER_TILE, CHUNK // 2, 2 * nqk_s), axis=1).T
        q_gla = pr[:, PR_GLA_Q:PR_GLA_Q + nqk_s] * (DK_SMALL ** -0.5)
        k_gla = pr[:, PR_GLA_K:PR_GLA_K + nqk_s]
        for d in range(2):
            la_d = la[:, d * nqk_s:(d + 1) * nqk_s]
            la_hi, la_lo = _split2(la_d)
            tri = tri_ge if d == 0 else tri_le
            b = _dot(tri, la_hi) + _dot(tri, la_lo)
            tot = _chunk_total(la_d)
            gq_ref[bb, :, d * nqk_s:(d + 1) * nqk_s] = (q_gla * jnp.exp(b)).astype(BF16)
            gk_ref[bb, :, d * nqk_s:(d + 1) * nqk_s] = (k_gla * jnp.exp(-b)).astype(BF16)
            ge_ref[bb, :, d * nqk_s:(d + 1) * nqk_s] = (k_gla * jnp.exp(tot - b)).astype(BF16)
            ht = half_t[d * nqk_s:(d + 1) * nqk_s, :]
            for c in range(CHUNKS_PER_TILE):
                tot_col = ht[:, 2 * c:2 * c + 1] + ht[:, 2 * c + 1:2 * c + 2]
                gdec_ref[bb, c, d * nqk_s:(d + 1) * nqk_s, :] = jnp.broadcast_to(jnp.exp(tot_col), (nqk_s, 128))

        q_ret = rope(pr[:, PR_RET_Q:PR_RET_Q + nqk_s])
        k_ret = rope(pr[:, PR_RET_K:PR_RET_K + nqk_s] * (DK_SMALL ** -0.5))
        rv_ref[bb] = pr[:, PR_RET_V:PR_RET_V + BRANCH_W].astype(BF16)
        rsg_ref[bb] = _silu(pr[:, PR_RET_G:PR_RET_G + BRANCH_W]).astype(BF16)
        for d in range(2):
            lg_d = lg[:, d * nqk_s:(d + 1) * nqk_s]
            steps = (t_in_chunk + 1.0) if d == 0 else (float(CHUNK) - t_in_chunk)
            b = steps * lg_d
            tot = float(CHUNK) * lg_d
            rq_ref[bb, :, d * nqk_s:(d + 1) * nqk_s] = (q_ret * jnp.exp(b)).astype(BF16)
            rk_ref[bb, :, d * nqk_s:(d + 1) * nqk_s] = (k_ret * jnp.exp(-b)).astype(BF16)
            re_ref[bb, :, d * nqk_s:(d + 1) * nqk_s] = (k_ret * jnp.exp(tot - b)).astype(BF16)


def _proj(tok, mods, g1, w, a2s, abias, conv_w, gate_bias, cos_t, sin_t, dl_row):
    B, T, _ = tok.shape
    nt = T // TILE
    nqk_s = HEADS * DK_SMALL
    nqk_m = HEADS * DK_ML
    tile = lambda width: pl.BlockSpec((B, TILE, width), lambda i: (0, i, 0))
    const = lambda shape: pl.BlockSpec(shape, lambda i: (0,) * len(shape))
    per_halo = TILE // HALO
    n_halo = T // HALO
    tok_shape = lambda width, dt: jax.ShapeDtypeStruct((B, T, width), dt)
    return pl.pallas_call(
        functools.partial(_proj_kernel, nt, B),
        out_shape=(tok_shape(2 * nqk_s, BF16), tok_shape(2 * nqk_s, BF16), tok_shape(2 * nqk_s, BF16),
                   tok_shape(BRANCH_W, BF16), tok_shape(BRANCH_W, BF16),
                   jax.ShapeDtypeStruct((B, T // CHUNK, 2 * nqk_s, 128), F32),
                   tok_shape(nqk_m, BF16), tok_shape(nqk_m, BF16), tok_shape(2 * nqk_m, BF16),
                   tok_shape(BRANCH_W, BF16), tok_shape(BRANCH_W, BF16),
                   jax.ShapeDtypeStruct((B, HEADS, T, 2 * CHUNK), F32),
                   tok_shape(2 * HEADS * 128, F32), tok_shape(2 * HEADS * 128, F32),
                   jax.ShapeDtypeStruct((B, T // CHUNK, 4 * HEADS, 128), F32),
                   tok_shape(2 * nqk_s, BF16), tok_shape(2 * nqk_s, BF16), tok_shape(2 * nqk_s, BF16),
                   tok_shape(BRANCH_W, BF16), tok_shape(BRANCH_W, BF16)),
        grid=(nt,),
        in_specs=[tile(D_MODEL),
                  pl.BlockSpec((B, HALO, D_MODEL), lambda i: (0, jnp.maximum(i * per_halo - 1, 0), 0)),
                  pl.BlockSpec((B, HALO, D_MODEL),
                               lambda i: (0, jnp.minimum((i + 1) * per_halo, n_halo - 1), 0)),
                  const((B + 1, 6, D_MODEL)), const((1, D_MODEL)),
                  pl.BlockSpec((D_MODEL, W_PROJ), lambda i: (0, 0), pipeline_mode=pl.Buffered(1)),
                  const((4 * GLA_RANK, 2 * nqk_s)), const((1, 2 * nqk_s)),
                  const((3, 2 * nqk_m)), const((1, 4 * HEADS)),
                  pl.BlockSpec((TILE, nqk_s), lambda i: (i, 0)),
                  pl.BlockSpec((TILE, nqk_s), lambda i: (i, 0)),
                  const((1, 2 * nqk_s))],
        out_specs=(tile(2 * nqk_s), tile(2 * nqk_s), tile(2 * nqk_s), tile(BRANCH_W), tile(BRANCH_W),
                   pl.BlockSpec((B, CHUNKS_PER_TILE, 2 * nqk_s, 128), lambda i: (0, i, 0, 0)),
                   tile(nqk_m), tile(nqk_m), tile(2 * nqk_m), tile(BRANCH_W), tile(BRANCH_W),
                   pl.BlockSpec((B, HEADS, TILE, 2 * CHUNK), lambda i: (0, 0, i, 0)),
                   tile(2 * HEADS * 128), tile(2 * HEADS * 128),
                   pl.BlockSpec((B, CHUNKS_PER_TILE, 4 * HEADS, 128), lambda i: (0, i, 0, 0)),
                   tile(2 * nqk_s), tile(2 * nqk_s), tile(2 * nqk_s), tile(BRANCH_W), tile(BRANCH_W)),
        scratch_shapes=[pltpu.VMEM((TILE + 2 * HALO, 2 * nqk_m), F32)],
        compiler_params=_cparams(1), name="proj",
    )(tok, tok, tok, mods, g1, w, a2s, abias, conv_w, gate_bias, cos_t, sin_t, dl_row)


PAIRS = HEADS // 2
PAIR_K = 2 * DK_SMALL
PAIR_V = 2 * DV


def _mix_kernel(*refs):
    (gq_f, gq_b, gk_f, gk_b, ge_f, ge_b, gv_f, gv_b, gd_f, gd_b,
     rq_f, rq_b, rk_f, rk_b, re_f, re_b, rv_f, rv_b, rdl,
     mq_f, mq_b, mk_f, mk_b, mw_f, mw_b, mv_f, mv_b, mp_f, mp_b, mu_f, mu_b, mb_f, mb_b, mc_f, mc_b,
     of_ref, ob_ref, sg_ref, sr_ref, sm_ref, m_ref) = refs
    j = pl.program_id(1)

    @pl.when(j == 0)
    def _():
        sg_ref[...] = jnp.zeros_like(sg_ref)
        sr_ref[...] = jnp.zeros_like(sr_ref)
        sm_ref[...] = jnp.zeros_like(sm_ref)
        m_ref[...] = jnp.full_like(m_ref, M_INIT)

    row_head = lax.broadcasted_iota(jnp.int32, (2 * CHUNK, PAIR_V), 0) // CHUNK
    diag_kv = row_head == lax.broadcasted_iota(jnp.int32, (2 * CHUNK, PAIR_V), 1) // DV
    diag_kk = (lax.broadcasted_iota(jnp.int32, (2 * CHUNK, PAIR_K), 0) // CHUNK
               == lax.broadcasted_iota(jnp.int32, (2 * CHUNK, PAIR_K), 1) // DK_SMALL)
    t_idx = lax.broadcasted_iota(jnp.int32, (CHUNK, 2 * CHUNK), 0)
    s_idx = lax.broadcasted_iota(jnp.int32, (CHUNK, 2 * CHUNK), 1) % CHUNK
    causal = (s_idx <= t_idx, s_idx >= t_idx)
    ones_v = jnp.ones((CHUNK, DV), BF16)
    zeros_vext = jnp.zeros((CHUNK, PAIR_V), BF16)
    ret_dec = jnp.exp(float(CHUNK) * _log_sigmoid(rdl[...]))

    decay_sets = (
        ((gq_f, gk_f, ge_f, gv_f), (gq_b, gk_b, ge_b, gv_b), sg_ref, 0),
        ((rq_f, rk_f, re_f, rv_f), (rq_b, rk_b, re_b, rv_b), sr_ref, 2 * BRANCH_W),
    )
    ml_sets = ((mq_f, mk_f, mw_f, mv_f, mp_f, mu_f, mb_f, mc_f),
               (mq_b, mk_b, mw_b, mv_b, mp_b, mu_b, mb_b, mc_b))
    outs = (of_ref, ob_ref)

    def chunk_step(step, carry):
        work = []
        for d in range(2):
            c = step if d == 0 else CHUNKS_PER_TILE - 1 - step
            r0 = pl.multiple_of(c * CHUNK, CHUNK)
            rows = pl.ds(r0, CHUNK)
            o_ref = outs[d]
            for bi, (fset, bset, s_ref, base) in enumerate(decay_sets):
                q_r, k_r, e_r, v_r = fset if d == 0 else bset
                for p in range(PAIRS):
                    ks = slice(p * PAIR_K, (p + 1) * PAIR_K)
                    q2 = q_r[0, rows, ks]
                    k2 = k_r[0, rows, ks]
                    v2 = v_r[0, rows, p * PAIR_V:(p + 1) * PAIR_V]
                    k_bd = jnp.where(diag_kk, jnp.concatenate([k2, k2], axis=0), 0.0)
                    if bi == 0:
                        dec = (gd_f if d == 0 else gd_b)[0, c, ks, :]
                    else:
                        dec = ret_dec[d, ks, :]
                    upd = jnp.where(diag_kv, _dot_tn(e_r[0, rows, ks], v2), 0.0)
                    work.append(dict(
                        kind="decay", d=d, p=p, rows=rows, o_ref=o_ref, s_ref=s_ref, base=base,
                        q2=q2, v2=v2, dec=jnp.concatenate([dec, dec], axis=1), upd=upd,
                        att=_dot_nt(q2, k_bd)))
            q_r, k_r, w_r, v_r, p_r, u_r, b_r, c_r = ml_sets[d]
            cs = c_r[0, c]
            for p in range(PAIRS):
                q2 = q_r[0, rows, p * 2 * DK_ML:(p + 1) * 2 * DK_ML]
                k2 = k_r[0, rows, p * 2 * DK_ML:(p + 1) * 2 * DK_ML]
                k_bd = jnp.where(diag_kv, jnp.concatenate([k2, k2], axis=0), 0.0)
                att = _dot_nt(q2, k_bd)
                heads = []
                for hl in range(2):
                    hd = 2 * p + hl
                    vext = jnp.concatenate([v_r[0, rows, hd * DV:(hd + 1) * DV], ones_v], axis=1)
                    st = sm_ref[d, hd]
                    heads.append(dict(
                        hd=hd, st=st, vext=vext,
                        inter=_dot(q2[:, hl * DK_ML:(hl + 1) * DK_ML], st.astype(BF16)),
                        upd=_dot_tn(w_r[0, rows, hd * DK_ML:(hd + 1) * DK_ML], vext),
                        u=u_r[0, rows, hd * 128:(hd + 1) * 128],
                        b=b_r[0, rows, hd * 128:(hd + 1) * 128]))
                work.append(dict(kind="ml", d=d, p=p, rows=rows, o_ref=o_ref, att=att, heads=heads,
                                 p0=p_r[0, p, rows, :], cs=cs))
        nhd = 2 * HEADS
        for w in work:
            d, p, rows, o_ref = w["d"], w["p"], w["rows"], w["o_ref"]
            if w["kind"] == "decay":
                s_ref = w["s_ref"]
                s_prev = s_ref[d, p]
                att = jnp.where(causal[d], w["att"], 0.0).astype(BF16)
                v_bd = jnp.where(diag_kv, jnp.concatenate([w["v2"], w["v2"]], axis=0), 0.0)
                lhs = jnp.concatenate([att, w["q2"]], axis=1)
                rhs = jnp.concatenate([v_bd, s_prev.astype(BF16)], axis=0)
                o_ref[0, rows, w["base"] + p * PAIR_V:w["base"] + (p + 1) * PAIR_V] = _dot(lhs, rhs)
                s_ref[d, p] = w["dec"] * s_prev + w["upd"]
            else:
                sc = (w["att"] * w["p0"]).astype(BF16)
                for hl, hw in enumerate(w["heads"]):
                    hd = hw["hd"]
                    jj = d * HEADS + hd
                    vz = (jnp.concatenate([hw["vext"], zeros_vext], axis=0) if hl == 0
                          else jnp.concatenate([zeros_vext, hw["vext"]], axis=0))
                    intra = _dot(sc, vz)
                    m_prev = m_ref[jj:jj + 1, :]
                    u = hw["u"]
                    mx = jnp.maximum(u, m_prev)
                    f = jnp.exp(u - mx)
                    e = jnp.exp(m_prev - mx)
                    num = f * intra[:, 0:DV] + e * hw["inter"][:, 0:DV]
                    den = f * intra[:, DV:] + e * hw["inter"][:, DV:]
                    den = jnp.maximum(jnp.abs(den), jnp.exp(-(hw["b"] + mx)))
                    o_ref[0, rows, BRANCH_W + hd * DV:BRANCH_W + (hd + 1) * DV] = num / den
                    tot = w["cs"][jj:jj + 1, :]
                    mloc = w["cs"][nhd + jj:nhd + jj + 1, :]
                    m_new = jnp.maximum(tot + m_prev, mloc)
                    keep = jnp.exp(tot + m_prev - m_new)
                    gain = jnp.exp(mloc - m_new)
                    sm_ref[d, hd] = (jnp.concatenate([keep, keep], axis=1) * hw["st"]
                                     + jnp.concatenate([gain, gain], axis=1) * hw["upd"])
                    m_ref[jj:jj + 1, :] = m_new
        return carry

    lax.fori_loop(0, CHUNKS_PER_TILE, chunk_step, 0)


def _mix(gla, ret, ml, ret_dl_col):
    gq, gk, ge, gv, gdec = gla
    rq, rk, re, rv = ret
    mq, mk, mw, mv, mp, mu, mb, mc = ml
    B, T, _ = gv.shape
    nt = T // TILE
    nqk_s = HEADS * DK_SMALL
    nqk_m = HEADS * DK_ML

    def tf(j):
        return j

    def tb(j):
        return jnp.where(j == 0, 0, nt - j)

    def pair(width, lane_block_b=0):
        return [pl.BlockSpec((1, TILE, width), lambda b, j: (b, tf(j), 0)),
                pl.BlockSpec((1, TILE, width), lambda b, j: (b, tb(j), lane_block_b))]

    dec_pair = [pl.BlockSpec((1, CHUNKS_PER_TILE, nqk_s, 128), lambda b, j: (b, tf(j), 0, 0)),
                pl.BlockSpec((1, CHUNKS_PER_TILE, nqk_s, 128), lambda b, j: (b, tb(j), 1, 0))]
    p0_pair = [pl.BlockSpec((1, PAIRS, TILE, 2 * CHUNK), lambda b, j: (b, 0, tf(j), 0)),
               pl.BlockSpec((1, PAIRS, TILE, 2 * CHUNK), lambda b, j: (b, 1, tb(j), 0))]
    cs_pair = [pl.BlockSpec((1, CHUNKS_PER_TILE, 4 * HEADS, 128), lambda b, j: (b, tf(j), 0, 0)),
               pl.BlockSpec((1, CHUNKS_PER_TILE, 4 * HEADS, 128), lambda b, j: (b, tb(j), 0, 0))]
    decay_specs = pair(nqk_s, 1) + pair(nqk_s, 1) + pair(nqk_s, 1) + pair(BRANCH_W)
    in_specs = (decay_specs + dec_pair + decay_specs
                + [pl.BlockSpec((2, nqk_s, 128), lambda b, j: (0, 0, 0))]
                + pair(nqk_m) + pair(nqk_m) + pair(nqk_m, 1) + pair(BRANCH_W) + p0_pair
                + pair(HEADS * 128, 1) + pair(HEADS * 128, 1) + cs_pair)
    args = (gq, gq, gk, gk, ge, ge, gv, gv, gdec, gdec,
            rq, rq, rk, rk, re, re, rv, rv, ret_dl_col,
            mq, mq, mk, mk, mw, mw, mv, mv, mp, mp, mu, mu, mb, mb, mc, mc)
    return pl.pallas_call(
        _mix_kernel,
        out_shape=(jax.ShapeDtypeStruct((B, T, 3 * BRANCH_W), F32),) * 2,
        grid=(B, nt),
        in_specs=in_specs,
        out_specs=(pl.BlockSpec((1, TILE, 3 * BRANCH_W), lambda b, j: (b, tf(j), 0)),
                   pl.BlockSpec((1, TILE, 3 * BRANCH_W), lambda b, j: (b, tb(j), 0))),
        scratch_shapes=[pltpu.VMEM((2, PAIRS, PAIR_K, PAIR_V), F32),
                        pltpu.VMEM((2, PAIRS, PAIR_K, PAIR_V), F32),
                        pltpu.VMEM((2, HEADS, DK_ML, 2 * DV), F32),
                        pltpu.VMEM((2 * HEADS, 128), F32)],
        compiler_params=_cparams(2), name="mix",
    )(*args)


def _merge_kernel(x_ref, mod_ref, g_ref, wg_ref, of_ref, ob_ref, sgg_ref, sgm_ref, sgr_ref,
                  ng_ref, wb_ref, wo_ref, out_ref):
    x = x_ref[0]
    h = _norm_mod(x, g_ref[...], mod_ref[0, 0:1, :], mod_ref[0, 1:2, :]).astype(BF16)
    o = of_ref[0] + ob_ref[0]
    side = (sgg_ref, sgm_ref, sgr_ref)
    acc = jnp.zeros((TILE, D_MODEL), F32)
    for br in range(3):
        parts = []
        for hd in range(HEADS):
            oh = o[:, br * BRANCH_W + hd * DV:br * BRANCH_W + (hd + 1) * DV]
            if br > 0:
                oh = oh - jnp.mean(oh, axis=-1, keepdims=True)
            parts.append(oh * lax.rsqrt(jnp.mean(oh * oh, axis=-1, keepdims=True) + EPS))
        y = jnp.concatenate(parts, axis=1) * ng_ref[:, br * BRANCH_W:(br + 1) * BRANCH_W]
        y = (y * side[br][0]).astype(BF16)
        gate = _sigmoid(_dot(h, wg_ref[:, br * D_MODEL:(br + 1) * D_MODEL]))
        acc = acc + gate * _dot(y, wb_ref[br])
    out = _dot(acc.astype(BF16), wo_ref[...])
    out_ref[0] = x + mod_ref[0, 2:3, :] * out


def _merge(tok, mods, g1, wgate, o_f, o_b, sg_gla, sz_ml, sg_ret, norm_g, wb, wo, skip):
    B, T, _ = tok.shape
    nt = T // TILE - skip
    tile = lambda width: pl.BlockSpec((1, TILE, width), lambda b, i: (b, i + skip, 0))
    const = lambda shape: pl.BlockSpec(shape, lambda b, i: (0,) * len(shape))
    return pl.pallas_call(
        _merge_kernel,
        out_shape=jax.ShapeDtypeStruct((B, nt * TILE, D_MODEL), F32),
        grid=(B, nt),
        in_specs=[tile(D_MODEL),
                  pl.BlockSpec((1, 6, D_MODEL), lambda b, i: (_mod_row_index(b, i + skip), 0, 0)),
                  const((1, D_MODEL)), const((D_MODEL, W_GATE)),
                  tile(3 * BRANCH_W), tile(3 * BRANCH_W),
                  tile(BRANCH_W), tile(BRANCH_W), tile(BRANCH_W),
                  const((1, 3 * BRANCH_W)), const((3, BRANCH_W, D_MODEL)), const((D_MODEL, D_MODEL))],
        out_specs=pl.BlockSpec((1, TILE, D_MODEL), lambda b, i: (b, i, 0)),
        compiler_params=_cparams(2), name="merge",
    )(tok, mods, g1, wgate, o_f, o_b, sg_gla, sz_ml, sg_ret, norm_g, wb, wo)


def _mlp_kernel(final, x_ref, mod_ref, g_ref, wu_ref, wd_ref, fg_ref, out_ref):
    x = x_ref[0]
    h = _norm_mod(x, g_ref[...], mod_ref[0, 3:4, :], mod_ref[0, 4:5, :]).astype(BF16)
    u = jnp.maximum(_dot(h, wu_ref[...]), 0.0)
    y = x + mod_ref[0, 5:6, :] * _dot((u * u).astype(BF16), wd_ref[...])
    if final:
        y = y * lax.rsqrt(jnp.mean(y * y, axis=-1, keepdims=True) + EPS) * fg_ref[...]
    out_ref[0] = y


def _mlp(tok, mods, g2, wu, wd, final_g, first_tile, final):
    B, T, _ = tok.shape
    const = lambda shape: pl.BlockSpec(shape, lambda b, i: (0,) * len(shape))
    return pl.pallas_call(
        functools.partial(_mlp_kernel, final),
        out_shape=jax.ShapeDtypeStruct((B, T, D_MODEL), F32),
        grid=(B, T // TILE),
        in_specs=[pl.BlockSpec((1, TILE, D_MODEL), lambda b, i: (b, i, 0)),
                  pl.BlockSpec((1, 6, D_MODEL), lambda b, i: (_mod_row_index(b, i + first_tile), 0, 0)),
                  const((1, D_MODEL)), const((D_MODEL, D_FF)), const((D_FF, D_MODEL)),
                  const((1, D_MODEL))],
        out_specs=pl.BlockSpec((1, TILE, D_MODEL), lambda b, i: (b, i, 0)),
        compiler_params=_cparams(2), name="mlp",
    )(tok, mods, g2, wu, wd, final_g)


def kernel(x, c, ctx, c_ctx, norm1_g, norm2_g, w_ada, b_ada, w_in, gla_a2, gla_a_bias, gla_norm_g,
           ml_conv, ml_i_bias, ml_f_bias, ml_norm_g, ret_decay_logit, ret_norm_g, w_branch, w_o,
           w_up, w_down, final_g):
    B, seq, _ = x.shape
    ctx_len = ctx.shape[1]
    assert B == 2 and ctx_len == TILE and seq % TILE == 0 and seq % GRID_W == 0
    tok = jnp.concatenate([ctx, x], axis=1)

    c_rows = jnp.concatenate([c, c_ctx[None, :], jnp.zeros((8 - B - 1, D_MODEL), F32)], axis=0)
    mods_all = _ada(c_rows, w_ada, b_ada)[:, 0:3].reshape(DEPTH, 3, 6, D_MODEL)

    cos64, sin64 = _rope_tables(seq)
    cos_t = jnp.concatenate([jnp.ones((ctx_len, HEADS * DK_SMALL), F32), jnp.tile(cos64, (1, HEADS))], axis=0)
    sin_t = jnp.concatenate([jnp.zeros((ctx_len, HEADS * DK_SMALL), F32), jnp.tile(sin64, (1, HEADS))], axis=0)

    o0 = W_GLA
    o1 = o0 + W_ML
    o2 = o1 + W_RET
    nqk_s = HEADS * DK_SMALL
    for l in range(DEPTH):
        last = l == DEPTH - 1
        skip = 1 if last else 0
        mods = mods_all[l]
        g1 = norm1_g[l][None, :]
        g2 = norm2_g[l][None, :]
        w_l = w_in[l]
        nqk_m = HEADS * DK_ML
        n_sg = 2 * nqk_s + 2 * BRANCH_W
        w_proj = jnp.concatenate([
            w_l[:, o0:o0 + 2 * nqk_m],
            w_l[:, 0:n_sg],
            w_l[:, o0 + 2 * nqk_m:o0 + 2 * nqk_m + 2 * BRANCH_W],
            w_l[:, o1:o2],
            w_l[:, n_sg:o0],
            w_l[:, o1 - 4 * HEADS:o1],
        ], axis=1).astype(BF16)
        w_gate = w_l[:, o2:].astype(BF16)
        zeros = jnp.zeros((GLA_RANK, nqk_s), F32)
        a2blk = jnp.concatenate([jnp.concatenate([gla_a2[l, 0], zeros], axis=1),
                                 jnp.concatenate([zeros, gla_a2[l, 1]], axis=1)], axis=0)
        a2s = jnp.concatenate([a2blk, a2blk], axis=0).astype(BF16)
        abias = gla_a_bias[l].reshape(1, 2 * nqk_s)
        gate_bias = jnp.concatenate([ml_i_bias[l].reshape(1, 2 * HEADS),
                                     ml_f_bias[l].reshape(1, 2 * HEADS)], axis=1)
        dl = ret_decay_logit[l]
        dl_row = jnp.repeat(dl, DK_SMALL, axis=1).reshape(1, 2 * nqk_s)
        dl_col = jnp.broadcast_to(jnp.repeat(dl, DK_SMALL, axis=1)[:, :, None], (2, nqk_s, 128))

        (gq, gk, ge, gv, sg_gla, gdec, mq, mk, mw, mv, sz_ml, mp, mu, mb, mc,
         rq, rk, re, rv, sg_ret) = _proj(tok, mods, g1, w_proj, a2s, abias, ml_conv[l], gate_bias,
                                         cos_t, sin_t, dl_row)
        o_f, o_b = _mix((gq, gk, ge, gv, gdec), (rq, rk, re, rv), (mq, mk, mw, mv, mp, mu, mb, mc), dl_col)
        norm_g = jnp.concatenate([gla_norm_g[l], ml_norm_g[l], ret_norm_g[l]])[None, :]
        tok = _merge(tok, mods, g1, w_gate, o_f, o_b, sg_gla, sz_ml, sg_ret, norm_g,
                     w_branch[l].astype(BF16), w_o[l].astype(BF16), skip)
        tok = _mlp(tok, mods, g2, w_up[l].astype(BF16), w_down[l].astype(BF16),
                   final_g[None, :], skip, last)
    return tok
```

```python
import functools

import jax
import jax.numpy as jnp
import numpy as np
from jax import lax
from jax.experimental import pallas as pl
from jax.experimental.pallas import tpu as pltpu

F32 = jnp.float32
BF16 = jnp.bfloat16

D_MODEL = 1024
DEPTH = 2
GRID_W = 64
CHUNK = 64
EPS = 1e-6
M_INIT = -1e30
HEADS = 4
DK_SMALL = 64
DK_ML = 128
DV = 128
BRANCH_W = HEADS * DV
GLA_RANK = 16
GLA_TAU = 16.0
ROPE_BASE = 10000.0
D_FF = 4 * D_MODEL

TILE = 256
CHUNKS_PER_TILE = TILE // CHUNK
HALO = 8
VMEM_LIMIT = 56 * 1024 * 1024

W_GLA = 2 * HEADS * DK_SMALL + 2 * BRANCH_W + 2 * GLA_RANK
W_ML = 2 * HEADS * DK_ML + 2 * BRANCH_W + 4 * HEADS
W_RET = 2 * HEADS * DK_SMALL + 2 * BRANCH_W
W_GATE = 3 * D_MODEL


def _cparams(n_axes):
    return pltpu.CompilerParams(dimension_semantics=("arbitrary",) * n_axes,
                                vmem_limit_bytes=VMEM_LIMIT)


def _log_sigmoid(z):
    return jnp.minimum(z, 0.0) - jnp.log(1.0 + jnp.exp(-jnp.abs(z)))


def _sigmoid(z):
    return 1.0 / (1.0 + jnp.exp(-z))


def _silu(z):
    return z * _sigmoid(z)


def _norm_mod(x, g, shift, scale):
    y = x * lax.rsqrt(jnp.mean(x * x, axis=-1, keepdims=True) + EPS) * g
    return y * (1.0 + scale) + shift


def _dot(a, b):
    return jnp.dot(a, b, preferred_element_type=F32)


def _dot_nt(a, b):
    return lax.dot_general(a, b, (((1,), (1,)), ((), ())), preferred_element_type=F32)


def _dot_tn(a, b):
    return lax.dot_general(a, b, (((0,), (0,)), ((), ())), preferred_element_type=F32)


def _dot_exact_lhs(a_bf16, x):
    hi = x.astype(BF16)
    r1 = x - hi.astype(F32)
    mid = r1.astype(BF16)
    lo = (r1 - mid.astype(F32)).astype(BF16)
    return _dot(a_bf16, hi) + _dot(a_bf16, mid) + _dot(a_bf16, lo)


def _chunk_tri(forward):
    t = lax.broadcasted_iota(jnp.int32, (TILE, TILE), 0)
    s = lax.broadcasted_iota(jnp.int32, (TILE, TILE), 1)
    same = (t // CHUNK) == (s // CHUNK)
    tri = (s <= t) if forward else (s >= t)
    return jnp.where(same & tri, 1.0, 0.0).astype(BF16)


def _chunk_total(x):
    w = x.shape[-1]
    tot = jnp.sum(x.reshape(CHUNKS_PER_TILE, CHUNK, w), axis=1, keepdims=True)
    return jnp.broadcast_to(tot, (CHUNKS_PER_TILE, CHUNK, w)).reshape(TILE, w)


def _chunk_max(x):
    w = x.shape[-1]
    mx = jnp.max(x.reshape(CHUNKS_PER_TILE, CHUNK, w), axis=1, keepdims=True)
    return jnp.broadcast_to(mx, (CHUNKS_PER_TILE, CHUNK, w)).reshape(TILE, w)


def _mod_row_index(b, i):
    return jnp.where(i == 0, 2, b)


def _ada_kernel(c_ref, w_ref, b_ref, o_ref):
    s = _silu(c_ref[...])
    o_ref[0] = jnp.dot(s, w_ref[0], precision=lax.Precision.HIGHEST,
                       preferred_element_type=F32) + b_ref[0]


def _ada(c_rows, w_ada, b_ada):
    tn = 1536
    return pl.pallas_call(
        _ada_kernel,
        out_shape=jax.ShapeDtypeStruct((DEPTH, 8, 6 * D_MODEL), F32),
        grid=(DEPTH, 6 * D_MODEL // tn),
        in_specs=[pl.BlockSpec((8, D_MODEL), lambda l, n: (0, 0)),
                  pl.BlockSpec((1, D_MODEL, tn), lambda l, n: (l, 0, n)),
                  pl.BlockSpec((1, 1, tn), lambda l, n: (l, 0, n))],
        out_specs=pl.BlockSpec((1, 8, tn), lambda l, n: (l, 0, n)),
        compiler_params=_cparams(2), name="ada",
    )(c_rows, w_ada, b_ada.reshape(DEPTH, 1, 6 * D_MODEL))


def _rope_kernel(ang_ref, cos_ref, sin_ref, nsin_ref):
    a = ang_ref[...]
    s = jnp.sin(a)
    cos_ref[...] = jnp.cos(a)
    sin_ref[...] = s
    nsin_ref[...] = -s


def _rope_tables(seq, ctx_len):
    rows = seq // GRID_W
    nf = DK_SMALL // 4
    inv = ROPE_BASE ** (-jnp.arange(nf, dtype=F32) / nf)
    ang = jnp.concatenate([jnp.arange(rows, dtype=F32)[:, None] * inv,
                           jnp.arange(GRID_W, dtype=F32)[:, None] * inv], axis=0)
    n = rows + GRID_W
    cos, sin, nsin = pl.pallas_call(
        _rope_kernel,
        out_shape=(jax.ShapeDtypeStruct((n, nf), F32),) * 3,
        name="rope_tables",
    )(ang)

    def per_token(first, second):
        def half(t):
            r = jnp.broadcast_to(t[:rows, None, :], (rows, GRID_W, nf))
            c = jnp.broadcast_to(t[None, rows:, :], (rows, GRID_W, nf))
            return jnp.concatenate([r, c], axis=-1)
        head = jnp.concatenate([half(first), half(second)], axis=-1).reshape(seq, DK_SMALL)
        return jnp.tile(head, (1, HEADS))

    ones = jnp.ones((ctx_len, HEADS * DK_SMALL), F32)
    cos_t = jnp.concatenate([ones, per_token(cos, cos)], axis=0)
    sin_t = jnp.concatenate([0.0 * ones, per_token(nsin, sin)], axis=0)
    return cos_t, sin_t


def _proj_gla_kernel(x_ref, mod_ref, g_ref, w_ref, a2_ref, ab_ref,
                     qin_ref, kin_ref, kend_ref, v_ref, sg_ref, decb_ref):
    h = _norm_mod(x_ref[0], g_ref[...], mod_ref[0, 0:1, :], mod_ref[0, 1:2, :]).astype(BF16)
    pr = _dot(h, w_ref[...])
    nqk = HEADS * DK_SMALL
    q = pr[:, 0:nqk] * (DK_SMALL ** -0.5)
    k = pr[:, nqk:2 * nqk]
    v_ref[0] = pr[:, 2 * nqk:2 * nqk + BRANCH_W].astype(BF16)
    sg_ref[0] = _silu(pr[:, 2 * nqk + BRANCH_W:2 * nqk + 2 * BRANCH_W])
    a = pr[:, 2 * nqk + 2 * BRANCH_W:]
    z = jnp.dot(a, a2_ref[...], precision=lax.Precision.HIGHEST,
                preferred_element_type=F32) + ab_ref[...]
    la = _log_sigmoid(z) * (1.0 / GLA_TAU)
    half = jnp.sum(la.reshape(2 * CHUNKS_PER_TILE, CHUNK // 2, 2 * nqk), axis=1)
    half_t = half.T
    for d in range(2):
        la_d = la[:, d * nqk:(d + 1) * nqk]
        b = _dot_exact_lhs(_chunk_tri(d == 0), la_d)
        tot = _chunk_total(la_d)
        qin_ref[0, :, d * nqk:(d + 1) * nqk] = (q * jnp.exp(b)).astype(BF16)
        kin_ref[0, :, d * nqk:(d + 1) * nqk] = (k * jnp.exp(-b)).astype(BF16)
        kend_ref[0, :, d * nqk:(d + 1) * nqk] = (k * jnp.exp(tot - b)).astype(BF16)
        ht = half_t[d * nqk:(d + 1) * nqk, :]
        for c in range(CHUNKS_PER_TILE):
            tot_col = ht[:, 2 * c:2 * c + 1] + ht[:, 2 * c + 1:2 * c + 2]
            decb_ref[0, c, d * nqk:(d + 1) * nqk, :] = jnp.broadcast_to(jnp.exp(tot_col), (nqk, 128))


def _proj_gla(tok, mods, g1, w, a2blk, abias):
    B, T, _ = tok.shape
    nt = T // TILE
    nqk = HEADS * DK_SMALL
    tile = lambda width: pl.BlockSpec((1, TILE, width), lambda b, i: (b, i, 0))
    const = lambda shape: pl.BlockSpec(shape, lambda b, i: (0,) * len(shape))
    return pl.pallas_call(
        _proj_gla_kernel,
        out_shape=(jax.ShapeDtypeStruct((B, T, 2 * nqk), BF16),
                   jax.ShapeDtypeStruct((B, T, 2 * nqk), BF16),
                   jax.ShapeDtypeStruct((B, T, 2 * nqk), BF16),
                   jax.ShapeDtypeStruct((B, T, BRANCH_W), BF16),
                   jax.ShapeDtypeStruct((B, T, BRANCH_W), F32),
                   jax.ShapeDtypeStruct((B, T // CHUNK, 2 * nqk, 128), F32)),
        grid=(B, nt),
        in_specs=[tile(D_MODEL),
                  pl.BlockSpec((1, 6, D_MODEL), lambda b, i: (_mod_row_index(b, i), 0, 0)),
                  const((1, D_MODEL)), const((D_MODEL, W_GLA)),
                  const((2 * GLA_RANK, 2 * nqk)), const((1, 2 * nqk))],
        out_specs=(tile(2 * nqk), tile(2 * nqk), tile(2 * nqk), tile(BRANCH_W), tile(BRANCH_W),
                   pl.BlockSpec((1, CHUNKS_PER_TILE, 2 * nqk, 128), lambda b, i: (b, i, 0, 0))),
        compiler_params=_cparams(2), name="proj_gla",
    )(tok, mods, g1, w, a2blk, abias)


def _proj_ret_kernel(x_ref, mod_ref, g_ref, w_ref, cos_ref, sin_ref, dl_ref,
                     qin_ref, kin_ref, kend_ref, v_ref, sg_ref):
    h = _norm_mod(x_ref[0], g_ref[...], mod_ref[0, 0:1, :], mod_ref[0, 1:2, :]).astype(BF16)
    pr = _dot(h, w_ref[...])
    nqk = HEADS * DK_SMALL
    cos = cos_ref[...]
    sin = sin_ref[...]
    lane = lax.broadcasted_iota(jnp.int32, (TILE, nqk), 1)
    first_half = (lane % DK_SMALL) < (DK_SMALL // 2)

    def rope(x):
        swapped = jnp.where(first_half, pltpu.roll(x, nqk - DK_SMALL // 2, 1),
                            pltpu.roll(x, DK_SMALL // 2, 1))
        return x * cos + swapped * sin

    q = rope(pr[:, 0:nqk])
    k = rope(pr[:, nqk:2 * nqk] * (DK_SMALL ** -0.5))
    v_ref[0] = pr[:, 2 * nqk:2 * nqk + BRANCH_W].astype(BF16)
    sg_ref[0] = _silu(pr[:, 2 * nqk + BRANCH_W:])
    lg = _log_sigmoid(dl_ref[...])
    t_in_chunk = (lax.broadcasted_iota(jnp.int32, (TILE, nqk), 0) % CHUNK).astype(F32)
    for d in range(2):
        lg_d = lg[:, d * nqk:(d + 1) * nqk]
        steps = (t_in_chunk + 1.0) if d == 0 else (float(CHUNK) - t_in_chunk)
        b = steps * lg_d
        tot = float(CHUNK) * lg_d
        qin_ref[0, :, d * nqk:(d + 1) * nqk] = (q * jnp.exp(b)).astype(BF16)
        kin_ref[0, :, d * nqk:(d + 1) * nqk] = (k * jnp.exp(-b)).astype(BF16)
        kend_ref[0, :, d * nqk:(d + 1) * nqk] = (k * jnp.exp(tot - b)).astype(BF16)


def _proj_ret(tok, mods, g1, w, cos_t, sin_t, dl_row):
    B, T, _ = tok.shape
    nt = T // TILE
    nqk = HEADS * DK_SMALL
    tile = lambda width: pl.BlockSpec((1, TILE, width), lambda b, i: (b, i, 0))
    const = lambda shape: pl.BlockSpec(shape, lambda b, i: (0,) * len(shape))
    return pl.pallas_call(
        _proj_ret_kernel,
        out_shape=(jax.ShapeDtypeStruct((B, T, 2 * nqk), BF16),
                   jax.ShapeDtypeStruct((B, T, 2 * nqk), BF16),
                   jax.ShapeDtypeStruct((B, T, 2 * nqk), BF16),
                   jax.ShapeDtypeStruct((B, T, BRANCH_W), BF16),
                   jax.ShapeDtypeStruct((B, T, BRANCH_W), F32)),
        grid=(B, nt),
        in_specs=[tile(D_MODEL),
                  pl.BlockSpec((1, 6, D_MODEL), lambda b, i: (_mod_row_index(b, i), 0, 0)),
                  const((1, D_MODEL)), const((D_MODEL, W_RET)),
                  pl.BlockSpec((TILE, nqk), lambda b, i: (i, 0)),
                  pl.BlockSpec((TILE, nqk), lambda b, i: (i, 0)),
                  const((1, 2 * nqk))],
        out_specs=(tile(2 * nqk), tile(2 * nqk), tile(2 * nqk), tile(BRANCH_W), tile(BRANCH_W)),
        compiler_params=_cparams(2), name="proj_ret",
    )(tok, mods, g1, w, cos_t, sin_t, dl_row)


def _proj_ml_kernel(nt, x_ref, xp_ref, xn_ref, mod_ref, g_ref, w_ref, cw_ref, gb_ref,
                    q_ref, k_ref, kw_ref, v_ref, sz_ref, p0_ref, ub_ref, bb_ref, cs_ref, ext_ref):
    i = pl.program_id(1)
    nqk = HEADS * DK_ML
    x_all = jnp.concatenate([xp_ref[0], x_ref[0], xn_ref[0]], axis=0)
    h_all = _norm_mod(x_all, g_ref[...], mod_ref[0, 0:1, :], mod_ref[0, 1:2, :])
    ext = _dot(h_all.astype(BF16), w_ref[:, 0:2 * nqk])
    row = lax.broadcasted_iota(jnp.int32, (TILE + 2 * HALO, 1), 0)
    prev_ok = i >= 2
    next_ok = jnp.logical_and(i >= 1, i <= nt - 2)
    dead = jnp.logical_or(jnp.logical_and(row < HALO, jnp.logical_not(prev_ok)),
                          jnp.logical_and(row >= TILE + HALO, jnp.logical_not(next_ok)))
    ext_ref[...] = jnp.where(dead, 0.0, ext)
    cw = cw_ref[...]
    qk = (cw[0:1, :] * ext_ref[HALO - 1:HALO - 1 + TILE, :]
          + cw[1:2, :] * ext_ref[HALO:HALO + TILE, :]
          + cw[2:3, :] * ext_ref[HALO + 1:HALO + 1 + TILE, :])
    qk = _silu(qk)
    q_ref[0] = qk[:, 0:nqk].astype(BF16)
    k = qk[:, nqk:] * (DK_ML ** -0.5)
    k_ref[0] = k.astype(BF16)

    h = h_all[HALO:HALO + TILE, :].astype(BF16)
    pr = _dot(h, w_ref[:, 2 * nqk:])
    v_ref[0] = pr[:, 0:BRANCH_W].astype(BF16)
    sz_ref[0] = _silu(pr[:, BRANCH_W:2 * BRANCH_W])
    gates = pr[:, 2 * BRANCH_W:] + gb_ref[...]
    nhd = 2 * HEADS
    li = gates[:, 0:nhd]
    lf = _log_sigmoid(gates[:, nhd:])
    colid = lax.broadcasted_iota(jnp.int32, (TILE, nhd), 1)
    b = jnp.where(colid < HEADS, _dot_exact_lhs(_chunk_tri(True), lf),
                  _dot_exact_lhs(_chunk_tri(False), lf))
    tot = _chunk_total(lf)
    g = tot - b + li
    mloc = _chunk_max(g)
    w = jnp.exp(g - mloc)
    r_t = (li - b).T
    tot_t = tot.T
    mloc_t = mloc.T
    ti = lax.broadcasted_iota(jnp.int32, (CHUNK, CHUNK), 0)
    si = lax.broadcasted_iota(jnp.int32, (CHUNK, CHUNK), 1)
    for c in range(CHUNKS_PER_TILE):
        cs_ref[0, c, 0:nhd, :] = jnp.broadcast_to(tot_t[:, c * CHUNK:c * CHUNK + 1], (nhd, 128))
        cs_ref[0, c, nhd:2 * nhd, :] = jnp.broadcast_to(mloc_t[:, c * CHUNK:c * CHUNK + 1], (nhd, 128))
    for j in range(nhd):
        d, hd = j // HEADS, j % HEADS
        mask = (si <= ti) if d == 0 else (si >= ti)
        kw_ref[0, :, d * nqk + hd * DK_ML:d * nqk + (hd + 1) * DK_ML] = (
            k[:, hd * DK_ML:(hd + 1) * DK_ML] * w[:, j:j + 1]).astype(BF16)
        bb_ref[0, :, j * 128:(j + 1) * 128] = jnp.broadcast_to(b[:, j:j + 1], (TILE, 128))
        for c in range(CHUNKS_PER_TILE):
            rows = slice(c * CHUNK, (c + 1) * CHUNK)
            a = jnp.where(mask, jnp.broadcast_to(r_t[j:j + 1, rows], (CHUNK, CHUNK)), -jnp.inf)
            cmax = jnp.max(a, axis=-1, keepdims=True)
            p0_ref[0, j // 2, rows, (j % 2) * CHUNK:(j % 2 + 1) * CHUNK] = jnp.exp(a - cmax)
            ub_ref[0, rows, j * 128:(j + 1) * 128] = jnp.broadcast_to(cmax, (CHUNK, 128))


def _proj_ml(tok, mods, g1, w, conv_w, gate_bias):
    B, T, _ = tok.shape
    nt = T // TILE
    nqk = HEADS * DK_ML
    tile = lambda width: pl.BlockSpec((1, TILE, width), lambda b, i: (b, i, 0))
    const = lambda shape: pl.BlockSpec(shape, lambda b, i: (0,) * len(shape))
    per_halo = TILE // HALO
    n_halo = T // HALO
    return pl.pallas_call(
        functools.partial(_proj_ml_kernel, nt),
        out_shape=(jax.ShapeDtypeStruct((B, T, nqk), BF16),
                   jax.ShapeDtypeStruct((B, T, nqk), BF16),
                   jax.ShapeDtypeStruct((B, T, 2 * nqk), BF16),
                   jax.ShapeDtypeStruct((B, T, BRANCH_W), BF16),
                   jax.ShapeDtypeStruct((B, T, BRANCH_W), F32),
                   jax.ShapeDtypeStruct((B, HEADS, T, 2 * CHUNK), F32),
                   jax.ShapeDtypeStruct((B, T, 2 * HEADS * 128), F32),
                   jax.ShapeDtypeStruct((B, T, 2 * HEADS * 128), F32),
                   jax.ShapeDtypeStruct((B, T // CHUNK, 4 * HEADS, 128), F32)),
        grid=(B, nt),
        in_specs=[tile(D_MODEL),
                  pl.BlockSpec((1, HALO, D_MODEL),
                               lambda b, i: (b, jnp.maximum(i * per_halo - 1, 0), 0)),
                  pl.BlockSpec((1, HALO, D_MODEL),
                               lambda b, i: (b, jnp.minimum((i + 1) * per_halo, n_halo - 1), 0)),
                  pl.BlockSpec((1, 6, D_MODEL), lambda b, i: (_mod_row_index(b, i), 0, 0)),
                  const((1, D_MODEL)), const((D_MODEL, W_ML)),
                  const((3, 2 * nqk)), const((1, 4 * HEADS))],
        out_specs=(tile(nqk), tile(nqk), tile(2 * nqk), tile(BRANCH_W), tile(BRANCH_W),
                   pl.BlockSpec((1, HEADS, TILE, 2 * CHUNK), lambda b, i: (b, 0, i, 0)),
                   tile(2 * HEADS * 128), tile(2 * HEADS * 128),
                   pl.BlockSpec((1, CHUNKS_PER_TILE, 4 * HEADS, 128), lambda b, i: (b, i, 0, 0))),
        scratch_shapes=[pltpu.VMEM((TILE + 2 * HALO, 2 * nqk), F32)],
        compiler_params=_cparams(2), name="proj_ml",
    )(tok, tok, tok, mods, g1, w, conv_w, gate_bias)


PC_MLQK = 0
PC_REST = 2 * HEADS * DK_ML
PR_GLA_Q = 0
PR_GLA_K = PR_GLA_Q + HEADS * DK_SMALL
PR_GLA_V = PR_GLA_K + HEADS * DK_SMALL
PR_GLA_G = PR_GLA_V + BRANCH_W
PR_ML_V = PR_GLA_G + BRANCH_W
PR_ML_Z = PR_ML_V + BRANCH_W
PR_RET_Q = PR_ML_Z + BRANCH_W
PR_RET_K = PR_RET_Q + HEADS * DK_SMALL
PR_RET_V = PR_RET_K + HEADS * DK_SMALL
PR_RET_G = PR_RET_V + BRANCH_W
PR_GLA_A = PR_RET_G + BRANCH_W
PR_ML_GATES = PR_GLA_A + 2 * GLA_RANK
PR_WIDTH = PR_ML_GATES + 4 * HEADS
W_PROJ = PC_REST + PR_WIDTH


def _split2(x):
    hi = x.astype(BF16)
    return hi, (x - hi.astype(F32)).astype(BF16)


def _proj_kernel(nt, batch, x_ref, xp_ref, xn_ref, mod_ref, g_ref, w_ref, a2_ref, ab_ref, cw_ref, gb_ref,
                 cos_ref, sin_ref, dl_ref,
                 gq_ref, gk_ref, ge_ref, gv_ref, gsg_ref, gdec_ref,
                 mq_ref, mk_ref, mkw_ref, mv_ref, msz_ref, mp0_ref, mub_ref, mbb_ref, mcs_ref,
                 rq_ref, rk_ref, re_ref, rv_ref, rsg_ref, ext_ref, rdec_ref):
    i = pl.program_id(0)
    nqk_s = HEADS * DK_SMALL
    nqk_m = HEADS * DK_ML
    nhd = 2 * HEADS
    tri_ge = _chunk_tri(True)
    tri_le = _chunk_tri(False)
    row8 = lax.broadcasted_iota(jnp.int32, (nhd, TILE), 0)
    lane_chunk = lax.broadcasted_iota(jnp.int32, (nhd, TILE), 1) // CHUNK
    ti = lax.broadcasted_iota(jnp.int32, (CHUNK, CHUNK), 0)
    si = lax.broadcasted_iota(jnp.int32, (CHUNK, CHUNK), 1)
    lane = lax.broadcasted_iota(jnp.int32, (TILE, nqk_s), 1)
    first_half = (lane % DK_SMALL) < (DK_SMALL // 2)
    t_in_chunk = (lax.broadcasted_iota(jnp.int32, (TILE, nqk_s), 0) % CHUNK).astype(F32)
    ext_row = lax.broadcasted_iota(jnp.int32, (TILE + 2 * HALO, 1), 0)
    prev_ok = i >= 2
    next_ok = jnp.logical_and(i >= 1, i <= nt - 2)
    dead = jnp.logical_or(jnp.logical_and(ext_row < HALO, jnp.logical_not(prev_ok)),
                          jnp.logical_and(ext_row >= TILE + HALO, jnp.logical_not(next_ok)))
    cos = cos_ref[...]
    sin = sin_ref[...]
    cw = cw_ref[...]

    @pl.when(i == 0)
    def _():
        lg = _log_sigmoid(dl_ref[...])
        for d in range(2):
            lg_d = lg[:, d * nqk_s:(d + 1) * nqk_s]
            steps = (t_in_chunk + 1.0) if d == 0 else (float(CHUNK) - t_in_chunk)
            b = steps * lg_d
            rdec_ref[3 * d] = jnp.exp(b)
            rdec_ref[3 * d + 1] = jnp.exp(-b)
            rdec_ref[3 * d + 2] = jnp.exp(float(CHUNK) * lg_d - b)

    def rope(x):
        swapped = jnp.where(first_half, pltpu.roll(x, nqk_s - DK_SMALL // 2, 1),
                            pltpu.roll(x, DK_SMALL // 2, 1))
        return x * cos + swapped * sin

    for bb in range(batch):
        mod = mod_ref[jnp.where(i == 0, batch, bb)]
        x_all = jnp.concatenate([xp_ref[bb], x_ref[bb], xn_ref[bb]], axis=0)
        h_all = _norm_mod(x_all, g_ref[...], mod[0:1, :], mod[1:2, :])
        ext = _dot(h_all.astype(BF16), w_ref[0, :, PC_MLQK:PC_REST])
        ext_ref[...] = jnp.where(dead, 0.0, ext)
        pr = _dot(h_all[HALO:HALO + TILE, :].astype(BF16), w_ref[0, :, PC_REST:])

        qk = _silu(cw[0:1, :] * ext_ref[HALO - 1:HALO - 1 + TILE, :]
                   + cw[1:2, :] * ext_ref[HALO:HALO + TILE, :]
                   + cw[2:3, :] * ext_ref[HALO + 1:HALO + 1 + TILE, :])
        mq_ref[bb] = qk[:, 0:nqk_m].astype(BF16)
        k_ml = qk[:, nqk_m:] * (DK_ML ** -0.5)
        mk_ref[bb] = k_ml.astype(BF16)
        mv_ref[bb] = pr[:, PR_ML_V:PR_ML_V + BRANCH_W].astype(BF16)
        msz_ref[bb] = _silu(pr[:, PR_ML_Z:PR_ML_Z + BRANCH_W]).astype(BF16)
        gates_t = (pr[:, PR_ML_GATES:] + gb_ref[...]).T
        li_t = gates_t[0:nhd]
        lf_t = _log_sigmoid(gates_t[nhd:])
        lf_hi, lf_lo = _split2(lf_t)
        lf2 = jnp.concatenate([lf_hi, lf_lo], axis=0)
        cum_f = _dot(lf2, tri_le)
        cum_b = _dot(lf2, tri_ge)
        b_t = jnp.where(row8 < HEADS, cum_f[0:nhd] + cum_f[nhd:], cum_b[0:nhd] + cum_b[nhd:])
        tot_t = jnp.zeros((nhd, TILE), F32)
        mloc_t = jnp.zeros((nhd, TILE), F32)
        for c in range(CHUNKS_PER_TILE):
            tot_c = jnp.sum(lf_t[:, c * CHUNK:(c + 1) * CHUNK], axis=-1, keepdims=True)
            tot_t = jnp.where(lane_chunk == c, tot_c, tot_t)
            mcs_ref[bb, c, 0:nhd, :] = jnp.broadcast_to(tot_c, (nhd, 128))
        g_t = tot_t - b_t + li_t
        for c in range(CHUNKS_PER_TILE):
            mloc_c = jnp.max(g_t[:, c * CHUNK:(c + 1) * CHUNK], axis=-1, keepdims=True)
            mloc_t = jnp.where(lane_chunk == c, mloc_c, mloc_t)
            mcs_ref[bb, c, nhd:2 * nhd, :] = jnp.broadcast_to(mloc_c, (nhd, 128))
        w_t = jnp.exp(g_t - mloc_t)
        r_t = li_t - b_t
        cols = jnp.concatenate([b_t, w_t], axis=0).T
        for j in range(nhd):
            d, hd = j // HEADS, j % HEADS
            mask = (si <= ti) if d == 0 else (si >= ti)
            mkw_ref[bb, :, d * nqk_m + hd * DK_ML:d * nqk_m + (hd + 1) * DK_ML] = (
                k_ml[:, hd * DK_ML:(hd + 1) * DK_ML] * cols[:, nhd + j:nhd + j + 1]).astype(BF16)
            mbb_ref[bb, :, j * 128:(j + 1) * 128] = jnp.broadcast_to(cols[:, j:j + 1], (TILE, 128))
            for c in range(CHUNKS_PER_TILE):
                rows = slice(c * CHUNK, (c + 1) * CHUNK)
                a = jnp.where(mask, jnp.broadcast_to(r_t[j:j + 1, rows], (CHUNK, CHUNK)), -jnp.inf)
                cmax = jnp.max(a, axis=-1, keepdims=True)
                mp0_ref[bb, j // 2, rows, (j % 2) * CHUNK:(j % 2 + 1) * CHUNK] = jnp.exp(a - cmax)
                mub_ref[bb, rows, j * 128:(j + 1) * 128] = jnp.broadcast_to(cmax, (CHUNK, 128))

        gv_ref[bb] = pr[:, PR_GLA_V:PR_GLA_V + BRANCH_W].astype(BF16)
        gsg_ref[bb] = _silu(pr[:, PR_GLA_G:PR_GLA_G + BRANCH_W]).astype(BF16)
        a_hi, a_lo = _split2(pr[:, PR_GLA_A:PR_GLA_A + 2 * GLA_RANK])
        z = _dot(jnp.concatenate([a_hi, a_lo], axis=1), a2_ref[...]) + ab_ref[...]
        la = _log_sigmoid(z) * (1.0 / GLA_TAU)
        half_t = jnp.sum(la.reshape(2 * CHUNKS_PER_TILE, CHUNK // 2, 2 * nqk_s), axis=1).T
        q_gla = pr[:, PR_GLA_Q:PR_GLA_Q + nqk_s] * (DK_SMALL ** -0.5)
        k_gla = pr[:, PR_GLA_K:PR_GLA_K + nqk_s]
        for d in range(2):
            la_d = la[:, d * nqk_s:(d + 1) * nqk_s]
            la_hi, la_lo = _split2(la_d)
            tri = tri_ge if d == 0 else tri_le
            b = _dot(tri, la_hi) + _dot(tri, la_lo)
            tot = _chunk_total(la_d)
            gq_ref[bb, :, d * nqk_s:(d + 1) * nqk_s] = (q_gla * jnp.exp(b)).astype(BF16)
            gk_ref[bb, :, d * nqk_s:(d + 1) * nqk_s] = (k_gla * jnp.exp(-b)).astype(BF16)
            ge_ref[bb, :, d * nqk_s:(d + 1) * nqk_s] = (k_gla * jnp.exp(tot - b)).astype(BF16)
            ht = half_t[d * nqk_s:(d + 1) * nqk_s, :]
            for c in range(CHUNKS_PER_TILE):
                tot_col = ht[:, 2 * c:2 * c + 1] + ht[:, 2 * c + 1:2 * c + 2]
                gdec_ref[bb, c, d * nqk_s:(d + 1) * nqk_s, :] = jnp.broadcast_to(jnp.exp(tot_col), (nqk_s, 128))

        q_ret = rope(pr[:, PR_RET_Q:PR_RET_Q + nqk_s])
        k_ret = rope(pr[:, PR_RET_K:PR_RET_K + nqk_s] * (DK_SMALL ** -0.5))
        rv_ref[bb] = pr[:, PR_RET_V:PR_RET_V + BRANCH_W].astype(BF16)
        rsg_ref[bb] = _silu(pr[:, PR_RET_G:PR_RET_G + BRANCH_W]).astype(BF16)
        for d in range(2):
            rq_ref[bb, :, d * nqk_s:(d + 1) * nqk_s] = (q_ret * rdec_ref[3 * d]).astype(BF16)
            rk_ref[bb, :, d * nqk_s:(d + 1) * nqk_s] = (k_ret * rdec_ref[3 * d + 1]).astype(BF16)
            re_ref[bb, :, d * nqk_s:(d + 1) * nqk_s] = (k_ret * rdec_ref[3 * d + 2]).astype(BF16)


def _proj(tok, mods, g1, w, layer, a2s, abias, conv_w, gate_bias, cos_t, sin_t, dl_row):
    B, T, _ = tok.shape
    nt = T // TILE
    nqk_s = HEADS * DK_SMALL
    nqk_m = HEADS * DK_ML
    tile = lambda width: pl.BlockSpec((B, TILE, width), lambda i: (0, i, 0))
    const = lambda shape: pl.BlockSpec(shape, lambda i: (0,) * len(shape))
    per_halo = TILE // HALO
    n_halo = T // HALO
    tok_shape = lambda width, dt: jax.ShapeDtypeStruct((B, T, width), dt)
    return pl.pallas_call(
        functools.partial(_proj_kernel, nt, B),
        out_shape=(tok_shape(2 * nqk_s, BF16), tok_shape(2 * nqk_s, BF16), tok_shape(2 * nqk_s, BF16),
                   tok_shape(BRANCH_W, BF16), tok_shape(BRANCH_W, BF16),
                   jax.ShapeDtypeStruct((B, T // CHUNK, 2 * nqk_s, 128), F32),
                   tok_shape(nqk_m, BF16), tok_shape(nqk_m, BF16), tok_shape(2 * nqk_m, BF16),
                   tok_shape(BRANCH_W, BF16), tok_shape(BRANCH_W, BF16),
                   jax.ShapeDtypeStruct((B, HEADS, T, 2 * CHUNK), F32),
                   tok_shape(2 * HEADS * 128, F32), tok_shape(2 * HEADS * 128, F32),
                   jax.ShapeDtypeStruct((B, T // CHUNK, 4 * HEADS, 128), F32),
                   tok_shape(2 * nqk_s, BF16), tok_shape(2 * nqk_s, BF16), tok_shape(2 * nqk_s, BF16),
                   tok_shape(BRANCH_W, BF16), tok_shape(BRANCH_W, BF16)),
        grid=(nt,),
        in_specs=[tile(D_MODEL),
                  pl.BlockSpec((B, HALO, D_MODEL), lambda i: (0, jnp.maximum(i * per_halo - 1, 0), 0)),
                  pl.BlockSpec((B, HALO, D_MODEL),
                               lambda i: (0, jnp.minimum((i + 1) * per_halo, n_halo - 1), 0)),
                  const((B + 1, 6, D_MODEL)), const((1, D_MODEL)),
                  pl.BlockSpec((1, D_MODEL, W_PROJ), lambda i: (layer, 0, 0), pipeline_mode=pl.Buffered(1)),
                  const((4 * GLA_RANK, 2 * nqk_s)), const((1, 2 * nqk_s)),
                  const((3, 2 * nqk_m)), const((1, 4 * HEADS)),
                  pl.BlockSpec((TILE, nqk_s), lambda i: (i, 0)),
                  pl.BlockSpec((TILE, nqk_s), lambda i: (i, 0)),
                  const((1, 2 * nqk_s))],
        out_specs=(tile(2 * nqk_s), tile(2 * nqk_s), tile(2 * nqk_s), tile(BRANCH_W), tile(BRANCH_W),
                   pl.BlockSpec((B, CHUNKS_PER_TILE, 2 * nqk_s, 128), lambda i: (0, i, 0, 0)),
                   tile(nqk_m), tile(nqk_m), tile(2 * nqk_m), tile(BRANCH_W), tile(BRANCH_W),
                   pl.BlockSpec((B, HEADS, TILE, 2 * CHUNK), lambda i: (0, 0, i, 0)),
                   tile(2 * HEADS * 128), tile(2 * HEADS * 128),
                   pl.BlockSpec((B, CHUNKS_PER_TILE, 4 * HEADS, 128), lambda i: (0, i, 0, 0)),
                   tile(2 * nqk_s), tile(2 * nqk_s), tile(2 * nqk_s), tile(BRANCH_W), tile(BRANCH_W)),
        scratch_shapes=[pltpu.VMEM((TILE + 2 * HALO, 2 * nqk_m), F32),
                        pltpu.VMEM((6, TILE, nqk_s), F32)],
        compiler_params=_cparams(1), name="proj",
    )(tok, tok, tok, mods, g1, w, a2s, abias, conv_w, gate_bias, cos_t, sin_t, dl_row)


PAIRS = HEADS // 2
PAIR_K = 2 * DK_SMALL
PAIR_V = 2 * DV


def _mix_kernel(*refs):
    (gq_f, gq_b, gk_f, gk_b, ge_f, ge_b, gv_f, gv_b, gd_f, gd_b,
     rq_f, rq_b, rk_f, rk_b, re_f, re_b, rv_f, rv_b, rdl,
     mq_f, mq_b, mk_f, mk_b, mw_f, mw_b, mv_f, mv_b, mp_f, mp_b, mu_f, mu_b, mb_f, mb_b, mc_f, mc_b,
     of_ref, ob_ref, sg_ref, sr_ref, sm_ref, m_ref) = refs
    j = pl.program_id(1)

    @pl.when(j == 0)
    def _():
        sg_ref[...] = jnp.zeros_like(sg_ref)
        sr_ref[...] = jnp.zeros_like(sr_ref)
        sm_ref[...] = jnp.zeros_like(sm_ref)
        m_ref[...] = jnp.full_like(m_ref, M_INIT)

    row_head = lax.broadcasted_iota(jnp.int32, (2 * CHUNK, PAIR_V), 0) // CHUNK
    diag_kv = row_head == lax.broadcasted_iota(jnp.int32, (2 * CHUNK, PAIR_V), 1) // DV
    diag_kk = (lax.broadcasted_iota(jnp.int32, (2 * CHUNK, PAIR_K), 0) // CHUNK
               == lax.broadcasted_iota(jnp.int32, (2 * CHUNK, PAIR_K), 1) // DK_SMALL)
    t_idx = lax.broadcasted_iota(jnp.int32, (CHUNK, 2 * CHUNK), 0)
    s_idx = lax.broadcasted_iota(jnp.int32, (CHUNK, 2 * CHUNK), 1) % CHUNK
    causal = (s_idx <= t_idx, s_idx >= t_idx)
    ones_v = jnp.ones((CHUNK, DV), BF16)
    zeros_vext = jnp.zeros((CHUNK, PAIR_V), BF16)
    ret_dec = jnp.exp(float(CHUNK) * _log_sigmoid(rdl[...]))

    decay_sets = (
        ((gq_f, gk_f, ge_f, gv_f), (gq_b, gk_b, ge_b, gv_b), sg_ref, 0),
        ((rq_f, rk_f, re_f, rv_f), (rq_b, rk_b, re_b, rv_b), sr_ref, 2 * BRANCH_W),
    )
    ml_sets = ((mq_f, mk_f, mw_f, mv_f, mp_f, mu_f, mb_f, mc_f),
               (mq_b, mk_b, mw_b, mv_b, mp_b, mu_b, mb_b, mc_b))
    outs = (of_ref, ob_ref)

    def chunk_step(step, carry):
        work = []
        for d in range(2):
            c = step if d == 0 else CHUNKS_PER_TILE - 1 - step
            rows = pl.ds(c * CHUNK, CHUNK)
            o_ref = outs[d]
            for bi, (fset, bset, s_ref, base) in enumerate(decay_sets):
                q_r, k_r, e_r, v_r = fset if d == 0 else bset
                for p in range(PAIRS):
                    ks = slice(p * PAIR_K, (p + 1) * PAIR_K)
                    q2 = q_r[0, rows, ks]
                    k2 = k_r[0, rows, ks]
                    v2 = v_r[0, rows, p * PAIR_V:(p + 1) * PAIR_V]
                    k_bd = jnp.where(diag_kk, jnp.concatenate([k2, k2], axis=0), 0.0)
                    if bi == 0:
                        dec = (gd_f if d == 0 else gd_b)[0, c, ks, :]
                    else:
                        dec = ret_dec[d, ks, :]
                    upd = jnp.where(diag_kv, _dot_tn(e_r[0, rows, ks], v2), 0.0)
                    work.append(dict(
                        kind="decay", d=d, p=p, rows=rows, o_ref=o_ref, s_ref=s_ref, base=base,
                        q2=q2, v2=v2, dec=jnp.concatenate([dec, dec], axis=1), upd=upd,
                        att=_dot_nt(q2, k_bd)))
            q_r, k_r, w_r, v_r, p_r, u_r, b_r, c_r = ml_sets[d]
            cs = c_r[0, c]
            for p in range(PAIRS):
                q2 = q_r[0, rows, p * 2 * DK_ML:(p + 1) * 2 * DK_ML]
                k2 = k_r[0, rows, p * 2 * DK_ML:(p + 1) * 2 * DK_ML]
                k_bd = jnp.where(diag_kv, jnp.concatenate([k2, k2], axis=0), 0.0)
                att = _dot_nt(q2, k_bd)
                heads = []
                for hl in range(2):
                    hd = 2 * p + hl
                    vext = jnp.concatenate([v_r[0, rows, hd * DV:(hd + 1) * DV], ones_v], axis=1)
                    st = sm_ref[d, hd]
                    heads.append(dict(
                        hd=hd, st=st, vext=vext,
                        inter=_dot(q2[:, hl * DK_ML:(hl + 1) * DK_ML], st.astype(BF16)),
                        upd=_dot_tn(w_r[0, rows, hd * DK_ML:(hd + 1) * DK_ML], vext),
                        u=u_r[0, rows, hd * 128:(hd + 1) * 128],
                        b=b_r[0, rows, hd * 128:(hd + 1) * 128]))
                work.append(dict(kind="ml", d=d, p=p, rows=rows, o_ref=o_ref, att=att, heads=heads,
                                 p0=p_r[0, p, rows, :], cs=cs))
        nhd = 2 * HEADS
        for w in work:
            d, p, rows, o_ref = w["d"], w["p"], w["rows"], w["o_ref"]
            if w["kind"] == "decay":
                s_ref = w["s_ref"]
                s_prev = s_ref[d, p]
                att = jnp.where(causal[d], w["att"], 0.0).astype(BF16)
                v_bd = jnp.where(diag_kv, jnp.concatenate([w["v2"], w["v2"]], axis=0), 0.0)
                lhs = jnp.concatenate([att, w["q2"]], axis=1)
                rhs = jnp.concatenate([v_bd, s_prev.astype(BF16)], axis=0)
                o_ref[0, rows, w["base"] + p * PAIR_V:w["base"] + (p + 1) * PAIR_V] = _dot(lhs, rhs)
                s_ref[d, p] = w["dec"] * s_prev + w["upd"]
            else:
                sc = (w["att"] * w["p0"]).astype(BF16)
                for hl, hw in enumerate(w["heads"]):
                    hd = hw["hd"]
                    jj = d * HEADS + hd
                    vz = (jnp.concatenate([hw["vext"], zeros_vext], axis=0) if hl == 0
                          else jnp.concatenate([zeros_vext, hw["vext"]], axis=0))
                    intra = _dot(sc, vz)
                    m_prev = m_ref[jj:jj + 1, :]
                    u = hw["u"]
                    mx = jnp.maximum(u, m_prev)
                    f = jnp.exp(u - mx)
                    e = jnp.exp(m_prev - mx)
                    num = f * intra[:, 0:DV] + e * hw["inter"][:, 0:DV]
                    den = f * intra[:, DV:] + e * hw["inter"][:, DV:]
                    den = jnp.maximum(jnp.abs(den), jnp.exp(-(hw["b"] + mx)))
                    o_ref[0, rows, BRANCH_W + hd * DV:BRANCH_W + (hd + 1) * DV] = num / den
                    tot = w["cs"][jj:jj + 1, :]
                    mloc = w["cs"][nhd + jj:nhd + jj + 1, :]
                    m_new = jnp.maximum(tot + m_prev, mloc)
                    keep = jnp.exp(tot + m_prev - m_new)
                    gain = jnp.exp(mloc - m_new)
                    sm_ref[d, hd] = (jnp.concatenate([keep, keep], axis=1) * hw["st"]
                                     + jnp.concatenate([gain, gain], axis=1) * hw["upd"])
                    m_ref[jj:jj + 1, :] = m_new
        return carry

    for step in range(CHUNKS_PER_TILE):
        chunk_step(step, 0)


def _mix(gla, ret, ml, ret_dl_col):
    gq, gk, ge, gv, gdec = gla
    rq, rk, re, rv = ret
    mq, mk, mw, mv, mp, mu, mb, mc = ml
    B, T, _ = gv.shape
    nt = T // TILE
    nqk_s = HEADS * DK_SMALL
    nqk_m = HEADS * DK_ML

    def tf(j):
        return j

    def tb(j):
        return jnp.where(j == 0, 0, nt - j)

    def pair(width, lane_block_b=0):
        return [pl.BlockSpec((1, TILE, width), lambda b, j: (b, tf(j), 0)),
                pl.BlockSpec((1, TILE, width), lambda b, j: (b, tb(j), lane_block_b))]

    dec_pair = [pl.BlockSpec((1, CHUNKS_PER_TILE, nqk_s, 128), lambda b, j: (b, tf(j), 0, 0)),
                pl.BlockSpec((1, CHUNKS_PER_TILE, nqk_s, 128), lambda b, j: (b, tb(j), 1, 0))]
    p0_pair = [pl.BlockSpec((1, PAIRS, TILE, 2 * CHUNK), lambda b, j: (b, 0, tf(j), 0)),
               pl.BlockSpec((1, PAIRS, TILE, 2 * CHUNK), lambda b, j: (b, 1, tb(j), 0))]
    cs_pair = [pl.BlockSpec((1, CHUNKS_PER_TILE, 4 * HEADS, 128), lambda b, j: (b, tf(j), 0, 0)),
               pl.BlockSpec((1, CHUNKS_PER_TILE, 4 * HEADS, 128), lambda b, j: (b, tb(j), 0, 0))]
    decay_specs = pair(nqk_s, 1) + pair(nqk_s, 1) + pair(nqk_s, 1) + pair(BRANCH_W)
    in_specs = (decay_specs + dec_pair + decay_specs
                + [pl.BlockSpec((2, nqk_s, 128), lambda b, j: (0, 0, 0))]
                + pair(nqk_m) + pair(nqk_m) + pair(nqk_m, 1) + pair(BRANCH_W) + p0_pair
                + pair(HEADS * 128, 1) + pair(HEADS * 128, 1) + cs_pair)
    args = (gq, gq, gk, gk, ge, ge, gv, gv, gdec, gdec,
            rq, rq, rk, rk, re, re, rv, rv, ret_dl_col,
            mq, mq, mk, mk, mw, mw, mv, mv, mp, mp, mu, mu, mb, mb, mc, mc)
    return pl.pallas_call(
        _mix_kernel,
        out_shape=(jax.ShapeDtypeStruct((B, T, 3 * BRANCH_W), F32),) * 2,
        grid=(B, nt),
        in_specs=in_specs,
        out_specs=(pl.BlockSpec((1, TILE, 3 * BRANCH_W), lambda b, j: (b, tf(j), 0)),
                   pl.BlockSpec((1, TILE, 3 * BRANCH_W), lambda b, j: (b, tb(j), 0))),
        scratch_shapes=[pltpu.VMEM((2, PAIRS, PAIR_K, PAIR_V), F32),
                        pltpu.VMEM((2, PAIRS, PAIR_K, PAIR_V), F32),
                        pltpu.VMEM((2, HEADS, DK_ML, 2 * DV), F32),
                        pltpu.VMEM((2 * HEADS, 128), F32)],
        compiler_params=_cparams(2), name="mix",
    )(*args)


def _merge_kernel(x_ref, mod_ref, g_ref, wg_ref, of_ref, ob_ref, sgg_ref, sgm_ref, sgr_ref,
                  ng_ref, wb_ref, wo_ref, out_ref):
    x = x_ref[0]
    h = _norm_mod(x, g_ref[...], mod_ref[0, 0:1, :], mod_ref[0, 1:2, :]).astype(BF16)
    o = of_ref[0] + ob_ref[0]
    side = (sgg_ref, sgm_ref, sgr_ref)
    acc = jnp.zeros((TILE, D_MODEL), F32)
    for br in range(3):
        parts = []
        for hd in range(HEADS):
            oh = o[:, br * BRANCH_W + hd * DV:br * BRANCH_W + (hd + 1) * DV]
            if br > 0:
                oh = oh - jnp.mean(oh, axis=-1, keepdims=True)
            parts.append(oh * lax.rsqrt(jnp.mean(oh * oh, axis=-1, keepdims=True) + EPS))
        y = jnp.concatenate(parts, axis=1) * ng_ref[:, br * BRANCH_W:(br + 1) * BRANCH_W]
        y = (y * side[br][0]).astype(BF16)
        gate = _sigmoid(_dot(h, wg_ref[0, :, br * D_MODEL:(br + 1) * D_MODEL]))
        acc = acc + gate * _dot(y, wb_ref[0, br])
    out = _dot(acc.astype(BF16), wo_ref[0])
    out_ref[0] = x + mod_ref[0, 2:3, :] * out


def _merge(tok, mods, g1, wgate, o_f, o_b, sg_gla, sz_ml, sg_ret, norm_g, wb, wo, layer, skip):
    B, T, _ = tok.shape
    nt = T // TILE - skip
    tile = lambda width: pl.BlockSpec((1, TILE, width), lambda b, i: (b, i + skip, 0))
    const = lambda shape: pl.BlockSpec(shape, lambda b, i: (0,) * len(shape))
    per_layer = lambda shape: pl.BlockSpec((1,) + shape, lambda b, i: (layer,) + (0,) * len(shape))
    return pl.pallas_call(
        _merge_kernel,
        out_shape=jax.ShapeDtypeStruct((B, nt * TILE, D_MODEL), F32),
        grid=(B, nt),
        in_specs=[tile(D_MODEL),
                  pl.BlockSpec((1, 6, D_MODEL), lambda b, i: (_mod_row_index(b, i + skip), 0, 0)),
                  const((1, D_MODEL)), per_layer((D_MODEL, W_GATE)),
                  tile(3 * BRANCH_W), tile(3 * BRANCH_W),
                  tile(BRANCH_W), tile(BRANCH_W), tile(BRANCH_W),
                  const((1, 3 * BRANCH_W)), per_layer((3, BRANCH_W, D_MODEL)),
                  per_layer((D_MODEL, D_MODEL))],
        out_specs=pl.BlockSpec((1, TILE, D_MODEL), lambda b, i: (b, i, 0)),
        compiler_params=_cparams(2), name="merge",
    )(tok, mods, g1, wgate, o_f, o_b, sg_gla, sz_ml, sg_ret, norm_g, wb, wo)


def _mlp_kernel(final, x_ref, mod_ref, g_ref, wu_ref, wd_ref, fg_ref, out_ref):
    x = x_ref[0]
    h = _norm_mod(x, g_ref[...], mod_ref[0, 3:4, :], mod_ref[0, 4:5, :]).astype(BF16)
    u = jnp.maximum(_dot(h, wu_ref[0]), 0.0)
    y = x + mod_ref[0, 5:6, :] * _dot((u * u).astype(BF16), wd_ref[0])
    if final:
        y = y * lax.rsqrt(jnp.mean(y * y, axis=-1, keepdims=True) + EPS) * fg_ref[...]
    out_ref[0] = y


def _mlp(tok, mods, g2, wu, wd, layer, final_g, first_tile, final):
    B, T, _ = tok.shape
    const = lambda shape: pl.BlockSpec(shape, lambda b, i: (0,) * len(shape))
    per_layer = lambda shape: pl.BlockSpec((1,) + shape, lambda b, i: (layer,) + (0,) * len(shape))
    return pl.pallas_call(
        functools.partial(_mlp_kernel, final),
        out_shape=jax.ShapeDtypeStruct((B, T, D_MODEL), F32),
        grid=(B, T // TILE),
        in_specs=[pl.BlockSpec((1, TILE, D_MODEL), lambda b, i: (b, i, 0)),
                  pl.BlockSpec((1, 6, D_MODEL), lambda b, i: (_mod_row_index(b, i + first_tile), 0, 0)),
                  const((1, D_MODEL)), per_layer((D_MODEL, D_FF)), per_layer((D_FF, D_MODEL)),
                  const((1, D_MODEL))],
        out_specs=pl.BlockSpec((1, TILE, D_MODEL), lambda b, i: (b, i, 0)),
        compiler_params=_cparams(2), name="mlp",
    )(tok, mods, g2, wu, wd, final_g)


def kernel(x, c, ctx, c_ctx, norm1_g, norm2_g, w_ada, b_ada, w_in, gla_a2, gla_a_bias, gla_norm_g,
           ml_conv, ml_i_bias, ml_f_bias, ml_norm_g, ret_decay_logit, ret_norm_g, w_branch, w_o,
           w_up, w_down, final_g):
    B, seq, _ = x.shape
    ctx_len = ctx.shape[1]
    assert B == 2 and ctx_len == TILE and seq % TILE == 0 and seq % GRID_W == 0
    tok = jnp.concatenate([ctx, x], axis=1)

    c_rows = jnp.concatenate([c, c_ctx[None, :], jnp.zeros((8 - B - 1, D_MODEL), F32)], axis=0)
    mods_all = _ada(c_rows, w_ada, b_ada)[:, 0:3].reshape(DEPTH, 3, 6, D_MODEL)

    cos_t, sin_t = _rope_tables(seq, ctx_len)

    o0 = W_GLA
    o1 = o0 + W_ML
    o2 = o1 + W_RET
    nqk_s = HEADS * DK_SMALL
    nqk_m = HEADS * DK_ML
    n_sg = 2 * nqk_s + 2 * BRANCH_W
    w_proj = jnp.concatenate([
        w_in[:, :, o0:o0 + 2 * nqk_m],
        w_in[:, :, 0:n_sg],
        w_in[:, :, o0 + 2 * nqk_m:o0 + 2 * nqk_m + 2 * BRANCH_W],
        w_in[:, :, o1:o2],
        w_in[:, :, n_sg:o0],
        w_in[:, :, o1 - 4 * HEADS:o1],
    ], axis=2).astype(BF16)
    w_gate = w_in[:, :, o2:].astype(BF16)
    w_branch_b = w_branch.astype(BF16)
    w_o_b = w_o.astype(BF16)
    w_up_b = w_up.astype(BF16)
    w_down_b = w_down.astype(BF16)
    for l in range(DEPTH):
        last = l == DEPTH - 1
        skip = 1 if last else 0
        mods = mods_all[l]
        g1 = norm1_g[l][None, :]
        g2 = norm2_g[l][None, :]
        zeros = jnp.zeros((GLA_RANK, nqk_s), F32)
        a2blk = jnp.concatenate([jnp.concatenate([gla_a2[l, 0], zeros], axis=1),
                                 jnp.concatenate([zeros, gla_a2[l, 1]], axis=1)], axis=0)
        a2s = jnp.concatenate([a2blk, a2blk], axis=0).astype(BF16)
        abias = gla_a_bias[l].reshape(1, 2 * nqk_s)
        gate_bias = jnp.concatenate([ml_i_bias[l].reshape(1, 2 * HEADS),
                                     ml_f_bias[l].reshape(1, 2 * HEADS)], axis=1)
        dl = ret_decay_logit[l]
        dl_row = jnp.repeat(dl, DK_SMALL, axis=1).reshape(1, 2 * nqk_s)
        dl_col = jnp.broadcast_to(jnp.repeat(dl, DK_SMALL, axis=1)[:, :, None], (2, nqk_s, 128))

        (gq, gk, ge, gv, sg_gla, gdec, mq, mk, mw, mv, sz_ml, mp, mu, mb, mc,
         rq, rk, re, rv, sg_ret) = _proj(tok, mods, g1, w_proj, l, a2s, abias, ml_conv[l], gate_bias,
                                         cos_t, sin_t, dl_row)
        o_f, o_b = _mix((gq, gk, ge, gv, gdec), (rq, rk, re, rv), (mq, mk, mw, mv, mp, mu, mb, mc), dl_col)
        norm_g = jnp.concatenate([gla_norm_g[l], ml_norm_g[l], ret_norm_g[l]])[None, :]
        tok = _merge(tok, mods, g1, w_gate, o_f, o_b, sg_gla, sz_ml, sg_ret, norm_g,
                     w_branch_b, w_o_b, l, skip)
        tok = _mlp(tok, mods, g2, w_up_b, w_down_b, l, final_g[None, :], skip, last)
    return tok
```

```python
import functools

import jax
import jax.numpy as jnp
from jax import lax
from jax.experimental import pallas as pl
from jax.experimental.pallas import tpu as pltpu

F32 = jnp.float32
BF16 = jnp.bfloat16

D_MODEL = 1024
DEPTH = 2
GRID_W = 64
CHUNK = 64
EPS = 1e-6
M_INIT = -1e30
HEADS = 4
DK_SMALL = 64
DK_ML = 128
DV = 128
BRANCH_W = HEADS * DV
GLA_RANK = 16
GLA_TAU = 16.0
ROPE_BASE = 10000.0
D_FF = 4 * D_MODEL
LANES = 128

TILE = 256
CHUNKS_PER_TILE = TILE // CHUNK
HALO = 8
VMEM_LIMIT = 56 * 1024 * 1024
VMEM_LIMIT_PROJ = 60 * 1024 * 1024

NQK_S = HEADS * DK_SMALL
NQK_M = HEADS * DK_ML
NHD = 2 * HEADS
W_GLA = 2 * NQK_S + 2 * BRANCH_W + 2 * GLA_RANK
W_ML = 2 * NQK_M + 2 * BRANCH_W + 2 * NHD
W_RET = 2 * NQK_S + 2 * BRANCH_W
W_GATE = 3 * D_MODEL

PW_MLQK = 0
PW_MLR = PW_MLQK + 2 * NQK_M
PW_MLR_WIDTH = 2 * BRANCH_W + LANES
PW_GR = PW_MLR + PW_MLR_WIDTH
PW_GR_WIDTH = 2 * (2 * NQK_S + 2 * BRANCH_W) + LANES
W_PROJ = PW_GR + PW_GR_WIDTH
MLR_V, MLR_Z, MLR_GATES = 0, BRANCH_W, 2 * BRANCH_W
GR_GLA_Q, GR_GLA_K, GR_GLA_V, GR_GLA_G = 0, NQK_S, 2 * NQK_S, 2 * NQK_S + BRANCH_W
GR_RET_Q = 2 * NQK_S + 2 * BRANCH_W
GR_RET_K, GR_RET_V, GR_RET_G = GR_RET_Q + NQK_S, GR_RET_Q + 2 * NQK_S, GR_RET_Q + 2 * NQK_S + BRANCH_W
GR_GLA_A = 2 * GR_RET_Q

MC_B, MC_W, MC_U = 0, NHD, 2 * NHD
MC_WIDTH = 3 * NHD

PAIRS = HEADS // 2
PAIR_K = 2 * DK_SMALL
PAIR_V = 2 * DV


def _cparams(n_axes, vmem_limit=VMEM_LIMIT):
    return pltpu.CompilerParams(dimension_semantics=("arbitrary",) * n_axes,
                                vmem_limit_bytes=vmem_limit)


def _log_sigmoid(z):
    return jnp.minimum(z, 0.0) - jnp.log(1.0 + jnp.exp(-jnp.abs(z)))


def _sigmoid(z):
    return 1.0 / (1.0 + jnp.exp(-z))


def _silu(z):
    return z * _sigmoid(z)


def _norm_mod(x, g, shift, scale):
    y = x * lax.rsqrt(jnp.mean(x * x, axis=-1, keepdims=True) + EPS) * g
    return y * (1.0 + scale) + shift


def _dot(a, b):
    return jnp.dot(a, b, preferred_element_type=F32)


def _dot_nt(a, b):
    return lax.dot_general(a, b, (((1,), (1,)), ((), ())), preferred_element_type=F32)


def _dot_tn(a, b):
    return lax.dot_general(a, b, (((0,), (0,)), ((), ())), preferred_element_type=F32)


def _split2(x):
    hi = x.astype(BF16)
    return hi, (x - hi.astype(F32)).astype(BF16)


def _chunk_tri(lower):
    r = lax.broadcasted_iota(jnp.int32, (TILE, TILE), 0)
    c = lax.broadcasted_iota(jnp.int32, (TILE, TILE), 1)
    same = (r // CHUNK) == (c // CHUNK)
    tri = (c <= r) if lower else (c >= r)
    return jnp.where(same & tri, 1.0, 0.0).astype(BF16)


def _chunk_total(x):
    w = x.shape[-1]
    tot = jnp.sum(x.reshape(CHUNKS_PER_TILE, CHUNK, w), axis=1, keepdims=True)
    return jnp.broadcast_to(tot, (CHUNKS_PER_TILE, CHUNK, w)).reshape(TILE, w)


def _mod_row_index(b, i):
    return jnp.where(i == 0, 2, b)


def _ada_kernel(c_ref, w_ref, b_ref, o_ref):
    s = _silu(c_ref[...])
    o_ref[0] = jnp.dot(s, w_ref[0], precision=lax.Precision.HIGHEST,
                       preferred_element_type=F32) + b_ref[0]


def _ada(c_rows, w_ada, b_ada):
    tn = 1536
    return pl.pallas_call(
        _ada_kernel,
        out_shape=jax.ShapeDtypeStruct((DEPTH, 8, 6 * D_MODEL), F32),
        grid=(DEPTH, 6 * D_MODEL // tn),
        in_specs=[pl.BlockSpec((8, D_MODEL), lambda l, n: (0, 0)),
                  pl.BlockSpec((1, D_MODEL, tn), lambda l, n: (l, 0, n)),
                  pl.BlockSpec((1, 1, tn), lambda l, n: (l, 0, n))],
        out_specs=pl.BlockSpec((1, 8, tn), lambda l, n: (l, 0, n)),
        compiler_params=_cparams(2), name="ada",
    )(c_rows, w_ada, b_ada.reshape(DEPTH, 1, 6 * D_MODEL))


def _rope_kernel(ang_ref, cos_ref, sin_ref, nsin_ref):
    a = ang_ref[...]
    s = jnp.sin(a)
    cos_ref[...] = jnp.cos(a)
    sin_ref[...] = s
    nsin_ref[...] = -s


def _rope_tables(seq, ctx_len):
    rows = seq // GRID_W
    nf = DK_SMALL // 4
    inv = ROPE_BASE ** (-jnp.arange(nf, dtype=F32) / nf)
    ang = jnp.concatenate([jnp.arange(rows, dtype=F32)[:, None] * inv,
                           jnp.arange(GRID_W, dtype=F32)[:, None] * inv], axis=0)
    n = rows + GRID_W
    cos, sin, nsin = pl.pallas_call(
        _rope_kernel,
        out_shape=(jax.ShapeDtypeStruct((n, nf), F32),) * 3,
        name="rope_tables",
    )(ang)

    def per_token(first, second):
        def half(t):
            r = jnp.broadcast_to(t[:rows, None, :], (rows, GRID_W, nf))
            c = jnp.broadcast_to(t[None, rows:, :], (rows, GRID_W, nf))
            return jnp.concatenate([r, c], axis=-1)
        head = jnp.concatenate([half(first), half(second)], axis=-1).reshape(seq, DK_SMALL)
        return jnp.tile(head, (1, HEADS))

    ones = jnp.ones((ctx_len, NQK_S), F32)
    cos_t = jnp.concatenate([ones, per_token(cos, cos)], axis=0)
    sin_t = jnp.concatenate([0.0 * ones, per_token(nsin, sin)], axis=0)
    return cos_t, sin_t


def _proj_kernel(nt, xa_ref, xap_ref, xan_ref, xb_ref, xbp_ref, xbn_ref, x0_ref,
                 mod_ref, g_ref, w_ref, a2_ref, ab_ref, cw_ref, gb_ref, cos_ref, sin_ref, dl_ref,
                 gq_ref, gk_ref, ge_ref, gv_ref, gsg_ref, gdec_ref,
                 mq_ref, mk_ref, mkw_ref, mv_ref, msz_ref, mp0_ref, mcol_ref, mcs_ref,
                 rq_ref, rk_ref, re_ref, rv_ref, rsg_ref,
                 ext0_ref, ext1_ref, prml0_ref, prml1_ref, prgr0_ref, prgr1_ref, rdec_ref):
    i = pl.program_id(0)
    ext_refs = (ext0_ref, ext1_ref)
    prml_refs = (prml0_ref, prml1_ref)
    prgr_refs = (prgr0_ref, prgr1_ref)
    tri_ge = _chunk_tri(True)
    tri_le = _chunk_tri(False)
    row8 = lax.broadcasted_iota(jnp.int32, (NHD, TILE), 0)
    lane8 = lax.broadcasted_iota(jnp.int32, (NHD, TILE), 1)
    lane_chunk = lane8 // CHUNK
    lane_in_chunk = lane8 % CHUNK
    ti = lax.broadcasted_iota(jnp.int32, (CHUNK, CHUNK), 0)
    si = lax.broadcasted_iota(jnp.int32, (CHUNK, CHUNK), 1)
    lane = lax.broadcasted_iota(jnp.int32, (TILE, NQK_S), 1)
    first_half = (lane % DK_SMALL) < (DK_SMALL // 2)
    ext_row = lax.broadcasted_iota(jnp.int32, (TILE + 2 * HALO, 1), 0)
    cos = cos_ref[...]
    sin = sin_ref[...]
    cw = cw_ref[...]

    def project_stages(x, x_prev, x_next, tile_idx, mod_row, slot):
        st = {}

        def mlstm_qk():
            mod = mod_ref[mod_row]
            x_all = jnp.concatenate([x_prev, x, x_next], axis=0)
            h_all = _norm_mod(x_all, g_ref[...], mod[0:1, :], mod[1:2, :])
            st["h"] = h_all[HALO:HALO + TILE, :].astype(BF16)
            prev_ok = tile_idx >= 2
            next_ok = jnp.logical_and(tile_idx >= 1, tile_idx <= nt - 2)
            dead = jnp.logical_or(jnp.logical_and(ext_row < HALO, jnp.logical_not(prev_ok)),
                                  jnp.logical_and(ext_row >= TILE + HALO, jnp.logical_not(next_ok)))
            ext = _dot(h_all.astype(BF16), w_ref[0, :, PW_MLQK:PW_MLR])
            ext_refs[slot][...] = jnp.where(dead, 0.0, ext)

        def mlstm_rest():
            prml_refs[slot][...] = _dot(st["h"], w_ref[0, :, PW_MLR:PW_GR])

        def gla():
            prgr_refs[slot][:, 0:GR_RET_Q] = _dot(st["h"], w_ref[0, :, PW_GR:PW_GR + GR_RET_Q])

        def ret():
            prgr_refs[slot][:, GR_RET_Q:] = _dot(st["h"], w_ref[0, :, PW_GR + GR_RET_Q:])

        return [mlstm_qk, mlstm_rest, gla, ret]

    def project(*args):
        for stage in project_stages(*args):
            stage()

    @pl.when(i == 0)
    def _():
        lg = _log_sigmoid(dl_ref[...])
        t_in_chunk = (lax.broadcasted_iota(jnp.int32, (TILE, NQK_S), 0) % CHUNK).astype(F32)
        for d in range(2):
            lg_d = lg[:, d * NQK_S:(d + 1) * NQK_S]
            steps = (t_in_chunk + 1.0) if d == 0 else (float(CHUNK) - t_in_chunk)
            b = steps * lg_d
            rdec_ref[3 * d] = jnp.exp(b)
            rdec_ref[3 * d + 1] = jnp.exp(-b)
            rdec_ref[3 * d + 2] = jnp.exp(float(CHUNK) * lg_d - b)
        halo0 = jnp.zeros((HALO, D_MODEL), F32)
        project(x0_ref[0], halo0, halo0, 0, 2, 0)

    def rope(x):
        swapped = jnp.where(first_half, pltpu.roll(x, NQK_S - DK_SMALL // 2, 1),
                            pltpu.roll(x, DK_SMALL // 2, 1))
        return x * cos + swapped * sin

    def seg_cummax(x, reverse):
        sh = 1
        while sh < CHUNK:
            if reverse:
                y, ok = pltpu.roll(x, TILE - sh, 1), lane_in_chunk < CHUNK - sh
            else:
                y, ok = pltpu.roll(x, sh, 1), lane_in_chunk >= sh
            x = jnp.where(ok, jnp.maximum(x, y), x)
            sh *= 2
        return x

    def prepare_stages(slot, bb):
        ext_ref, prml_ref, prgr_ref = ext_refs[slot], prml_refs[slot], prgr_refs[slot]
        st = {}

        def mlstm_conv():
            qk = _silu(cw[0:1, :] * ext_ref[HALO - 1:HALO - 1 + TILE, :]
                       + cw[1:2, :] * ext_ref[HALO:HALO + TILE, :]
                       + cw[2:3, :] * ext_ref[HALO + 1:HALO + 1 + TILE, :])
            mq_ref[bb] = qk[:, 0:NQK_M].astype(BF16)
            st["k_ml"] = qk[:, NQK_M:] * (DK_ML ** -0.5)
            mk_ref[bb] = st["k_ml"].astype(BF16)

        def mlstm_gates():
            mlstm_gates_body(prml_ref, st["k_ml"], bb)

        def gla():
            gla_body(prgr_ref, bb)

        def ret():
            ret_body(prgr_ref, bb)

        return [mlstm_conv, mlstm_gates, gla, ret]

    def mlstm_gates_body(prml_ref, k_ml, bb):
        mv_ref[bb] = prml_ref[:,MLR_V:MLR_V + BRANCH_W].astype(BF16)
        msz_ref[bb] = _silu(prml_ref[:,MLR_Z:MLR_Z + BRANCH_W]).astype(BF16)
        gates_t = (prml_ref[:,MLR_GATES:MLR_GATES + 2 * NHD] + gb_ref[...]).T
        li_t = gates_t[0:NHD]
        lf_t = _log_sigmoid(gates_t[NHD:])
        lf_hi, lf_lo = _split2(lf_t)
        lf2 = jnp.concatenate([lf_hi, lf_lo], axis=0)
        cum_f = _dot(lf2, tri_le)
        cum_b = _dot(lf2, tri_ge)
        b_t = jnp.where(row8 < HEADS, cum_f[0:NHD] + cum_f[NHD:], cum_b[0:NHD] + cum_b[NHD:])
        tot_t = jnp.zeros((NHD, TILE), F32)
        mloc_t = jnp.zeros((NHD, TILE), F32)
        for c in range(CHUNKS_PER_TILE):
            tot_c = jnp.sum(lf_t[:, c * CHUNK:(c + 1) * CHUNK], axis=-1, keepdims=True)
            tot_t = jnp.where(lane_chunk == c, tot_c, tot_t)
            mcs_ref[bb, c, 0:NHD, :] = jnp.broadcast_to(tot_c, (NHD, LANES))
        g_t = tot_t - b_t + li_t
        for c in range(CHUNKS_PER_TILE):
            mloc_c = jnp.max(g_t[:, c * CHUNK:(c + 1) * CHUNK], axis=-1, keepdims=True)
            mloc_t = jnp.where(lane_chunk == c, mloc_c, mloc_t)
            mcs_ref[bb, c, NHD:2 * NHD, :] = jnp.broadcast_to(mloc_c, (NHD, LANES))
        w_t = jnp.exp(g_t - mloc_t)
        r_t = li_t - b_t
        u_t = jnp.where(row8 < HEADS, seg_cummax(r_t, False), seg_cummax(r_t, True))
        cols = jnp.concatenate([b_t, w_t, u_t], axis=0).T
        mcol_ref[bb] = cols
        for j in range(NHD):
            d, hd = j // HEADS, j % HEADS
            mask = (si <= ti) if d == 0 else (si >= ti)
            mkw_ref[bb, :, d * NQK_M + hd * DK_ML:d * NQK_M + (hd + 1) * DK_ML] = (
                k_ml[:, hd * DK_ML:(hd + 1) * DK_ML] * cols[:, MC_W + j:MC_W + j + 1]).astype(BF16)
            for c in range(CHUNKS_PER_TILE):
                rows = slice(c * CHUNK, (c + 1) * CHUNK)
                a = r_t[j:j + 1, rows] - cols[rows, MC_U + j:MC_U + j + 1]
                mp0_ref[bb, j // 2, rows, (j % 2) * CHUNK:(j % 2 + 1) * CHUNK] = jnp.where(mask, jnp.exp(a), 0.0)

    def gla_body(prgr_ref, bb):
        gv_ref[bb] = prgr_ref[:,GR_GLA_V:GR_GLA_V + BRANCH_W].astype(BF16)
        gsg_ref[bb] = _silu(prgr_ref[:,GR_GLA_G:GR_GLA_G + BRANCH_W]).astype(BF16)
        a_hi, a_lo = _split2(prgr_ref[:,GR_GLA_A:GR_GLA_A + 2 * GLA_RANK])
        z = _dot(jnp.concatenate([a_hi, a_lo], axis=1), a2_ref[...]) + ab_ref[...]
        la = _log_sigmoid(z) * (1.0 / GLA_TAU)
        half_t = jnp.sum(la.reshape(2 * CHUNKS_PER_TILE, CHUNK // 2, 2 * NQK_S), axis=1).T
        q_gla = prgr_ref[:,GR_GLA_Q:GR_GLA_Q + NQK_S] * (DK_SMALL ** -0.5)
        k_gla = prgr_ref[:,GR_GLA_K:GR_GLA_K + NQK_S]
        for d in range(2):
            la_d = la[:, d * NQK_S:(d + 1) * NQK_S]
            la_hi, la_lo = _split2(la_d)
            tri = tri_ge if d == 0 else tri_le
            b = _dot(tri, la_hi) + _dot(tri, la_lo)
            tot = _chunk_total(la_d)
            gq_ref[bb, :, d * NQK_S:(d + 1) * NQK_S] = (q_gla * jnp.exp(b)).astype(BF16)
            gk_ref[bb, :, d * NQK_S:(d + 1) * NQK_S] = (k_gla * jnp.exp(-b)).astype(BF16)
            ge_ref[bb, :, d * NQK_S:(d + 1) * NQK_S] = (k_gla * jnp.exp(tot - b)).astype(BF16)
            ht = half_t[d * NQK_S:(d + 1) * NQK_S, :]
            for c in range(CHUNKS_PER_TILE):
                tot_col = ht[:, 2 * c:2 * c + 1] + ht[:, 2 * c + 1:2 * c + 2]
                gdec_ref[bb, c, d * NQK_S:(d + 1) * NQK_S, :] = jnp.broadcast_to(jnp.exp(tot_col), (NQK_S, LANES))

    def ret_body(prgr_ref, bb):
        q_ret = rope(prgr_ref[:,GR_RET_Q:GR_RET_Q + NQK_S])
        k_ret = rope(prgr_ref[:,GR_RET_K:GR_RET_K + NQK_S] * (DK_SMALL ** -0.5))
        rv_ref[bb] = prgr_ref[:,GR_RET_V:GR_RET_V + BRANCH_W].astype(BF16)
        rsg_ref[bb] = _silu(prgr_ref[:,GR_RET_G:GR_RET_G + BRANCH_W]).astype(BF16)
        for d in range(2):
            rq_ref[bb, :, d * NQK_S:(d + 1) * NQK_S] = (q_ret * rdec_ref[3 * d]).astype(BF16)
            rk_ref[bb, :, d * NQK_S:(d + 1) * NQK_S] = (k_ret * rdec_ref[3 * d + 1]).astype(BF16)
            re_ref[bb, :, d * NQK_S:(d + 1) * NQK_S] = (k_ret * rdec_ref[3 * d + 2]).astype(BF16)

    def interleave(matrix_stages, vector_stages):
        for m_stage, v_stage in zip(matrix_stages, vector_stages):
            m_stage()
            v_stage()

    next_tile = jnp.minimum(i + 1, nt - 1)
    interleave(project_stages(xa_ref[0], xap_ref[0], xan_ref[0], i, jnp.where(i == 0, 2, 1), 1),
               prepare_stages(0, 0))
    interleave(project_stages(xb_ref[0], xbp_ref[0], xbn_ref[0], next_tile, 0, 0),
               prepare_stages(1, 1))


def _proj(tok, mods, g1, w, layer, a2s, abias, conv_w, gate_bias, cos_t, sin_t, dl_row):
    B, T, _ = tok.shape
    nt = T // TILE
    tile = lambda width: pl.BlockSpec((B, TILE, width), lambda i: (0, i, 0))
    const = lambda shape: pl.BlockSpec(shape, lambda i: (0,) * len(shape))
    per_halo = TILE // HALO
    n_halo = T // HALO
    nxt = lambda i: jnp.minimum(i + 1, nt - 1)
    prev_halo = lambda t: jnp.maximum(t * per_halo - 1, 0)
    next_halo = lambda t: jnp.minimum((t + 1) * per_halo, n_halo - 1)
    tok_shape = lambda width, dt: jax.ShapeDtypeStruct((B, T, width), dt)
    return pl.pallas_call(
        functools.partial(_proj_kernel, nt),
        out_shape=(tok_shape(2 * NQK_S, BF16), tok_shape(2 * NQK_S, BF16), tok_shape(2 * NQK_S, BF16),
                   tok_shape(BRANCH_W, BF16), tok_shape(BRANCH_W, BF16),
                   jax.ShapeDtypeStruct((B, T // CHUNK, 2 * NQK_S, LANES), F32),
                   tok_shape(NQK_M, BF16), tok_shape(NQK_M, BF16), tok_shape(2 * NQK_M, BF16),
                   tok_shape(BRANCH_W, BF16), tok_shape(BRANCH_W, BF16),
                   jax.ShapeDtypeStruct((B, 2 * PAIRS, T, 2 * CHUNK), F32),
                   tok_shape(MC_WIDTH, F32),
                   jax.ShapeDtypeStruct((B, T // CHUNK, 2 * NHD, LANES), F32),
                   tok_shape(2 * NQK_S, BF16), tok_shape(2 * NQK_S, BF16), tok_shape(2 * NQK_S, BF16),
                   tok_shape(BRANCH_W, BF16), tok_shape(BRANCH_W, BF16)),
        grid=(nt,),
        in_specs=[pl.BlockSpec((1, TILE, D_MODEL), lambda i: (1, i, 0)),
                  pl.BlockSpec((1, HALO, D_MODEL), lambda i: (1, prev_halo(i), 0)),
                  pl.BlockSpec((1, HALO, D_MODEL), lambda i: (1, next_halo(i), 0)),
                  pl.BlockSpec((1, TILE, D_MODEL), lambda i: (0, nxt(i), 0)),
                  pl.BlockSpec((1, HALO, D_MODEL), lambda i: (0, prev_halo(nxt(i)), 0)),
                  pl.BlockSpec((1, HALO, D_MODEL), lambda i: (0, next_halo(nxt(i)), 0)),
                  pl.BlockSpec((1, TILE, D_MODEL), lambda i: (0, 0, 0)),
                  const((B + 1, 6, D_MODEL)), const((1, D_MODEL)),
                  pl.BlockSpec((1, D_MODEL, W_PROJ), lambda i: (layer, 0, 0), pipeline_mode=pl.Buffered(1)),
                  const((4 * GLA_RANK, 2 * NQK_S)), const((1, 2 * NQK_S)),
                  const((3, 2 * NQK_M)), const((1, 2 * NHD)),
                  pl.BlockSpec((TILE, NQK_S), lambda i: (i, 0)),
                  pl.BlockSpec((TILE, NQK_S), lambda i: (i, 0)),
                  const((1, 2 * NQK_S))],
        out_specs=(tile(2 * NQK_S), tile(2 * NQK_S), tile(2 * NQK_S), tile(BRANCH_W), tile(BRANCH_W),
                   pl.BlockSpec((B, CHUNKS_PER_TILE, 2 * NQK_S, LANES), lambda i: (0, i, 0, 0)),
                   tile(NQK_M), tile(NQK_M), tile(2 * NQK_M), tile(BRANCH_W), tile(BRANCH_W),
                   pl.BlockSpec((B, 2 * PAIRS, TILE, 2 * CHUNK), lambda i: (0, 0, i, 0)),
                   tile(MC_WIDTH),
                   pl.BlockSpec((B, CHUNKS_PER_TILE, 2 * NHD, LANES), lambda i: (0, i, 0, 0)),
                   tile(2 * NQK_S), tile(2 * NQK_S), tile(2 * NQK_S), tile(BRANCH_W), tile(BRANCH_W)),
        scratch_shapes=[pltpu.VMEM((TILE + 2 * HALO, 2 * NQK_M), F32)] * 2
                       + [pltpu.VMEM((TILE, PW_MLR_WIDTH), F32)] * 2
                       + [pltpu.VMEM((TILE, PW_GR_WIDTH), F32)] * 2
                       + [pltpu.VMEM((6, TILE, NQK_S), F32)],
        compiler_params=_cparams(1, VMEM_LIMIT_PROJ), name="proj",
    )(tok, tok, tok, tok, tok, tok, tok, mods, g1, w, a2s, abias, conv_w, gate_bias, cos_t, sin_t, dl_row)


def _mix_kernel(*refs):
    (gq_f, gq_b, gk_f, gk_b, ge_f, ge_b, gv_f, gv_b, gd_f, gd_b,
     rq_f, rq_b, rk_f, rk_b, re_f, re_b, rv_f, rv_b, rdl,
     mq_f, mq_b, mk_f, mk_b, mw_f, mw_b, mv_f, mv_b, mp_f, mp_b, mcol_f, mcol_b, mc_f, mc_b,
     of_ref, ob_ref, sg_ref, sr_ref, sm_ref, m_ref) = refs
    j = pl.program_id(1)

    @pl.when(j == 0)
    def _():
        sg_ref[...] = jnp.zeros_like(sg_ref)
        sr_ref[...] = jnp.zeros_like(sr_ref)
        sm_ref[...] = jnp.zeros_like(sm_ref)
        m_ref[...] = jnp.full_like(m_ref, M_INIT)

    row_head = lax.broadcasted_iota(jnp.int32, (2 * CHUNK, PAIR_V), 0) // CHUNK
    diag_kv = row_head == lax.broadcasted_iota(jnp.int32, (2 * CHUNK, PAIR_V), 1) // DV
    diag_kk = (lax.broadcasted_iota(jnp.int32, (2 * CHUNK, PAIR_K), 0) // CHUNK
               == lax.broadcasted_iota(jnp.int32, (2 * CHUNK, PAIR_K), 1) // DK_SMALL)
    t_idx = lax.broadcasted_iota(jnp.int32, (CHUNK, 2 * CHUNK), 0)
    s_idx = lax.broadcasted_iota(jnp.int32, (CHUNK, 2 * CHUNK), 1) % CHUNK
    causal = (s_idx <= t_idx, s_idx >= t_idx)
    ones_v = jnp.ones((CHUNK, DV), BF16)
    zeros_vext = jnp.zeros((CHUNK, PAIR_V), BF16)
    ret_dec = jnp.exp(float(CHUNK) * _log_sigmoid(rdl[...]))

    decay_sets = (
        ((gq_f, gk_f, ge_f, gv_f), (gq_b, gk_b, ge_b, gv_b), sg_ref, 0),
        ((rq_f, rk_f, re_f, rv_f), (rq_b, rk_b, re_b, rv_b), sr_ref, 2 * BRANCH_W),
    )
    ml_sets = ((mq_f, mk_f, mw_f, mv_f, mp_f, mcol_f, mc_f),
               (mq_b, mk_b, mw_b, mv_b, mp_b, mcol_b, mc_b))
    outs = (of_ref, ob_ref)

    def chunk_step(step):
        work = []
        for d in range(2):
            c = step if d == 0 else CHUNKS_PER_TILE - 1 - step
            rows = pl.ds(c * CHUNK, CHUNK)
            o_ref = outs[d]
            for bi, (fset, bset, s_ref, base) in enumerate(decay_sets):
                q_r, k_r, e_r, v_r = fset if d == 0 else bset
                for p in range(PAIRS):
                    ks = slice(p * PAIR_K, (p + 1) * PAIR_K)
                    q2 = q_r[0, rows, ks]
                    k2 = k_r[0, rows, ks]
                    v2 = v_r[0, rows, p * PAIR_V:(p + 1) * PAIR_V]
                    k_bd = jnp.where(diag_kk, jnp.concatenate([k2, k2], axis=0), 0.0)
                    if bi == 0:
                        dec = (gd_f if d == 0 else gd_b)[0, c, ks, :]
                    else:
                        dec = ret_dec[d, ks, :]
                    upd = jnp.where(diag_kv, _dot_tn(e_r[0, rows, ks], v2), 0.0)
                    work.append(dict(
                        kind="decay", d=d, p=p, rows=rows, o_ref=o_ref, s_ref=s_ref, base=base,
                        q2=q2, v2=v2, dec=jnp.concatenate([dec, dec], axis=1), upd=upd,
                        att=_dot_nt(q2, k_bd)))
            q_r, k_r, w_r, v_r, p_r, col_r, c_r = ml_sets[d]
            cs = c_r[0, c]
            cols = col_r[0, rows, :]
            for p in range(PAIRS):
                q2 = q_r[0, rows, p * 2 * DK_ML:(p + 1) * 2 * DK_ML]
                k2 = k_r[0, rows, p * 2 * DK_ML:(p + 1) * 2 * DK_ML]
                k_bd = jnp.where(diag_kv, jnp.concatenate([k2, k2], axis=0), 0.0)
                att = _dot_nt(q2, k_bd)
                heads = []
                for hl in range(2):
                    hd = 2 * p + hl
                    jj = d * HEADS + hd
                    vext = jnp.concatenate([v_r[0, rows, hd * DV:(hd + 1) * DV], ones_v], axis=1)
                    st = sm_ref[d, hd]
                    heads.append(dict(
                        hd=hd, st=st, vext=vext,
                        inter=_dot(q2[:, hl * DK_ML:(hl + 1) * DK_ML], st.astype(BF16)),
                        upd=_dot_tn(w_r[0, rows, hd * DK_ML:(hd + 1) * DK_ML], vext),
                        u=jnp.broadcast_to(cols[:, MC_U + jj:MC_U + jj + 1], (CHUNK, LANES)),
                        b=jnp.broadcast_to(cols[:, MC_B + jj:MC_B + jj + 1], (CHUNK, LANES))))
                work.append(dict(kind="ml", d=d, p=p, rows=rows, o_ref=o_ref, att=att, heads=heads,
                                 p0=p_r[0, p, rows, :], cs=cs))
        for w in work:
            d, p, rows, o_ref = w["d"], w["p"], w["rows"], w["o_ref"]
            if w["kind"] == "decay":
                s_ref = w["s_ref"]
                s_prev = s_ref[d, p]
                att = jnp.where(causal[d], w["att"], 0.0).astype(BF16)
                v_bd = jnp.where(diag_kv, jnp.concatenate([w["v2"], w["v2"]], axis=0), 0.0)
                lhs = jnp.concatenate([att, w["q2"]], axis=1)
                rhs = jnp.concatenate([v_bd, s_prev.astype(BF16)], axis=0)
                o_ref[0, rows, w["base"] + p * PAIR_V:w["base"] + (p + 1) * PAIR_V] = _dot(lhs, rhs)
                s_ref[d, p] = w["dec"] * s_prev + w["upd"]
            else:
                sc = (w["att"] * w["p0"]).astype(BF16)
                for hl, hw in enumerate(w["heads"]):
                    hd = hw["hd"]
                    jj = d * HEADS + hd
                    vz = (jnp.concatenate([hw["vext"], zeros_vext], axis=0) if hl == 0
                          else jnp.concatenate([zeros_vext, hw["vext"]], axis=0))
                    intra = _dot(sc, vz)
                    m_prev = m_ref[jj:jj + 1, :]
                    u = hw["u"]
                    mx = jnp.maximum(u, m_prev)
                    f = jnp.exp(u - mx)
                    e = jnp.exp(m_prev - mx)
                    num = f * intra[:, 0:DV] + e * hw["inter"][:, 0:DV]
                    den = f * intra[:, DV:] + e * hw["inter"][:, DV:]
                    den = jnp.maximum(jnp.abs(den), jnp.exp(-(hw["b"] + mx)))
                    o_ref[0, rows, BRANCH_W + hd * DV:BRANCH_W + (hd + 1) * DV] = num / den
                    tot = w["cs"][jj:jj + 1, :]
                    mloc = w["cs"][NHD + jj:NHD + jj + 1, :]
                    m_new = jnp.maximum(tot + m_prev, mloc)
                    keep = jnp.exp(tot + m_prev - m_new)
                    gain = jnp.exp(mloc - m_new)
                    sm_ref[d, hd] = (jnp.concatenate([keep, keep], axis=1) * hw["st"]
                                     + jnp.concatenate([gain, gain], axis=1) * hw["upd"])
                    m_ref[jj:jj + 1, :] = m_new

    for step in range(CHUNKS_PER_TILE):
        chunk_step(step)


def _mix(gla, ret, ml, ret_dl_col):
    gq, gk, ge, gv, gdec = gla
    rq, rk, re, rv = ret
    mq, mk, mw, mv, mp, mcol, mc = ml
    B, T, _ = gv.shape
    nt = T // TILE

    def tf(j):
        return j

    def tb(j):
        return jnp.where(j == 0, 0, nt - j)

    def pair(width, lane_block_b=0):
        return [pl.BlockSpec((1, TILE, width), lambda b, j: (b, tf(j), 0)),
                pl.BlockSpec((1, TILE, width), lambda b, j: (b, tb(j), lane_block_b))]

    dec_pair = [pl.BlockSpec((1, CHUNKS_PER_TILE, NQK_S, LANES), lambda b, j: (b, tf(j), 0, 0)),
                pl.BlockSpec((1, CHUNKS_PER_TILE, NQK_S, LANES), lambda b, j: (b, tb(j), 1, 0))]
    p0_pair = [pl.BlockSpec((1, PAIRS, TILE, 2 * CHUNK), lambda b, j: (b, 0, tf(j), 0)),
               pl.BlockSpec((1, PAIRS, TILE, 2 * CHUNK), lambda b, j: (b, 1, tb(j), 0))]
    cs_pair = [pl.BlockSpec((1, CHUNKS_PER_TILE, 2 * NHD, LANES), lambda b, j: (b, tf(j), 0, 0)),
               pl.BlockSpec((1, CHUNKS_PER_TILE, 2 * NHD, LANES), lambda b, j: (b, tb(j), 0, 0))]
    decay_specs = pair(NQK_S, 1) + pair(NQK_S, 1) + pair(NQK_S, 1) + pair(BRANCH_W)
    in_specs = (decay_specs + dec_pair + decay_specs
                + [pl.BlockSpec((2, NQK_S, LANES), lambda b, j: (0, 0, 0))]
                + pair(NQK_M) + pair(NQK_M) + pair(NQK_M, 1) + pair(BRANCH_W) + p0_pair
                + pair(MC_WIDTH) + cs_pair)
    args = (gq, gq, gk, gk, ge, ge, gv, gv, gdec, gdec,
            rq, rq, rk, rk, re, re, rv, rv, ret_dl_col,
            mq, mq, mk, mk, mw, mw, mv, mv, mp, mp, mcol, mcol, mc, mc)
    return pl.pallas_call(
        _mix_kernel,
        out_shape=(jax.ShapeDtypeStruct((B, T, 3 * BRANCH_W), F32),) * 2,
        grid=(B, nt),
        in_specs=in_specs,
        out_specs=(pl.BlockSpec((1, TILE, 3 * BRANCH_W), lambda b, j: (b, tf(j), 0)),
                   pl.BlockSpec((1, TILE, 3 * BRANCH_W), lambda b, j: (b, tb(j), 0))),
        scratch_shapes=[pltpu.VMEM((2, PAIRS, PAIR_K, PAIR_V), F32),
                        pltpu.VMEM((2, PAIRS, PAIR_K, PAIR_V), F32),
                        pltpu.VMEM((2, HEADS, DK_ML, 2 * DV), F32),
                        pltpu.VMEM((NHD, LANES), F32)],
        compiler_params=_cparams(2), name="mix",
    )(*args)


def _merge_kernel(x_ref, mod_ref, g_ref, wg_ref, of_ref, ob_ref, sgg_ref, sgm_ref, sgr_ref,
                  ng_ref, wb_ref, wo_ref, out_ref):
    x = x_ref[0]
    h = _norm_mod(x, g_ref[...], mod_ref[0, 0:1, :], mod_ref[0, 1:2, :]).astype(BF16)
    o = of_ref[0] + ob_ref[0]
    side = (sgg_ref, sgm_ref, sgr_ref)
    acc = jnp.zeros((TILE, D_MODEL), F32)
    for br in range(3):
        parts = []
        for hd in range(HEADS):
            oh = o[:, br * BRANCH_W + hd * DV:br * BRANCH_W + (hd + 1) * DV]
            if br > 0:
                oh = oh - jnp.mean(oh, axis=-1, keepdims=True)
            parts.append(oh * lax.rsqrt(jnp.mean(oh * oh, axis=-1, keepdims=True) + EPS))
        y = jnp.concatenate(parts, axis=1) * ng_ref[:, br * BRANCH_W:(br + 1) * BRANCH_W]
        y = (y * side[br][0]).astype(BF16)
        gate = _sigmoid(_dot(h, wg_ref[0, :, br * D_MODEL:(br + 1) * D_MODEL]))
        acc = acc + gate * _dot(y, wb_ref[0, br])
    out = _dot(acc.astype(BF16), wo_ref[0])
    out_ref[0] = x + mod_ref[0, 2:3, :] * out


def _merge(tok, mods, g1, wgate, o_f, o_b, sg_gla, sz_ml, sg_ret, norm_g, wb, wo, layer, skip):
    B, T, _ = tok.shape
    nt = T // TILE - skip
    tile = lambda width: pl.BlockSpec((1, TILE, width), lambda b, i: (b, i + skip, 0))
    const = lambda shape: pl.BlockSpec(shape, lambda b, i: (0,) * len(shape))
    per_layer = lambda shape: pl.BlockSpec((1,) + shape, lambda b, i: (layer,) + (0,) * len(shape))
    return pl.pallas_call(
        _merge_kernel,
        out_shape=jax.ShapeDtypeStruct((B, nt * TILE, D_MODEL), F32),
        grid=(B, nt),
        in_specs=[tile(D_MODEL),
                  pl.BlockSpec((1, 6, D_MODEL), lambda b, i: (_mod_row_index(b, i + skip), 0, 0)),
                  const((1, D_MODEL)), per_layer((D_MODEL, W_GATE)),
                  tile(3 * BRANCH_W), tile(3 * BRANCH_W),
                  tile(BRANCH_W), tile(BRANCH_W), tile(BRANCH_W),
                  const((1, 3 * BRANCH_W)), per_layer((3, BRANCH_W, D_MODEL)),
                  per_layer((D_MODEL, D_MODEL))],
        out_specs=pl.BlockSpec((1, TILE, D_MODEL), lambda b, i: (b, i, 0)),
        compiler_params=_cparams(2), name="merge",
    )(tok, mods, g1, wgate, o_f, o_b, sg_gla, sz_ml, sg_ret, norm_g, wb, wo)


def _mlp_kernel(final, x_ref, mod_ref, g_ref, wu_ref, wd_ref, fg_ref, out_ref):
    x = x_ref[0]
    h = _norm_mod(x, g_ref[...], mod_ref[0, 3:4, :], mod_ref[0, 4:5, :]).astype(BF16)
    u = jnp.maximum(_dot(h, wu_ref[0]), 0.0)
    y = x + mod_ref[0, 5:6, :] * _dot((u * u).astype(BF16), wd_ref[0])
    if final:
        y = y * lax.rsqrt(jnp.mean(y * y, axis=-1, keepdims=True) + EPS) * fg_ref[...]
    out_ref[0] = y


def _mlp(tok, mods, g2, wu, wd, layer, final_g, first_tile, final):
    B, T, _ = tok.shape
    const = lambda shape: pl.BlockSpec(shape, lambda b, i: (0,) * len(shape))
    per_layer = lambda shape: pl.BlockSpec((1,) + shape, lambda b, i: (layer,) + (0,) * len(shape))
    return pl.pallas_call(
        functools.partial(_mlp_kernel, final),
        out_shape=jax.ShapeDtypeStruct((B, T, D_MODEL), F32),
        grid=(B, T // TILE),
        in_specs=[pl.BlockSpec((1, TILE, D_MODEL), lambda b, i: (b, i, 0)),
                  pl.BlockSpec((1, 6, D_MODEL), lambda b, i: (_mod_row_index(b, i + first_tile), 0, 0)),
                  const((1, D_MODEL)), per_layer((D_MODEL, D_FF)), per_layer((D_FF, D_MODEL)),
                  const((1, D_MODEL))],
        out_specs=pl.BlockSpec((1, TILE, D_MODEL), lambda b, i: (b, i, 0)),
        compiler_params=_cparams(2), name="mlp",
    )(tok, mods, g2, wu, wd, final_g)


def kernel(x, c, ctx, c_ctx, norm1_g, norm2_g, w_ada, b_ada, w_in, gla_a2, gla_a_bias, gla_norm_g,
           ml_conv, ml_i_bias, ml_f_bias, ml_norm_g, ret_decay_logit, ret_norm_g, w_branch, w_o,
           w_up, w_down, final_g):
    B, seq, _ = x.shape
    ctx_len = ctx.shape[1]
    assert B == 2 and ctx_len == TILE and seq % TILE == 0 and seq % GRID_W == 0
    tok = jnp.concatenate([ctx, x], axis=1)

    c_rows = jnp.concatenate([c, c_ctx[None, :], jnp.zeros((8 - B - 1, D_MODEL), F32)], axis=0)
    mods_all = _ada(c_rows, w_ada, b_ada)[:, 0:3].reshape(DEPTH, 3, 6, D_MODEL)

    cos_t, sin_t = _rope_tables(seq, ctx_len)

    o0 = W_GLA
    o1 = o0 + W_ML
    o2 = o1 + W_RET
    n_sg = 2 * NQK_S + 2 * BRANCH_W
    pad = lambda n: jnp.zeros((DEPTH, D_MODEL, n), F32)
    w_proj = jnp.concatenate([
        w_in[:, :, o0:o0 + 2 * NQK_M],
        w_in[:, :, o0 + 2 * NQK_M:o1], pad(LANES - 2 * NHD),
        w_in[:, :, 0:n_sg],
        w_in[:, :, o1:o2],
        w_in[:, :, n_sg:o0], pad(LANES - 2 * GLA_RANK),
    ], axis=2).astype(BF16)
    w_gate = w_in[:, :, o2:].astype(BF16)
    w_branch_b = w_branch.astype(BF16)
    w_o_b = w_o.astype(BF16)
    w_up_b = w_up.astype(BF16)
    w_down_b = w_down.astype(BF16)
    for l in range(DEPTH):
        last = l == DEPTH - 1
        skip = 1 if last else 0
        mods = mods_all[l]
        g1 = norm1_g[l][None, :]
        g2 = norm2_g[l][None, :]
        zeros = jnp.zeros((GLA_RANK, NQK_S), F32)
        a2blk = jnp.concatenate([jnp.concatenate([gla_a2[l, 0], zeros], axis=1),
                                 jnp.concatenate([zeros, gla_a2[l, 1]], axis=1)], axis=0)
        a2s = jnp.concatenate([a2blk, a2blk], axis=0).astype(BF16)
        abias = gla_a_bias[l].reshape(1, 2 * NQK_S)
        gate_bias = jnp.concatenate([ml_i_bias[l].reshape(1, NHD), ml_f_bias[l].reshape(1, NHD)], axis=1)
        dl = ret_decay_logit[l]
        dl_row = jnp.repeat(dl, DK_SMALL, axis=1).reshape(1, 2 * NQK_S)
        dl_col = jnp.broadcast_to(jnp.repeat(dl, DK_SMALL, axis=1)[:, :, None], (2, NQK_S, LANES))

        (gq, gk, ge, gv, sg_gla, gdec, mq, mk, mw, mv, sz_ml, mp, mcol, mc,
         rq, rk, re, rv, sg_ret) = _proj(tok, mods, g1, w_proj, l, a2s, abias, ml_conv[l], gate_bias,
                                         cos_t, sin_t, dl_row)
        o_f, o_b = _mix((gq, gk, ge, gv, gdec), (rq, rk, re, rv), (mq, mk, mw, mv, mp, mcol, mc), dl_col)
        norm_g = jnp.concatenate([gla_norm_g[l], ml_norm_g[l], ret_norm_g[l]])[None, :]
        tok = _merge(tok, mods, g1, w_gate, o_f, o_b, sg_gla, sz_ml, sg_ret, norm_g,
                     w_branch_b, w_o_b, l, skip)
        tok = _mlp(tok, mods, g2, w_up_b, w_down_b, l, final_g[None, :], skip, last)
    return tok
```

```python
import functools

import jax
import jax.numpy as jnp
from jax import lax
from jax.experimental import pallas as pl
from jax.experimental.pallas import tpu as pltpu

F32 = jnp.float32
BF16 = jnp.bfloat16

D_MODEL = 1024
DEPTH = 2
GRID_W = 64
CHUNK = 64
EPS = 1e-6
M_INIT = -1e30
HEADS = 4
DK_SMALL = 64
DK_ML = 128
DV = 128
BRANCH_W = HEADS * DV
GLA_RANK = 16
GLA_TAU = 16.0
ROPE_BASE = 10000.0
D_FF = 4 * D_MODEL
LANES = 128

TILE = 256
CHUNKS_PER_TILE = TILE // CHUNK
HALO = 8
VMEM_LIMIT = 56 * 1024 * 1024
VMEM_LIMIT_PROJ = 60 * 1024 * 1024

NQK_S = HEADS * DK_SMALL
NQK_M = HEADS * DK_ML
NHD = 2 * HEADS
W_GLA = 2 * NQK_S + 2 * BRANCH_W + 2 * GLA_RANK
W_ML = 2 * NQK_M + 2 * BRANCH_W + 2 * NHD
W_RET = 2 * NQK_S + 2 * BRANCH_W
W_GATE = 3 * D_MODEL

PW_MLQK = 0
PW_MLR = PW_MLQK + 2 * NQK_M
PW_MLR_WIDTH = 2 * BRANCH_W + LANES
PW_GR = PW_MLR + PW_MLR_WIDTH
PW_GR_WIDTH = 2 * (2 * NQK_S + 2 * BRANCH_W) + LANES
W_PROJ = PW_GR + PW_GR_WIDTH
MLR_V, MLR_Z, MLR_GATES = 0, BRANCH_W, 2 * BRANCH_W
GR_GLA_Q, GR_GLA_K, GR_GLA_V, GR_GLA_G = 0, NQK_S, 2 * NQK_S, 2 * NQK_S + BRANCH_W
GR_RET_Q = 2 * NQK_S + 2 * BRANCH_W
GR_RET_K, GR_RET_V, GR_RET_G = GR_RET_Q + NQK_S, GR_RET_Q + 2 * NQK_S, GR_RET_Q + 2 * NQK_S + BRANCH_W
GR_GLA_A = 2 * GR_RET_Q

MC_B, MC_W, MC_U = 0, NHD, 2 * NHD
MC_WIDTH = 3 * NHD

PAIRS = HEADS // 2
PAIR_K = 2 * DK_SMALL
PAIR_V = 2 * DV


def _cparams(n_axes, vmem_limit=VMEM_LIMIT):
    return pltpu.CompilerParams(dimension_semantics=("arbitrary",) * n_axes,
                                vmem_limit_bytes=vmem_limit)


def _log_sigmoid(z):
    return jnp.minimum(z, 0.0) - jnp.log(1.0 + jnp.exp(-jnp.abs(z)))


def _sigmoid(z):
    return 1.0 / (1.0 + jnp.exp(-z))


def _silu(z):
    return z * _sigmoid(z)


def _norm_mod(x, g, shift, scale):
    y = x * lax.rsqrt(jnp.mean(x * x, axis=-1, keepdims=True) + EPS) * g
    return y * (1.0 + scale) + shift


def _dot(a, b):
    return jnp.dot(a, b, preferred_element_type=F32)


def _dot_nt(a, b):
    return lax.dot_general(a, b, (((1,), (1,)), ((), ())), preferred_element_type=F32)


def _dot_tn(a, b):
    return lax.dot_general(a, b, (((0,), (0,)), ((), ())), preferred_element_type=F32)


def _split2(x):
    hi = x.astype(BF16)
    return hi, (x - hi.astype(F32)).astype(BF16)


def _chunk_tri(lower):
    r = lax.broadcasted_iota(jnp.int32, (TILE, TILE), 0)
    c = lax.broadcasted_iota(jnp.int32, (TILE, TILE), 1)
    same = (r // CHUNK) == (c // CHUNK)
    tri = (c <= r) if lower else (c >= r)
    return jnp.where(same & tri, 1.0, 0.0).astype(BF16)


def _chunk_total(x):
    w = x.shape[-1]
    tot = jnp.sum(x.reshape(CHUNKS_PER_TILE, CHUNK, w), axis=1, keepdims=True)
    return jnp.broadcast_to(tot, (CHUNKS_PER_TILE, CHUNK, w)).reshape(TILE, w)


def _ada_kernel(c_ref, w_ref, b_ref, o_ref):
    s = _silu(c_ref[...])
    o_ref[0] = jnp.dot(s, w_ref[0], precision=lax.Precision.HIGHEST,
                       preferred_element_type=F32) + b_ref[0]


def _ada(c_rows, w_ada, b_ada):
    tn = 1536
    return pl.pallas_call(
        _ada_kernel,
        out_shape=jax.ShapeDtypeStruct((DEPTH, 8, 6 * D_MODEL), F32),
        grid=(DEPTH, 6 * D_MODEL // tn),
        in_specs=[pl.BlockSpec((8, D_MODEL), lambda l, n: (0, 0)),
                  pl.BlockSpec((1, D_MODEL, tn), lambda l, n: (l, 0, n)),
                  pl.BlockSpec((1, 1, tn), lambda l, n: (l, 0, n))],
        out_specs=pl.BlockSpec((1, 8, tn), lambda l, n: (l, 0, n)),
        compiler_params=_cparams(2), name="ada",
    )(c_rows, w_ada, b_ada.reshape(DEPTH, 1, 6 * D_MODEL))


def _rope_kernel(ang_ref, cos_ref, sin_ref, nsin_ref):
    a = ang_ref[...]
    s = jnp.sin(a)
    cos_ref[...] = jnp.cos(a)
    sin_ref[...] = s
    nsin_ref[...] = -s


def _rope_tables(seq, ctx_len):
    rows = seq // GRID_W
    nf = DK_SMALL // 4
    inv = ROPE_BASE ** (-jnp.arange(nf, dtype=F32) / nf)
    ang = jnp.concatenate([jnp.arange(rows, dtype=F32)[:, None] * inv,
                           jnp.arange(GRID_W, dtype=F32)[:, None] * inv], axis=0)
    n = rows + GRID_W
    cos, sin, nsin = pl.pallas_call(
        _rope_kernel,
        out_shape=(jax.ShapeDtypeStruct((n, nf), F32),) * 3,
        name="rope_tables",
    )(ang)

    def per_token(first, second):
        def half(t):
            r = jnp.broadcast_to(t[:rows, None, :], (rows, GRID_W, nf))
            c = jnp.broadcast_to(t[None, rows:, :], (rows, GRID_W, nf))
            return jnp.concatenate([r, c], axis=-1)
        head = jnp.concatenate([half(first), half(second)], axis=-1).reshape(seq, DK_SMALL)
        return jnp.tile(head, (1, HEADS))

    ones = jnp.ones((ctx_len, NQK_S), F32)
    cos_t = jnp.concatenate([ones, per_token(cos, cos)], axis=0)
    sin_t = jnp.concatenate([0.0 * ones, per_token(nsin, sin)], axis=0)
    return cos_t, sin_t


def _proj_kernel(nt, xa_ref, xap_ref, xan_ref, xb_ref, xbp_ref, xbn_ref, x0_ref,
                 mod_ref, g_ref, w_ref, a2_ref, ab_ref, cw_ref, gb_ref, cos_ref, sin_ref, dl_ref,
                 gq_ref, gk_ref, ge_ref, gv_ref, gsg_ref, gdec_ref,
                 mq_ref, mk_ref, mkw_ref, mv_ref, msz_ref, mp0_ref, mcol_ref, mcs_ref,
                 rq_ref, rk_ref, re_ref, rv_ref, rsg_ref,
                 ext0_ref, ext1_ref, prml0_ref, prml1_ref, prgr0_ref, prgr1_ref, rdec_ref):
    i = pl.program_id(0)
    ext_refs = (ext0_ref, ext1_ref)
    prml_refs = (prml0_ref, prml1_ref)
    prgr_refs = (prgr0_ref, prgr1_ref)
    tri_ge = _chunk_tri(True)
    tri_le = _chunk_tri(False)
    row8 = lax.broadcasted_iota(jnp.int32, (NHD, TILE), 0)
    lane8 = lax.broadcasted_iota(jnp.int32, (NHD, TILE), 1)
    lane_chunk = lane8 // CHUNK
    lane_in_chunk = lane8 % CHUNK
    ti = lax.broadcasted_iota(jnp.int32, (CHUNK, CHUNK), 0)
    si = lax.broadcasted_iota(jnp.int32, (CHUNK, CHUNK), 1)
    lane = lax.broadcasted_iota(jnp.int32, (TILE, NQK_S), 1)
    first_half = (lane % DK_SMALL) < (DK_SMALL // 2)
    ext_row = lax.broadcasted_iota(jnp.int32, (TILE + 2 * HALO, 1), 0)
    cos = cos_ref[...]
    sin = sin_ref[...]
    cw = cw_ref[...]

    def project_stages(x, x_prev, x_next, tile_idx, mod_row, slot):
        st = {}

        def mlstm_qk():
            mod = mod_ref[mod_row]
            x_all = jnp.concatenate([x_prev, x, x_next], axis=0)
            h_all = _norm_mod(x_all, g_ref[...], mod[0:1, :], mod[1:2, :])
            st["h"] = h_all[HALO:HALO + TILE, :].astype(BF16)
            prev_ok = tile_idx >= 2
            next_ok = jnp.logical_and(tile_idx >= 1, tile_idx <= nt - 2)
            dead = jnp.logical_or(jnp.logical_and(ext_row < HALO, jnp.logical_not(prev_ok)),
                                  jnp.logical_and(ext_row >= TILE + HALO, jnp.logical_not(next_ok)))
            ext = _dot(h_all.astype(BF16), w_ref[0, :, PW_MLQK:PW_MLR])
            ext_refs[slot][...] = jnp.where(dead, 0.0, ext)

        def mlstm_rest():
            prml_refs[slot][...] = _dot(st["h"], w_ref[0, :, PW_MLR:PW_GR])

        def gla():
            prgr_refs[slot][:, 0:GR_RET_Q] = _dot(st["h"], w_ref[0, :, PW_GR:PW_GR + GR_RET_Q])

        def ret():
            prgr_refs[slot][:, GR_RET_Q:] = _dot(st["h"], w_ref[0, :, PW_GR + GR_RET_Q:])

        return [mlstm_qk, mlstm_rest, gla, ret]

    def project(*args):
        for stage in project_stages(*args):
            stage()

    @pl.when(i == 0)
    def _():
        lg = _log_sigmoid(dl_ref[...])
        t_in_chunk = (lax.broadcasted_iota(jnp.int32, (TILE, NQK_S), 0) % CHUNK).astype(F32)
        for d in range(2):
            lg_d = lg[:, d * NQK_S:(d + 1) * NQK_S]
            steps = (t_in_chunk + 1.0) if d == 0 else (float(CHUNK) - t_in_chunk)
            b = steps * lg_d
            rdec_ref[3 * d] = jnp.exp(b)
            rdec_ref[3 * d + 1] = jnp.exp(-b)
            rdec_ref[3 * d + 2] = jnp.exp(float(CHUNK) * lg_d - b)
        halo0 = jnp.zeros((HALO, D_MODEL), F32)
        project(x0_ref[0], halo0, halo0, 0, 2, 0)

    def rope(x):
        swapped = jnp.where(first_half, pltpu.roll(x, NQK_S - DK_SMALL // 2, 1),
                            pltpu.roll(x, DK_SMALL // 2, 1))
        return x * cos + swapped * sin

    def seg_cummax(x, reverse):
        sh = 1
        while sh < CHUNK:
            if reverse:
                y, ok = pltpu.roll(x, TILE - sh, 1), lane_in_chunk < CHUNK - sh
            else:
                y, ok = pltpu.roll(x, sh, 1), lane_in_chunk >= sh
            x = jnp.where(ok, jnp.maximum(x, y), x)
            sh *= 2
        return x

    def prepare_stages(slot, bb):
        ext_ref, prml_ref, prgr_ref = ext_refs[slot], prml_refs[slot], prgr_refs[slot]
        st = {}

        def mlstm_conv():
            qk = _silu(cw[0:1, :] * ext_ref[HALO - 1:HALO - 1 + TILE, :]
                       + cw[1:2, :] * ext_ref[HALO:HALO + TILE, :]
                       + cw[2:3, :] * ext_ref[HALO + 1:HALO + 1 + TILE, :])
            mq_ref[bb] = qk[:, 0:NQK_M].astype(BF16)
            st["k_ml"] = qk[:, NQK_M:] * (DK_ML ** -0.5)
            mk_ref[bb] = st["k_ml"].astype(BF16)

        def mlstm_gates():
            mlstm_gates_body(prml_ref, st["k_ml"], bb)

        def gla():
            gla_body(prgr_ref, bb)

        def ret():
            ret_body(prgr_ref, bb)

        return [mlstm_conv, mlstm_gates, gla, ret]

    def mlstm_gates_body(prml_ref, k_ml, bb):
        mv_ref[bb] = prml_ref[:,MLR_V:MLR_V + BRANCH_W].astype(BF16)
        msz_ref[bb] = _silu(prml_ref[:,MLR_Z:MLR_Z + BRANCH_W]).astype(BF16)
        gates_t = (prml_ref[:,MLR_GATES:MLR_GATES + 2 * NHD] + gb_ref[...]).T
        li_t = gates_t[0:NHD]
        lf_t = _log_sigmoid(gates_t[NHD:])
        lf_hi, lf_lo = _split2(lf_t)
        lf2 = jnp.concatenate([lf_hi, lf_lo], axis=0)
        cum_f = _dot(lf2, tri_le)
        cum_b = _dot(lf2, tri_ge)
        b_t = jnp.where(row8 < HEADS, cum_f[0:NHD] + cum_f[NHD:], cum_b[0:NHD] + cum_b[NHD:])
        tot_t = jnp.zeros((NHD, TILE), F32)
        mloc_t = jnp.zeros((NHD, TILE), F32)
        for c in range(CHUNKS_PER_TILE):
            tot_c = jnp.sum(lf_t[:, c * CHUNK:(c + 1) * CHUNK], axis=-1, keepdims=True)
            tot_t = jnp.where(lane_chunk == c, tot_c, tot_t)
            mcs_ref[bb, c, 0:NHD, :] = jnp.broadcast_to(tot_c, (NHD, LANES))
        g_t = tot_t - b_t + li_t
        for c in range(CHUNKS_PER_TILE):
            mloc_c = jnp.max(g_t[:, c * CHUNK:(c + 1) * CHUNK], axis=-1, keepdims=True)
            mloc_t = jnp.where(lane_chunk == c, mloc_c, mloc_t)
            mcs_ref[bb, c, NHD:2 * NHD, :] = jnp.broadcast_to(mloc_c, (NHD, LANES))
        w_t = jnp.exp(g_t - mloc_t)
        r_t = li_t - b_t
        u_t = jnp.where(row8 < HEADS, seg_cummax(r_t, False), seg_cummax(r_t, True))
        cols = jnp.concatenate([b_t, w_t, u_t], axis=0).T
        mcol_ref[bb] = cols
        for j in range(NHD):
            d, hd = j // HEADS, j % HEADS
            mask = (si <= ti) if d == 0 else (si >= ti)
            mkw_ref[bb, :, d * NQK_M + hd * DK_ML:d * NQK_M + (hd + 1) * DK_ML] = (
                k_ml[:, hd * DK_ML:(hd + 1) * DK_ML] * cols[:, MC_W + j:MC_W + j + 1]).astype(BF16)
            for c in range(CHUNKS_PER_TILE):
                rows = slice(c * CHUNK, (c + 1) * CHUNK)
                a = r_t[j:j + 1, rows] - cols[rows, MC_U + j:MC_U + j + 1]
                mp0_ref[bb, j // 2, rows, (j % 2) * CHUNK:(j % 2 + 1) * CHUNK] = jnp.where(mask, jnp.exp(a), 0.0)

    def gla_body(prgr_ref, bb):
        gv_ref[bb] = prgr_ref[:,GR_GLA_V:GR_GLA_V + BRANCH_W].astype(BF16)
        gsg_ref[bb] = _silu(prgr_ref[:,GR_GLA_G:GR_GLA_G + BRANCH_W]).astype(BF16)
        a_hi, a_lo = _split2(prgr_ref[:,GR_GLA_A:GR_GLA_A + 2 * GLA_RANK])
        z = _dot(jnp.concatenate([a_hi, a_lo], axis=1), a2_ref[...]) + ab_ref[...]
        la = _log_sigmoid(z) * (1.0 / GLA_TAU)
        half_t = jnp.sum(la.reshape(2 * CHUNKS_PER_TILE, CHUNK // 2, 2 * NQK_S), axis=1).T
        q_gla = prgr_ref[:,GR_GLA_Q:GR_GLA_Q + NQK_S] * (DK_SMALL ** -0.5)
        k_gla = prgr_ref[:,GR_GLA_K:GR_GLA_K + NQK_S]
        for d in range(2):
            la_d = la[:, d * NQK_S:(d + 1) * NQK_S]
            la_hi, la_lo = _split2(la_d)
            tri = tri_ge if d == 0 else tri_le
            b = _dot(tri, la_hi) + _dot(tri, la_lo)
            tot = _chunk_total(la_d)
            gq_ref[bb, :, d * NQK_S:(d + 1) * NQK_S] = (q_gla * jnp.exp(b)).astype(BF16)
            gk_ref[bb, :, d * NQK_S:(d + 1) * NQK_S] = (k_gla * jnp.exp(-b)).astype(BF16)
            ge_ref[bb, :, d * NQK_S:(d + 1) * NQK_S] = (k_gla * jnp.exp(tot - b)).astype(BF16)
            ht = half_t[d * NQK_S:(d + 1) * NQK_S, :]
            for c in range(CHUNKS_PER_TILE):
                tot_col = ht[:, 2 * c:2 * c + 1] + ht[:, 2 * c + 1:2 * c + 2]
                gdec_ref[bb, c, d * NQK_S:(d + 1) * NQK_S, :] = jnp.broadcast_to(jnp.exp(tot_col), (NQK_S, LANES))

    def ret_body(prgr_ref, bb):
        q_ret = rope(prgr_ref[:,GR_RET_Q:GR_RET_Q + NQK_S])
        k_ret = rope(prgr_ref[:,GR_RET_K:GR_RET_K + NQK_S] * (DK_SMALL ** -0.5))
        rv_ref[bb] = prgr_ref[:,GR_RET_V:GR_RET_V + BRANCH_W].astype(BF16)
        rsg_ref[bb] = _silu(prgr_ref[:,GR_RET_G:GR_RET_G + BRANCH_W]).astype(BF16)
        for d in range(2):
            rq_ref[bb, :, d * NQK_S:(d + 1) * NQK_S] = (q_ret * rdec_ref[3 * d]).astype(BF16)
            rk_ref[bb, :, d * NQK_S:(d + 1) * NQK_S] = (k_ret * rdec_ref[3 * d + 1]).astype(BF16)
            re_ref[bb, :, d * NQK_S:(d + 1) * NQK_S] = (k_ret * rdec_ref[3 * d + 2]).astype(BF16)

    def interleave(matrix_stages, vector_stages):
        for m_stage, v_stage in zip(matrix_stages, vector_stages):
            m_stage()
            v_stage()

    next_tile = jnp.minimum(i + 1, nt - 1)
    interleave(project_stages(xa_ref[0], xap_ref[0], xan_ref[0], i, jnp.where(i == 0, 2, 1), 1),
               prepare_stages(0, 0))
    interleave(project_stages(xb_ref[0], xbp_ref[0], xbn_ref[0], next_tile, 0, 0),
               prepare_stages(1, 1))


def _proj(tok, mods, g1, w, layer, a2s, abias, conv_w, gate_bias, cos_t, sin_t, dl_row):
    B, T, _ = tok.shape
    nt = T // TILE
    tile = lambda width: pl.BlockSpec((B, TILE, width), lambda i: (0, i, 0))
    const = lambda shape: pl.BlockSpec(shape, lambda i: (0,) * len(shape))
    per_halo = TILE // HALO
    n_halo = T // HALO
    nxt = lambda i: jnp.minimum(i + 1, nt - 1)
    prev_halo = lambda t: jnp.maximum(t * per_halo - 1, 0)
    next_halo = lambda t: jnp.minimum((t + 1) * per_halo, n_halo - 1)
    tok_shape = lambda width, dt: jax.ShapeDtypeStruct((B, T, width), dt)
    return pl.pallas_call(
        functools.partial(_proj_kernel, nt),
        out_shape=(tok_shape(2 * NQK_S, BF16), tok_shape(2 * NQK_S, BF16), tok_shape(2 * NQK_S, BF16),
                   tok_shape(BRANCH_W, BF16), tok_shape(BRANCH_W, BF16),
                   jax.ShapeDtypeStruct((B, T // CHUNK, 2 * NQK_S, LANES), F32),
                   tok_shape(NQK_M, BF16), tok_shape(NQK_M, BF16), tok_shape(2 * NQK_M, BF16),
                   tok_shape(BRANCH_W, BF16), tok_shape(BRANCH_W, BF16),
                   jax.ShapeDtypeStruct((B, 2 * PAIRS, T, 2 * CHUNK), F32),
                   tok_shape(MC_WIDTH, F32),
                   jax.ShapeDtypeStruct((B, T // CHUNK, 2 * NHD, LANES), F32),
                   tok_shape(2 * NQK_S, BF16), tok_shape(2 * NQK_S, BF16), tok_shape(2 * NQK_S, BF16),
                   tok_shape(BRANCH_W, BF16), tok_shape(BRANCH_W, BF16)),
        grid=(nt,),
        in_specs=[pl.BlockSpec((1, TILE, D_MODEL), lambda i: (1, i, 0)),
                  pl.BlockSpec((1, HALO, D_MODEL), lambda i: (1, prev_halo(i), 0)),
                  pl.BlockSpec((1, HALO, D_MODEL), lambda i: (1, next_halo(i), 0)),
                  pl.BlockSpec((1, TILE, D_MODEL), lambda i: (0, nxt(i), 0)),
                  pl.BlockSpec((1, HALO, D_MODEL), lambda i: (0, prev_halo(nxt(i)), 0)),
                  pl.BlockSpec((1, HALO, D_MODEL), lambda i: (0, next_halo(nxt(i)), 0)),
                  pl.BlockSpec((1, TILE, D_MODEL), lambda i: (0, 0, 0)),
                  const((B + 1, 6, D_MODEL)), const((1, D_MODEL)),
                  pl.BlockSpec((1, D_MODEL, W_PROJ), lambda i: (layer, 0, 0), pipeline_mode=pl.Buffered(1)),
                  const((4 * GLA_RANK, 2 * NQK_S)), const((1, 2 * NQK_S)),
                  const((3, 2 * NQK_M)), const((1, 2 * NHD)),
                  pl.BlockSpec((TILE, NQK_S), lambda i: (i, 0)),
                  pl.BlockSpec((TILE, NQK_S), lambda i: (i, 0)),
                  const((1, 2 * NQK_S))],
        out_specs=(tile(2 * NQK_S), tile(2 * NQK_S), tile(2 * NQK_S), tile(BRANCH_W), tile(BRANCH_W),
                   pl.BlockSpec((B, CHUNKS_PER_TILE, 2 * NQK_S, LANES), lambda i: (0, i, 0, 0)),
                   tile(NQK_M), tile(NQK_M), tile(2 * NQK_M), tile(BRANCH_W), tile(BRANCH_W),
                   pl.BlockSpec((B, 2 * PAIRS, TILE, 2 * CHUNK), lambda i: (0, 0, i, 0)),
                   tile(MC_WIDTH),
                   pl.BlockSpec((B, CHUNKS_PER_TILE, 2 * NHD, LANES), lambda i: (0, i, 0, 0)),
                   tile(2 * NQK_S), tile(2 * NQK_S), tile(2 * NQK_S), tile(BRANCH_W), tile(BRANCH_W)),
        scratch_shapes=[pltpu.VMEM((TILE + 2 * HALO, 2 * NQK_M), F32)] * 2
                       + [pltpu.VMEM((TILE, PW_MLR_WIDTH), F32)] * 2
                       + [pltpu.VMEM((TILE, PW_GR_WIDTH), F32)] * 2
                       + [pltpu.VMEM((6, TILE, NQK_S), F32)],
        compiler_params=_cparams(1, VMEM_LIMIT_PROJ), name="proj",
    )(tok, tok, tok, tok, tok, tok, tok, mods, g1, w, a2s, abias, conv_w, gate_bias, cos_t, sin_t, dl_row)


def _mix_kernel(batch, *refs):
    (gq_f, gq_b, gk_f, gk_b, ge_f, ge_b, gv_f, gv_b, gd_f, gd_b,
     rq_f, rq_b, rk_f, rk_b, re_f, re_b, rv_f, rv_b, rdl,
     mq_f, mq_b, mk_f, mk_b, mw_f, mw_b, mv_f, mv_b, mp_f, mp_b, mcol_f, mcol_b, mc_f, mc_b,
     of_ref, ob_ref, sg_ref, sr_ref, sm_ref, m_ref) = refs
    j = pl.program_id(0)

    @pl.when(j == 0)
    def _():
        sg_ref[...] = jnp.zeros_like(sg_ref)
        sr_ref[...] = jnp.zeros_like(sr_ref)
        sm_ref[...] = jnp.zeros_like(sm_ref)
        m_ref[...] = jnp.full_like(m_ref, M_INIT)

    row_head = lax.broadcasted_iota(jnp.int32, (2 * CHUNK, PAIR_V), 0) // CHUNK
    diag_kv = row_head == lax.broadcasted_iota(jnp.int32, (2 * CHUNK, PAIR_V), 1) // DV
    diag_kk = (lax.broadcasted_iota(jnp.int32, (2 * CHUNK, PAIR_K), 0) // CHUNK
               == lax.broadcasted_iota(jnp.int32, (2 * CHUNK, PAIR_K), 1) // DK_SMALL)
    t_idx = lax.broadcasted_iota(jnp.int32, (CHUNK, 2 * CHUNK), 0)
    s_idx = lax.broadcasted_iota(jnp.int32, (CHUNK, 2 * CHUNK), 1) % CHUNK
    causal = (s_idx <= t_idx, s_idx >= t_idx)
    ones_v = jnp.ones((CHUNK, DV), BF16)
    zeros_vext = jnp.zeros((CHUNK, PAIR_V), BF16)
    ret_dec = jnp.exp(float(CHUNK) * _log_sigmoid(rdl[...]))

    decay_sets = (
        ((gq_f, gk_f, ge_f, gv_f), (gq_b, gk_b, ge_b, gv_b), sg_ref, 0),
        ((rq_f, rk_f, re_f, rv_f), (rq_b, rk_b, re_b, rv_b), sr_ref, 2 * BRANCH_W),
    )
    ml_sets = ((mq_f, mk_f, mw_f, mv_f, mp_f, mcol_f, mc_f),
               (mq_b, mk_b, mw_b, mv_b, mp_b, mcol_b, mc_b))
    outs = (of_ref, ob_ref)

    def chunk_step(step):
        work = []
        for bb, d in ((b_, d_) for b_ in range(batch) for d_ in range(2)):
            c = step if d == 0 else CHUNKS_PER_TILE - 1 - step
            rows = pl.ds(c * CHUNK, CHUNK)
            o_ref = outs[d]
            for bi, (fset, bset, s_ref, base) in enumerate(decay_sets):
                q_r, k_r, e_r, v_r = fset if d == 0 else bset
                for p in range(PAIRS):
                    ks = slice(p * PAIR_K, (p + 1) * PAIR_K)
                    q2 = q_r[bb, rows,ks]
                    k2 = k_r[bb, rows,ks]
                    v2 = v_r[bb, rows,p * PAIR_V:(p + 1) * PAIR_V]
                    k_bd = jnp.where(diag_kk, jnp.concatenate([k2, k2], axis=0), 0.0)
                    if bi == 0:
                        dec = (gd_f if d == 0 else gd_b)[bb, c, ks, :]
                    else:
                        dec = ret_dec[d, ks, :]
                    upd = jnp.where(diag_kv, _dot_tn(e_r[bb, rows,ks], v2), 0.0)
                    work.append(dict(
                        kind="decay", bb=bb, d=d, p=p, rows=rows, o_ref=o_ref, s_ref=s_ref, base=base,
                        q2=q2, v2=v2, dec=jnp.concatenate([dec, dec], axis=1), upd=upd,
                        att=_dot_nt(q2, k_bd)))
            q_r, k_r, w_r, v_r, p_r, col_r, c_r = ml_sets[d]
            cs = c_r[bb, c]
            cols = col_r[bb, rows,:]
            for p in range(PAIRS):
                q2 = q_r[bb, rows,p * 2 * DK_ML:(p + 1) * 2 * DK_ML]
                k2 = k_r[bb, rows,p * 2 * DK_ML:(p + 1) * 2 * DK_ML]
                k_bd = jnp.where(diag_kv, jnp.concatenate([k2, k2], axis=0), 0.0)
                att = _dot_nt(q2, k_bd)
                heads = []
                for hl in range(2):
                    hd = 2 * p + hl
                    jj = d * HEADS + hd
                    vext = jnp.concatenate([v_r[bb, rows,hd * DV:(hd + 1) * DV], ones_v], axis=1)
                    st = sm_ref[bb, d, hd]
                    heads.append(dict(
                        hd=hd, st=st, vext=vext,
                        inter=_dot(q2[:, hl * DK_ML:(hl + 1) * DK_ML], st.astype(BF16)),
                        upd=_dot_tn(w_r[bb, rows,hd * DK_ML:(hd + 1) * DK_ML], vext),
                        u=jnp.broadcast_to(cols[:, MC_U + jj:MC_U + jj + 1], (CHUNK, LANES)),
                        b=jnp.broadcast_to(cols[:, MC_B + jj:MC_B + jj + 1], (CHUNK, LANES))))
                work.append(dict(kind="ml", bb=bb, d=d, p=p, rows=rows, o_ref=o_ref, att=att, heads=heads,
                                 p0=p_r[bb, p, rows, :], cs=cs))
        for w in work:
            bb, d, p, rows, o_ref = w["bb"], w["d"], w["p"], w["rows"], w["o_ref"]
            if w["kind"] == "decay":
                s_ref = w["s_ref"]
                s_prev = s_ref[bb, d, p]
                att = jnp.where(causal[d], w["att"], 0.0).astype(BF16)
                v_bd = jnp.where(diag_kv, jnp.concatenate([w["v2"], w["v2"]], axis=0), 0.0)
                lhs = jnp.concatenate([att, w["q2"]], axis=1)
                rhs = jnp.concatenate([v_bd, s_prev.astype(BF16)], axis=0)
                o_ref[bb, rows, w["base"] + p * PAIR_V:w["base"] + (p + 1) * PAIR_V] = _dot(lhs, rhs)
                s_ref[bb, d, p] = w["dec"] * s_prev + w["upd"]
            else:
                sc = (w["att"] * w["p0"]).astype(BF16)
                for hl, hw in enumerate(w["heads"]):
                    hd = hw["hd"]
                    jj = d * HEADS + hd
                    vz = (jnp.concatenate([hw["vext"], zeros_vext], axis=0) if hl == 0
                          else jnp.concatenate([zeros_vext, hw["vext"]], axis=0))
                    intra = _dot(sc, vz)
                    m_prev = m_ref[bb, jj:jj + 1, :]
                    u = hw["u"]
                    mx = jnp.maximum(u, m_prev)
                    f = jnp.exp(u - mx)
                    e = jnp.exp(m_prev - mx)
                    num = f * intra[:, 0:DV] + e * hw["inter"][:, 0:DV]
                    den = f * intra[:, DV:] + e * hw["inter"][:, DV:]
                    den = jnp.maximum(jnp.abs(den), jnp.exp(-(hw["b"] + mx)))
                    o_ref[bb, rows, BRANCH_W + hd * DV:BRANCH_W + (hd + 1) * DV] = num / den
                    tot = w["cs"][jj:jj + 1, :]
                    mloc = w["cs"][NHD + jj:NHD + jj + 1, :]
                    m_new = jnp.maximum(tot + m_prev, mloc)
                    keep = jnp.exp(tot + m_prev - m_new)
                    gain = jnp.exp(mloc - m_new)
                    sm_ref[bb, d, hd] = (jnp.concatenate([keep, keep], axis=1) * hw["st"]
                                         + jnp.concatenate([gain, gain], axis=1) * hw["upd"])
                    m_ref[bb, jj:jj + 1, :] = m_new

    for step in range(CHUNKS_PER_TILE):
        chunk_step(step)


def _mix(gla, ret, ml, ret_dl_col):
    gq, gk, ge, gv, gdec = gla
    rq, rk, re, rv = ret
    mq, mk, mw, mv, mp, mcol, mc = ml
    B, T, _ = gv.shape
    nt = T // TILE

    def tf(j):
        return j

    def tb(j):
        return jnp.where(j == 0, 0, nt - j)

    def pair(width, lane_block_b=0):
        return [pl.BlockSpec((B, TILE, width), lambda j: (0, tf(j), 0)),
                pl.BlockSpec((B, TILE, width), lambda j: (0, tb(j), lane_block_b))]

    dec_pair = [pl.BlockSpec((B, CHUNKS_PER_TILE, NQK_S, LANES), lambda j: (0, tf(j), 0, 0)),
                pl.BlockSpec((B, CHUNKS_PER_TILE, NQK_S, LANES), lambda j: (0, tb(j), 1, 0))]
    p0_pair = [pl.BlockSpec((B, PAIRS, TILE, 2 * CHUNK), lambda j: (0, 0, tf(j), 0)),
               pl.BlockSpec((B, PAIRS, TILE, 2 * CHUNK), lambda j: (0, 1, tb(j), 0))]
    cs_pair = [pl.BlockSpec((B, CHUNKS_PER_TILE, 2 * NHD, LANES), lambda j: (0, tf(j), 0, 0)),
               pl.BlockSpec((B, CHUNKS_PER_TILE, 2 * NHD, LANES), lambda j: (0, tb(j), 0, 0))]
    decay_specs = pair(NQK_S, 1) + pair(NQK_S, 1) + pair(NQK_S, 1) + pair(BRANCH_W)
    in_specs = (decay_specs + dec_pair + decay_specs
                + [pl.BlockSpec((2, NQK_S, LANES), lambda j: (0, 0, 0))]
                + pair(NQK_M) + pair(NQK_M) + pair(NQK_M, 1) + pair(BRANCH_W) + p0_pair
                + pair(MC_WIDTH) + cs_pair)
    args = (gq, gq, gk, gk, ge, ge, gv, gv, gdec, gdec,
            rq, rq, rk, rk, re, re, rv, rv, ret_dl_col,
            mq, mq, mk, mk, mw, mw, mv, mv, mp, mp, mcol, mcol, mc, mc)
    return pl.pallas_call(
        functools.partial(_mix_kernel, B),
        out_shape=(jax.ShapeDtypeStruct((B, T, 3 * BRANCH_W), F32),) * 2,
        grid=(nt,),
        in_specs=in_specs,
        out_specs=(pl.BlockSpec((B, TILE, 3 * BRANCH_W), lambda j: (0, tf(j), 0)),
                   pl.BlockSpec((B, TILE, 3 * BRANCH_W), lambda j: (0, tb(j), 0))),
        scratch_shapes=[pltpu.VMEM((B, 2, PAIRS, PAIR_K, PAIR_V), F32),
                        pltpu.VMEM((B, 2, PAIRS, PAIR_K, PAIR_V), F32),
                        pltpu.VMEM((B, 2, HEADS, DK_ML, 2 * DV), F32),
                        pltpu.VMEM((B, NHD, LANES), F32)],
        compiler_params=_cparams(1), name="mix",
    )(*args)


def _merge_kernel(batch, first_tile, x_ref, mod_ref, g_ref, wg_ref, of_ref, ob_ref, sgg_ref, sgm_ref, sgr_ref,
                  ng_ref, wb_ref, wo_ref, out_ref):
    tile_idx = pl.program_id(0) + first_tile
    side = (sgg_ref, sgm_ref, sgr_ref)
    mods = [mod_ref[jnp.where(tile_idx == 0, batch, bb)] for bb in range(batch)]
    hs = [_norm_mod(x_ref[bb], g_ref[...], mods[bb][0:1, :], mods[bb][1:2, :]).astype(BF16)
          for bb in range(batch)]
    accs = [jnp.zeros((TILE, D_MODEL), F32) for _ in range(batch)]
    for br in range(3):
        for bb in range(batch):
            parts = []
            for hd in range(HEADS):
                cols = slice(br * BRANCH_W + hd * DV, br * BRANCH_W + (hd + 1) * DV)
                oh = of_ref[bb, :, cols] + ob_ref[bb, :, cols]
                if br > 0:
                    oh = oh - jnp.mean(oh, axis=-1, keepdims=True)
                parts.append(oh * lax.rsqrt(jnp.mean(oh * oh, axis=-1, keepdims=True) + EPS))
            y = jnp.concatenate(parts, axis=1) * ng_ref[:, br * BRANCH_W:(br + 1) * BRANCH_W]
            y = (y * side[br][bb]).astype(BF16)
            gate = _sigmoid(_dot(hs[bb], wg_ref[0, :, br * D_MODEL:(br + 1) * D_MODEL]))
            accs[bb] = accs[bb] + gate * _dot(y, wb_ref[0, br])
    for bb in range(batch):
        out = _dot(accs[bb].astype(BF16), wo_ref[0])
        out_ref[bb] = x_ref[bb] + mods[bb][2:3, :] * out


def _merge(tok, mods, g1, wgate, o_f, o_b, sg_gla, sz_ml, sg_ret, norm_g, wb, wo, layer, skip):
    B, T, _ = tok.shape
    nt = T // TILE - skip
    tile = lambda width: pl.BlockSpec((B, TILE, width), lambda i: (0, i + skip, 0))
    const = lambda shape: pl.BlockSpec(shape, lambda i: (0,) * len(shape))
    per_layer = lambda shape: pl.BlockSpec((1,) + shape, lambda i: (layer,) + (0,) * len(shape),
                                           pipeline_mode=pl.Buffered(1))
    return pl.pallas_call(
        functools.partial(_merge_kernel, B, skip),
        out_shape=jax.ShapeDtypeStruct((B, nt * TILE, D_MODEL), F32),
        grid=(nt,),
        in_specs=[tile(D_MODEL), const((B + 1, 6, D_MODEL)),
                  const((1, D_MODEL)), per_layer((D_MODEL, W_GATE)),
                  tile(3 * BRANCH_W), tile(3 * BRANCH_W),
                  tile(BRANCH_W), tile(BRANCH_W), tile(BRANCH_W),
                  const((1, 3 * BRANCH_W)), per_layer((3, BRANCH_W, D_MODEL)),
                  per_layer((D_MODEL, D_MODEL))],
        out_specs=pl.BlockSpec((B, TILE, D_MODEL), lambda i: (0, i, 0)),
        compiler_params=_cparams(1), name="merge",
    )(tok, mods, g1, wgate, o_f, o_b, sg_gla, sz_ml, sg_ret, norm_g, wb, wo)


def _mlp_kernel(batch, first_tile, final, x_ref, mod_ref, g_ref, wu_ref, wd_ref, fg_ref, out_ref):
    tile_idx = pl.program_id(0) + first_tile
    mods = [mod_ref[jnp.where(tile_idx == 0, batch, bb)] for bb in range(batch)]
    ups = []
    for bb in range(batch):
        h = _norm_mod(x_ref[bb], g_ref[...], mods[bb][3:4, :], mods[bb][4:5, :]).astype(BF16)
        ups.append(_dot(h, wu_ref[0]))
    for bb in range(batch):
        u = jnp.maximum(ups[bb], 0.0)
        y = x_ref[bb] + mods[bb][5:6, :] * _dot((u * u).astype(BF16), wd_ref[0])
        if final:
            y = y * lax.rsqrt(jnp.mean(y * y, axis=-1, keepdims=True) + EPS) * fg_ref[...]
        out_ref[bb] = y


def _mlp(tok, mods, g2, wu, wd, layer, final_g, first_tile, final):
    B, T, _ = tok.shape
    const = lambda shape: pl.BlockSpec(shape, lambda i: (0,) * len(shape))
    per_layer = lambda shape: pl.BlockSpec((1,) + shape, lambda i: (layer,) + (0,) * len(shape),
                                           pipeline_mode=pl.Buffered(1))
    return pl.pallas_call(
        functools.partial(_mlp_kernel, B, first_tile, final),
        out_shape=jax.ShapeDtypeStruct((B, T, D_MODEL), F32),
        grid=(T // TILE,),
        in_specs=[pl.BlockSpec((B, TILE, D_MODEL), lambda i: (0, i, 0)),
                  const((B + 1, 6, D_MODEL)),
                  const((1, D_MODEL)), per_layer((D_MODEL, D_FF)), per_layer((D_FF, D_MODEL)),
                  const((1, D_MODEL))],
        out_specs=pl.BlockSpec((B, TILE, D_MODEL), lambda i: (0, i, 0)),
        compiler_params=_cparams(1), name="mlp",
    )(tok, mods, g2, wu, wd, final_g)


def kernel(x, c, ctx, c_ctx, norm1_g, norm2_g, w_ada, b_ada, w_in, gla_a2, gla_a_bias, gla_norm_g,
           ml_conv, ml_i_bias, ml_f_bias, ml_norm_g, ret_decay_logit, ret_norm_g, w_branch, w_o,
           w_up, w_down, final_g):
    B, seq, _ = x.shape
    ctx_len = ctx.shape[1]
    assert B == 2 and ctx_len == TILE and seq % TILE == 0 and seq % GRID_W == 0
    tok = jnp.concatenate([ctx, x], axis=1)

    c_rows = jnp.concatenate([c, c_ctx[None, :], jnp.zeros((8 - B - 1, D_MODEL), F32)], axis=0)
    mods_all = _ada(c_rows, w_ada, b_ada)[:, 0:3].reshape(DEPTH, 3, 6, D_MODEL)

    cos_t, sin_t = _rope_tables(seq, ctx_len)

    o0 = W_GLA
    o1 = o0 + W_ML
    o2 = o1 + W_RET
    n_sg = 2 * NQK_S + 2 * BRANCH_W
    pad = lambda n: jnp.zeros((DEPTH, D_MODEL, n), F32)
    w_proj = jnp.concatenate([
        w_in[:, :, o0:o0 + 2 * NQK_M],
        w_in[:, :, o0 + 2 * NQK_M:o1], pad(LANES - 2 * NHD),
        w_in[:, :, 0:n_sg],
        w_in[:, :, o1:o2],
        w_in[:, :, n_sg:o0], pad(LANES - 2 * GLA_RANK),
    ], axis=2).astype(BF16)
    w_gate = w_in[:, :, o2:].astype(BF16)
    w_branch_b = w_branch.astype(BF16)
    w_o_b = w_o.astype(BF16)
    w_up_b = w_up.astype(BF16)
    w_down_b = w_down.astype(BF16)
    for l in range(DEPTH):
        last = l == DEPTH - 1
        skip = 1 if last else 0
        mods = mods_all[l]
        g1 = norm1_g[l][None, :]
        g2 = norm2_g[l][None, :]
        zeros = jnp.zeros((GLA_RANK, NQK_S), F32)
        a2blk = jnp.concatenate([jnp.concatenate([gla_a2[l, 0], zeros], axis=1),
                                 jnp.concatenate([zeros, gla_a2[l, 1]], axis=1)], axis=0)
        a2s = jnp.concatenate([a2blk, a2blk], axis=0).astype(BF16)
        abias = gla_a_bias[l].reshape(1, 2 * NQK_S)
        gate_bias = jnp.concatenate([ml_i_bias[l].reshape(1, NHD), ml_f_bias[l].reshape(1, NHD)], axis=1)
        dl = ret_decay_logit[l]
        dl_row = jnp.repeat(dl, DK_SMALL, axis=1).reshape(1, 2 * NQK_S)
        dl_col = jnp.broadcast_to(jnp.repeat(dl, DK_SMALL, axis=1)[:, :, None], (2, NQK_S, LANES))

        (gq, gk, ge, gv, sg_gla, gdec, mq, mk, mw, mv, sz_ml, mp, mcol, mc,
         rq, rk, re, rv, sg_ret) = _proj(tok, mods, g1, w_proj, l, a2s, abias, ml_conv[l], gate_bias,
                                         cos_t, sin_t, dl_row)
        o_f, o_b = _mix((gq, gk, ge, gv, gdec), (rq, rk, re, rv), (mq, mk, mw, mv, mp, mcol, mc), dl_col)
        norm_g = jnp.concatenate([gla_norm_g[l], ml_norm_g[l], ret_norm_g[l]])[None, :]
        tok = _merge(tok, mods, g1, w_gate, o_f, o_b, sg_gla, sz_ml, sg_ret, norm_g,
                     w_branch_b, w_o_b, l, skip)
        tok = _mlp(tok, mods, g2, w_up_b, w_down_b, l, final_g[None, :], skip, last)
    return tok
```

```python
import functools

import jax
import jax.numpy as jnp
from jax import lax
from jax.experimental import pallas as pl
from jax.experimental.pallas import tpu as pltpu

F32 = jnp.float32
BF16 = jnp.bfloat16

D_MODEL = 1024
DEPTH = 2
GRID_W = 64
CHUNK = 64
EPS = 1e-6
M_INIT = -1e30
HEADS = 4
DK_SMALL = 64
DK_ML = 128
DV = 128
BRANCH_W = HEADS * DV
GLA_RANK = 16
GLA_TAU = 16.0
ROPE_BASE = 10000.0
D_FF = 4 * D_MODEL
LANES = 128

TILE = 256
CHUNKS_PER_TILE = TILE // CHUNK
HALO = 8
VMEM_LIMIT = 56 * 1024 * 1024
VMEM_LIMIT_PROJ = 60 * 1024 * 1024

NQK_S = HEADS * DK_SMALL
NQK_M = HEADS * DK_ML
NHD = 2 * HEADS
W_GLA = 2 * NQK_S + 2 * BRANCH_W + 2 * GLA_RANK
W_ML = 2 * NQK_M + 2 * BRANCH_W + 2 * NHD
W_RET = 2 * NQK_S + 2 * BRANCH_W
W_GATE = 3 * D_MODEL

PW_MLQK = 0
PW_MLR = PW_MLQK + 2 * NQK_M
PW_MLR_WIDTH = 2 * BRANCH_W + LANES
PW_GR = PW_MLR + PW_MLR_WIDTH
PW_GR_WIDTH = 2 * (2 * NQK_S + 2 * BRANCH_W) + LANES
W_PROJ = PW_GR + PW_GR_WIDTH
MLR_V, MLR_Z, MLR_GATES = 0, BRANCH_W, 2 * BRANCH_W
GR_GLA_Q, GR_GLA_K, GR_GLA_V, GR_GLA_G = 0, NQK_S, 2 * NQK_S, 2 * NQK_S + BRANCH_W
GR_RET_Q = 2 * NQK_S + 2 * BRANCH_W
GR_RET_K, GR_RET_V, GR_RET_G = GR_RET_Q + NQK_S, GR_RET_Q + 2 * NQK_S, GR_RET_Q + 2 * NQK_S + BRANCH_W
GR_GLA_A = 2 * GR_RET_Q

MC_B, MC_W, MC_U = 0, NHD, 2 * NHD
MC_WIDTH = 3 * NHD

PAIRS = HEADS // 2
PAIR_K = 2 * DK_SMALL
PAIR_V = 2 * DV


def _cparams(n_axes, vmem_limit=VMEM_LIMIT):
    return pltpu.CompilerParams(dimension_semantics=("arbitrary",) * n_axes,
                                vmem_limit_bytes=vmem_limit)


def _log_sigmoid(z):
    return jnp.minimum(z, 0.0) - jnp.log(1.0 + jnp.exp(-jnp.abs(z)))


def _sigmoid(z):
    return 1.0 / (1.0 + jnp.exp(-z))


def _silu(z):
    return z * _sigmoid(z)


def _norm_mod(x, g, shift, scale):
    y = x * lax.rsqrt(jnp.mean(x * x, axis=-1, keepdims=True) + EPS) * g
    return y * (1.0 + scale) + shift


def _dot(a, b):
    return jnp.dot(a, b, preferred_element_type=F32)


def _dot_nt(a, b):
    return lax.dot_general(a, b, (((1,), (1,)), ((), ())), preferred_element_type=F32)


def _dot_tn(a, b):
    return lax.dot_general(a, b, (((0,), (0,)), ((), ())), preferred_element_type=F32)


def _split2(x):
    hi = x.astype(BF16)
    return hi, (x - hi.astype(F32)).astype(BF16)


def _chunk_tri(lower):
    r = lax.broadcasted_iota(jnp.int32, (TILE, TILE), 0)
    c = lax.broadcasted_iota(jnp.int32, (TILE, TILE), 1)
    same = (r // CHUNK) == (c // CHUNK)
    tri = (c <= r) if lower else (c >= r)
    return jnp.where(same & tri, 1.0, 0.0).astype(BF16)


def _chunk_total(x):
    w = x.shape[-1]
    tot = jnp.sum(x.reshape(CHUNKS_PER_TILE, CHUNK, w), axis=1, keepdims=True)
    return jnp.broadcast_to(tot, (CHUNKS_PER_TILE, CHUNK, w)).reshape(TILE, w)


def _ada_kernel(c_ref, w_ref, b_ref, o_ref):
    s = _silu(c_ref[...])
    o_ref[0] = jnp.dot(s, w_ref[0], precision=lax.Precision.HIGHEST,
                       preferred_element_type=F32) + b_ref[0]


def _ada(c_rows, w_ada, b_ada):
    tn = 1536
    return pl.pallas_call(
        _ada_kernel,
        out_shape=jax.ShapeDtypeStruct((DEPTH, 8, 6 * D_MODEL), F32),
        grid=(DEPTH, 6 * D_MODEL // tn),
        in_specs=[pl.BlockSpec((8, D_MODEL), lambda l, n: (0, 0)),
                  pl.BlockSpec((1, D_MODEL, tn), lambda l, n: (l, 0, n)),
                  pl.BlockSpec((1, 1, tn), lambda l, n: (l, 0, n))],
        out_specs=pl.BlockSpec((1, 8, tn), lambda l, n: (l, 0, n)),
        compiler_params=_cparams(2), name="ada",
    )(c_rows, w_ada, b_ada.reshape(DEPTH, 1, 6 * D_MODEL))


def _rope_kernel(ang_ref, cos_ref, sin_ref, nsin_ref):
    a = ang_ref[...]
    s = jnp.sin(a)
    cos_ref[...] = jnp.cos(a)
    sin_ref[...] = s
    nsin_ref[...] = -s


def _rope_tables(seq, ctx_len):
    rows = seq // GRID_W
    nf = DK_SMALL // 4
    inv = ROPE_BASE ** (-jnp.arange(nf, dtype=F32) / nf)
    ang = jnp.concatenate([jnp.arange(rows, dtype=F32)[:, None] * inv,
                           jnp.arange(GRID_W, dtype=F32)[:, None] * inv], axis=0)
    n = rows + GRID_W
    cos, sin, nsin = pl.pallas_call(
        _rope_kernel,
        out_shape=(jax.ShapeDtypeStruct((n, nf), F32),) * 3,
        name="rope_tables",
    )(ang)

    def per_token(first, second):
        def half(t):
            r = jnp.broadcast_to(t[:rows, None, :], (rows, GRID_W, nf))
            c = jnp.broadcast_to(t[None, rows:, :], (rows, GRID_W, nf))
            return jnp.concatenate([r, c], axis=-1)
        head = jnp.concatenate([half(first), half(second)], axis=-1).reshape(seq, DK_SMALL)
        return jnp.tile(head, (1, HEADS))

    ones = jnp.ones((ctx_len, NQK_S), F32)
    cos_t = jnp.concatenate([ones, per_token(cos, cos)], axis=0)
    sin_t = jnp.concatenate([0.0 * ones, per_token(nsin, sin)], axis=0)
    return cos_t, sin_t


def _wprep_kernel(w_ref, wp_ref, wg_ref):
    o0, o1, o2 = W_GLA, W_GLA + W_ML, W_GLA + W_ML + W_RET
    n_sg = 2 * NQK_S + 2 * BRANCH_W
    rows = w_ref.shape[1]

    def copy(dst_ref, dst, src, n):
        dst_ref[0, :, dst:dst + n] = w_ref[0, :, src:src + n].astype(BF16)

    copy(wp_ref, PW_MLQK, o0, 2 * NQK_M)
    n_mlr = 2 * BRANCH_W + 2 * NHD
    copy(wp_ref, PW_MLR, o0 + 2 * NQK_M, n_mlr)
    wp_ref[0, :, PW_MLR + n_mlr:PW_GR] = jnp.zeros((rows, PW_GR - PW_MLR - n_mlr), BF16)
    copy(wp_ref, PW_GR + GR_GLA_Q, 0, n_sg)
    copy(wp_ref, PW_GR + GR_RET_Q, o1, W_RET)
    copy(wp_ref, PW_GR + GR_GLA_A, n_sg, 2 * GLA_RANK)
    pad0 = PW_GR + GR_GLA_A + 2 * GLA_RANK
    wp_ref[0, :, pad0:W_PROJ] = jnp.zeros((rows, W_PROJ - pad0), BF16)
    copy(wg_ref, 0, o2, W_GATE)


def _wprep(w_in):
    rows = 256
    return pl.pallas_call(
        _wprep_kernel,
        out_shape=(jax.ShapeDtypeStruct((DEPTH, D_MODEL, W_PROJ), BF16),
                   jax.ShapeDtypeStruct((DEPTH, D_MODEL, W_GATE), BF16)),
        grid=(DEPTH, D_MODEL // rows),
        in_specs=[pl.BlockSpec((1, rows, w_in.shape[2]), lambda l, r: (l, r, 0))],
        out_specs=(pl.BlockSpec((1, rows, W_PROJ), lambda l, r: (l, r, 0)),
                   pl.BlockSpec((1, rows, W_GATE), lambda l, r: (l, r, 0))),
        compiler_params=_cparams(2), name="wprep",
    )(w_in)


def _proj_kernel(nt, xa_ref, xap_ref, xan_ref, xb_ref, xbp_ref, xbn_ref, x0_ref,
                 mod_ref, g_ref, w_ref, a2_ref, ab_ref, cw_ref, gb_ref, cos_ref, sin_ref, dl_ref,
                 gq_ref, gk_ref, ge_ref, gv_ref, gsg_ref, gdec_ref,
                 mq_ref, mk_ref, mkw_ref, mv_ref, msz_ref, mp0_ref, mcol_ref, mcs_ref,
                 rq_ref, rk_ref, re_ref, rv_ref, rsg_ref,
                 ext0_ref, ext1_ref, prml0_ref, prml1_ref, prgr0_ref, prgr1_ref, rdec_ref):
    i = pl.program_id(0)
    ext_refs = (ext0_ref, ext1_ref)
    prml_refs = (prml0_ref, prml1_ref)
    prgr_refs = (prgr0_ref, prgr1_ref)
    tri_ge = _chunk_tri(True)
    tri_le = _chunk_tri(False)
    row8 = lax.broadcasted_iota(jnp.int32, (NHD, TILE), 0)
    lane8 = lax.broadcasted_iota(jnp.int32, (NHD, TILE), 1)
    lane_chunk = lane8 // CHUNK
    lane_in_chunk = lane8 % CHUNK
    ti = lax.broadcasted_iota(jnp.int32, (CHUNK, CHUNK), 0)
    si = lax.broadcasted_iota(jnp.int32, (CHUNK, CHUNK), 1)
    lane = lax.broadcasted_iota(jnp.int32, (TILE, NQK_S), 1)
    first_half = (lane % DK_SMALL) < (DK_SMALL // 2)
    ext_row = lax.broadcasted_iota(jnp.int32, (TILE + 2 * HALO, 1), 0)
    cos = cos_ref[...]
    sin = sin_ref[...]
    cw = cw_ref[...]

    def project_stages(x, x_prev, x_next, tile_idx, mod_row, slot):
        st = {}

        def mlstm_qk():
            mod = mod_ref[mod_row]
            x_all = jnp.concatenate([x_prev, x, x_next], axis=0)
            h_all = _norm_mod(x_all, g_ref[...], mod[0:1, :], mod[1:2, :])
            st["h"] = h_all[HALO:HALO + TILE, :].astype(BF16)
            prev_ok = tile_idx >= 2
            next_ok = jnp.logical_and(tile_idx >= 1, tile_idx <= nt - 2)
            dead = jnp.logical_or(jnp.logical_and(ext_row < HALO, jnp.logical_not(prev_ok)),
                                  jnp.logical_and(ext_row >= TILE + HALO, jnp.logical_not(next_ok)))
            ext = _dot(h_all.astype(BF16), w_ref[0, :, PW_MLQK:PW_MLR])
            ext_refs[slot][...] = jnp.where(dead, 0.0, ext)

        def mlstm_rest():
            prml_refs[slot][...] = _dot(st["h"], w_ref[0, :, PW_MLR:PW_GR])

        def gla():
            prgr_refs[slot][:, 0:GR_RET_Q] = _dot(st["h"], w_ref[0, :, PW_GR:PW_GR + GR_RET_Q])

        def ret():
            prgr_refs[slot][:, GR_RET_Q:] = _dot(st["h"], w_ref[0, :, PW_GR + GR_RET_Q:])

        return [mlstm_qk, mlstm_rest, gla, ret]

    def project(*args):
        for stage in project_stages(*args):
            stage()

    @pl.when(i == 0)
    def _():
        lg = _log_sigmoid(dl_ref[...])
        t_in_chunk = (lax.broadcasted_iota(jnp.int32, (TILE, NQK_S), 0) % CHUNK).astype(F32)
        for d in range(2):
            lg_d = lg[:, d * NQK_S:(d + 1) * NQK_S]
            steps = (t_in_chunk + 1.0) if d == 0 else (float(CHUNK) - t_in_chunk)
            b = steps * lg_d
            rdec_ref[3 * d] = jnp.exp(b)
            rdec_ref[3 * d + 1] = jnp.exp(-b)
            rdec_ref[3 * d + 2] = jnp.exp(float(CHUNK) * lg_d - b)
        halo0 = jnp.zeros((HALO, D_MODEL), F32)
        project(x0_ref[0], halo0, halo0, 0, 2, 0)

    def rope(x):
        swapped = jnp.where(first_half, pltpu.roll(x, NQK_S - DK_SMALL // 2, 1),
                            pltpu.roll(x, DK_SMALL // 2, 1))
        return x * cos + swapped * sin

    def seg_cummax(x, reverse):
        sh = 1
        while sh < CHUNK:
            if reverse:
                y, ok = pltpu.roll(x, TILE - sh, 1), lane_in_chunk < CHUNK - sh
            else:
                y, ok = pltpu.roll(x, sh, 1), lane_in_chunk >= sh
            x = jnp.where(ok, jnp.maximum(x, y), x)
            sh *= 2
        return x

    def prepare_stages(slot, bb):
        ext_ref, prml_ref, prgr_ref = ext_refs[slot], prml_refs[slot], prgr_refs[slot]
        st = {}

        def conv():
            qk = _silu(cw[0:1, :] * ext_ref[HALO - 1:HALO - 1 + TILE, :]
                       + cw[1:2, :] * ext_ref[HALO:HALO + TILE, :]
                       + cw[2:3, :] * ext_ref[HALO + 1:HALO + 1 + TILE, :])
            mq_ref[bb] = qk[:, 0:NQK_M].astype(BF16)
            st["k_ml"] = qk[:, NQK_M:] * (DK_ML ** -0.5)
            mk_ref[bb] = st["k_ml"].astype(BF16)

        def gates_in():
            mv_ref[bb] = prml_ref[:,MLR_V:MLR_V + BRANCH_W].astype(BF16)
            msz_ref[bb] = _silu(prml_ref[:,MLR_Z:MLR_Z + BRANCH_W]).astype(BF16)
            gates_t = (prml_ref[:,MLR_GATES:MLR_GATES + 2 * NHD] + gb_ref[...]).T
            st["li_t"] = gates_t[0:NHD]
            st["lf_t"] = _log_sigmoid(gates_t[NHD:])
            lf_hi, lf_lo = _split2(st["lf_t"])
            st["lf2"] = jnp.concatenate([lf_hi, lf_lo], axis=0)

        def gates_mm():
            st["cum_f"] = _dot(st["lf2"], tri_le)
            st["cum_b"] = _dot(st["lf2"], tri_ge)

        def gates_out():
            mlstm_gates_body(st["li_t"], st["lf_t"], st["cum_f"], st["cum_b"], st["k_ml"], bb)

        def gla_in():
            gv_ref[bb] = prgr_ref[:,GR_GLA_V:GR_GLA_V + BRANCH_W].astype(BF16)
            gsg_ref[bb] = _silu(prgr_ref[:,GR_GLA_G:GR_GLA_G + BRANCH_W]).astype(BF16)
            a_hi, a_lo = _split2(prgr_ref[:,GR_GLA_A:GR_GLA_A + 2 * GLA_RANK])
            st["a2"] = jnp.concatenate([a_hi, a_lo], axis=1)

        def gla_z_mm():
            st["z"] = _dot(st["a2"], a2_ref[...])

        def gla_la():
            la = _log_sigmoid(st["z"] + ab_ref[...]) * (1.0 / GLA_TAU)
            st["la"] = la
            st["la_split"] = [_split2(la[:, d * NQK_S:(d + 1) * NQK_S]) for d in range(2)]

        def gla_cum_mm():
            st["b"] = []
            for d in range(2):
                tri = tri_ge if d == 0 else tri_le
                la_hi, la_lo = st["la_split"][d]
                st["b"].append(_dot(tri, la_hi) + _dot(tri, la_lo))

        def gla_out():
            gla_body(prgr_ref, st["la"], st["b"], bb)

        def ret():
            ret_body(prgr_ref, bb)

        return dict(conv=conv, gates_in=gates_in, gates_mm=gates_mm, gates_out=gates_out,
                    gla_in=gla_in, gla_z_mm=gla_z_mm, gla_la=gla_la, gla_cum_mm=gla_cum_mm,
                    gla_out=gla_out, ret=ret)

    def mlstm_gates_body(li_t, lf_t, cum_f, cum_b, k_ml, bb):
        b_t = jnp.where(row8 < HEADS, cum_f[0:NHD] + cum_f[NHD:], cum_b[0:NHD] + cum_b[NHD:])
        tot_t = jnp.zeros((NHD, TILE), F32)
        mloc_t = jnp.zeros((NHD, TILE), F32)
        for c in range(CHUNKS_PER_TILE):
            tot_c = jnp.sum(lf_t[:, c * CHUNK:(c + 1) * CHUNK], axis=-1, keepdims=True)
            tot_t = jnp.where(lane_chunk == c, tot_c, tot_t)
            mcs_ref[bb, c, 0:NHD, :] = jnp.broadcast_to(tot_c, (NHD, LANES))
        g_t = tot_t - b_t + li_t
        for c in range(CHUNKS_PER_TILE):
            mloc_c = jnp.max(g_t[:, c * CHUNK:(c + 1) * CHUNK], axis=-1, keepdims=True)
            mloc_t = jnp.where(lane_chunk == c, mloc_c, mloc_t)
            mcs_ref[bb, c, NHD:2 * NHD, :] = jnp.broadcast_to(mloc_c, (NHD, LANES))
        w_t = jnp.exp(g_t - mloc_t)
        r_t = li_t - b_t
        u_t = jnp.where(row8 < HEADS, seg_cummax(r_t, False), seg_cummax(r_t, True))
        cols = jnp.concatenate([b_t, w_t, u_t], axis=0).T
        mcol_ref[bb] = cols
        for j in range(NHD):
            d, hd = j // HEADS, j % HEADS
            mask = (si <= ti) if d == 0 else (si >= ti)
            mkw_ref[bb, :, d * NQK_M + hd * DK_ML:d * NQK_M + (hd + 1) * DK_ML] = (
                k_ml[:, hd * DK_ML:(hd + 1) * DK_ML] * cols[:, MC_W + j:MC_W + j + 1]).astype(BF16)
            for c in range(CHUNKS_PER_TILE):
                rows = slice(c * CHUNK, (c + 1) * CHUNK)
                a = r_t[j:j + 1, rows] - cols[rows, MC_U + j:MC_U + j + 1]
                mp0_ref[bb, j // 2, rows, (j % 2) * CHUNK:(j % 2 + 1) * CHUNK] = jnp.where(mask, jnp.exp(a), 0.0)

    def gla_body(prgr_ref, la, cums, bb):
        half_t = jnp.sum(la.reshape(2 * CHUNKS_PER_TILE, CHUNK // 2, 2 * NQK_S), axis=1).T
        q_gla = prgr_ref[:,GR_GLA_Q:GR_GLA_Q + NQK_S] * (DK_SMALL ** -0.5)
        k_gla = prgr_ref[:,GR_GLA_K:GR_GLA_K + NQK_S]
        for d in range(2):
            la_d = la[:, d * NQK_S:(d + 1) * NQK_S]
            b = cums[d]
            tot = _chunk_total(la_d)
            gq_ref[bb, :, d * NQK_S:(d + 1) * NQK_S] = (q_gla * jnp.exp(b)).astype(BF16)
            gk_ref[bb, :, d * NQK_S:(d + 1) * NQK_S] = (k_gla * jnp.exp(-b)).astype(BF16)
            ge_ref[bb, :, d * NQK_S:(d + 1) * NQK_S] = (k_gla * jnp.exp(tot - b)).astype(BF16)
            ht = half_t[d * NQK_S:(d + 1) * NQK_S, :]
            for c in range(CHUNKS_PER_TILE):
                tot_col = ht[:, 2 * c:2 * c + 1] + ht[:, 2 * c + 1:2 * c + 2]
                gdec_ref[bb, c, d * NQK_S:(d + 1) * NQK_S, :] = jnp.broadcast_to(jnp.exp(tot_col), (NQK_S, LANES))

    def ret_body(prgr_ref, bb):
        q_ret = rope(prgr_ref[:,GR_RET_Q:GR_RET_Q + NQK_S])
        k_ret = rope(prgr_ref[:,GR_RET_K:GR_RET_K + NQK_S] * (DK_SMALL ** -0.5))
        rv_ref[bb] = prgr_ref[:,GR_RET_V:GR_RET_V + BRANCH_W].astype(BF16)
        rsg_ref[bb] = _silu(prgr_ref[:,GR_RET_G:GR_RET_G + BRANCH_W]).astype(BF16)
        for d in range(2):
            rq_ref[bb, :, d * NQK_S:(d + 1) * NQK_S] = (q_ret * rdec_ref[3 * d]).astype(BF16)
            rk_ref[bb, :, d * NQK_S:(d + 1) * NQK_S] = (k_ret * rdec_ref[3 * d + 1]).astype(BF16)
            re_ref[bb, :, d * NQK_S:(d + 1) * NQK_S] = (k_ret * rdec_ref[3 * d + 2]).astype(BF16)

    next_tile = jnp.minimum(i + 1, nt - 1)
    a_ext, a_mlr, a_gla, a_ret = project_stages(xa_ref[0], xap_ref[0], xan_ref[0], i, jnp.where(i == 0, 2, 1), 1)
    b_ext, b_mlr, b_gla, b_ret = project_stages(xb_ref[0], xbp_ref[0], xbn_ref[0], next_tile, 0, 0)
    x = prepare_stages(0, 0)
    y = prepare_stages(1, 1)
    order = [a_ext, x["conv"], x["gates_in"],
             a_mlr, x["gates_mm"], x["gates_out"], x["gla_in"],
             a_gla, x["gla_z_mm"], x["gla_la"],
             a_ret, x["gla_cum_mm"],
             b_ext, x["gla_out"], x["ret"], y["conv"], y["gates_in"],
             b_mlr, y["gates_mm"], y["gates_out"], y["gla_in"],
             b_gla, y["gla_z_mm"], y["gla_la"],
             b_ret, y["gla_cum_mm"], y["gla_out"], y["ret"]]
    for stage in order:
        stage()


def _proj(tok, mods, g1, w, layer, a2s, abias, conv_w, gate_bias, cos_t, sin_t, dl_row):
    B, T, _ = tok.shape
    nt = T // TILE
    tile = lambda width: pl.BlockSpec((B, TILE, width), lambda i: (0, i, 0))
    const = lambda shape: pl.BlockSpec(shape, lambda i: (0,) * len(shape))
    per_halo = TILE // HALO
    n_halo = T // HALO
    nxt = lambda i: jnp.minimum(i + 1, nt - 1)
    prev_halo = lambda t: jnp.maximum(t * per_halo - 1, 0)
    next_halo = lambda t: jnp.minimum((t + 1) * per_halo, n_halo - 1)
    tok_shape = lambda width, dt: jax.ShapeDtypeStruct((B, T, width), dt)
    return pl.pallas_call(
        functools.partial(_proj_kernel, nt),
        out_shape=(tok_shape(2 * NQK_S, BF16), tok_shape(2 * NQK_S, BF16), tok_shape(2 * NQK_S, BF16),
                   tok_shape(BRANCH_W, BF16), tok_shape(BRANCH_W, BF16),
                   jax.ShapeDtypeStruct((B, T // CHUNK, 2 * NQK_S, LANES), F32),
                   tok_shape(NQK_M, BF16), tok_shape(NQK_M, BF16), tok_shape(2 * NQK_M, BF16),
                   tok_shape(BRANCH_W, BF16), tok_shape(BRANCH_W, BF16),
                   jax.ShapeDtypeStruct((B, 2 * PAIRS, T, 2 * CHUNK), F32),
                   tok_shape(MC_WIDTH, F32),
                   jax.ShapeDtypeStruct((B, T // CHUNK, 2 * NHD, LANES), F32),
                   tok_shape(2 * NQK_S, BF16), tok_shape(2 * NQK_S, BF16), tok_shape(2 * NQK_S, BF16),
                   tok_shape(BRANCH_W, BF16), tok_shape(BRANCH_W, BF16)),
        grid=(nt,),
        in_specs=[pl.BlockSpec((1, TILE, D_MODEL), lambda i: (1, i, 0)),
                  pl.BlockSpec((1, HALO, D_MODEL), lambda i: (1, prev_halo(i), 0)),
                  pl.BlockSpec((1, HALO, D_MODEL), lambda i: (1, next_halo(i), 0)),
                  pl.BlockSpec((1, TILE, D_MODEL), lambda i: (0, nxt(i), 0)),
                  pl.BlockSpec((1, HALO, D_MODEL), lambda i: (0, prev_halo(nxt(i)), 0)),
                  pl.BlockSpec((1, HALO, D_MODEL), lambda i: (0, next_halo(nxt(i)), 0)),
                  pl.BlockSpec((1, TILE, D_MODEL), lambda i: (0, 0, 0)),
                  const((B + 1, 6, D_MODEL)), const((1, D_MODEL)),
                  pl.BlockSpec((1, D_MODEL, W_PROJ), lambda i: (layer, 0, 0), pipeline_mode=pl.Buffered(1)),
                  const((4 * GLA_RANK, 2 * NQK_S)), const((1, 2 * NQK_S)),
                  const((3, 2 * NQK_M)), const((1, 2 * NHD)),
                  pl.BlockSpec((TILE, NQK_S), lambda i: (i, 0)),
                  pl.BlockSpec((TILE, NQK_S), lambda i: (i, 0)),
                  const((1, 2 * NQK_S))],
        out_specs=(tile(2 * NQK_S), tile(2 * NQK_S), tile(2 * NQK_S), tile(BRANCH_W), tile(BRANCH_W),
                   pl.BlockSpec((B, CHUNKS_PER_TILE, 2 * NQK_S, LANES), lambda i: (0, i, 0, 0)),
                   tile(NQK_M), tile(NQK_M), tile(2 * NQK_M), tile(BRANCH_W), tile(BRANCH_W),
                   pl.BlockSpec((B, 2 * PAIRS, TILE, 2 * CHUNK), lambda i: (0, 0, i, 0)),
                   tile(MC_WIDTH),
                   pl.BlockSpec((B, CHUNKS_PER_TILE, 2 * NHD, LANES), lambda i: (0, i, 0, 0)),
                   tile(2 * NQK_S), tile(2 * NQK_S), tile(2 * NQK_S), tile(BRANCH_W), tile(BRANCH_W)),
        scratch_shapes=[pltpu.VMEM((TILE + 2 * HALO, 2 * NQK_M), F32)] * 2
                       + [pltpu.VMEM((TILE, PW_MLR_WIDTH), F32)] * 2
                       + [pltpu.VMEM((TILE, PW_GR_WIDTH), F32)] * 2
                       + [pltpu.VMEM((6, TILE, NQK_S), F32)],
        compiler_params=_cparams(1, VMEM_LIMIT_PROJ), name="proj",
    )(tok, tok, tok, tok, tok, tok, tok, mods, g1, w, a2s, abias, conv_w, gate_bias, cos_t, sin_t, dl_row)


def _mix_kernel(batch, *refs):
    (gq_f, gq_b, gk_f, gk_b, ge_f, ge_b, gv_f, gv_b, gd_f, gd_b,
     rq_f, rq_b, rk_f, rk_b, re_f, re_b, rv_f, rv_b, rdl,
     mq_f, mq_b, mk_f, mk_b, mw_f, mw_b, mv_f, mv_b, mp_f, mp_b, mcol_f, mcol_b, mc_f, mc_b,
     of_ref, ob_ref, sg_ref, sr_ref, sm_ref, m_ref) = refs
    j = pl.program_id(0)

    @pl.when(j == 0)
    def _():
        sg_ref[...] = jnp.zeros_like(sg_ref)
        sr_ref[...] = jnp.zeros_like(sr_ref)
        sm_ref[...] = jnp.zeros_like(sm_ref)
        m_ref[...] = jnp.full_like(m_ref, M_INIT)

    row_head = lax.broadcasted_iota(jnp.int32, (2 * CHUNK, PAIR_V), 0) // CHUNK
    diag_kv = row_head == lax.broadcasted_iota(jnp.int32, (2 * CHUNK, PAIR_V), 1) // DV
    diag_kk = (lax.broadcasted_iota(jnp.int32, (2 * CHUNK, PAIR_K), 0) // CHUNK
               == lax.broadcasted_iota(jnp.int32, (2 * CHUNK, PAIR_K), 1) // DK_SMALL)
    t_idx = lax.broadcasted_iota(jnp.int32, (CHUNK, 2 * CHUNK), 0)
    s_idx = lax.broadcasted_iota(jnp.int32, (CHUNK, 2 * CHUNK), 1) % CHUNK
    causal = (s_idx <= t_idx, s_idx >= t_idx)
    ones_v = jnp.ones((CHUNK, DV), BF16)
    zeros_vext = jnp.zeros((CHUNK, PAIR_V), BF16)
    ret_dec = jnp.exp(float(CHUNK) * _log_sigmoid(rdl[...]))

    decay_sets = (
        ((gq_f, gk_f, ge_f, gv_f), (gq_b, gk_b, ge_b, gv_b), sg_ref, 0),
        ((rq_f, rk_f, re_f, rv_f), (rq_b, rk_b, re_b, rv_b), sr_ref, 2 * BRANCH_W),
    )
    ml_sets = ((mq_f, mk_f, mw_f, mv_f, mp_f, mcol_f, mc_f),
               (mq_b, mk_b, mw_b, mv_b, mp_b, mcol_b, mc_b))
    outs = (of_ref, ob_ref)

    def chunk_step(step):
        work = []
        for bb, d in ((b_, d_) for b_ in range(batch) for d_ in range(2)):
            c = step if d == 0 else CHUNKS_PER_TILE - 1 - step
            rows = pl.ds(c * CHUNK, CHUNK)
            o_ref = outs[d]
            for bi, (fset, bset, s_ref, base) in enumerate(decay_sets):
                q_r, k_r, e_r, v_r = fset if d == 0 else bset
                for p in range(PAIRS):
                    ks = slice(p * PAIR_K, (p + 1) * PAIR_K)
                    q2 = q_r[bb, rows,ks]
                    k2 = k_r[bb, rows,ks]
                    v2 = v_r[bb, rows,p * PAIR_V:(p + 1) * PAIR_V]
                    k_bd = jnp.where(diag_kk, jnp.concatenate([k2, k2], axis=0), 0.0)
                    if bi == 0:
                        dec = (gd_f if d == 0 else gd_b)[bb, c, ks, :]
                    else:
                        dec = ret_dec[d, ks, :]
                    upd = jnp.where(diag_kv, _dot_tn(e_r[bb, rows,ks], v2), 0.0)
                    work.append(dict(
                        kind="decay", bb=bb, d=d, p=p, rows=rows, o_ref=o_ref, s_ref=s_ref, base=base,
                        q2=q2, v2=v2, dec=jnp.concatenate([dec, dec], axis=1), upd=upd,
                        att=_dot_nt(q2, k_bd)))
            q_r, k_r, w_r, v_r, p_r, col_r, c_r = ml_sets[d]
            cs = c_r[bb, c]
            cols = col_r[bb, rows,:]
            for p in range(PAIRS):
                q2 = q_r[bb, rows,p * 2 * DK_ML:(p + 1) * 2 * DK_ML]
                k2 = k_r[bb, rows,p * 2 * DK_ML:(p + 1) * 2 * DK_ML]
                k_bd = jnp.where(diag_kv, jnp.concatenate([k2, k2], axis=0), 0.0)
                att = _dot_nt(q2, k_bd)
                heads = []
                for hl in range(2):
                    hd = 2 * p + hl
                    jj = d * HEADS + hd
                    vext = jnp.concatenate([v_r[bb, rows,hd * DV:(hd + 1) * DV], ones_v], axis=1)
                    st = sm_ref[bb, d, hd]
                    heads.append(dict(
                        hd=hd, st=st, vext=vext,
                        inter=_dot(q2[:, hl * DK_ML:(hl + 1) * DK_ML], st.astype(BF16)),
                        upd=_dot_tn(w_r[bb, rows,hd * DK_ML:(hd + 1) * DK_ML], vext),
                        u=jnp.broadcast_to(cols[:, MC_U + jj:MC_U + jj + 1], (CHUNK, LANES)),
                        b=jnp.broadcast_to(cols[:, MC_B + jj:MC_B + jj + 1], (CHUNK, LANES))))
                work.append(dict(kind="ml", bb=bb, d=d, p=p, rows=rows, o_ref=o_ref, att=att, heads=heads,
                                 p0=p_r[bb, p, rows, :], cs=cs))
        for w in work:
            bb, d, p, rows, o_ref = w["bb"], w["d"], w["p"], w["rows"], w["o_ref"]
            if w["kind"] == "decay":
                s_ref = w["s_ref"]
                s_prev = s_ref[bb, d, p]
                att = jnp.where(causal[d], w["att"], 0.0).astype(BF16)
                v_bd = jnp.where(diag_kv, jnp.concatenate([w["v2"], w["v2"]], axis=0), 0.0)
                lhs = jnp.concatenate([att, w["q2"]], axis=1)
                rhs = jnp.concatenate([v_bd, s_prev.astype(BF16)], axis=0)
                o_ref[bb, rows, w["base"] + p * PAIR_V:w["base"] + (p + 1) * PAIR_V] = _dot(lhs, rhs)
                s_ref[bb, d, p] = w["dec"] * s_prev + w["upd"]
            else:
                sc = (w["att"] * w["p0"]).astype(BF16)
                for hl, hw in enumerate(w["heads"]):
                    hd = hw["hd"]
                    jj = d * HEADS + hd
                    vz = (jnp.concatenate([hw["vext"], zeros_vext], axis=0) if hl == 0
                          else jnp.concatenate([zeros_vext, hw["vext"]], axis=0))
                    intra = _dot(sc, vz)
                    m_prev = m_ref[bb, jj:jj + 1, :]
                    u = hw["u"]
                    mx = jnp.maximum(u, m_prev)
                    f = jnp.exp(u - mx)
                    e = jnp.exp(m_prev - mx)
                    num = f * intra[:, 0:DV] + e * hw["inter"][:, 0:DV]
                    den = f * intra[:, DV:] + e * hw["inter"][:, DV:]
                    den = jnp.maximum(jnp.abs(den), jnp.exp(-(hw["b"] + mx)))
                    o_ref[bb, rows, BRANCH_W + hd * DV:BRANCH_W + (hd + 1) * DV] = num / den
                    tot = w["cs"][jj:jj + 1, :]
                    mloc = w["cs"][NHD + jj:NHD + jj + 1, :]
                    m_new = jnp.maximum(tot + m_prev, mloc)
                    keep = jnp.exp(tot + m_prev - m_new)
                    gain = jnp.exp(mloc - m_new)
                    sm_ref[bb, d, hd] = (jnp.concatenate([keep, keep], axis=1) * hw["st"]
                                         + jnp.concatenate([gain, gain], axis=1) * hw["upd"])
                    m_ref[bb, jj:jj + 1, :] = m_new

    for step in range(CHUNKS_PER_TILE):
        chunk_step(step)


def _mix(gla, ret, ml, ret_dl_col):
    gq, gk, ge, gv, gdec = gla
    rq, rk, re, rv = ret
    mq, mk, mw, mv, mp, mcol, mc = ml
    B, T, _ = gv.shape
    nt = T // TILE

    def tf(j):
        return j

    def tb(j):
        return jnp.where(j == 0, 0, nt - j)

    def pair(width, lane_block_b=0):
        return [pl.BlockSpec((B, TILE, width), lambda j: (0, tf(j), 0)),
                pl.BlockSpec((B, TILE, width), lambda j: (0, tb(j), lane_block_b))]

    dec_pair = [pl.BlockSpec((B, CHUNKS_PER_TILE, NQK_S, LANES), lambda j: (0, tf(j), 0, 0)),
                pl.BlockSpec((B, CHUNKS_PER_TILE, NQK_S, LANES), lambda j: (0, tb(j), 1, 0))]
    p0_pair = [pl.BlockSpec((B, PAIRS, TILE, 2 * CHUNK), lambda j: (0, 0, tf(j), 0)),
               pl.BlockSpec((B, PAIRS, TILE, 2 * CHUNK), lambda j: (0, 1, tb(j), 0))]
    cs_pair = [pl.BlockSpec((B, CHUNKS_PER_TILE, 2 * NHD, LANES), lambda j: (0, tf(j), 0, 0)),
               pl.BlockSpec((B, CHUNKS_PER_TILE, 2 * NHD, LANES), lambda j: (0, tb(j), 0, 0))]
    decay_specs = pair(NQK_S, 1) + pair(NQK_S, 1) + pair(NQK_S, 1) + pair(BRANCH_W)
    in_specs = (decay_specs + dec_pair + decay_specs
                + [pl.BlockSpec((2, NQK_S, LANES), lambda j: (0, 0, 0))]
                + pair(NQK_M) + pair(NQK_M) + pair(NQK_M, 1) + pair(BRANCH_W) + p0_pair
                + pair(MC_WIDTH) + cs_pair)
    args = (gq, gq, gk, gk, ge, ge, gv, gv, gdec, gdec,
            rq, rq, rk, rk, re, re, rv, rv, ret_dl_col,
            mq, mq, mk, mk, mw, mw, mv, mv, mp, mp, mcol, mcol, mc, mc)
    return pl.pallas_call(
        functools.partial(_mix_kernel, B),
        out_shape=(jax.ShapeDtypeStruct((B, T, 3 * BRANCH_W), F32),) * 2,
        grid=(nt,),
        in_specs=in_specs,
        out_specs=(pl.BlockSpec((B, TILE, 3 * BRANCH_W), lambda j: (0, tf(j), 0)),
                   pl.BlockSpec((B, TILE, 3 * BRANCH_W), lambda j: (0, tb(j), 0))),
        scratch_shapes=[pltpu.VMEM((B, 2, PAIRS, PAIR_K, PAIR_V), F32),
                        pltpu.VMEM((B, 2, PAIRS, PAIR_K, PAIR_V), F32),
                        pltpu.VMEM((B, 2, HEADS, DK_ML, 2 * DV), F32),
                        pltpu.VMEM((B, NHD, LANES), F32)],
        compiler_params=_cparams(1), name="mix",
    )(*args)


def _merge_kernel(batch, first_tile, x_ref, mod_ref, g_ref, wg_ref, of_ref, ob_ref, sgg_ref, sgm_ref, sgr_ref,
                  ng_ref, wb_ref, wo_ref, out_ref):
    tile_idx = pl.program_id(0) + first_tile
    side = (sgg_ref, sgm_ref, sgr_ref)
    mods = [mod_ref[jnp.where(tile_idx == 0, batch, bb)] for bb in range(batch)]
    hs = [_norm_mod(x_ref[bb], g_ref[...], mods[bb][0:1, :], mods[bb][1:2, :]).astype(BF16)
          for bb in range(batch)]
    accs = [jnp.zeros((TILE, D_MODEL), F32) for _ in range(batch)]
    for br in range(3):
        for bb in range(batch):
            parts = []
            for hd in range(HEADS):
                cols = slice(br * BRANCH_W + hd * DV, br * BRANCH_W + (hd + 1) * DV)
                oh = of_ref[bb, :, cols] + ob_ref[bb, :, cols]
                if br > 0:
                    oh = oh - jnp.mean(oh, axis=-1, keepdims=True)
                parts.append(oh * lax.rsqrt(jnp.mean(oh * oh, axis=-1, keepdims=True) + EPS))
            y = jnp.concatenate(parts, axis=1) * ng_ref[:, br * BRANCH_W:(br + 1) * BRANCH_W]
            y = (y * side[br][bb]).astype(BF16)
            gate = _sigmoid(_dot(hs[bb], wg_ref[0, :, br * D_MODEL:(br + 1) * D_MODEL]))
            accs[bb] = accs[bb] + gate * _dot(y, wb_ref[0, br])
    for bb in range(batch):
        out = _dot(accs[bb].astype(BF16), wo_ref[0])
        out_ref[bb] = x_ref[bb] + mods[bb][2:3, :] * out


def _merge(tok, mods, g1, wgate, o_f, o_b, sg_gla, sz_ml, sg_ret, norm_g, wb, wo, layer, skip):
    B, T, _ = tok.shape
    nt = T // TILE - skip
    tile = lambda width: pl.BlockSpec((B, TILE, width), lambda i: (0, i + skip, 0))
    const = lambda shape: pl.BlockSpec(shape, lambda i: (0,) * len(shape))
    per_layer = lambda shape: pl.BlockSpec((1,) + shape, lambda i: (layer,) + (0,) * len(shape),
                                           pipeline_mode=pl.Buffered(1))
    return pl.pallas_call(
        functools.partial(_merge_kernel, B, skip),
        out_shape=jax.ShapeDtypeStruct((B, nt * TILE, D_MODEL), F32),
        grid=(nt,),
        in_specs=[tile(D_MODEL), const((B + 1, 6, D_MODEL)),
                  const((1, D_MODEL)), per_layer((D_MODEL, W_GATE)),
                  tile(3 * BRANCH_W), tile(3 * BRANCH_W),
                  tile(BRANCH_W), tile(BRANCH_W), tile(BRANCH_W),
                  const((1, 3 * BRANCH_W)), per_layer((3, BRANCH_W, D_MODEL)),
                  per_layer((D_MODEL, D_MODEL))],
        out_specs=pl.BlockSpec((B, TILE, D_MODEL), lambda i: (0, i, 0)),
        compiler_params=_cparams(1), name="merge",
    )(tok, mods, g1, wgate, o_f, o_b, sg_gla, sz_ml, sg_ret, norm_g, wb, wo)


def _mlp_kernel(batch, first_tile, final, x_ref, mod_ref, g_ref, wu_ref, wd_ref, fg_ref, out_ref):
    tile_idx = pl.program_id(0) + first_tile
    mods = [mod_ref[jnp.where(tile_idx == 0, batch, bb)] for bb in range(batch)]
    ups = []
    for bb in range(batch):
        h = _norm_mod(x_ref[bb], g_ref[...], mods[bb][3:4, :], mods[bb][4:5, :]).astype(BF16)
        ups.append(_dot(h, wu_ref[0]))
    for bb in range(batch):
        u = jnp.maximum(ups[bb], 0.0)
        y = x_ref[bb] + mods[bb][5:6, :] * _dot((u * u).astype(BF16), wd_ref[0])
        if final:
            y = y * lax.rsqrt(jnp.mean(y * y, axis=-1, keepdims=True) + EPS) * fg_ref[...]
        out_ref[bb] = y


def _mlp(tok, mods, g2, wu, wd, layer, final_g, first_tile, final):
    B, T, _ = tok.shape
    const = lambda shape: pl.BlockSpec(shape, lambda i: (0,) * len(shape))
    per_layer = lambda shape: pl.BlockSpec((1,) + shape, lambda i: (layer,) + (0,) * len(shape),
                                           pipeline_mode=pl.Buffered(1))
    return pl.pallas_call(
        functools.partial(_mlp_kernel, B, first_tile, final),
        out_shape=jax.ShapeDtypeStruct((B, T, D_MODEL), F32),
        grid=(T // TILE,),
        in_specs=[pl.BlockSpec((B, TILE, D_MODEL), lambda i: (0, i, 0)),
                  const((B + 1, 6, D_MODEL)),
                  const((1, D_MODEL)), per_layer((D_MODEL, D_FF)), per_layer((D_FF, D_MODEL)),
                  const((1, D_MODEL))],
        out_specs=pl.BlockSpec((B, TILE, D_MODEL), lambda i: (0, i, 0)),
        compiler_params=_cparams(1), name="mlp",
    )(tok, mods, g2, wu, wd, final_g)


def kernel(x, c, ctx, c_ctx, norm1_g, norm2_g, w_ada, b_ada, w_in, gla_a2, gla_a_bias, gla_norm_g,
           ml_conv, ml_i_bias, ml_f_bias, ml_norm_g, ret_decay_logit, ret_norm_g, w_branch, w_o,
           w_up, w_down, final_g):
    B, seq, _ = x.shape
    ctx_len = ctx.shape[1]
    assert B == 2 and ctx_len == TILE and seq % TILE == 0 and seq % GRID_W == 0
    tok = jnp.concatenate([ctx, x], axis=1)

    c_rows = jnp.concatenate([c, c_ctx[None, :], jnp.zeros((8 - B - 1, D_MODEL), F32)], axis=0)
    mods_all = _ada(c_rows, w_ada, b_ada)[:, 0:3].reshape(DEPTH, 3, 6, D_MODEL)

    cos_t, sin_t = _rope_tables(seq, ctx_len)

    w_proj, w_gate = _wprep(w_in)
    w_branch_b = w_branch.astype(BF16)
    w_o_b = w_o.astype(BF16)
    w_up_b = w_up.astype(BF16)
    w_down_b = w_down.astype(BF16)
    for l in range(DEPTH):
        last = l == DEPTH - 1
        skip = 1 if last else 0
        mods = mods_all[l]
        g1 = norm1_g[l][None, :]
        g2 = norm2_g[l][None, :]
        zeros = jnp.zeros((GLA_RANK, NQK_S), F32)
        a2blk = jnp.concatenate([jnp.concatenate([gla_a2[l, 0], zeros], axis=1),
                                 jnp.concatenate([zeros, gla_a2[l, 1]], axis=1)], axis=0)
        a2s = jnp.concatenate([a2blk, a2blk], axis=0).astype(BF16)
        abias = gla_a_bias[l].reshape(1, 2 * NQK_S)
        gate_bias = jnp.concatenate([ml_i_bias[l].reshape(1, NHD), ml_f_bias[l].reshape(1, NHD)], axis=1)
        dl = ret_decay_logit[l]
        dl_row = jnp.repeat(dl, DK_SMALL, axis=1).reshape(1, 2 * NQK_S)
        dl_col = jnp.broadcast_to(jnp.repeat(dl, DK_SMALL, axis=1)[:, :, None], (2, NQK_S, LANES))

        (gq, gk, ge, gv, sg_gla, gdec, mq, mk, mw, mv, sz_ml, mp, mcol, mc,
         rq, rk, re, rv, sg_ret) = _proj(tok, mods, g1, w_proj, l, a2s, abias, ml_conv[l], gate_bias,
                                         cos_t, sin_t, dl_row)
        o_f, o_b = _mix((gq, gk, ge, gv, gdec), (rq, rk, re, rv), (mq, mk, mw, mv, mp, mcol, mc), dl_col)
        norm_g = jnp.concatenate([gla_norm_g[l], ml_norm_g[l], ret_norm_g[l]])[None, :]
        tok = _merge(tok, mods, g1, w_gate, o_f, o_b, sg_gla, sz_ml, sg_ret, norm_g,
                     w_branch_b, w_o_b, l, skip)
        tok = _mlp(tok, mods, g2, w_up_b, w_down_b, l, final_g[None, :], skip, last)
    return tok
```

```python
import functools

import jax
import jax.numpy as jnp
from jax import lax
from jax.experimental import pallas as pl
from jax.experimental.pallas import tpu as pltpu

F32 = jnp.float32
BF16 = jnp.bfloat16

D_MODEL = 1024
DEPTH = 2
GRID_W = 64
CHUNK = 64
EPS = 1e-6
M_INIT = -1e30
HEADS = 4
DK_SMALL = 64
DK_ML = 128
DV = 128
BRANCH_W = HEADS * DV
GLA_RANK = 16
GLA_TAU = 16.0
ROPE_BASE = 10000.0
D_FF = 4 * D_MODEL
LANES = 128

TILE = 256
CHUNKS_PER_TILE = TILE // CHUNK
HALO = 8
VMEM_LIMIT = 56 * 1024 * 1024
VMEM_LIMIT_PROJ = 60 * 1024 * 1024

NQK_S = HEADS * DK_SMALL
NQK_M = HEADS * DK_ML
NHD = 2 * HEADS
W_GLA = 2 * NQK_S + 2 * BRANCH_W + 2 * GLA_RANK
W_ML = 2 * NQK_M + 2 * BRANCH_W + 2 * NHD
W_RET = 2 * NQK_S + 2 * BRANCH_W
W_GATE = 3 * D_MODEL

PW_MLQK = 0
PW_MLR = PW_MLQK + 2 * NQK_M
PW_MLR_WIDTH = 2 * BRANCH_W + LANES
PW_GR = PW_MLR + PW_MLR_WIDTH
PW_GR_WIDTH = 2 * (2 * NQK_S + 2 * BRANCH_W) + LANES
W_PROJ = PW_GR + PW_GR_WIDTH
MLR_V, MLR_Z, MLR_GATES = 0, BRANCH_W, 2 * BRANCH_W
GR_GLA_Q, GR_GLA_K, GR_GLA_V, GR_GLA_G = 0, NQK_S, 2 * NQK_S, 2 * NQK_S + BRANCH_W
GR_RET_Q = 2 * NQK_S + 2 * BRANCH_W
GR_RET_K, GR_RET_V, GR_RET_G = GR_RET_Q + NQK_S, GR_RET_Q + 2 * NQK_S, GR_RET_Q + 2 * NQK_S + BRANCH_W
GR_GLA_A = 2 * GR_RET_Q

MC_B, MC_W, MC_U = 0, NHD, 2 * NHD
MC_WIDTH = 3 * NHD

PAIRS = HEADS // 2
PAIR_K = 2 * DK_SMALL
PAIR_V = 2 * DV


def _cparams(n_axes, vmem_limit=VMEM_LIMIT):
    return pltpu.CompilerParams(dimension_semantics=("arbitrary",) * n_axes,
                                vmem_limit_bytes=vmem_limit)


def _log_sigmoid(z):
    return jnp.minimum(z, 0.0) - jnp.log(1.0 + jnp.exp(-jnp.abs(z)))


def _sigmoid(z):
    return 1.0 / (1.0 + jnp.exp(-z))


def _silu(z):
    return z * _sigmoid(z)


def _norm_mod(x, g, shift, scale):
    y = x * lax.rsqrt(jnp.mean(x * x, axis=-1, keepdims=True) + EPS) * g
    return y * (1.0 + scale) + shift


def _dot(a, b):
    return jnp.dot(a, b, preferred_element_type=F32)


def _dot_nt(a, b):
    return lax.dot_general(a, b, (((1,), (1,)), ((), ())), preferred_element_type=F32)


def _dot_tn(a, b):
    return lax.dot_general(a, b, (((0,), (0,)), ((), ())), preferred_element_type=F32)


def _split2(x):
    hi = x.astype(BF16)
    return hi, (x - hi.astype(F32)).astype(BF16)


def _chunk_tri(lower):
    r = lax.broadcasted_iota(jnp.int32, (TILE, TILE), 0)
    c = lax.broadcasted_iota(jnp.int32, (TILE, TILE), 1)
    same = (r // CHUNK) == (c // CHUNK)
    tri = (c <= r) if lower else (c >= r)
    return jnp.where(same & tri, 1.0, 0.0).astype(BF16)


def _chunk_total(x):
    w = x.shape[-1]
    tot = jnp.sum(x.reshape(CHUNKS_PER_TILE, CHUNK, w), axis=1, keepdims=True)
    return jnp.broadcast_to(tot, (CHUNKS_PER_TILE, CHUNK, w)).reshape(TILE, w)


def _ada_kernel(c_ref, w_ref, b_ref, o_ref):
    s = _silu(c_ref[...])
    o_ref[0] = jnp.dot(s, w_ref[0], precision=lax.Precision.HIGHEST,
                       preferred_element_type=F32) + b_ref[0]


def _ada(c_rows, w_ada, b_ada):
    tn = 1536
    return pl.pallas_call(
        _ada_kernel,
        out_shape=jax.ShapeDtypeStruct((DEPTH, 8, 6 * D_MODEL), F32),
        grid=(DEPTH, 6 * D_MODEL // tn),
        in_specs=[pl.BlockSpec((8, D_MODEL), lambda l, n: (0, 0)),
                  pl.BlockSpec((1, D_MODEL, tn), lambda l, n: (l, 0, n)),
                  pl.BlockSpec((1, 1, tn), lambda l, n: (l, 0, n))],
        out_specs=pl.BlockSpec((1, 8, tn), lambda l, n: (l, 0, n)),
        compiler_params=_cparams(2), name="ada",
    )(c_rows, w_ada, b_ada.reshape(DEPTH, 1, 6 * D_MODEL))


def _rope_kernel(ang_ref, cos_ref, sin_ref, nsin_ref):
    a = ang_ref[...]
    s = jnp.sin(a)
    cos_ref[...] = jnp.cos(a)
    sin_ref[...] = s
    nsin_ref[...] = -s


def _rope_tables(seq, ctx_len):
    rows = seq // GRID_W
    nf = DK_SMALL // 4
    inv = ROPE_BASE ** (-jnp.arange(nf, dtype=F32) / nf)
    ang = jnp.concatenate([jnp.arange(rows, dtype=F32)[:, None] * inv,
                           jnp.arange(GRID_W, dtype=F32)[:, None] * inv], axis=0)
    n = rows + GRID_W
    cos, sin, nsin = pl.pallas_call(
        _rope_kernel,
        out_shape=(jax.ShapeDtypeStruct((n, nf), F32),) * 3,
        name="rope_tables",
    )(ang)

    def per_token(first, second):
        def half(t):
            r = jnp.broadcast_to(t[:rows, None, :], (rows, GRID_W, nf))
            c = jnp.broadcast_to(t[None, rows:, :], (rows, GRID_W, nf))
            return jnp.concatenate([r, c], axis=-1)
        head = jnp.concatenate([half(first), half(second)], axis=-1).reshape(seq, DK_SMALL)
        return jnp.tile(head, (1, HEADS))

    ones = jnp.ones((ctx_len, NQK_S), F32)
    cos_t = jnp.concatenate([ones, per_token(cos, cos)], axis=0)
    sin_t = jnp.concatenate([0.0 * ones, per_token(nsin, sin)], axis=0)
    return cos_t, sin_t


def _transpose_kernel(w_ref, o_ref):
    o_ref[0] = w_ref[0].T


def _transposed_weight(w_rows, row0, n):
    blk = 256
    return pl.pallas_call(
        _transpose_kernel,
        out_shape=jax.ShapeDtypeStruct((DEPTH, D_MODEL, n), BF16),
        grid=(DEPTH, n // blk),
        in_specs=[pl.BlockSpec((1, blk, D_MODEL), lambda l, c: (l, row0 // blk + c, 0))],
        out_specs=pl.BlockSpec((1, D_MODEL, blk), lambda l, c: (l, 0, c)),
        compiler_params=_cparams(2), name="weight_transpose",
    )(w_rows)


def _wprep(w_in):
    o0, o1, o2 = W_GLA, W_GLA + W_ML, W_GLA + W_ML + W_RET
    n_sg = 2 * NQK_S + 2 * BRANCH_W
    w_t = jnp.swapaxes(w_in, 1, 2)
    pad = lambda n: jnp.zeros((DEPTH, n, D_MODEL), F32)
    w_rows = jnp.concatenate([
        w_t[:, o0:o0 + 2 * NQK_M],
        w_t[:, o0 + 2 * NQK_M:o1], pad(LANES - 2 * NHD),
        w_t[:, 0:n_sg],
        w_t[:, o1:o2],
        w_t[:, n_sg:o0], pad(LANES - 2 * GLA_RANK),
        w_t[:, o2:],
    ], axis=1).astype(BF16)
    return _transposed_weight(w_rows, 0, W_PROJ), _transposed_weight(w_rows, W_PROJ, W_GATE)


def _proj_kernel(nt, xa_ref, xap_ref, xan_ref, xb_ref, xbp_ref, xbn_ref, x0_ref,
                 mod_ref, g_ref, w_ref, a2_ref, ab_ref, cw_ref, gb_ref, cos_ref, sin_ref, dl_ref,
                 gq_ref, gk_ref, ge_ref, gv_ref, gsg_ref, gdec_ref,
                 mq_ref, mk_ref, mkw_ref, mv_ref, msz_ref, mp0_ref, mcol_ref, mcs_ref,
                 rq_ref, rk_ref, re_ref, rv_ref, rsg_ref,
                 ext0_ref, ext1_ref, prml0_ref, prml1_ref, prgr0_ref, prgr1_ref, rdec_ref):
    i = pl.program_id(0)
    ext_refs = (ext0_ref, ext1_ref)
    prml_refs = (prml0_ref, prml1_ref)
    prgr_refs = (prgr0_ref, prgr1_ref)
    tri_ge = _chunk_tri(True)
    tri_le = _chunk_tri(False)
    row8 = lax.broadcasted_iota(jnp.int32, (NHD, TILE), 0)
    lane8 = lax.broadcasted_iota(jnp.int32, (NHD, TILE), 1)
    lane_chunk = lane8 // CHUNK
    lane_in_chunk = lane8 % CHUNK
    ti = lax.broadcasted_iota(jnp.int32, (CHUNK, CHUNK), 0)
    si = lax.broadcasted_iota(jnp.int32, (CHUNK, CHUNK), 1)
    lane = lax.broadcasted_iota(jnp.int32, (TILE, NQK_S), 1)
    first_half = (lane % DK_SMALL) < (DK_SMALL // 2)
    ext_row = lax.broadcasted_iota(jnp.int32, (TILE + 2 * HALO, 1), 0)
    cos = cos_ref[...]
    sin = sin_ref[...]
    cw = cw_ref[...]

    def project_stages(x, x_prev, x_next, tile_idx, mod_row, slot):
        st = {}

        def mlstm_qk():
            mod = mod_ref[mod_row]
            x_all = jnp.concatenate([x_prev, x, x_next], axis=0)
            h_all = _norm_mod(x_all, g_ref[...], mod[0:1, :], mod[1:2, :])
            st["h"] = h_all[HALO:HALO + TILE, :].astype(BF16)
            prev_ok = tile_idx >= 2
            next_ok = jnp.logical_and(tile_idx >= 1, tile_idx <= nt - 2)
            dead = jnp.logical_or(jnp.logical_and(ext_row < HALO, jnp.logical_not(prev_ok)),
                                  jnp.logical_and(ext_row >= TILE + HALO, jnp.logical_not(next_ok)))
            ext = _dot(h_all.astype(BF16), w_ref[0, :, PW_MLQK:PW_MLR])
            ext_refs[slot][...] = jnp.where(dead, 0.0, ext)

        def mlstm_rest():
            prml_refs[slot][...] = _dot(st["h"], w_ref[0, :, PW_MLR:PW_GR])

        def gla():
            prgr_refs[slot][:, 0:GR_RET_Q] = _dot(st["h"], w_ref[0, :, PW_GR:PW_GR + GR_RET_Q])

        def ret():
            prgr_refs[slot][:, GR_RET_Q:] = _dot(st["h"], w_ref[0, :, PW_GR + GR_RET_Q:])

        return [mlstm_qk, mlstm_rest, gla, ret]

    def project(*args):
        for stage in project_stages(*args):
            stage()

    @pl.when(i == 0)
    def _():
        lg = _log_sigmoid(dl_ref[...])
        t_in_chunk = (lax.broadcasted_iota(jnp.int32, (TILE, NQK_S), 0) % CHUNK).astype(F32)
        for d in range(2):
            lg_d = lg[:, d * NQK_S:(d + 1) * NQK_S]
            steps = (t_in_chunk + 1.0) if d == 0 else (float(CHUNK) - t_in_chunk)
            b = steps * lg_d
            rdec_ref[3 * d] = jnp.exp(b)
            rdec_ref[3 * d + 1] = jnp.exp(-b)
            rdec_ref[3 * d + 2] = jnp.exp(float(CHUNK) * lg_d - b)
        halo0 = jnp.zeros((HALO, D_MODEL), F32)
        project(x0_ref[0], halo0, halo0, 0, 2, 0)

    def rope(x):
        swapped = jnp.where(first_half, pltpu.roll(x, NQK_S - DK_SMALL // 2, 1),
                            pltpu.roll(x, DK_SMALL // 2, 1))
        return x * cos + swapped * sin

    def seg_cummax(x, reverse):
        sh = 1
        while sh < CHUNK:
            if reverse:
                y, ok = pltpu.roll(x, TILE - sh, 1), lane_in_chunk < CHUNK - sh
            else:
                y, ok = pltpu.roll(x, sh, 1), lane_in_chunk >= sh
            x = jnp.where(ok, jnp.maximum(x, y), x)
            sh *= 2
        return x

    def prepare_stages(slot, bb):
        ext_ref, prml_ref, prgr_ref = ext_refs[slot], prml_refs[slot], prgr_refs[slot]
        st = {}

        def conv():
            qk = _silu(cw[0:1, :] * ext_ref[HALO - 1:HALO - 1 + TILE, :]
                       + cw[1:2, :] * ext_ref[HALO:HALO + TILE, :]
                       + cw[2:3, :] * ext_ref[HALO + 1:HALO + 1 + TILE, :])
            mq_ref[bb] = qk[:, 0:NQK_M].astype(BF16)
            st["k_ml"] = qk[:, NQK_M:] * (DK_ML ** -0.5)
            mk_ref[bb] = st["k_ml"].astype(BF16)

        def gates_in():
            mv_ref[bb] = prml_ref[:,MLR_V:MLR_V + BRANCH_W].astype(BF16)
            msz_ref[bb] = _silu(prml_ref[:,MLR_Z:MLR_Z + BRANCH_W]).astype(BF16)
            gates_t = (prml_ref[:,MLR_GATES:MLR_GATES + 2 * NHD] + gb_ref[...]).T
            st["li_t"] = gates_t[0:NHD]
            st["lf_t"] = _log_sigmoid(gates_t[NHD:])
            lf_hi, lf_lo = _split2(st["lf_t"])
            st["lf2"] = jnp.concatenate([lf_hi, lf_lo], axis=0)

        def gates_mm():
            st["cum_f"] = _dot(st["lf2"], tri_le)
            st["cum_b"] = _dot(st["lf2"], tri_ge)

        def gates_out():
            mlstm_gates_body(st["li_t"], st["lf_t"], st["cum_f"], st["cum_b"], st["k_ml"], bb)

        def gla_in():
            gv_ref[bb] = prgr_ref[:,GR_GLA_V:GR_GLA_V + BRANCH_W].astype(BF16)
            gsg_ref[bb] = _silu(prgr_ref[:,GR_GLA_G:GR_GLA_G + BRANCH_W]).astype(BF16)
            a_hi, a_lo = _split2(prgr_ref[:,GR_GLA_A:GR_GLA_A + 2 * GLA_RANK])
            st["a2"] = jnp.concatenate([a_hi, a_lo], axis=1)

        def gla_z_mm():
            st["z"] = _dot(st["a2"], a2_ref[...])

        def gla_la():
            la = _log_sigmoid(st["z"] + ab_ref[...]) * (1.0 / GLA_TAU)
            st["la"] = la
            st["la_split"] = [_split2(la[:, d * NQK_S:(d + 1) * NQK_S]) for d in range(2)]

        def gla_cum_mm():
            st["b"] = []
            for d in range(2):
                tri = tri_ge if d == 0 else tri_le
                la_hi, la_lo = st["la_split"][d]
                st["b"].append(_dot(tri, la_hi) + _dot(tri, la_lo))

        def gla_out():
            gla_body(prgr_ref, st["la"], st["b"], bb)

        def ret():
            ret_body(prgr_ref, bb)

        return dict(conv=conv, gates_in=gates_in, gates_mm=gates_mm, gates_out=gates_out,
                    gla_in=gla_in, gla_z_mm=gla_z_mm, gla_la=gla_la, gla_cum_mm=gla_cum_mm,
                    gla_out=gla_out, ret=ret)

    def mlstm_gates_body(li_t, lf_t, cum_f, cum_b, k_ml, bb):
        b_t = jnp.where(row8 < HEADS, cum_f[0:NHD] + cum_f[NHD:], cum_b[0:NHD] + cum_b[NHD:])
        tot_t = jnp.zeros((NHD, TILE), F32)
        mloc_t = jnp.zeros((NHD, TILE), F32)
        for c in range(CHUNKS_PER_TILE):
            tot_c = jnp.sum(lf_t[:, c * CHUNK:(c + 1) * CHUNK], axis=-1, keepdims=True)
            tot_t = jnp.where(lane_chunk == c, tot_c, tot_t)
            mcs_ref[bb, c, 0:NHD, :] = jnp.broadcast_to(tot_c, (NHD, LANES))
        g_t = tot_t - b_t + li_t
        for c in range(CHUNKS_PER_TILE):
            mloc_c = jnp.max(g_t[:, c * CHUNK:(c + 1) * CHUNK], axis=-1, keepdims=True)
            mloc_t = jnp.where(lane_chunk == c, mloc_c, mloc_t)
            mcs_ref[bb, c, NHD:2 * NHD, :] = jnp.broadcast_to(mloc_c, (NHD, LANES))
        w_t = jnp.exp(g_t - mloc_t)
        r_t = li_t - b_t
        u_t = jnp.where(row8 < HEADS, seg_cummax(r_t, False), seg_cummax(r_t, True))
        cols = jnp.concatenate([b_t, w_t, u_t], axis=0).T
        mcol_ref[bb] = cols
        for j in range(NHD):
            d, hd = j // HEADS, j % HEADS
            mask = (si <= ti) if d == 0 else (si >= ti)
            mkw_ref[bb, :, d * NQK_M + hd * DK_ML:d * NQK_M + (hd + 1) * DK_ML] = (
                k_ml[:, hd * DK_ML:(hd + 1) * DK_ML] * cols[:, MC_W + j:MC_W + j + 1]).astype(BF16)
            for c in range(CHUNKS_PER_TILE):
                rows = slice(c * CHUNK, (c + 1) * CHUNK)
                a = r_t[j:j + 1, rows] - cols[rows, MC_U + j:MC_U + j + 1]
                mp0_ref[bb, j // 2, rows, (j % 2) * CHUNK:(j % 2 + 1) * CHUNK] = jnp.where(mask, jnp.exp(a), 0.0)

    def gla_body(prgr_ref, la, cums, bb):
        half_t = jnp.sum(la.reshape(2 * CHUNKS_PER_TILE, CHUNK // 2, 2 * NQK_S), axis=1).T
        q_gla = prgr_ref[:,GR_GLA_Q:GR_GLA_Q + NQK_S] * (DK_SMALL ** -0.5)
        k_gla = prgr_ref[:,GR_GLA_K:GR_GLA_K + NQK_S]
        for d in range(2):
            la_d = la[:, d * NQK_S:(d + 1) * NQK_S]
            b = cums[d]
            tot = _chunk_total(la_d)
            gq_ref[bb, :, d * NQK_S:(d + 1) * NQK_S] = (q_gla * jnp.exp(b)).astype(BF16)
            gk_ref[bb, :, d * NQK_S:(d + 1) * NQK_S] = (k_gla * jnp.exp(-b)).astype(BF16)
            ge_ref[bb, :, d * NQK_S:(d + 1) * NQK_S] = (k_gla * jnp.exp(tot - b)).astype(BF16)
            ht = half_t[d * NQK_S:(d + 1) * NQK_S, :]
            for c in range(CHUNKS_PER_TILE):
                tot_col = ht[:, 2 * c:2 * c + 1] + ht[:, 2 * c + 1:2 * c + 2]
                gdec_ref[bb, c, d * NQK_S:(d + 1) * NQK_S, :] = jnp.broadcast_to(jnp.exp(tot_col), (NQK_S, LANES))

    def ret_body(prgr_ref, bb):
        q_ret = rope(prgr_ref[:,GR_RET_Q:GR_RET_Q + NQK_S])
        k_ret = rope(prgr_ref[:,GR_RET_K:GR_RET_K + NQK_S] * (DK_SMALL ** -0.5))
        rv_ref[bb] = prgr_ref[:,GR_RET_V:GR_RET_V + BRANCH_W].astype(BF16)
        rsg_ref[bb] = _silu(prgr_ref[:,GR_RET_G:GR_RET_G + BRANCH_W]).astype(BF16)
        for d in range(2):
            rq_ref[bb, :, d * NQK_S:(d + 1) * NQK_S] = (q_ret * rdec_ref[3 * d]).astype(BF16)
            rk_ref[bb, :, d * NQK_S:(d + 1) * NQK_S] = (k_ret * rdec_ref[3 * d + 1]).astype(BF16)
            re_ref[bb, :, d * NQK_S:(d + 1) * NQK_S] = (k_ret * rdec_ref[3 * d + 2]).astype(BF16)

    next_tile = jnp.minimum(i + 1, nt - 1)
    a_ext, a_mlr, a_gla, a_ret = project_stages(xa_ref[0], xap_ref[0], xan_ref[0], i, jnp.where(i == 0, 2, 1), 1)
    b_ext, b_mlr, b_gla, b_ret = project_stages(xb_ref[0], xbp_ref[0], xbn_ref[0], next_tile, 0, 0)
    x = prepare_stages(0, 0)
    y = prepare_stages(1, 1)
    order = [a_ext, x["conv"], x["gates_in"],
             a_mlr, x["gates_mm"], x["gates_out"], x["gla_in"],
             a_gla, x["gla_z_mm"], x["gla_la"],
             a_ret, x["gla_cum_mm"],
             b_ext, x["gla_out"], x["ret"], y["conv"], y["gates_in"],
             b_mlr, y["gates_mm"], y["gates_out"], y["gla_in"],
             b_gla, y["gla_z_mm"], y["gla_la"],
             b_ret, y["gla_cum_mm"], y["gla_out"], y["ret"]]
    for stage in order:
        stage()


def _proj(tok, mods, g1, w, layer, a2s, abias, conv_w, gate_bias, cos_t, sin_t, dl_row):
    B, T, _ = tok.shape
    nt = T // TILE
    tile = lambda width: pl.BlockSpec((B, TILE, width), lambda i: (0, i, 0))
    const = lambda shape: pl.BlockSpec(shape, lambda i: (0,) * len(shape))
    per_halo = TILE // HALO
    n_halo = T // HALO
    nxt = lambda i: jnp.minimum(i + 1, nt - 1)
    prev_halo = lambda t: jnp.maximum(t * per_halo - 1, 0)
    next_halo = lambda t: jnp.minimum((t + 1) * per_halo, n_halo - 1)
    tok_shape = lambda width, dt: jax.ShapeDtypeStruct((B, T, width), dt)
    return pl.pallas_call(
        functools.partial(_proj_kernel, nt),
        out_shape=(tok_shape(2 * NQK_S, BF16), tok_shape(2 * NQK_S, BF16), tok_shape(2 * NQK_S, BF16),
                   tok_shape(BRANCH_W, BF16), tok_shape(BRANCH_W, BF16),
                   jax.ShapeDtypeStruct((B, T // CHUNK, 2 * NQK_S, LANES), F32),
                   tok_shape(NQK_M, BF16), tok_shape(NQK_M, BF16), tok_shape(2 * NQK_M, BF16),
                   tok_shape(BRANCH_W, BF16), tok_shape(BRANCH_W, BF16),
                   jax.ShapeDtypeStruct((B, 2 * PAIRS, T, 2 * CHUNK), F32),
                   tok_shape(MC_WIDTH, F32),
                   jax.ShapeDtypeStruct((B, T // CHUNK, 2 * NHD, LANES), F32),
                   tok_shape(2 * NQK_S, BF16), tok_shape(2 * NQK_S, BF16), tok_shape(2 * NQK_S, BF16),
                   tok_shape(BRANCH_W, BF16), tok_shape(BRANCH_W, BF16)),
        grid=(nt,),
        in_specs=[pl.BlockSpec((1, TILE, D_MODEL), lambda i: (1, i, 0)),
                  pl.BlockSpec((1, HALO, D_MODEL), lambda i: (1, prev_halo(i), 0)),
                  pl.BlockSpec((1, HALO, D_MODEL), lambda i: (1, next_halo(i), 0)),
                  pl.BlockSpec((1, TILE, D_MODEL), lambda i: (0, nxt(i), 0)),
                  pl.BlockSpec((1, HALO, D_MODEL), lambda i: (0, prev_halo(nxt(i)), 0)),
                  pl.BlockSpec((1, HALO, D_MODEL), lambda i: (0, next_halo(nxt(i)), 0)),
                  pl.BlockSpec((1, TILE, D_MODEL), lambda i: (0, 0, 0)),
                  const((B + 1, 6, D_MODEL)), const((1, D_MODEL)),
                  pl.BlockSpec((1, D_MODEL, W_PROJ), lambda i: (layer, 0, 0), pipeline_mode=pl.Buffered(1)),
                  const((4 * GLA_RANK, 2 * NQK_S)), const((1, 2 * NQK_S)),
                  const((3, 2 * NQK_M)), const((1, 2 * NHD)),
                  pl.BlockSpec((TILE, NQK_S), lambda i: (i, 0)),
                  pl.BlockSpec((TILE, NQK_S), lambda i: (i, 0)),
                  const((1, 2 * NQK_S))],
        out_specs=(tile(2 * NQK_S), tile(2 * NQK_S), tile(2 * NQK_S), tile(BRANCH_W), tile(BRANCH_W),
                   pl.BlockSpec((B, CHUNKS_PER_TILE, 2 * NQK_S, LANES), lambda i: (0, i, 0, 0)),
                   tile(NQK_M), tile(NQK_M), tile(2 * NQK_M), tile(BRANCH_W), tile(BRANCH_W),
                   pl.BlockSpec((B, 2 * PAIRS, TILE, 2 * CHUNK), lambda i: (0, 0, i, 0)),
                   tile(MC_WIDTH),
                   pl.BlockSpec((B, CHUNKS_PER_TILE, 2 * NHD, LANES), lambda i: (0, i, 0, 0)),
                   tile(2 * NQK_S), tile(2 * NQK_S), tile(2 * NQK_S), tile(BRANCH_W), tile(BRANCH_W)),
        scratch_shapes=[pltpu.VMEM((TILE + 2 * HALO, 2 * NQK_M), F32)] * 2
                       + [pltpu.VMEM((TILE, PW_MLR_WIDTH), F32)] * 2
                       + [pltpu.VMEM((TILE, PW_GR_WIDTH), F32)] * 2
                       + [pltpu.VMEM((6, TILE, NQK_S), F32)],
        compiler_params=_cparams(1, VMEM_LIMIT_PROJ), name="proj",
    )(tok, tok, tok, tok, tok, tok, tok, mods, g1, w, a2s, abias, conv_w, gate_bias, cos_t, sin_t, dl_row)


def _mix_kernel(batch, *refs):
    (gq_f, gq_b, gk_f, gk_b, ge_f, ge_b, gv_f, gv_b, gd_f, gd_b,
     rq_f, rq_b, rk_f, rk_b, re_f, re_b, rv_f, rv_b, rdl,
     mq_f, mq_b, mk_f, mk_b, mw_f, mw_b, mv_f, mv_b, mp_f, mp_b, mcol_f, mcol_b, mc_f, mc_b,
     of_ref, ob_ref, sg_ref, sr_ref, sm_ref, m_ref) = refs
    j = pl.program_id(0)

    @pl.when(j == 0)
    def _():
        sg_ref[...] = jnp.zeros_like(sg_ref)
        sr_ref[...] = jnp.zeros_like(sr_ref)
        sm_ref[...] = jnp.zeros_like(sm_ref)
        m_ref[...] = jnp.full_like(m_ref, M_INIT)

    row_head = lax.broadcasted_iota(jnp.int32, (2 * CHUNK, PAIR_V), 0) // CHUNK
    diag_kv = row_head == lax.broadcasted_iota(jnp.int32, (2 * CHUNK, PAIR_V), 1) // DV
    diag_kk = (lax.broadcasted_iota(jnp.int32, (2 * CHUNK, PAIR_K), 0) // CHUNK
               == lax.broadcasted_iota(jnp.int32, (2 * CHUNK, PAIR_K), 1) // DK_SMALL)
    t_idx = lax.broadcasted_iota(jnp.int32, (CHUNK, 2 * CHUNK), 0)
    s_idx = lax.broadcasted_iota(jnp.int32, (CHUNK, 2 * CHUNK), 1) % CHUNK
    causal = (s_idx <= t_idx, s_idx >= t_idx)
    ones_v = jnp.ones((CHUNK, DV), BF16)
    zeros_vext = jnp.zeros((CHUNK, PAIR_V), BF16)
    ret_dec = jnp.exp(float(CHUNK) * _log_sigmoid(rdl[...]))

    decay_sets = (
        ((gq_f, gk_f, ge_f, gv_f), (gq_b, gk_b, ge_b, gv_b), sg_ref, 0),
        ((rq_f, rk_f, re_f, rv_f), (rq_b, rk_b, re_b, rv_b), sr_ref, 2 * BRANCH_W),
    )
    ml_sets = ((mq_f, mk_f, mw_f, mv_f, mp_f, mcol_f, mc_f),
               (mq_b, mk_b, mw_b, mv_b, mp_b, mcol_b, mc_b))
    outs = (of_ref, ob_ref)

    def chunk_step(step):
        work = []
        for bb, d in ((b_, d_) for b_ in range(batch) for d_ in range(2)):
            c = step if d == 0 else CHUNKS_PER_TILE - 1 - step
            rows = pl.ds(c * CHUNK, CHUNK)
            o_ref = outs[d]
            for bi, (fset, bset, s_ref, base) in enumerate(decay_sets):
                q_r, k_r, e_r, v_r = fset if d == 0 else bset
                for p in range(PAIRS):
                    ks = slice(p * PAIR_K, (p + 1) * PAIR_K)
                    q2 = q_r[bb, rows,ks]
                    k2 = k_r[bb, rows,ks]
                    v2 = v_r[bb, rows,p * PAIR_V:(p + 1) * PAIR_V]
                    k_bd = jnp.where(diag_kk, jnp.concatenate([k2, k2], axis=0), 0.0)
                    if bi == 0:
                        dec = (gd_f if d == 0 else gd_b)[bb, c, ks, :]
                    else:
                        dec = ret_dec[d, ks, :]
                    upd = jnp.where(diag_kv, _dot_tn(e_r[bb, rows,ks], v2), 0.0)
                    work.append(dict(
                        kind="decay", bb=bb, d=d, p=p, rows=rows, o_ref=o_ref, s_ref=s_ref, base=base,
                        q2=q2, v2=v2, dec=jnp.concatenate([dec, dec], axis=1), upd=upd,
                        att=_dot_nt(q2, k_bd)))
            q_r, k_r, w_r, v_r, p_r, col_r, c_r = ml_sets[d]
            cs = c_r[bb, c]
            cols = col_r[bb, rows,:]
            for p in range(PAIRS):
                q2 = q_r[bb, rows,p * 2 * DK_ML:(p + 1) * 2 * DK_ML]
                k2 = k_r[bb, rows,p * 2 * DK_ML:(p + 1) * 2 * DK_ML]
                k_bd = jnp.where(diag_kv, jnp.concatenate([k2, k2], axis=0), 0.0)
                att = _dot_nt(q2, k_bd)
                heads = []
                for hl in range(2):
                    hd = 2 * p + hl
                    jj = d * HEADS + hd
                    vext = jnp.concatenate([v_r[bb, rows,hd * DV:(hd + 1) * DV], ones_v], axis=1)
                    st = sm_ref[bb, d, hd]
                    heads.append(dict(
                        hd=hd, st=st, vext=vext,
                        inter=_dot(q2[:, hl * DK_ML:(hl + 1) * DK_ML], st.astype(BF16)),
                        upd=_dot_tn(w_r[bb, rows,hd * DK_ML:(hd + 1) * DK_ML], vext),
                        u=jnp.broadcast_to(cols[:, MC_U + jj:MC_U + jj + 1], (CHUNK, LANES)),
                        b=jnp.broadcast_to(cols[:, MC_B + jj:MC_B + jj + 1], (CHUNK, LANES))))
                work.append(dict(kind="ml", bb=bb, d=d, p=p, rows=rows, o_ref=o_ref, att=att, heads=heads,
                                 p0=p_r[bb, p, rows, :], cs=cs))
        for w in work:
            bb, d, p, rows, o_ref = w["bb"], w["d"], w["p"], w["rows"], w["o_ref"]
            if w["kind"] == "decay":
                s_ref = w["s_ref"]
                s_prev = s_ref[bb, d, p]
                att = jnp.where(causal[d], w["att"], 0.0).astype(BF16)
                v_bd = jnp.where(diag_kv, jnp.concatenate([w["v2"], w["v2"]], axis=0), 0.0)
                lhs = jnp.concatenate([att, w["q2"]], axis=1)
                rhs = jnp.concatenate([v_bd, s_prev.astype(BF16)], axis=0)
                o_ref[bb, rows, w["base"] + p * PAIR_V:w["base"] + (p + 1) * PAIR_V] = _dot(lhs, rhs)
                s_ref[bb, d, p] = w["dec"] * s_prev + w["upd"]
            else:
                sc = (w["att"] * w["p0"]).astype(BF16)
                for hl, hw in enumerate(w["heads"]):
                    hd = hw["hd"]
                    jj = d * HEADS + hd
                    vz = (jnp.concatenate([hw["vext"], zeros_vext], axis=0) if hl == 0
                          else jnp.concatenate([zeros_vext, hw["vext"]], axis=0))
                    intra = _dot(sc, vz)
                    m_prev = m_ref[bb, jj:jj + 1, :]
                    u = hw["u"]
                    mx = jnp.maximum(u, m_prev)
                    f = jnp.exp(u - mx)
                    e = jnp.exp(m_prev - mx)
                    num = f * intra[:, 0:DV] + e * hw["inter"][:, 0:DV]
                    den = f * intra[:, DV:] + e * hw["inter"][:, DV:]
                    den = jnp.maximum(jnp.abs(den), jnp.exp(-(hw["b"] + mx)))
                    o_ref[bb, rows, BRANCH_W + hd * DV:BRANCH_W + (hd + 1) * DV] = num / den
                    tot = w["cs"][jj:jj + 1, :]
                    mloc = w["cs"][NHD + jj:NHD + jj + 1, :]
                    m_new = jnp.maximum(tot + m_prev, mloc)
                    keep = jnp.exp(tot + m_prev - m_new)
                    gain = jnp.exp(mloc - m_new)
                    sm_ref[bb, d, hd] = (jnp.concatenate([keep, keep], axis=1) * hw["st"]
                                         + jnp.concatenate([gain, gain], axis=1) * hw["upd"])
                    m_ref[bb, jj:jj + 1, :] = m_new

    for step in range(CHUNKS_PER_TILE):
        chunk_step(step)


def _mix(gla, ret, ml, ret_dl_col):
    gq, gk, ge, gv, gdec = gla
    rq, rk, re, rv = ret
    mq, mk, mw, mv, mp, mcol, mc = ml
    B, T, _ = gv.shape
    nt = T // TILE

    def tf(j):
        return j

    def tb(j):
        return jnp.where(j == 0, 0, nt - j)

    def pair(width, lane_block_b=0):
        return [pl.BlockSpec((B, TILE, width), lambda j: (0, tf(j), 0)),
                pl.BlockSpec((B, TILE, width), lambda j: (0, tb(j), lane_block_b))]

    dec_pair = [pl.BlockSpec((B, CHUNKS_PER_TILE, NQK_S, LANES), lambda j: (0, tf(j), 0, 0)),
                pl.BlockSpec((B, CHUNKS_PER_TILE, NQK_S, LANES), lambda j: (0, tb(j), 1, 0))]
    p0_pair = [pl.BlockSpec((B, PAIRS, TILE, 2 * CHUNK), lambda j: (0, 0, tf(j), 0)),
               pl.BlockSpec((B, PAIRS, TILE, 2 * CHUNK), lambda j: (0, 1, tb(j), 0))]
    cs_pair = [pl.BlockSpec((B, CHUNKS_PER_TILE, 2 * NHD, LANES), lambda j: (0, tf(j), 0, 0)),
               pl.BlockSpec((B, CHUNKS_PER_TILE, 2 * NHD, LANES), lambda j: (0, tb(j), 0, 0))]
    decay_specs = pair(NQK_S, 1) + pair(NQK_S, 1) + pair(NQK_S, 1) + pair(BRANCH_W)
    in_specs = (decay_specs + dec_pair + decay_specs
                + [pl.BlockSpec((2, NQK_S, LANES), lambda j: (0, 0, 0))]
                + pair(NQK_M) + pair(NQK_M) + pair(NQK_M, 1) + pair(BRANCH_W) + p0_pair
                + pair(MC_WIDTH) + cs_pair)
    args = (gq, gq, gk, gk, ge, ge, gv, gv, gdec, gdec,
            rq, rq, rk, rk, re, re, rv, rv, ret_dl_col,
            mq, mq, mk, mk, mw, mw, mv, mv, mp, mp, mcol, mcol, mc, mc)
    return pl.pallas_call(
        functools.partial(_mix_kernel, B),
        out_shape=(jax.ShapeDtypeStruct((B, T, 3 * BRANCH_W), F32),) * 2,
        grid=(nt,),
        in_specs=in_specs,
        out_specs=(pl.BlockSpec((B, TILE, 3 * BRANCH_W), lambda j: (0, tf(j), 0)),
                   pl.BlockSpec((B, TILE, 3 * BRANCH_W), lambda j: (0, tb(j), 0))),
        scratch_shapes=[pltpu.VMEM((B, 2, PAIRS, PAIR_K, PAIR_V), F32),
                        pltpu.VMEM((B, 2, PAIRS, PAIR_K, PAIR_V), F32),
                        pltpu.VMEM((B, 2, HEADS, DK_ML, 2 * DV), F32),
                        pltpu.VMEM((B, NHD, LANES), F32)],
        compiler_params=_cparams(1), name="mix",
    )(*args)


def _merge_kernel(batch, first_tile, x_ref, mod_ref, g_ref, wg_ref, of_ref, ob_ref, sgg_ref, sgm_ref, sgr_ref,
                  ng_ref, wb_ref, wo_ref, out_ref):
    tile_idx = pl.program_id(0) + first_tile
    side = (sgg_ref, sgm_ref, sgr_ref)
    mods = [mod_ref[jnp.where(tile_idx == 0, batch, bb)] for bb in range(batch)]
    hs = [_norm_mod(x_ref[bb], g_ref[...], mods[bb][0:1, :], mods[bb][1:2, :]).astype(BF16)
          for bb in range(batch)]
    accs = [jnp.zeros((TILE, D_MODEL), F32) for _ in range(batch)]
    for br in range(3):
        for bb in range(batch):
            parts = []
            for hd in range(HEADS):
                cols = slice(br * BRANCH_W + hd * DV, br * BRANCH_W + (hd + 1) * DV)
                oh = of_ref[bb, :, cols] + ob_ref[bb, :, cols]
                if br > 0:
                    oh = oh - jnp.mean(oh, axis=-1, keepdims=True)
                parts.append(oh * lax.rsqrt(jnp.mean(oh * oh, axis=-1, keepdims=True) + EPS))
            y = jnp.concatenate(parts, axis=1) * ng_ref[:, br * BRANCH_W:(br + 1) * BRANCH_W]
            y = (y * side[br][bb]).astype(BF16)
            gate = _sigmoid(_dot(hs[bb], wg_ref[0, :, br * D_MODEL:(br + 1) * D_MODEL]))
            accs[bb] = accs[bb] + gate * _dot(y, wb_ref[0, br])
    for bb in range(batch):
        out = _dot(accs[bb].astype(BF16), wo_ref[0])
        out_ref[bb] = x_ref[bb] + mods[bb][2:3, :] * out


def _merge(tok, mods, g1, wgate, o_f, o_b, sg_gla, sz_ml, sg_ret, norm_g, wb, wo, layer, skip):
    B, T, _ = tok.shape
    nt = T // TILE - skip
    tile = lambda width: pl.BlockSpec((B, TILE, width), lambda i: (0, i + skip, 0))
    const = lambda shape: pl.BlockSpec(shape, lambda i: (0,) * len(shape))
    per_layer = lambda shape: pl.BlockSpec((1,) + shape, lambda i: (layer,) + (0,) * len(shape),
                                           pipeline_mode=pl.Buffered(1))
    return pl.pallas_call(
        functools.partial(_merge_kernel, B, skip),
        out_shape=jax.ShapeDtypeStruct((B, nt * TILE, D_MODEL), F32),
        grid=(nt,),
        in_specs=[tile(D_MODEL), const((B + 1, 6, D_MODEL)),
                  const((1, D_MODEL)), per_layer((D_MODEL, W_GATE)),
                  tile(3 * BRANCH_W), tile(3 * BRANCH_W),
                  tile(BRANCH_W), tile(BRANCH_W), tile(BRANCH_W),
                  const((1, 3 * BRANCH_W)), per_layer((3, BRANCH_W, D_MODEL)),
                  per_layer((D_MODEL, D_MODEL))],
        out_specs=pl.BlockSpec((B, TILE, D_MODEL), lambda i: (0, i, 0)),
        compiler_params=_cparams(1), name="merge",
    )(tok, mods, g1, wgate, o_f, o_b, sg_gla, sz_ml, sg_ret, norm_g, wb, wo)


def _mlp_kernel(batch, first_tile, final, x_ref, mod_ref, g_ref, wu_ref, wd_ref, fg_ref, out_ref):
    tile_idx = pl.program_id(0) + first_tile
    mods = [mod_ref[jnp.where(tile_idx == 0, batch, bb)] for bb in range(batch)]
    ups = []
    for bb in range(batch):
        h = _norm_mod(x_ref[bb], g_ref[...], mods[bb][3:4, :], mods[bb][4:5, :]).astype(BF16)
        ups.append(_dot(h, wu_ref[0]))
    for bb in range(batch):
        u = jnp.maximum(ups[bb], 0.0)
        y = x_ref[bb] + mods[bb][5:6, :] * _dot((u * u).astype(BF16), wd_ref[0])
        if final:
            y = y * lax.rsqrt(jnp.mean(y * y, axis=-1, keepdims=True) + EPS) * fg_ref[...]
        out_ref[bb] = y


def _mlp(tok, mods, g2, wu, wd, layer, final_g, first_tile, final):
    B, T, _ = tok.shape
    const = lambda shape: pl.BlockSpec(shape, lambda i: (0,) * len(shape))
    per_layer = lambda shape: pl.BlockSpec((1,) + shape, lambda i: (layer,) + (0,) * len(shape),
                                           pipeline_mode=pl.Buffered(1))
    return pl.pallas_call(
        functools.partial(_mlp_kernel, B, first_tile, final),
        out_shape=jax.ShapeDtypeStruct((B, T, D_MODEL), F32),
        grid=(T // TILE,),
        in_specs=[pl.BlockSpec((B, TILE, D_MODEL), lambda i: (0, i, 0)),
                  const((B + 1, 6, D_MODEL)),
                  const((1, D_MODEL)), per_layer((D_MODEL, D_FF)), per_layer((D_FF, D_MODEL)),
                  const((1, D_MODEL))],
        out_specs=pl.BlockSpec((B, TILE, D_MODEL), lambda i: (0, i, 0)),
        compiler_params=_cparams(1), name="mlp",
    )(tok, mods, g2, wu, wd, final_g)


def kernel(x, c, ctx, c_ctx, norm1_g, norm2_g, w_ada, b_ada, w_in, gla_a2, gla_a_bias, gla_norm_g,
           ml_conv, ml_i_bias, ml_f_bias, ml_norm_g, ret_decay_logit, ret_norm_g, w_branch, w_o,
           w_up, w_down, final_g):
    B, seq, _ = x.shape
    ctx_len = ctx.shape[1]
    assert B == 2 and ctx_len == TILE and seq % TILE == 0 and seq % GRID_W == 0
    tok = jnp.concatenate([ctx, x], axis=1)

    c_rows = jnp.concatenate([c, c_ctx[None, :], jnp.zeros((8 - B - 1, D_MODEL), F32)], axis=0)
    mods_all = _ada(c_rows, w_ada, b_ada)[:, 0:3].reshape(DEPTH, 3, 6, D_MODEL)

    cos_t, sin_t = _rope_tables(seq, ctx_len)

    w_proj, w_gate = _wprep(w_in)
    w_branch_b = w_branch.astype(BF16)
    w_o_b = w_o.astype(BF16)
    w_up_b = w_up.astype(BF16)
    w_down_b = w_down.astype(BF16)
    for l in range(DEPTH):
        last = l == DEPTH - 1
        skip = 1 if last else 0
        mods = mods_all[l]
        g1 = norm1_g[l][None, :]
        g2 = norm2_g[l][None, :]
        zeros = jnp.zeros((GLA_RANK, NQK_S), F32)
        a2blk = jnp.concatenate([jnp.concatenate([gla_a2[l, 0], zeros], axis=1),
                                 jnp.concatenate([zeros, gla_a2[l, 1]], axis=1)], axis=0)
        a2s = jnp.concatenate([a2blk, a2blk], axis=0).astype(BF16)
        abias = gla_a_bias[l].reshape(1, 2 * NQK_S)
        gate_bias = jnp.concatenate([ml_i_bias[l].reshape(1, NHD), ml_f_bias[l].reshape(1, NHD)], axis=1)
        dl = ret_decay_logit[l]
        dl_row = jnp.repeat(dl, DK_SMALL, axis=1).reshape(1, 2 * NQK_S)
        dl_col = jnp.broadcast_to(jnp.repeat(dl, DK_SMALL, axis=1)[:, :, None], (2, NQK_S, LANES))

        (gq, gk, ge, gv, sg_gla, gdec, mq, mk, mw, mv, sz_ml, mp, mcol, mc,
         rq, rk, re, rv, sg_ret) = _proj(tok, mods, g1, w_proj, l, a2s, abias, ml_conv[l], gate_bias,
                                         cos_t, sin_t, dl_row)
        o_f, o_b = _mix((gq, gk, ge, gv, gdec), (rq, rk, re, rv), (mq, mk, mw, mv, mp, mcol, mc), dl_col)
        norm_g = jnp.concatenate([gla_norm_g[l], ml_norm_g[l], ret_norm_g[l]])[None, :]
        tok = _merge(tok, mods, g1, w_gate, o_f, o_b, sg_gla, sz_ml, sg_ret, norm_g,
                     w_branch_b, w_o_b, l, skip)
        tok = _mlp(tok, mods, g2, w_up_b, w_down_b, l, final_g[None, :], skip, last)
    return tok
```

```python
import functools

import jax
import jax.numpy as jnp
from jax import lax
from jax.experimental import pallas as pl
from jax.experimental.pallas import tpu as pltpu

F32 = jnp.float32
BF16 = jnp.bfloat16

D_MODEL = 1024
DEPTH = 2
GRID_W = 64
CHUNK = 64
EPS = 1e-6
M_INIT = -1e30
HEADS = 4
DK_SMALL = 64
DK_ML = 128
DV = 128
BRANCH_W = HEADS * DV
GLA_RANK = 16
GLA_TAU = 16.0
ROPE_BASE = 10000.0
D_FF = 4 * D_MODEL
LANES = 128

TILE = 256
CHUNKS_PER_TILE = TILE // CHUNK
HALO = 8
VMEM_LIMIT = 56 * 1024 * 1024
VMEM_LIMIT_PROJ = 60 * 1024 * 1024

NQK_S = HEADS * DK_SMALL
NQK_M = HEADS * DK_ML
NHD = 2 * HEADS
W_GLA = 2 * NQK_S + 2 * BRANCH_W + 2 * GLA_RANK
W_ML = 2 * NQK_M + 2 * BRANCH_W + 2 * NHD
W_RET = 2 * NQK_S + 2 * BRANCH_W
W_GATE = 3 * D_MODEL

PW_MLQK = 0
PW_MLR = PW_MLQK + 2 * NQK_M
PW_MLR_WIDTH = 2 * BRANCH_W + LANES
PW_GR = PW_MLR + PW_MLR_WIDTH
PW_GR_WIDTH = 2 * (2 * NQK_S + 2 * BRANCH_W) + LANES
W_PROJ = PW_GR + PW_GR_WIDTH
MLR_V, MLR_Z, MLR_GATES = 0, BRANCH_W, 2 * BRANCH_W
GR_GLA_Q, GR_GLA_K, GR_GLA_V, GR_GLA_G = 0, NQK_S, 2 * NQK_S, 2 * NQK_S + BRANCH_W
GR_RET_Q = 2 * NQK_S + 2 * BRANCH_W
GR_RET_K, GR_RET_V, GR_RET_G = GR_RET_Q + NQK_S, GR_RET_Q + 2 * NQK_S, GR_RET_Q + 2 * NQK_S + BRANCH_W
GR_GLA_A = 2 * GR_RET_Q

MC_B, MC_W, MC_U = 0, NHD, 2 * NHD
MC_WIDTH = 3 * NHD

PAIRS = HEADS // 2
PAIR_K = 2 * DK_SMALL
PAIR_V = 2 * DV


def _cparams(n_axes, vmem_limit=VMEM_LIMIT):
    return pltpu.CompilerParams(dimension_semantics=("arbitrary",) * n_axes,
                                vmem_limit_bytes=vmem_limit)


def _log_sigmoid(z):
    return jnp.minimum(z, 0.0) - jnp.log(1.0 + jnp.exp(-jnp.abs(z)))


def _sigmoid(z):
    return 1.0 / (1.0 + jnp.exp(-z))


def _silu(z):
    return z * _sigmoid(z)


def _norm_mod(x, g, shift, scale):
    y = x * lax.rsqrt(jnp.mean(x * x, axis=-1, keepdims=True) + EPS) * g
    return y * (1.0 + scale) + shift


def _dot(a, b):
    return jnp.dot(a, b, preferred_element_type=F32)


def _dot_nt(a, b):
    return lax.dot_general(a, b, (((1,), (1,)), ((), ())), preferred_element_type=F32)


def _dot_tn(a, b):
    return lax.dot_general(a, b, (((0,), (0,)), ((), ())), preferred_element_type=F32)


def _split2(x):
    hi = x.astype(BF16)
    return hi, (x - hi.astype(F32)).astype(BF16)


def _chunk_tri(lower):
    r = lax.broadcasted_iota(jnp.int32, (TILE, TILE), 0)
    c = lax.broadcasted_iota(jnp.int32, (TILE, TILE), 1)
    same = (r // CHUNK) == (c // CHUNK)
    tri = (c <= r) if lower else (c >= r)
    return jnp.where(same & tri, 1.0, 0.0).astype(BF16)


def _chunk_total(x):
    w = x.shape[-1]
    tot = jnp.sum(x.reshape(CHUNKS_PER_TILE, CHUNK, w), axis=1, keepdims=True)
    return jnp.broadcast_to(tot, (CHUNKS_PER_TILE, CHUNK, w)).reshape(TILE, w)


def _ada_kernel(c_ref, w_ref, b_ref, o_ref):
    s = _silu(c_ref[...])
    o_ref[0] = jnp.dot(s, w_ref[0], precision=lax.Precision.HIGHEST,
                       preferred_element_type=F32) + b_ref[0]


def _ada(c_rows, w_ada, b_ada):
    tn = 1536
    return pl.pallas_call(
        _ada_kernel,
        out_shape=jax.ShapeDtypeStruct((DEPTH, 8, 6 * D_MODEL), F32),
        grid=(DEPTH, 6 * D_MODEL // tn),
        in_specs=[pl.BlockSpec((8, D_MODEL), lambda l, n: (0, 0)),
                  pl.BlockSpec((1, D_MODEL, tn), lambda l, n: (l, 0, n)),
                  pl.BlockSpec((1, 1, tn), lambda l, n: (l, 0, n))],
        out_specs=pl.BlockSpec((1, 8, tn), lambda l, n: (l, 0, n)),
        compiler_params=_cparams(2), name="ada",
    )(c_rows, w_ada, b_ada.reshape(DEPTH, 1, 6 * D_MODEL))


def _rope_kernel(ang_ref, cos_ref, sin_ref, nsin_ref):
    a = ang_ref[...]
    s = jnp.sin(a)
    cos_ref[...] = jnp.cos(a)
    sin_ref[...] = s
    nsin_ref[...] = -s


def _rope_tables(seq, ctx_len):
    rows = seq // GRID_W
    nf = DK_SMALL // 4
    inv = ROPE_BASE ** (-jnp.arange(nf, dtype=F32) / nf)
    ang = jnp.concatenate([jnp.arange(rows, dtype=F32)[:, None] * inv,
                           jnp.arange(GRID_W, dtype=F32)[:, None] * inv], axis=0)
    n = rows + GRID_W
    cos, sin, nsin = pl.pallas_call(
        _rope_kernel,
        out_shape=(jax.ShapeDtypeStruct((n, nf), F32),) * 3,
        name="rope_tables",
    )(ang)

    def per_token(first, second):
        def half(t):
            r = jnp.broadcast_to(t[:rows, None, :], (rows, GRID_W, nf))
            c = jnp.broadcast_to(t[None, rows:, :], (rows, GRID_W, nf))
            return jnp.concatenate([r, c], axis=-1)
        head = jnp.concatenate([half(first), half(second)], axis=-1).reshape(seq, DK_SMALL)
        return jnp.tile(head, (1, HEADS))

    ones = jnp.ones((ctx_len, NQK_S), F32)
    cos_t = jnp.concatenate([ones, per_token(cos, cos)], axis=0)
    sin_t = jnp.concatenate([0.0 * ones, per_token(nsin, sin)], axis=0)
    return cos_t, sin_t


def _wprep_kernel(w_ref, wp_ref, wg_ref):
    o0, o1, o2 = W_GLA, W_GLA + W_ML, W_GLA + W_ML + W_RET
    n_sg = 2 * NQK_S + 2 * BRANCH_W
    rows = w_ref.shape[1]

    def copy(dst_ref, dst, src, n):
        dst_ref[0, :, dst:dst + n] = w_ref[0, :, src:src + n].astype(BF16)

    copy(wp_ref, PW_MLQK, o0, 2 * NQK_M)
    n_mlr = 2 * BRANCH_W + 2 * NHD
    copy(wp_ref, PW_MLR, o0 + 2 * NQK_M, n_mlr)
    wp_ref[0, :, PW_MLR + n_mlr:PW_GR] = jnp.zeros((rows, PW_GR - PW_MLR - n_mlr), BF16)
    copy(wp_ref, PW_GR + GR_GLA_Q, 0, n_sg)
    copy(wp_ref, PW_GR + GR_RET_Q, o1, W_RET)
    copy(wp_ref, PW_GR + GR_GLA_A, n_sg, 2 * GLA_RANK)
    pad0 = PW_GR + GR_GLA_A + 2 * GLA_RANK
    wp_ref[0, :, pad0:W_PROJ] = jnp.zeros((rows, W_PROJ - pad0), BF16)
    copy(wg_ref, 0, o2, W_GATE)


def _wprep(w_in):
    rows = 256
    return pl.pallas_call(
        _wprep_kernel,
        out_shape=(jax.ShapeDtypeStruct((DEPTH, D_MODEL, W_PROJ), BF16),
                   jax.ShapeDtypeStruct((DEPTH, D_MODEL, W_GATE), BF16)),
        grid=(DEPTH, D_MODEL // rows),
        in_specs=[pl.BlockSpec((1, rows, w_in.shape[2]), lambda l, r: (l, r, 0))],
        out_specs=(pl.BlockSpec((1, rows, W_PROJ), lambda l, r: (l, r, 0)),
                   pl.BlockSpec((1, rows, W_GATE), lambda l, r: (l, r, 0))),
        compiler_params=_cparams(2), name="wprep",
    )(w_in)


def _proj_kernel(nt, xa_ref, xap_ref, xan_ref, xb_ref, xbp_ref, xbn_ref, ctx_ref,
                 mod_ref, g_ref, w_ref, a2_ref, ab_ref, cw_ref, gb_ref, cos_ref, sin_ref, dl_ref,
                 gq_ref, gk_ref, ge_ref, gv_ref, gsg_ref, gdec_ref,
                 mq_ref, mk_ref, mkw_ref, mv_ref, msz_ref, mp0_ref, mcol_ref, mcs_ref,
                 rq_ref, rk_ref, re_ref, rv_ref, rsg_ref,
                 ext0_ref, ext1_ref, prml0_ref, prml1_ref, prgr0_ref, prgr1_ref, rdec_ref, tri_ref):
    i = pl.program_id(0)
    ext_refs = (ext0_ref, ext1_ref)
    prml_refs = (prml0_ref, prml1_ref)
    prgr_refs = (prgr0_ref, prgr1_ref)
    row8 = lax.broadcasted_iota(jnp.int32, (NHD, TILE), 0)
    lane8 = lax.broadcasted_iota(jnp.int32, (NHD, TILE), 1)
    lane_chunk = lane8 // CHUNK
    lane_in_chunk = lane8 % CHUNK
    ti = lax.broadcasted_iota(jnp.int32, (CHUNK, CHUNK), 0)
    si = lax.broadcasted_iota(jnp.int32, (CHUNK, CHUNK), 1)
    ext_row = lax.broadcasted_iota(jnp.int32, (TILE + 2 * HALO, 1), 0)

    def project_stages(x, x_prev, x_next, tile_idx, mod_row, slot):
        st = {}

        def mlstm_qk():
            mod = mod_ref[mod_row]
            x_all = jnp.concatenate([x_prev, x, x_next], axis=0)
            h_all = _norm_mod(x_all, g_ref[...], mod[0:1, :], mod[1:2, :])
            st["h"] = h_all[HALO:HALO + TILE, :].astype(BF16)
            prev_ok = tile_idx >= 2
            next_ok = jnp.logical_and(tile_idx >= 1, tile_idx <= nt - 2)
            dead = jnp.logical_or(jnp.logical_and(ext_row < HALO, jnp.logical_not(prev_ok)),
                                  jnp.logical_and(ext_row >= TILE + HALO, jnp.logical_not(next_ok)))
            ext = _dot(h_all.astype(BF16), w_ref[0, :, PW_MLQK:PW_MLR])
            ext_refs[slot][...] = jnp.where(dead, 0.0, ext)

        def mlstm_rest():
            prml_refs[slot][...] = _dot(st["h"], w_ref[0, :, PW_MLR:PW_GR])

        def gla():
            prgr_refs[slot][:, 0:GR_RET_Q] = _dot(st["h"], w_ref[0, :, PW_GR:PW_GR + GR_RET_Q])

        def ret():
            prgr_refs[slot][:, GR_RET_Q:] = _dot(st["h"], w_ref[0, :, PW_GR + GR_RET_Q:])

        return [mlstm_qk, mlstm_rest, gla, ret]

    def project(*args):
        for stage in project_stages(*args):
            stage()

    @pl.when(i == 0)
    def _():
        tri_ref[0] = _chunk_tri(True)
        tri_ref[1] = _chunk_tri(False)
        lg = _log_sigmoid(dl_ref[...])
        t_in_chunk = (lax.broadcasted_iota(jnp.int32, (TILE, NQK_S), 0) % CHUNK).astype(F32)
        for d in range(2):
            lg_d = lg[:, d * NQK_S:(d + 1) * NQK_S]
            steps = (t_in_chunk + 1.0) if d == 0 else (float(CHUNK) - t_in_chunk)
            b = steps * lg_d
            rdec_ref[3 * d] = jnp.exp(b)
            rdec_ref[3 * d + 1] = jnp.exp(-b)
            rdec_ref[3 * d + 2] = jnp.exp(float(CHUNK) * lg_d - b)
        halo0 = jnp.zeros((HALO, D_MODEL), F32)
        project(ctx_ref[0], halo0, halo0, 0, 2, 0)

    def rope(x):
        lane = lax.broadcasted_iota(jnp.int32, (TILE, NQK_S), 1)
        first_half = (lane % DK_SMALL) < (DK_SMALL // 2)
        swapped = jnp.where(first_half, pltpu.roll(x, NQK_S - DK_SMALL // 2, 1),
                            pltpu.roll(x, DK_SMALL // 2, 1))
        return x * cos_ref[...] + swapped * sin_ref[...]

    def seg_cummax(x, reverse):
        sh = 1
        while sh < CHUNK:
            if reverse:
                y, ok = pltpu.roll(x, TILE - sh, 1), lane_in_chunk < CHUNK - sh
            else:
                y, ok = pltpu.roll(x, sh, 1), lane_in_chunk >= sh
            x = jnp.where(ok, jnp.maximum(x, y), x)
            sh *= 2
        return x

    def prepare_stages(slot, bb):
        ext_ref, prml_ref, prgr_ref = ext_refs[slot], prml_refs[slot], prgr_refs[slot]
        st = {}

        def conv():
            qk = _silu(cw_ref[0:1, :] * ext_ref[HALO - 1:HALO - 1 + TILE, :]
                       + cw_ref[1:2, :] * ext_ref[HALO:HALO + TILE, :]
                       + cw_ref[2:3, :] * ext_ref[HALO + 1:HALO + 1 + TILE, :])
            mq_ref[bb] = qk[:, 0:NQK_M].astype(BF16)
            st["k_ml"] = qk[:, NQK_M:] * (DK_ML ** -0.5)
            mk_ref[bb] = st["k_ml"].astype(BF16)

        def gates_in():
            mv_ref[bb] = prml_ref[:,MLR_V:MLR_V + BRANCH_W].astype(BF16)
            msz_ref[bb] = _silu(prml_ref[:,MLR_Z:MLR_Z + BRANCH_W]).astype(BF16)
            gates_t = (prml_ref[:,MLR_GATES:MLR_GATES + 2 * NHD] + gb_ref[...]).T
            st["li_t"] = gates_t[0:NHD]
            st["lf_t"] = _log_sigmoid(gates_t[NHD:])
            lf_hi, lf_lo = _split2(st["lf_t"])
            st["lf2"] = jnp.concatenate([lf_hi, lf_lo], axis=0)

        def gates_mm():
            st["cum_f"] = _dot(st["lf2"], tri_ref[1])
            st["cum_b"] = _dot(st["lf2"], tri_ref[0])

        def gates_out():
            mlstm_gates_body(st["li_t"], st["lf_t"], st["cum_f"], st["cum_b"], st["k_ml"], bb)

        def gla_in():
            gv_ref[bb] = prgr_ref[:,GR_GLA_V:GR_GLA_V + BRANCH_W].astype(BF16)
            gsg_ref[bb] = _silu(prgr_ref[:,GR_GLA_G:GR_GLA_G + BRANCH_W]).astype(BF16)
            a_hi, a_lo = _split2(prgr_ref[:,GR_GLA_A:GR_GLA_A + 2 * GLA_RANK])
            st["a2"] = jnp.concatenate([a_hi, a_lo], axis=1)

        def gla_z_mm():
            st["z"] = _dot(st["a2"], a2_ref[...])

        def gla_la():
            la = _log_sigmoid(st["z"] + ab_ref[...]) * (1.0 / GLA_TAU)
            st["la"] = la
            st["la_split"] = [_split2(la[:, d * NQK_S:(d + 1) * NQK_S]) for d in range(2)]

        def gla_cum_mm():
            st["b"] = []
            for d in range(2):
                tri = tri_ref[d]
                la_hi, la_lo = st["la_split"][d]
                st["b"].append(_dot(tri, la_hi) + _dot(tri, la_lo))

        def gla_out():
            gla_body(prgr_ref, st["la"], st["b"], bb)

        def ret():
            ret_body(prgr_ref, bb)

        return dict(conv=conv, gates_in=gates_in, gates_mm=gates_mm, gates_out=gates_out,
                    gla_in=gla_in, gla_z_mm=gla_z_mm, gla_la=gla_la, gla_cum_mm=gla_cum_mm,
                    gla_out=gla_out, ret=ret)

    def mlstm_gates_body(li_t, lf_t, cum_f, cum_b, k_ml, bb):
        b_t = jnp.where(row8 < HEADS, cum_f[0:NHD] + cum_f[NHD:], cum_b[0:NHD] + cum_b[NHD:])
        tot_t = jnp.zeros((NHD, TILE), F32)
        mloc_t = jnp.zeros((NHD, TILE), F32)
        for c in range(CHUNKS_PER_TILE):
            tot_c = jnp.sum(lf_t[:, c * CHUNK:(c + 1) * CHUNK], axis=-1, keepdims=True)
            tot_t = jnp.where(lane_chunk == c, tot_c, tot_t)
            mcs_ref[bb, c, 0:NHD, :] = jnp.broadcast_to(tot_c, (NHD, LANES))
        g_t = tot_t - b_t + li_t
        for c in range(CHUNKS_PER_TILE):
            mloc_c = jnp.max(g_t[:, c * CHUNK:(c + 1) * CHUNK], axis=-1, keepdims=True)
            mloc_t = jnp.where(lane_chunk == c, mloc_c, mloc_t)
            mcs_ref[bb, c, NHD:2 * NHD, :] = jnp.broadcast_to(mloc_c, (NHD, LANES))
        w_t = jnp.exp(g_t - mloc_t)
        r_t = li_t - b_t
        u_t = jnp.where(row8 < HEADS, seg_cummax(r_t, False), seg_cummax(r_t, True))
        cols = jnp.concatenate([b_t, w_t, u_t], axis=0).T
        mcol_ref[bb] = cols
        for j in range(NHD):
            d, hd = j // HEADS, j % HEADS
            mask = (si <= ti) if d == 0 else (si >= ti)
            mkw_ref[bb, :, d * NQK_M + hd * DK_ML:d * NQK_M + (hd + 1) * DK_ML] = (
                k_ml[:, hd * DK_ML:(hd + 1) * DK_ML] * cols[:, MC_W + j:MC_W + j + 1]).astype(BF16)
            for c in range(CHUNKS_PER_TILE):
                rows = slice(c * CHUNK, (c + 1) * CHUNK)
                a = r_t[j:j + 1, rows] - cols[rows, MC_U + j:MC_U + j + 1]
                mp0_ref[bb, j // 2, rows, (j % 2) * CHUNK:(j % 2 + 1) * CHUNK] = jnp.where(mask, jnp.exp(a), 0.0)

    def gla_body(prgr_ref, la, cums, bb):
        half_t = jnp.sum(la.reshape(2 * CHUNKS_PER_TILE, CHUNK // 2, 2 * NQK_S), axis=1).T
        q_gla = prgr_ref[:,GR_GLA_Q:GR_GLA_Q + NQK_S] * (DK_SMALL ** -0.5)
        k_gla = prgr_ref[:,GR_GLA_K:GR_GLA_K + NQK_S]
        for d in range(2):
            la_d = la[:, d * NQK_S:(d + 1) * NQK_S]
            b = cums[d]
            tot = _chunk_total(la_d)
            gq_ref[bb, :, d * NQK_S:(d + 1) * NQK_S] = (q_gla * jnp.exp(b)).astype(BF16)
            gk_ref[bb, :, d * NQK_S:(d + 1) * NQK_S] = (k_gla * jnp.exp(-b)).astype(BF16)
            ge_ref[bb, :, d * NQK_S:(d + 1) * NQK_S] = (k_gla * jnp.exp(tot - b)).astype(BF16)
            ht = half_t[d * NQK_S:(d + 1) * NQK_S, :]
            for c in range(CHUNKS_PER_TILE):
                tot_col = ht[:, 2 * c:2 * c + 1] + ht[:, 2 * c + 1:2 * c + 2]
                gdec_ref[bb, c, d * NQK_S:(d + 1) * NQK_S, :] = jnp.broadcast_to(jnp.exp(tot_col), (NQK_S, LANES))

    def ret_body(prgr_ref, bb):
        q_ret = rope(prgr_ref[:,GR_RET_Q:GR_RET_Q + NQK_S])
        k_ret = rope(prgr_ref[:,GR_RET_K:GR_RET_K + NQK_S] * (DK_SMALL ** -0.5))
        rv_ref[bb] = prgr_ref[:,GR_RET_V:GR_RET_V + BRANCH_W].astype(BF16)
        rsg_ref[bb] = _silu(prgr_ref[:,GR_RET_G:GR_RET_G + BRANCH_W]).astype(BF16)
        for d in range(2):
            rq_ref[bb, :, d * NQK_S:(d + 1) * NQK_S] = (q_ret * rdec_ref[3 * d]).astype(BF16)
            rk_ref[bb, :, d * NQK_S:(d + 1) * NQK_S] = (k_ret * rdec_ref[3 * d + 1]).astype(BF16)
            re_ref[bb, :, d * NQK_S:(d + 1) * NQK_S] = (k_ret * rdec_ref[3 * d + 2]).astype(BF16)

    next_tile = jnp.minimum(i + 1, nt - 1)
    xa = jnp.where(i == 0, ctx_ref[1], xa_ref[0])
    a_ext, a_mlr, a_gla, a_ret = project_stages(xa, xap_ref[0], xan_ref[0], i, jnp.where(i == 0, 2, 1), 1)
    b_ext, b_mlr, b_gla, b_ret = project_stages(xb_ref[0], xbp_ref[0], xbn_ref[0], next_tile, 0, 0)
    x = prepare_stages(0, 0)
    y = prepare_stages(1, 1)
    order = [a_ext, x["conv"], x["gates_in"],
             a_mlr, x["gates_mm"], x["gates_out"], x["gla_in"],
             a_gla, x["gla_z_mm"], x["gla_la"],
             a_ret, x["gla_cum_mm"],
             b_ext, x["gla_out"], x["ret"], y["conv"], y["gates_in"],
             b_mlr, y["gates_mm"], y["gates_out"], y["gla_in"],
             b_gla, y["gla_z_mm"], y["gla_la"],
             b_ret, y["gla_cum_mm"], y["gla_out"], y["ret"]]
    for stage in order:
        stage()


def _proj(ctx_src, x_src, off, mods, g1, w, layer, a2s, abias, conv_w, gate_bias, cos_t, sin_t, dl_row):
    B = x_src.shape[0]
    nt = x_src.shape[1] // TILE + off
    T = nt * TILE
    tile = lambda width: pl.BlockSpec((B, TILE, width), lambda i: (0, i, 0))
    const = lambda shape: pl.BlockSpec(shape, lambda i: (0,) * len(shape))
    per_halo = TILE // HALO
    n_halo = x_src.shape[1] // HALO
    nxt = lambda i: jnp.minimum(i + 1, nt - 1)
    blk = lambda t: jnp.maximum(t - off, 0)
    prev_halo = lambda t: jnp.maximum((t - off) * per_halo - 1, 0)
    next_halo = lambda t: jnp.clip((t - off + 1) * per_halo, 0, n_halo - 1)
    tok_shape = lambda width, dt: jax.ShapeDtypeStruct((B, T, width), dt)
    return pl.pallas_call(
        functools.partial(_proj_kernel, nt),
        out_shape=(tok_shape(2 * NQK_S, BF16), tok_shape(2 * NQK_S, BF16), tok_shape(2 * NQK_S, BF16),
                   tok_shape(BRANCH_W, BF16), tok_shape(BRANCH_W, BF16),
                   jax.ShapeDtypeStruct((B, T // CHUNK, 2 * NQK_S, LANES), F32),
                   tok_shape(NQK_M, BF16), tok_shape(NQK_M, BF16), tok_shape(2 * NQK_M, BF16),
                   tok_shape(BRANCH_W, BF16), tok_shape(BRANCH_W, BF16),
                   jax.ShapeDtypeStruct((B, 2 * PAIRS, T, 2 * CHUNK), F32),
                   tok_shape(MC_WIDTH, F32),
                   jax.ShapeDtypeStruct((B, T // CHUNK, 2 * NHD, LANES), F32),
                   tok_shape(2 * NQK_S, BF16), tok_shape(2 * NQK_S, BF16), tok_shape(2 * NQK_S, BF16),
                   tok_shape(BRANCH_W, BF16), tok_shape(BRANCH_W, BF16)),
        grid=(nt,),
        in_specs=[pl.BlockSpec((1, TILE, D_MODEL), lambda i: (1, blk(i), 0)),
                  pl.BlockSpec((1, HALO, D_MODEL), lambda i: (1, prev_halo(i), 0)),
                  pl.BlockSpec((1, HALO, D_MODEL), lambda i: (1, next_halo(i), 0)),
                  pl.BlockSpec((1, TILE, D_MODEL), lambda i: (0, blk(nxt(i)), 0)),
                  pl.BlockSpec((1, HALO, D_MODEL), lambda i: (0, prev_halo(nxt(i)), 0)),
                  pl.BlockSpec((1, HALO, D_MODEL), lambda i: (0, next_halo(nxt(i)), 0)),
                  pl.BlockSpec((B, TILE, D_MODEL), lambda i: (0, 0, 0)),
                  const((B + 1, 6, D_MODEL)), const((1, D_MODEL)),
                  pl.BlockSpec((1, D_MODEL, W_PROJ), lambda i: (layer, 0, 0), pipeline_mode=pl.Buffered(1)),
                  const((4 * GLA_RANK, 2 * NQK_S)), const((1, 2 * NQK_S)),
                  const((3, 2 * NQK_M)), const((1, 2 * NHD)),
                  pl.BlockSpec((TILE, NQK_S), lambda i: (i, 0)),
                  pl.BlockSpec((TILE, NQK_S), lambda i: (i, 0)),
                  const((1, 2 * NQK_S))],
        out_specs=(tile(2 * NQK_S), tile(2 * NQK_S), tile(2 * NQK_S), tile(BRANCH_W), tile(BRANCH_W),
                   pl.BlockSpec((B, CHUNKS_PER_TILE, 2 * NQK_S, LANES), lambda i: (0, i, 0, 0)),
                   tile(NQK_M), tile(NQK_M), tile(2 * NQK_M), tile(BRANCH_W), tile(BRANCH_W),
                   pl.BlockSpec((B, 2 * PAIRS, TILE, 2 * CHUNK), lambda i: (0, 0, i, 0)),
                   tile(MC_WIDTH),
                   pl.BlockSpec((B, CHUNKS_PER_TILE, 2 * NHD, LANES), lambda i: (0, i, 0, 0)),
                   tile(2 * NQK_S), tile(2 * NQK_S), tile(2 * NQK_S), tile(BRANCH_W), tile(BRANCH_W)),
        scratch_shapes=[pltpu.VMEM((TILE + 2 * HALO, 2 * NQK_M), F32)] * 2
                       + [pltpu.VMEM((TILE, PW_MLR_WIDTH), F32)] * 2
                       + [pltpu.VMEM((TILE, PW_GR_WIDTH), F32)] * 2
                       + [pltpu.VMEM((6, TILE, NQK_S), F32), pltpu.VMEM((2, TILE, TILE), BF16)],
        compiler_params=_cparams(1, VMEM_LIMIT_PROJ), name="proj",
    )(x_src, x_src, x_src, x_src, x_src, x_src, ctx_src,
      mods, g1, w, a2s, abias, conv_w, gate_bias, cos_t, sin_t, dl_row)


def _mix_kernel(batch, *refs):
    (gq_f, gq_b, gk_f, gk_b, ge_f, ge_b, gv_f, gv_b, gd_f, gd_b,
     rq_f, rq_b, rk_f, rk_b, re_f, re_b, rv_f, rv_b, rdl,
     mq_f, mq_b, mk_f, mk_b, mw_f, mw_b, mv_f, mv_b, mp_f, mp_b, mcol_f, mcol_b, mc_f, mc_b,
     of_ref, ob_ref, sg_ref, sr_ref, sm_ref, m_ref) = refs
    j = pl.program_id(0)

    @pl.when(j == 0)
    def _():
        sg_ref[...] = jnp.zeros_like(sg_ref)
        sr_ref[...] = jnp.zeros_like(sr_ref)
        sm_ref[...] = jnp.zeros_like(sm_ref)
        m_ref[...] = jnp.full_like(m_ref, M_INIT)

    row_head = lax.broadcasted_iota(jnp.int32, (2 * CHUNK, PAIR_V), 0) // CHUNK
    diag_kv = row_head == lax.broadcasted_iota(jnp.int32, (2 * CHUNK, PAIR_V), 1) // DV
    diag_kk = (lax.broadcasted_iota(jnp.int32, (2 * CHUNK, PAIR_K), 0) // CHUNK
               == lax.broadcasted_iota(jnp.int32, (2 * CHUNK, PAIR_K), 1) // DK_SMALL)
    t_idx = lax.broadcasted_iota(jnp.int32, (CHUNK, 2 * CHUNK), 0)
    s_idx = lax.broadcasted_iota(jnp.int32, (CHUNK, 2 * CHUNK), 1) % CHUNK
    causal = (s_idx <= t_idx, s_idx >= t_idx)
    ones_v = jnp.ones((CHUNK, DV), BF16)
    zeros_vext = jnp.zeros((CHUNK, PAIR_V), BF16)
    ret_dec = jnp.exp(float(CHUNK) * _log_sigmoid(rdl[...]))

    decay_sets = (
        ((gq_f, gk_f, ge_f, gv_f), (gq_b, gk_b, ge_b, gv_b), sg_ref, 0),
        ((rq_f, rk_f, re_f, rv_f), (rq_b, rk_b, re_b, rv_b), sr_ref, 2 * BRANCH_W),
    )
    ml_sets = ((mq_f, mk_f, mw_f, mv_f, mp_f, mcol_f, mc_f),
               (mq_b, mk_b, mw_b, mv_b, mp_b, mcol_b, mc_b))
    outs = (of_ref, ob_ref)

    def chunk_step(step):
        work = []
        for bb, d in ((b_, d_) for b_ in range(batch) for d_ in range(2)):
            c = step if d == 0 else CHUNKS_PER_TILE - 1 - step
            rows = pl.ds(c * CHUNK, CHUNK)
            o_ref = outs[d]
            for bi, (fset, bset, s_ref, base) in enumerate(decay_sets):
                q_r, k_r, e_r, v_r = fset if d == 0 else bset
                for p in range(PAIRS):
                    ks = slice(p * PAIR_K, (p + 1) * PAIR_K)
                    q2 = q_r[bb, rows,ks]
                    k2 = k_r[bb, rows,ks]
                    v2 = v_r[bb, rows,p * PAIR_V:(p + 1) * PAIR_V]
                    k_bd = jnp.where(diag_kk, jnp.concatenate([k2, k2], axis=0), 0.0)
                    if bi == 0:
                        dec = (gd_f if d == 0 else gd_b)[bb, c, ks, :]
                    else:
                        dec = ret_dec[d, ks, :]
                    upd = jnp.where(diag_kv, _dot_tn(e_r[bb, rows,ks], v2), 0.0)
                    work.append(dict(
                        kind="decay", bb=bb, d=d, p=p, rows=rows, o_ref=o_ref, s_ref=s_ref, base=base,
                        q2=q2, v2=v2, dec=jnp.concatenate([dec, dec], axis=1), upd=upd,
                        att=_dot_nt(q2, k_bd)))
            q_r, k_r, w_r, v_r, p_r, col_r, c_r = ml_sets[d]
            cs = c_r[bb, c]
            cols = col_r[bb, rows,:]
            for p in range(PAIRS):
                q2 = q_r[bb, rows,p * 2 * DK_ML:(p + 1) * 2 * DK_ML]
                k2 = k_r[bb, rows,p * 2 * DK_ML:(p + 1) * 2 * DK_ML]
                k_bd = jnp.where(diag_kv, jnp.concatenate([k2, k2], axis=0), 0.0)
                att = _dot_nt(q2, k_bd)
                heads = []
                for hl in range(2):
                    hd = 2 * p + hl
                    jj = d * HEADS + hd
                    vext = jnp.concatenate([v_r[bb, rows,hd * DV:(hd + 1) * DV], ones_v], axis=1)
                    st = sm_ref[bb, d, hd]
                    heads.append(dict(
                        hd=hd, st=st, vext=vext,
                        inter=_dot(q2[:, hl * DK_ML:(hl + 1) * DK_ML], st.astype(BF16)),
                        upd=_dot_tn(w_r[bb, rows,hd * DK_ML:(hd + 1) * DK_ML], vext),
                        u=jnp.broadcast_to(cols[:, MC_U + jj:MC_U + jj + 1], (CHUNK, LANES)),
                        b=jnp.broadcast_to(cols[:, MC_B + jj:MC_B + jj + 1], (CHUNK, LANES))))
                work.append(dict(kind="ml", bb=bb, d=d, p=p, rows=rows, o_ref=o_ref, att=att, heads=heads,
                                 p0=p_r[bb, p, rows, :], cs=cs))
        for w in work:
            bb, d, p, rows, o_ref = w["bb"], w["d"], w["p"], w["rows"], w["o_ref"]
            if w["kind"] == "decay":
                s_ref = w["s_ref"]
                s_prev = s_ref[bb, d, p]
                att = jnp.where(causal[d], w["att"], 0.0).astype(BF16)
                v_bd = jnp.where(diag_kv, jnp.concatenate([w["v2"], w["v2"]], axis=0), 0.0)
                lhs = jnp.concatenate([att, w["q2"]], axis=1)
                rhs = jnp.concatenate([v_bd, s_prev.astype(BF16)], axis=0)
                o_ref[bb, rows, w["base"] + p * PAIR_V:w["base"] + (p + 1) * PAIR_V] = _dot(lhs, rhs)
                s_ref[bb, d, p] = w["dec"] * s_prev + w["upd"]
            else:
                sc = (w["att"] * w["p0"]).astype(BF16)
                for hl, hw in enumerate(w["heads"]):
                    hd = hw["hd"]
                    jj = d * HEADS + hd
                    vz = (jnp.concatenate([hw["vext"], zeros_vext], axis=0) if hl == 0
                          else jnp.concatenate([zeros_vext, hw["vext"]], axis=0))
                    intra = _dot(sc, vz)
                    m_prev = m_ref[bb, jj:jj + 1, :]
                    u = hw["u"]
                    mx = jnp.maximum(u, m_prev)
                    f = jnp.exp(u - mx)
                    e = jnp.exp(m_prev - mx)
                    num = f * intra[:, 0:DV] + e * hw["inter"][:, 0:DV]
                    den = f * intra[:, DV:] + e * hw["inter"][:, DV:]
                    den = jnp.maximum(jnp.abs(den), jnp.exp(-(hw["b"] + mx)))
                    o_ref[bb, rows, BRANCH_W + hd * DV:BRANCH_W + (hd + 1) * DV] = num / den
                    tot = w["cs"][jj:jj + 1, :]
                    mloc = w["cs"][NHD + jj:NHD + jj + 1, :]
                    m_new = jnp.maximum(tot + m_prev, mloc)
                    keep = jnp.exp(tot + m_prev - m_new)
                    gain = jnp.exp(mloc - m_new)
                    sm_ref[bb, d, hd] = (jnp.concatenate([keep, keep], axis=1) * hw["st"]
                                         + jnp.concatenate([gain, gain], axis=1) * hw["upd"])
                    m_ref[bb, jj:jj + 1, :] = m_new

    for step in range(CHUNKS_PER_TILE):
        chunk_step(step)


def _mix(gla, ret, ml, ret_dl_col):
    gq, gk, ge, gv, gdec = gla
    rq, rk, re, rv = ret
    mq, mk, mw, mv, mp, mcol, mc = ml
    B, T, _ = gv.shape
    nt = T // TILE

    def tf(j):
        return j

    def tb(j):
        return jnp.where(j == 0, 0, nt - j)

    def pair(width, lane_block_b=0):
        return [pl.BlockSpec((B, TILE, width), lambda j: (0, tf(j), 0)),
                pl.BlockSpec((B, TILE, width), lambda j: (0, tb(j), lane_block_b))]

    dec_pair = [pl.BlockSpec((B, CHUNKS_PER_TILE, NQK_S, LANES), lambda j: (0, tf(j), 0, 0)),
                pl.BlockSpec((B, CHUNKS_PER_TILE, NQK_S, LANES), lambda j: (0, tb(j), 1, 0))]
    p0_pair = [pl.BlockSpec((B, PAIRS, TILE, 2 * CHUNK), lambda j: (0, 0, tf(j), 0)),
               pl.BlockSpec((B, PAIRS, TILE, 2 * CHUNK), lambda j: (0, 1, tb(j), 0))]
    cs_pair = [pl.BlockSpec((B, CHUNKS_PER_TILE, 2 * NHD, LANES), lambda j: (0, tf(j), 0, 0)),
               pl.BlockSpec((B, CHUNKS_PER_TILE, 2 * NHD, LANES), lambda j: (0, tb(j), 0, 0))]
    decay_specs = pair(NQK_S, 1) + pair(NQK_S, 1) + pair(NQK_S, 1) + pair(BRANCH_W)
    in_specs = (decay_specs + dec_pair + decay_specs
                + [pl.BlockSpec((2, NQK_S, LANES), lambda j: (0, 0, 0))]
                + pair(NQK_M) + pair(NQK_M) + pair(NQK_M, 1) + pair(BRANCH_W) + p0_pair
                + pair(MC_WIDTH) + cs_pair)
    args = (gq, gq, gk, gk, ge, ge, gv, gv, gdec, gdec,
            rq, rq, rk, rk, re, re, rv, rv, ret_dl_col,
            mq, mq, mk, mk, mw, mw, mv, mv, mp, mp, mcol, mcol, mc, mc)
    return pl.pallas_call(
        functools.partial(_mix_kernel, B),
        out_shape=(jax.ShapeDtypeStruct((B, T, 3 * BRANCH_W), F32),) * 2,
        grid=(nt,),
        in_specs=in_specs,
        out_specs=(pl.BlockSpec((B, TILE, 3 * BRANCH_W), lambda j: (0, tf(j), 0)),
                   pl.BlockSpec((B, TILE, 3 * BRANCH_W), lambda j: (0, tb(j), 0))),
        scratch_shapes=[pltpu.VMEM((B, 2, PAIRS, PAIR_K, PAIR_V), F32),
                        pltpu.VMEM((B, 2, PAIRS, PAIR_K, PAIR_V), F32),
                        pltpu.VMEM((B, 2, HEADS, DK_ML, 2 * DV), F32),
                        pltpu.VMEM((B, NHD, LANES), F32)],
        compiler_params=_cparams(1), name="mix",
    )(*args)


def _merge_kernel(batch, first_tile, x_ref, ctx_ref, mod_ref, g_ref, wg_ref, of_ref, ob_ref,
                  sgg_ref, sgm_ref, sgr_ref, ng_ref, wb_ref, wo_ref, out_ref):
    tile_idx = pl.program_id(0) + first_tile
    side = (sgg_ref, sgm_ref, sgr_ref)
    mods = [mod_ref[jnp.where(tile_idx == 0, batch, bb)] for bb in range(batch)]
    xs = [jnp.where(tile_idx == 0, ctx_ref[bb], x_ref[bb]) for bb in range(batch)]
    hs = [_norm_mod(xs[bb], g_ref[...], mods[bb][0:1, :], mods[bb][1:2, :]).astype(BF16)
          for bb in range(batch)]
    accs = [jnp.zeros((TILE, D_MODEL), F32) for _ in range(batch)]
    for br in range(3):
        for bb in range(batch):
            parts = []
            for hd in range(HEADS):
                cols = slice(br * BRANCH_W + hd * DV, br * BRANCH_W + (hd + 1) * DV)
                oh = of_ref[bb, :, cols] + ob_ref[bb, :, cols]
                if br > 0:
                    oh = oh - jnp.mean(oh, axis=-1, keepdims=True)
                parts.append(oh * lax.rsqrt(jnp.mean(oh * oh, axis=-1, keepdims=True) + EPS))
            y = jnp.concatenate(parts, axis=1) * ng_ref[:, br * BRANCH_W:(br + 1) * BRANCH_W]
            y = (y * side[br][bb]).astype(BF16)
            gate = _sigmoid(_dot(hs[bb], wg_ref[0, :, br * D_MODEL:(br + 1) * D_MODEL]))
            accs[bb] = accs[bb] + gate * _dot(y, wb_ref[0, br])
    for bb in range(batch):
        out = _dot(accs[bb].astype(BF16), wo_ref[0])
        out_ref[bb] = xs[bb] + mods[bb][2:3, :] * out


def _merge(ctx_src, x_src, off, mods, g1, wgate, o_f, o_b, sg_gla, sz_ml, sg_ret, norm_g, wb, wo, layer, skip):
    B = x_src.shape[0]
    nt = x_src.shape[1] // TILE + off - skip
    tile = lambda width: pl.BlockSpec((B, TILE, width), lambda i: (0, i + skip, 0))
    const = lambda shape: pl.BlockSpec(shape, lambda i: (0,) * len(shape))
    per_layer = lambda shape: pl.BlockSpec((1,) + shape, lambda i: (layer,) + (0,) * len(shape),
                                           pipeline_mode=pl.Buffered(1))
    return pl.pallas_call(
        functools.partial(_merge_kernel, B, skip),
        out_shape=jax.ShapeDtypeStruct((B, nt * TILE, D_MODEL), F32),
        grid=(nt,),
        in_specs=[pl.BlockSpec((B, TILE, D_MODEL), lambda i: (0, jnp.maximum(i + skip - off, 0), 0)),
                  pl.BlockSpec((B, TILE, D_MODEL), lambda i: (0, 0, 0)),
                  const((B + 1, 6, D_MODEL)),
                  const((1, D_MODEL)), per_layer((D_MODEL, W_GATE)),
                  tile(3 * BRANCH_W), tile(3 * BRANCH_W),
                  tile(BRANCH_W), tile(BRANCH_W), tile(BRANCH_W),
                  const((1, 3 * BRANCH_W)), per_layer((3, BRANCH_W, D_MODEL)),
                  per_layer((D_MODEL, D_MODEL))],
        out_specs=pl.BlockSpec((B, TILE, D_MODEL), lambda i: (0, i, 0)),
        compiler_params=_cparams(1), name="merge",
    )(x_src, ctx_src, mods, g1, wgate, o_f, o_b, sg_gla, sz_ml, sg_ret, norm_g, wb, wo)


def _mlp_kernel(batch, first_tile, final, x_ref, mod_ref, g_ref, wu_ref, wd_ref, fg_ref, out_ref):
    tile_idx = pl.program_id(0) + first_tile
    mods = [mod_ref[jnp.where(tile_idx == 0, batch, bb)] for bb in range(batch)]
    ups = []
    for bb in range(batch):
        h = _norm_mod(x_ref[bb], g_ref[...], mods[bb][3:4, :], mods[bb][4:5, :]).astype(BF16)
        ups.append(_dot(h, wu_ref[0]))
    for bb in range(batch):
        u = jnp.maximum(ups[bb], 0.0)
        y = x_ref[bb] + mods[bb][5:6, :] * _dot((u * u).astype(BF16), wd_ref[0])
        if final:
            y = y * lax.rsqrt(jnp.mean(y * y, axis=-1, keepdims=True) + EPS) * fg_ref[...]
        out_ref[bb] = y


def _mlp(tok, mods, g2, wu, wd, layer, final_g, first_tile, final):
    B, T, _ = tok.shape
    const = lambda shape: pl.BlockSpec(shape, lambda i: (0,) * len(shape))
    per_layer = lambda shape: pl.BlockSpec((1,) + shape, lambda i: (layer,) + (0,) * len(shape),
                                           pipeline_mode=pl.Buffered(1))
    return pl.pallas_call(
        functools.partial(_mlp_kernel, B, first_tile, final),
        out_shape=jax.ShapeDtypeStruct((B, T, D_MODEL), F32),
        grid=(T // TILE,),
        in_specs=[pl.BlockSpec((B, TILE, D_MODEL), lambda i: (0, i, 0)),
                  const((B + 1, 6, D_MODEL)),
                  const((1, D_MODEL)), per_layer((D_MODEL, D_FF)), per_layer((D_FF, D_MODEL)),
                  const((1, D_MODEL))],
        out_specs=pl.BlockSpec((B, TILE, D_MODEL), lambda i: (0, i, 0)),
        compiler_params=_cparams(1), name="mlp",
    )(tok, mods, g2, wu, wd, final_g)


def kernel(x, c, ctx, c_ctx, norm1_g, norm2_g, w_ada, b_ada, w_in, gla_a2, gla_a_bias, gla_norm_g,
           ml_conv, ml_i_bias, ml_f_bias, ml_norm_g, ret_decay_logit, ret_norm_g, w_branch, w_o,
           w_up, w_down, final_g):
    B, seq, _ = x.shape
    ctx_len = ctx.shape[1]
    assert B == 2 and ctx_len == TILE and seq % TILE == 0 and seq % GRID_W == 0
    ctx_src, x_src, off = ctx, x, 1

    c_rows = jnp.concatenate([c, c_ctx[None, :], jnp.zeros((8 - B - 1, D_MODEL), F32)], axis=0)
    mods_all = _ada(c_rows, w_ada, b_ada)[:, 0:3].reshape(DEPTH, 3, 6, D_MODEL)

    cos_t, sin_t = _rope_tables(seq, ctx_len)

    w_proj, w_gate = _wprep(w_in)
    w_branch_b = w_branch.astype(BF16)
    w_o_b = w_o.astype(BF16)
    w_up_b = w_up.astype(BF16)
    w_down_b = w_down.astype(BF16)
    for l in range(DEPTH):
        last = l == DEPTH - 1
        skip = 1 if last else 0
        mods = mods_all[l]
        g1 = norm1_g[l][None, :]
        g2 = norm2_g[l][None, :]
        zeros = jnp.zeros((GLA_RANK, NQK_S), F32)
        a2blk = jnp.concatenate([jnp.concatenate([gla_a2[l, 0], zeros], axis=1),
                                 jnp.concatenate([zeros, gla_a2[l, 1]], axis=1)], axis=0)
        a2s = jnp.concatenate([a2blk, a2blk], axis=0).astype(BF16)
        abias = gla_a_bias[l].reshape(1, 2 * NQK_S)
        gate_bias = jnp.concatenate([ml_i_bias[l].reshape(1, NHD), ml_f_bias[l].reshape(1, NHD)], axis=1)
        dl = ret_decay_logit[l]
        dl_row = jnp.repeat(dl, DK_SMALL, axis=1).reshape(1, 2 * NQK_S)
        dl_col = jnp.broadcast_to(jnp.repeat(dl, DK_SMALL, axis=1)[:, :, None], (2, NQK_S, LANES))

        (gq, gk, ge, gv, sg_gla, gdec, mq, mk, mw, mv, sz_ml, mp, mcol, mc,
         rq, rk, re, rv, sg_ret) = _proj(ctx_src, x_src, off, mods, g1, w_proj, l, a2s, abias, ml_conv[l],
                                         gate_bias, cos_t, sin_t, dl_row)
        o_f, o_b = _mix((gq, gk, ge, gv, gdec), (rq, rk, re, rv), (mq, mk, mw, mv, mp, mcol, mc), dl_col)
        norm_g = jnp.concatenate([gla_norm_g[l], ml_norm_g[l], ret_norm_g[l]])[None, :]
        tok = _merge(ctx_src, x_src, off, mods, g1, w_gate, o_f, o_b, sg_gla, sz_ml, sg_ret, norm_g,
                     w_branch_b, w_o_b, l, skip)
        tok = _mlp(tok, mods, g2, w_up_b, w_down_b, l, final_g[None, :], skip, last)
        ctx_src, x_src, off = tok, tok, 0
    return tok
```

```python
import functools

import jax
import jax.numpy as jnp
from jax import lax
from jax.experimental import pallas as pl
from jax.experimental.pallas import tpu as pltpu

F32 = jnp.float32
BF16 = jnp.bfloat16

D_MODEL = 1024
DEPTH = 2
GRID_W = 64
CHUNK = 64
EPS = 1e-6
M_INIT = -1e30
HEADS = 4
DK_SMALL = 64
DK_ML = 128
DV = 128
BRANCH_W = HEADS * DV
GLA_RANK = 16
GLA_TAU = 16.0
ROPE_BASE = 10000.0
D_FF = 4 * D_MODEL
LANES = 128

TILE = 256
CHUNKS_PER_TILE = TILE // CHUNK
HALO = 8
VMEM_LIMIT = 56 * 1024 * 1024
VMEM_LIMIT_PROJ = 60 * 1024 * 1024

NQK_S = HEADS * DK_SMALL
NQK_M = HEADS * DK_ML
NHD = 2 * HEADS
W_GLA = 2 * NQK_S + 2 * BRANCH_W + 2 * GLA_RANK
W_ML = 2 * NQK_M + 2 * BRANCH_W + 2 * NHD
W_RET = 2 * NQK_S + 2 * BRANCH_W
W_GATE = 3 * D_MODEL

PW_MLQK = 0
PW_MLR = PW_MLQK + 2 * NQK_M
PW_MLR_WIDTH = 2 * BRANCH_W + LANES
PW_GR = PW_MLR + PW_MLR_WIDTH
PW_GR_WIDTH = 2 * (2 * NQK_S + 2 * BRANCH_W) + LANES
W_PROJ = PW_GR + PW_GR_WIDTH
MLR_V, MLR_Z, MLR_GATES = 0, BRANCH_W, 2 * BRANCH_W
GR_GLA_Q, GR_GLA_K, GR_GLA_V, GR_GLA_G = 0, NQK_S, 2 * NQK_S, 2 * NQK_S + BRANCH_W
GR_RET_Q = 2 * NQK_S + 2 * BRANCH_W
GR_RET_K, GR_RET_V, GR_RET_G = GR_RET_Q + NQK_S, GR_RET_Q + 2 * NQK_S, GR_RET_Q + 2 * NQK_S + BRANCH_W
GR_GLA_A = 2 * GR_RET_Q

MC_B, MC_W, MC_U = 0, NHD, 2 * NHD
MC_WIDTH = 3 * NHD

PAIRS = HEADS // 2
PAIR_K = 2 * DK_SMALL
PAIR_V = 2 * DV
MIX_LOOKAHEAD = 23


def _cparams(n_axes, vmem_limit=VMEM_LIMIT):
    return pltpu.CompilerParams(dimension_semantics=("arbitrary",) * n_axes,
                                vmem_limit_bytes=vmem_limit)


def _log_sigmoid(z):
    return jnp.minimum(z, 0.0) - jnp.log(1.0 + jnp.exp(-jnp.abs(z)))


def _sigmoid(z):
    return 1.0 / (1.0 + jnp.exp(-z))


def _silu(z):
    return z * _sigmoid(z)


def _norm_mod(x, g, shift, scale):
    y = x * lax.rsqrt(jnp.mean(x * x, axis=-1, keepdims=True) + EPS) * g
    return y * (1.0 + scale) + shift


def _dot(a, b):
    return jnp.dot(a, b, preferred_element_type=F32)


def _dot_nt(a, b):
    return lax.dot_general(a, b, (((1,), (1,)), ((), ())), preferred_element_type=F32)


def _dot_tn(a, b):
    return lax.dot_general(a, b, (((0,), (0,)), ((), ())), preferred_element_type=F32)


def _split2(x):
    hi = x.astype(BF16)
    return hi, (x - hi.astype(F32)).astype(BF16)


def _chunk_tri(lower):
    r = lax.broadcasted_iota(jnp.int32, (TILE, TILE), 0)
    c = lax.broadcasted_iota(jnp.int32, (TILE, TILE), 1)
    same = (r // CHUNK) == (c // CHUNK)
    tri = (c <= r) if lower else (c >= r)
    return jnp.where(same & tri, 1.0, 0.0).astype(BF16)


def _chunk_total(x):
    w = x.shape[-1]
    tot = jnp.sum(x.reshape(CHUNKS_PER_TILE, CHUNK, w), axis=1, keepdims=True)
    return jnp.broadcast_to(tot, (CHUNKS_PER_TILE, CHUNK, w)).reshape(TILE, w)


def _ada_kernel(c_ref, w_ref, b_ref, o_ref):
    s = _silu(c_ref[...])
    s_hi = s.astype(BF16).astype(F32)
    s2 = jnp.concatenate([s_hi, s - s_hi], axis=0).astype(BF16)
    w_hi, w_lo = _split2(w_ref[0])
    r = _dot(s2, w_hi) + _dot(s2, w_lo)
    o_ref[0] = r[0:8] + r[8:16] + b_ref[0]


def _ada(c_rows, w_ada, b_ada):
    tn = 1536
    return pl.pallas_call(
        _ada_kernel,
        out_shape=jax.ShapeDtypeStruct((DEPTH, 8, 6 * D_MODEL), F32),
        grid=(DEPTH, 6 * D_MODEL // tn),
        in_specs=[pl.BlockSpec((8, D_MODEL), lambda l, n: (0, 0)),
                  pl.BlockSpec((1, D_MODEL, tn), lambda l, n: (l, 0, n)),
                  pl.BlockSpec((1, 1, tn), lambda l, n: (l, 0, n))],
        out_specs=pl.BlockSpec((1, 8, tn), lambda l, n: (l, 0, n)),
        compiler_params=_cparams(2), name="ada",
    )(c_rows, w_ada, b_ada.reshape(DEPTH, 1, 6 * D_MODEL))


def _rope_kernel(ang_ref, cos_ref, sin_ref, nsin_ref):
    a = ang_ref[...]
    s = jnp.sin(a)
    cos_ref[...] = jnp.cos(a)
    sin_ref[...] = s
    nsin_ref[...] = -s


def _rope_tables(seq, ctx_len):
    rows = seq // GRID_W
    nf = DK_SMALL // 4
    inv = ROPE_BASE ** (-jnp.arange(nf, dtype=F32) / nf)
    ang = jnp.concatenate([jnp.arange(rows, dtype=F32)[:, None] * inv,
                           jnp.arange(GRID_W, dtype=F32)[:, None] * inv], axis=0)
    n = rows + GRID_W
    cos, sin, nsin = pl.pallas_call(
        _rope_kernel,
        out_shape=(jax.ShapeDtypeStruct((n, nf), F32),) * 3,
        name="rope_tables",
    )(ang)

    def per_token(first, second):
        def half(t):
            r = jnp.broadcast_to(t[:rows, None, :], (rows, GRID_W, nf))
            c = jnp.broadcast_to(t[None, rows:, :], (rows, GRID_W, nf))
            return jnp.concatenate([r, c], axis=-1)
        head = jnp.concatenate([half(first), half(second)], axis=-1).reshape(seq, DK_SMALL)
        return jnp.tile(head, (1, HEADS))

    ones = jnp.ones((ctx_len, NQK_S), F32)
    cos_t = jnp.concatenate([ones, per_token(cos, cos)], axis=0)
    sin_t = jnp.concatenate([0.0 * ones, per_token(nsin, sin)], axis=0)
    return cos_t, sin_t


def _wprep_kernel(w_ref, wp_ref, wg_ref):
    o0, o1, o2 = W_GLA, W_GLA + W_ML, W_GLA + W_ML + W_RET
    n_sg = 2 * NQK_S + 2 * BRANCH_W
    rows = w_ref.shape[1]

    def copy(dst_ref, dst, src, n):
        dst_ref[0, :, dst:dst + n] = w_ref[0, :, src:src + n].astype(BF16)

    copy(wp_ref, PW_MLQK, o0, 2 * NQK_M)
    n_mlr = 2 * BRANCH_W + 2 * NHD
    copy(wp_ref, PW_MLR, o0 + 2 * NQK_M, n_mlr)
    wp_ref[0, :, PW_MLR + n_mlr:PW_GR] = jnp.zeros((rows, PW_GR - PW_MLR - n_mlr), BF16)
    copy(wp_ref, PW_GR + GR_GLA_Q, 0, n_sg)
    copy(wp_ref, PW_GR + GR_RET_Q, o1, W_RET)
    copy(wp_ref, PW_GR + GR_GLA_A, n_sg, 2 * GLA_RANK)
    pad0 = PW_GR + GR_GLA_A + 2 * GLA_RANK
    wp_ref[0, :, pad0:W_PROJ] = jnp.zeros((rows, W_PROJ - pad0), BF16)
    copy(wg_ref, 0, o2, W_GATE)


def _wprep(w_in):
    rows = 256
    return pl.pallas_call(
        _wprep_kernel,
        out_shape=(jax.ShapeDtypeStruct((DEPTH, D_MODEL, W_PROJ), BF16),
                   jax.ShapeDtypeStruct((DEPTH, D_MODEL, W_GATE), BF16)),
        grid=(DEPTH, D_MODEL // rows),
        in_specs=[pl.BlockSpec((1, rows, w_in.shape[2]), lambda l, r: (l, r, 0))],
        out_specs=(pl.BlockSpec((1, rows, W_PROJ), lambda l, r: (l, r, 0)),
                   pl.BlockSpec((1, rows, W_GATE), lambda l, r: (l, r, 0))),
        compiler_params=_cparams(2), name="wprep",
    )(w_in)


def _proj_kernel(nt, xa_ref, xap_ref, xan_ref, xb_ref, xbp_ref, xbn_ref, ctx_ref,
                 mod_ref, g_ref, w_ref, a2_ref, ab_ref, cw_ref, gb_ref, cos_ref, sin_ref, dl_ref,
                 gq_ref, gk_ref, ge_ref, gv_ref, gsg_ref, gdec_ref,
                 mq_ref, mk_ref, mkw_ref, mv_ref, msz_ref, mp0_ref, mcol_ref, mcs_ref,
                 rq_ref, rk_ref, re_ref, rv_ref, rsg_ref,
                 ext0_ref, ext1_ref, prml0_ref, prml1_ref, prgr0_ref, prgr1_ref, rdec_ref, tri_ref,
                 h0_ref, h1_ref):
    i = pl.program_id(0)
    ext_refs = (ext0_ref, ext1_ref)
    prml_refs = (prml0_ref, prml1_ref)
    prgr_refs = (prgr0_ref, prgr1_ref)
    h_refs = (h0_ref, h1_ref)
    row8 = lax.broadcasted_iota(jnp.int32, (NHD, TILE), 0)
    lane8 = lax.broadcasted_iota(jnp.int32, (NHD, TILE), 1)
    lane_chunk = lane8 // CHUNK
    lane_in_chunk = lane8 % CHUNK
    ti = lax.broadcasted_iota(jnp.int32, (CHUNK, CHUNK), 0)
    si = lax.broadcasted_iota(jnp.int32, (CHUNK, CHUNK), 1)
    ext_row = lax.broadcasted_iota(jnp.int32, (TILE + 2 * HALO, 1), 0)

    def project_stages(x, x_prev, x_next, tile_idx, mod_row, slot):
        st = {}

        def mlstm_qk():
            mod = mod_ref[mod_row]
            x_all = jnp.concatenate([x_prev, x, x_next], axis=0)
            h_all = _norm_mod(x_all, g_ref[...], mod[0:1, :], mod[1:2, :])
            h_refs[slot][...] = h_all[HALO:HALO + TILE, :].astype(BF16)
            prev_ok = tile_idx >= 2
            next_ok = jnp.logical_and(tile_idx >= 1, tile_idx <= nt - 2)
            dead = jnp.logical_or(jnp.logical_and(ext_row < HALO, jnp.logical_not(prev_ok)),
                                  jnp.logical_and(ext_row >= TILE + HALO, jnp.logical_not(next_ok)))
            ext = _dot(h_all.astype(BF16), w_ref[0, :, PW_MLQK:PW_MLR])
            ext_refs[slot][...] = jnp.where(dead, 0.0, ext)

        def mlstm_rest():
            prml_refs[slot][...] = _dot(h_refs[slot][...], w_ref[0, :, PW_MLR:PW_GR])

        def gla():
            prgr_refs[slot][:, 0:GR_RET_Q] = _dot(h_refs[slot][...], w_ref[0, :, PW_GR:PW_GR + GR_RET_Q])

        def ret():
            prgr_refs[slot][:, GR_RET_Q:] = _dot(h_refs[slot][...], w_ref[0, :, PW_GR + GR_RET_Q:])

        return [mlstm_qk, mlstm_rest, gla, ret]

    def project(*args):
        for stage in project_stages(*args):
            stage()

    @pl.when(i == 0)
    def _():
        tri_ref[0] = _chunk_tri(True)
        tri_ref[1] = _chunk_tri(False)
        lg = _log_sigmoid(dl_ref[...])
        t_in_chunk = (lax.broadcasted_iota(jnp.int32, (TILE, NQK_S), 0) % CHUNK).astype(F32)
        for d in range(2):
            lg_d = lg[:, d * NQK_S:(d + 1) * NQK_S]
            steps = (t_in_chunk + 1.0) if d == 0 else (float(CHUNK) - t_in_chunk)
            b = steps * lg_d
            rdec_ref[3 * d] = jnp.exp(b)
            rdec_ref[3 * d + 1] = jnp.exp(-b)
            rdec_ref[3 * d + 2] = jnp.exp(float(CHUNK) * lg_d - b)
        halo0 = jnp.zeros((HALO, D_MODEL), F32)
        project(ctx_ref[0], halo0, halo0, 0, 2, 0)

    def rope(x):
        lane = lax.broadcasted_iota(jnp.int32, (TILE, NQK_S), 1)
        first_half = (lane % DK_SMALL) < (DK_SMALL // 2)
        swapped = jnp.where(first_half, pltpu.roll(x, NQK_S - DK_SMALL // 2, 1),
                            pltpu.roll(x, DK_SMALL // 2, 1))
        return x * cos_ref[...] + swapped * sin_ref[...]

    def seg_cummax(x, reverse):
        sh = 1
        while sh < CHUNK:
            if reverse:
                y, ok = pltpu.roll(x, TILE - sh, 1), lane_in_chunk < CHUNK - sh
            else:
                y, ok = pltpu.roll(x, sh, 1), lane_in_chunk >= sh
            x = jnp.where(ok, jnp.maximum(x, y), x)
            sh *= 2
        return x

    def prepare_stages(slot, bb):
        ext_ref, prml_ref, prgr_ref = ext_refs[slot], prml_refs[slot], prgr_refs[slot]
        st = {}

        def conv():
            qk = _silu(cw_ref[0:1, :] * ext_ref[HALO - 1:HALO - 1 + TILE, :]
                       + cw_ref[1:2, :] * ext_ref[HALO:HALO + TILE, :]
                       + cw_ref[2:3, :] * ext_ref[HALO + 1:HALO + 1 + TILE, :])
            mq_ref[bb] = qk[:, 0:NQK_M].astype(BF16)
            st["k_ml"] = qk[:, NQK_M:] * (DK_ML ** -0.5)
            mk_ref[bb] = st["k_ml"].astype(BF16)

        def gates_in():
            mv_ref[bb] = prml_ref[:,MLR_V:MLR_V + BRANCH_W].astype(BF16)
            msz_ref[bb] = _silu(prml_ref[:,MLR_Z:MLR_Z + BRANCH_W]).astype(BF16)
            gates_t = (prml_ref[:,MLR_GATES:MLR_GATES + 2 * NHD] + gb_ref[...]).T
            st["li_t"] = gates_t[0:NHD]
            st["lf_t"] = _log_sigmoid(gates_t[NHD:])
            lf_hi, lf_lo = _split2(st["lf_t"])
            st["lf2"] = jnp.concatenate([lf_hi, lf_lo], axis=0)

        def gates_mm():
            st["cum_f"] = _dot(st["lf2"], tri_ref[1])
            st["cum_b"] = _dot(st["lf2"], tri_ref[0])

        def gates_out():
            mlstm_gates_body(st["li_t"], st["lf_t"], st["cum_f"], st["cum_b"], st["k_ml"], bb)

        def gla_in():
            gv_ref[bb] = prgr_ref[:,GR_GLA_V:GR_GLA_V + BRANCH_W].astype(BF16)
            gsg_ref[bb] = _silu(prgr_ref[:,GR_GLA_G:GR_GLA_G + BRANCH_W]).astype(BF16)
            a_hi, a_lo = _split2(prgr_ref[:,GR_GLA_A:GR_GLA_A + 2 * GLA_RANK])
            st["a2"] = jnp.concatenate([a_hi, a_lo], axis=1)

        def gla_z_mm():
            st["z"] = _dot(st["a2"], a2_ref[...])

        def gla_la():
            la = _log_sigmoid(st["z"] + ab_ref[...]) * (1.0 / GLA_TAU)
            st["la"] = la
            st["la_split"] = [_split2(la[:, d * NQK_S:(d + 1) * NQK_S]) for d in range(2)]

        def gla_cum_mm():
            st["b"] = []
            for d in range(2):
                tri = tri_ref[d]
                la_hi, la_lo = st["la_split"][d]
                st["b"].append(_dot(tri, la_hi) + _dot(tri, la_lo))

        def gla_out():
            gla_body(prgr_ref, st["la"], st["b"], bb)

        def ret():
            ret_body(prgr_ref, bb)

        return dict(conv=conv, gates_in=gates_in, gates_mm=gates_mm, gates_out=gates_out,
                    gla_in=gla_in, gla_z_mm=gla_z_mm, gla_la=gla_la, gla_cum_mm=gla_cum_mm,
                    gla_out=gla_out, ret=ret)

    def mlstm_gates_body(li_t, lf_t, cum_f, cum_b, k_ml, bb):
        b_t = jnp.where(row8 < HEADS, cum_f[0:NHD] + cum_f[NHD:], cum_b[0:NHD] + cum_b[NHD:])
        tot_t = jnp.zeros((NHD, TILE), F32)
        mloc_t = jnp.zeros((NHD, TILE), F32)
        for c in range(CHUNKS_PER_TILE):
            tot_c = jnp.sum(lf_t[:, c * CHUNK:(c + 1) * CHUNK], axis=-1, keepdims=True)
            tot_t = jnp.where(lane_chunk == c, tot_c, tot_t)
            mcs_ref[bb, c, 0:NHD, :] = jnp.broadcast_to(tot_c, (NHD, LANES))
        g_t = tot_t - b_t + li_t
        for c in range(CHUNKS_PER_TILE):
            mloc_c = jnp.max(g_t[:, c * CHUNK:(c + 1) * CHUNK], axis=-1, keepdims=True)
            mloc_t = jnp.where(lane_chunk == c, mloc_c, mloc_t)
            mcs_ref[bb, c, NHD:2 * NHD, :] = jnp.broadcast_to(mloc_c, (NHD, LANES))
        w_t = jnp.exp(g_t - mloc_t)
        r_t = li_t - b_t
        u_t = jnp.where(row8 < HEADS, seg_cummax(r_t, False), seg_cummax(r_t, True))
        cols = jnp.concatenate([b_t, w_t, u_t], axis=0).T
        mcol_ref[bb] = cols
        for j in range(NHD):
            d, hd = j // HEADS, j % HEADS
            mask = (si <= ti) if d == 0 else (si >= ti)
            mkw_ref[bb, :, d * NQK_M + hd * DK_ML:d * NQK_M + (hd + 1) * DK_ML] = (
                k_ml[:, hd * DK_ML:(hd + 1) * DK_ML] * cols[:, MC_W + j:MC_W + j + 1]).astype(BF16)
            for c in range(CHUNKS_PER_TILE):
                rows = slice(c * CHUNK, (c + 1) * CHUNK)
                a = r_t[j:j + 1, rows] - cols[rows, MC_U + j:MC_U + j + 1]
                mp0_ref[bb, j // 2, rows, (j % 2) * CHUNK:(j % 2 + 1) * CHUNK] = jnp.where(mask, jnp.exp(a), 0.0)

    def gla_body(prgr_ref, la, cums, bb):
        half_t = jnp.sum(la.reshape(2 * CHUNKS_PER_TILE, CHUNK // 2, 2 * NQK_S), axis=1).T
        q_gla = prgr_ref[:,GR_GLA_Q:GR_GLA_Q + NQK_S] * (DK_SMALL ** -0.5)
        k_gla = prgr_ref[:,GR_GLA_K:GR_GLA_K + NQK_S]
        for d in range(2):
            la_d = la[:, d * NQK_S:(d + 1) * NQK_S]
            b = cums[d]
            tot = _chunk_total(la_d)
            gq_ref[bb, :, d * NQK_S:(d + 1) * NQK_S] = (q_gla * jnp.exp(b)).astype(BF16)
            gk_ref[bb, :, d * NQK_S:(d + 1) * NQK_S] = (k_gla * jnp.exp(-b)).astype(BF16)
            ge_ref[bb, :, d * NQK_S:(d + 1) * NQK_S] = (k_gla * jnp.exp(tot - b)).astype(BF16)
            ht = half_t[d * NQK_S:(d + 1) * NQK_S, :]
            for c in range(CHUNKS_PER_TILE):
                tot_col = ht[:, 2 * c:2 * c + 1] + ht[:, 2 * c + 1:2 * c + 2]
                gdec_ref[bb, c, d * NQK_S:(d + 1) * NQK_S, :] = jnp.broadcast_to(jnp.exp(tot_col), (NQK_S, LANES))

    def ret_body(prgr_ref, bb):
        q_ret = rope(prgr_ref[:,GR_RET_Q:GR_RET_Q + NQK_S])
        k_ret = rope(prgr_ref[:,GR_RET_K:GR_RET_K + NQK_S] * (DK_SMALL ** -0.5))
        rv_ref[bb] = prgr_ref[:,GR_RET_V:GR_RET_V + BRANCH_W].astype(BF16)
        rsg_ref[bb] = _silu(prgr_ref[:,GR_RET_G:GR_RET_G + BRANCH_W]).astype(BF16)
        for d in range(2):
            rq_ref[bb, :, d * NQK_S:(d + 1) * NQK_S] = (q_ret * rdec_ref[3 * d]).astype(BF16)
            rk_ref[bb, :, d * NQK_S:(d + 1) * NQK_S] = (k_ret * rdec_ref[3 * d + 1]).astype(BF16)
            re_ref[bb, :, d * NQK_S:(d + 1) * NQK_S] = (k_ret * rdec_ref[3 * d + 2]).astype(BF16)

    next_tile = jnp.minimum(i + 1, nt - 1)
    xa = jnp.where(i == 0, ctx_ref[1], xa_ref[0])
    a_ext, a_mlr, a_gla, a_ret = project_stages(xa, xap_ref[0], xan_ref[0], i, jnp.where(i == 0, 2, 1), 1)
    b_ext, b_mlr, b_gla, b_ret = project_stages(xb_ref[0], xbp_ref[0], xbn_ref[0], next_tile, 0, 0)
    x = prepare_stages(0, 0)
    y = prepare_stages(1, 1)
    order = [a_ext, x["conv"], x["gates_in"],
             a_mlr, x["gates_mm"], x["gates_out"], x["gla_in"],
             a_gla, x["gla_z_mm"], x["gla_la"],
             a_ret, x["gla_cum_mm"],
             b_ext, x["gla_out"], x["ret"], y["conv"], y["gates_in"],
             b_mlr, y["gates_mm"], y["gates_out"], y["gla_in"],
             b_gla, y["gla_z_mm"], y["gla_la"],
             b_ret, y["gla_cum_mm"], y["gla_out"], y["ret"]]
    for stage in order:
        stage()


def _proj(ctx_src, x_src, off, mods, g1, w, layer, a2s, abias, conv_w, gate_bias, cos_t, sin_t, dl_row):
    B = x_src.shape[0]
    nt = x_src.shape[1] // TILE + off
    T = nt * TILE
    tile = lambda width: pl.BlockSpec((B, TILE, width), lambda i: (0, i, 0))
    const = lambda shape: pl.BlockSpec(shape, lambda i: (0,) * len(shape))
    per_halo = TILE // HALO
    n_halo = x_src.shape[1] // HALO
    nxt = lambda i: jnp.minimum(i + 1, nt - 1)
    blk = lambda t: jnp.maximum(t - off, 0)
    prev_halo = lambda t: jnp.maximum((t - off) * per_halo - 1, 0)
    next_halo = lambda t: jnp.clip((t - off + 1) * per_halo, 0, n_halo - 1)
    tok_shape = lambda width, dt: jax.ShapeDtypeStruct((B, T, width), dt)
    return pl.pallas_call(
        functools.partial(_proj_kernel, nt),
        out_shape=(tok_shape(2 * NQK_S, BF16), tok_shape(2 * NQK_S, BF16), tok_shape(2 * NQK_S, BF16),
                   tok_shape(BRANCH_W, BF16), tok_shape(BRANCH_W, BF16),
                   jax.ShapeDtypeStruct((B, T // CHUNK, 2 * NQK_S, LANES), F32),
                   tok_shape(NQK_M, BF16), tok_shape(NQK_M, BF16), tok_shape(2 * NQK_M, BF16),
                   tok_shape(BRANCH_W, BF16), tok_shape(BRANCH_W, BF16),
                   jax.ShapeDtypeStruct((B, 2 * PAIRS, T, 2 * CHUNK), F32),
                   tok_shape(MC_WIDTH, F32),
                   jax.ShapeDtypeStruct((B, T // CHUNK, 2 * NHD, LANES), F32),
                   tok_shape(2 * NQK_S, BF16), tok_shape(2 * NQK_S, BF16), tok_shape(2 * NQK_S, BF16),
                   tok_shape(BRANCH_W, BF16), tok_shape(BRANCH_W, BF16)),
        grid=(nt,),
        in_specs=[pl.BlockSpec((1, TILE, D_MODEL), lambda i: (1, blk(i), 0)),
                  pl.BlockSpec((1, HALO, D_MODEL), lambda i: (1, prev_halo(i), 0)),
                  pl.BlockSpec((1, HALO, D_MODEL), lambda i: (1, next_halo(i), 0)),
                  pl.BlockSpec((1, TILE, D_MODEL), lambda i: (0, blk(nxt(i)), 0)),
                  pl.BlockSpec((1, HALO, D_MODEL), lambda i: (0, prev_halo(nxt(i)), 0)),
                  pl.BlockSpec((1, HALO, D_MODEL), lambda i: (0, next_halo(nxt(i)), 0)),
                  pl.BlockSpec((B, TILE, D_MODEL), lambda i: (0, 0, 0)),
                  const((B + 1, 6, D_MODEL)), const((1, D_MODEL)),
                  pl.BlockSpec((1, D_MODEL, W_PROJ), lambda i: (layer, 0, 0), pipeline_mode=pl.Buffered(1)),
                  const((4 * GLA_RANK, 2 * NQK_S)), const((1, 2 * NQK_S)),
                  const((3, 2 * NQK_M)), const((1, 2 * NHD)),
                  pl.BlockSpec((TILE, NQK_S), lambda i: (i, 0)),
                  pl.BlockSpec((TILE, NQK_S), lambda i: (i, 0)),
                  const((1, 2 * NQK_S))],
        out_specs=(tile(2 * NQK_S), tile(2 * NQK_S), tile(2 * NQK_S), tile(BRANCH_W), tile(BRANCH_W),
                   pl.BlockSpec((B, CHUNKS_PER_TILE, 2 * NQK_S, LANES), lambda i: (0, i, 0, 0)),
                   tile(NQK_M), tile(NQK_M), tile(2 * NQK_M), tile(BRANCH_W), tile(BRANCH_W),
                   pl.BlockSpec((B, 2 * PAIRS, TILE, 2 * CHUNK), lambda i: (0, 0, i, 0)),
                   tile(MC_WIDTH),
                   pl.BlockSpec((B, CHUNKS_PER_TILE, 2 * NHD, LANES), lambda i: (0, i, 0, 0)),
                   tile(2 * NQK_S), tile(2 * NQK_S), tile(2 * NQK_S), tile(BRANCH_W), tile(BRANCH_W)),
        scratch_shapes=[pltpu.VMEM((TILE + 2 * HALO, 2 * NQK_M), F32)] * 2
                       + [pltpu.VMEM((TILE, PW_MLR_WIDTH), F32)] * 2
                       + [pltpu.VMEM((TILE, PW_GR_WIDTH), F32)] * 2
                       + [pltpu.VMEM((6, TILE, NQK_S), F32), pltpu.VMEM((2, TILE, TILE), BF16)]
                       + [pltpu.VMEM((TILE, D_MODEL), BF16)] * 2,
        compiler_params=_cparams(1, VMEM_LIMIT_PROJ), name="proj",
    )(x_src, x_src, x_src, x_src, x_src, x_src, ctx_src,
      mods, g1, w, a2s, abias, conv_w, gate_bias, cos_t, sin_t, dl_row)


def _mix_kernel(batch, *refs):
    (gq_f, gq_b, gk_f, gk_b, ge_f, ge_b, gv_f, gv_b, gd_f, gd_b,
     rq_f, rq_b, rk_f, rk_b, re_f, re_b, rv_f, rv_b, rdl,
     mq_f, mq_b, mk_f, mk_b, mw_f, mw_b, mv_f, mv_b, mp_f, mp_b, mcol_f, mcol_b, mc_f, mc_b,
     of_ref, ob_ref, sg_ref, sr_ref, sm_ref, m_ref) = refs
    j = pl.program_id(0)

    @pl.when(j == 0)
    def _():
        sg_ref[...] = jnp.zeros_like(sg_ref)
        sr_ref[...] = jnp.zeros_like(sr_ref)
        sm_ref[...] = jnp.zeros_like(sm_ref)
        m_ref[...] = jnp.full_like(m_ref, M_INIT)

    row_head = lax.broadcasted_iota(jnp.int32, (2 * CHUNK, PAIR_V), 0) // CHUNK
    diag_kv = row_head == lax.broadcasted_iota(jnp.int32, (2 * CHUNK, PAIR_V), 1) // DV
    diag_kk = (lax.broadcasted_iota(jnp.int32, (2 * CHUNK, PAIR_K), 0) // CHUNK
               == lax.broadcasted_iota(jnp.int32, (2 * CHUNK, PAIR_K), 1) // DK_SMALL)
    t_idx = lax.broadcasted_iota(jnp.int32, (CHUNK, 2 * CHUNK), 0)
    s_idx = lax.broadcasted_iota(jnp.int32, (CHUNK, 2 * CHUNK), 1) % CHUNK
    causal = (s_idx <= t_idx, s_idx >= t_idx)
    ones_v = jnp.ones((CHUNK, DV), BF16)
    zeros_vext = jnp.zeros((CHUNK, PAIR_V), BF16)
    ret_dec = jnp.exp(float(CHUNK) * _log_sigmoid(rdl[...]))

    decay_sets = (
        ((gq_f, gk_f, ge_f, gv_f), (gq_b, gk_b, ge_b, gv_b), sg_ref, 0),
        ((rq_f, rk_f, re_f, rv_f), (rq_b, rk_b, re_b, rv_b), sr_ref, 2 * BRANCH_W),
    )
    ml_sets = ((mq_f, mk_f, mw_f, mv_f, mp_f, mcol_f, mc_f),
               (mq_b, mk_b, mw_b, mv_b, mp_b, mcol_b, mc_b))
    outs = (of_ref, ob_ref)


    def decay_phase1(bb, d, c, rows, bi, p, refs):
        q_r, k_r, e_r, v_r = refs
        ks = slice(p * PAIR_K, (p + 1) * PAIR_K)
        q2 = q_r[bb, rows, ks]
        k2 = k_r[bb, rows, ks]
        v2 = v_r[bb, rows, p * PAIR_V:(p + 1) * PAIR_V]
        k_bd = jnp.where(diag_kk, jnp.concatenate([k2, k2], axis=0), 0.0)
        dec = (gd_f if d == 0 else gd_b)[bb, c, ks, :] if bi == 0 else ret_dec[d, ks, :]
        upd = jnp.where(diag_kv, _dot_tn(e_r[bb, rows, ks], v2), 0.0)
        return dict(q2=q2, v2=v2, dec=jnp.concatenate([dec, dec], axis=1), upd=upd,
                    att=_dot_nt(q2, k_bd))

    def decay_phase2(bb, d, p, rows, o_ref, s_ref, base, w):
        s_prev = s_ref[bb, d, p]
        att = jnp.where(causal[d], w["att"], 0.0).astype(BF16)
        v_bd = jnp.where(diag_kv, jnp.concatenate([w["v2"], w["v2"]], axis=0), 0.0)
        lhs = jnp.concatenate([att, w["q2"]], axis=1)
        rhs = jnp.concatenate([v_bd, s_prev.astype(BF16)], axis=0)
        o_ref[bb, rows, base + p * PAIR_V:base + (p + 1) * PAIR_V] = _dot(lhs, rhs)
        s_ref[bb, d, p] = w["dec"] * s_prev + w["upd"]

    def ml_phase1(bb, d, c, rows, p, refs):
        q_r, k_r, w_r, v_r, p_r, col_r, c_r = refs
        cols = col_r[bb, rows, :]
        q2 = q_r[bb, rows, p * 2 * DK_ML:(p + 1) * 2 * DK_ML]
        k2 = k_r[bb, rows, p * 2 * DK_ML:(p + 1) * 2 * DK_ML]
        k_bd = jnp.where(diag_kv, jnp.concatenate([k2, k2], axis=0), 0.0)
        att = _dot_nt(q2, k_bd)
        heads = []
        for hl in range(2):
            hd = 2 * p + hl
            jj = d * HEADS + hd
            vext = jnp.concatenate([v_r[bb, rows, hd * DV:(hd + 1) * DV], ones_v], axis=1)
            st = sm_ref[bb, d, hd]
            heads.append(dict(
                hd=hd, st=st, vext=vext,
                inter=_dot(q2[:, hl * DK_ML:(hl + 1) * DK_ML], st.astype(BF16)),
                upd=_dot_tn(w_r[bb, rows, hd * DK_ML:(hd + 1) * DK_ML], vext),
                u=jnp.broadcast_to(cols[:, MC_U + jj:MC_U + jj + 1], (CHUNK, LANES)),
                b=jnp.broadcast_to(cols[:, MC_B + jj:MC_B + jj + 1], (CHUNK, LANES))))
        return dict(att=att, heads=heads, p0=p_r[bb, p, rows, :], cs=c_r[bb, c])

    def ml_phase2(bb, d, rows, o_ref, w):
        sc = (w["att"] * w["p0"]).astype(BF16)
        for hl, hw in enumerate(w["heads"]):
            hd = hw["hd"]
            jj = d * HEADS + hd
            vz = (jnp.concatenate([hw["vext"], zeros_vext], axis=0) if hl == 0
                  else jnp.concatenate([zeros_vext, hw["vext"]], axis=0))
            intra = _dot(sc, vz)
            m_prev = m_ref[bb, jj:jj + 1, :]
            u = hw["u"]
            mx = jnp.maximum(u, m_prev)
            f = jnp.exp(u - mx)
            e = jnp.exp(m_prev - mx)
            num = f * intra[:, 0:DV] + e * hw["inter"][:, 0:DV]
            den = f * intra[:, DV:] + e * hw["inter"][:, DV:]
            den = jnp.maximum(jnp.abs(den), jnp.exp(-(hw["b"] + mx)))
            o_ref[bb, rows, BRANCH_W + hd * DV:BRANCH_W + (hd + 1) * DV] = num / den
            tot = w["cs"][jj:jj + 1, :]
            mloc = w["cs"][NHD + jj:NHD + jj + 1, :]
            m_new = jnp.maximum(tot + m_prev, mloc)
            keep = jnp.exp(tot + m_prev - m_new)
            gain = jnp.exp(mloc - m_new)
            sm_ref[bb, d, hd] = (jnp.concatenate([keep, keep], axis=1) * hw["st"]
                                 + jnp.concatenate([gain, gain], axis=1) * hw["upd"])
            m_ref[bb, jj:jj + 1, :] = m_new

    units = []
    for step in range(CHUNKS_PER_TILE):
        for bb, d in ((b_, d_) for b_ in range(batch) for d_ in range(2)):
            c = step if d == 0 else CHUNKS_PER_TILE - 1 - step
            rows = pl.ds(c * CHUNK, CHUNK)
            for bi, (fset, bset, s_ref, base) in enumerate(decay_sets):
                for p in range(PAIRS):
                    units.append((functools.partial(decay_phase1, bb, d, c, rows, bi, p, fset if d == 0 else bset),
                                  functools.partial(decay_phase2, bb, d, p, rows, outs[d], s_ref, base)))
            for p in range(PAIRS):
                units.append((functools.partial(ml_phase1, bb, d, c, rows, p, ml_sets[d]),
                              functools.partial(ml_phase2, bb, d, rows, outs[d])))

    assert MIX_LOOKAHEAD < len(units) // CHUNKS_PER_TILE
    pending = []
    for phase1, phase2 in units:
        pending.append((phase2, phase1()))
        if len(pending) > MIX_LOOKAHEAD:
            fn, w = pending.pop(0)
            fn(w)
    for fn, w in pending:
        fn(w)


def _mix(gla, ret, ml, ret_dl_col):
    gq, gk, ge, gv, gdec = gla
    rq, rk, re, rv = ret
    mq, mk, mw, mv, mp, mcol, mc = ml
    B, T, _ = gv.shape
    nt = T // TILE

    def tf(j):
        return j

    def tb(j):
        return jnp.where(j == 0, 0, nt - j)

    def pair(width, lane_block_b=0):
        return [pl.BlockSpec((B, TILE, width), lambda j: (0, tf(j), 0)),
                pl.BlockSpec((B, TILE, width), lambda j: (0, tb(j), lane_block_b))]

    dec_pair = [pl.BlockSpec((B, CHUNKS_PER_TILE, NQK_S, LANES), lambda j: (0, tf(j), 0, 0)),
                pl.BlockSpec((B, CHUNKS_PER_TILE, NQK_S, LANES), lambda j: (0, tb(j), 1, 0))]
    p0_pair = [pl.BlockSpec((B, PAIRS, TILE, 2 * CHUNK), lambda j: (0, 0, tf(j), 0)),
               pl.BlockSpec((B, PAIRS, TILE, 2 * CHUNK), lambda j: (0, 1, tb(j), 0))]
    cs_pair = [pl.BlockSpec((B, CHUNKS_PER_TILE, 2 * NHD, LANES), lambda j: (0, tf(j), 0, 0)),
               pl.BlockSpec((B, CHUNKS_PER_TILE, 2 * NHD, LANES), lambda j: (0, tb(j), 0, 0))]
    decay_specs = pair(NQK_S, 1) + pair(NQK_S, 1) + pair(NQK_S, 1) + pair(BRANCH_W)
    in_specs = (decay_specs + dec_pair + decay_specs
                + [pl.BlockSpec((2, NQK_S, LANES), lambda j: (0, 0, 0))]
                + pair(NQK_M) + pair(NQK_M) + pair(NQK_M, 1) + pair(BRANCH_W) + p0_pair
                + pair(MC_WIDTH) + cs_pair)
    args = (gq, gq, gk, gk, ge, ge, gv, gv, gdec, gdec,
            rq, rq, rk, rk, re, re, rv, rv, ret_dl_col,
            mq, mq, mk, mk, mw, mw, mv, mv, mp, mp, mcol, mcol, mc, mc)
    return pl.pallas_call(
        functools.partial(_mix_kernel, B),
        out_shape=(jax.ShapeDtypeStruct((B, T, 3 * BRANCH_W), F32),) * 2,
        grid=(nt,),
        in_specs=in_specs,
        out_specs=(pl.BlockSpec((B, TILE, 3 * BRANCH_W), lambda j: (0, tf(j), 0)),
                   pl.BlockSpec((B, TILE, 3 * BRANCH_W), lambda j: (0, tb(j), 0))),
        scratch_shapes=[pltpu.VMEM((B, 2, PAIRS, PAIR_K, PAIR_V), F32),
                        pltpu.VMEM((B, 2, PAIRS, PAIR_K, PAIR_V), F32),
                        pltpu.VMEM((B, 2, HEADS, DK_ML, 2 * DV), F32),
                        pltpu.VMEM((B, NHD, LANES), F32)],
        compiler_params=_cparams(1), name="mix",
    )(*args)


def _merge_kernel(batch, first_tile, x_ref, ctx_ref, mod_ref, g_ref, wg_ref, of_ref, ob_ref,
                  sgg_ref, sgm_ref, sgr_ref, ng_ref, wb_ref, wo_ref, out_ref):
    tile_idx = pl.program_id(0) + first_tile
    side = (sgg_ref, sgm_ref, sgr_ref)
    mods = [mod_ref[jnp.where(tile_idx == 0, batch, bb)] for bb in range(batch)]
    xs = [jnp.where(tile_idx == 0, ctx_ref[bb], x_ref[bb]) for bb in range(batch)]
    hs = [_norm_mod(xs[bb], g_ref[...], mods[bb][0:1, :], mods[bb][1:2, :]).astype(BF16)
          for bb in range(batch)]
    accs = [jnp.zeros((TILE, D_MODEL), F32) for _ in range(batch)]
    for br in range(3):
        for bb in range(batch):
            parts = []
            for hd in range(HEADS):
                cols = slice(br * BRANCH_W + hd * DV, br * BRANCH_W + (hd + 1) * DV)
                oh = of_ref[bb, :, cols] + ob_ref[bb, :, cols]
                if br > 0:
                    oh = oh - jnp.mean(oh, axis=-1, keepdims=True)
                parts.append(oh * lax.rsqrt(jnp.mean(oh * oh, axis=-1, keepdims=True) + EPS))
            y = jnp.concatenate(parts, axis=1) * ng_ref[:, br * BRANCH_W:(br + 1) * BRANCH_W]
            y = (y * side[br][bb]).astype(BF16)
            gate = _sigmoid(_dot(hs[bb], wg_ref[0, :, br * D_MODEL:(br + 1) * D_MODEL]))
            accs[bb] = accs[bb] + gate * _dot(y, wb_ref[0, br])
    for bb in range(batch):
        out = _dot(accs[bb].astype(BF16), wo_ref[0])
        out_ref[bb] = xs[bb] + mods[bb][2:3, :] * out


def _merge(ctx_src, x_src, off, mods, g1, wgate, o_f, o_b, sg_gla, sz_ml, sg_ret, norm_g, wb, wo, layer, skip):
    B = x_src.shape[0]
    nt = x_src.shape[1] // TILE + off - skip
    tile = lambda width: pl.BlockSpec((B, TILE, width), lambda i: (0, i + skip, 0))
    const = lambda shape: pl.BlockSpec(shape, lambda i: (0,) * len(shape))
    per_layer = lambda shape: pl.BlockSpec((1,) + shape, lambda i: (layer,) + (0,) * len(shape),
                                           pipeline_mode=pl.Buffered(1))
    return pl.pallas_call(
        functools.partial(_merge_kernel, B, skip),
        out_shape=jax.ShapeDtypeStruct((B, nt * TILE, D_MODEL), F32),
        grid=(nt,),
        in_specs=[pl.BlockSpec((B, TILE, D_MODEL), lambda i: (0, jnp.maximum(i + skip - off, 0), 0)),
                  pl.BlockSpec((B, TILE, D_MODEL), lambda i: (0, 0, 0)),
                  const((B + 1, 6, D_MODEL)),
                  const((1, D_MODEL)), per_layer((D_MODEL, W_GATE)),
                  tile(3 * BRANCH_W), tile(3 * BRANCH_W),
                  tile(BRANCH_W), tile(BRANCH_W), tile(BRANCH_W),
                  const((1, 3 * BRANCH_W)), per_layer((3, BRANCH_W, D_MODEL)),
                  per_layer((D_MODEL, D_MODEL))],
        out_specs=pl.BlockSpec((B, TILE, D_MODEL), lambda i: (0, i, 0)),
        compiler_params=_cparams(1), name="merge",
    )(x_src, ctx_src, mods, g1, wgate, o_f, o_b, sg_gla, sz_ml, sg_ret, norm_g, wb, wo)


def _mlp_kernel(batch, first_tile, final, x_ref, mod_ref, g_ref, wu_ref, wd_ref, fg_ref, out_ref):
    tile_idx = pl.program_id(0) + first_tile
    mods = [mod_ref[jnp.where(tile_idx == 0, batch, bb)] for bb in range(batch)]
    ups = []
    for bb in range(batch):
        h = _norm_mod(x_ref[bb], g_ref[...], mods[bb][3:4, :], mods[bb][4:5, :]).astype(BF16)
        ups.append(_dot(h, wu_ref[0]))
    for bb in range(batch):
        u = jnp.maximum(ups[bb], 0.0)
        y = x_ref[bb] + mods[bb][5:6, :] * _dot((u * u).astype(BF16), wd_ref[0])
        if final:
            y = y * lax.rsqrt(jnp.mean(y * y, axis=-1, keepdims=True) + EPS) * fg_ref[...]
        out_ref[bb] = y


def _mlp(tok, mods, g2, wu, wd, layer, final_g, first_tile, final):
    B, T, _ = tok.shape
    const = lambda shape: pl.BlockSpec(shape, lambda i: (0,) * len(shape))
    per_layer = lambda shape: pl.BlockSpec((1,) + shape, lambda i: (layer,) + (0,) * len(shape),
                                           pipeline_mode=pl.Buffered(1))
    return pl.pallas_call(
        functools.partial(_mlp_kernel, B, first_tile, final),
        out_shape=jax.ShapeDtypeStruct((B, T, D_MODEL), F32),
        grid=(T // TILE,),
        in_specs=[pl.BlockSpec((B, TILE, D_MODEL), lambda i: (0, i, 0)),
                  const((B + 1, 6, D_MODEL)),
                  const((1, D_MODEL)), per_layer((D_MODEL, D_FF)), per_layer((D_FF, D_MODEL)),
                  const((1, D_MODEL))],
        out_specs=pl.BlockSpec((B, TILE, D_MODEL), lambda i: (0, i, 0)),
        compiler_params=_cparams(1), name="mlp",
    )(tok, mods, g2, wu, wd, final_g)


def kernel(x, c, ctx, c_ctx, norm1_g, norm2_g, w_ada, b_ada, w_in, gla_a2, gla_a_bias, gla_norm_g,
           ml_conv, ml_i_bias, ml_f_bias, ml_norm_g, ret_decay_logit, ret_norm_g, w_branch, w_o,
           w_up, w_down, final_g):
    B, seq, _ = x.shape
    ctx_len = ctx.shape[1]
    assert B == 2 and ctx_len == TILE and seq % TILE == 0 and seq % GRID_W == 0
    ctx_src, x_src, off = ctx, x, 1

    c_rows = jnp.concatenate([c, c_ctx[None, :], jnp.zeros((8 - B - 1, D_MODEL), F32)], axis=0)
    mods_all = _ada(c_rows, w_ada, b_ada)[:, 0:3].reshape(DEPTH, 3, 6, D_MODEL)

    cos_t, sin_t = _rope_tables(seq, ctx_len)

    w_proj, w_gate = _wprep(w_in)
    w_branch_b = w_branch.astype(BF16)
    w_o_b = w_o.astype(BF16)
    w_up_b = w_up.astype(BF16)
    w_down_b = w_down.astype(BF16)
    for l in range(DEPTH):
        last = l == DEPTH - 1
        skip = 1 if last else 0
        mods = mods_all[l]
        g1 = norm1_g[l][None, :]
        g2 = norm2_g[l][None, :]
        zeros = jnp.zeros((GLA_RANK, NQK_S), F32)
        a2blk = jnp.concatenate([jnp.concatenate([gla_a2[l, 0], zeros], axis=1),
                                 jnp.concatenate([zeros, gla_a2[l, 1]], axis=1)], axis=0)
        a2s = jnp.concatenate([a2blk, a2blk], axis=0).astype(BF16)
        abias = gla_a_bias[l].reshape(1, 2 * NQK_S)
        gate_bias = jnp.concatenate([ml_i_bias[l].reshape(1, NHD), ml_f_bias[l].reshape(1, NHD)], axis=1)
        dl = ret_decay_logit[l]
        dl_row = jnp.repeat(dl, DK_SMALL, axis=1).reshape(1, 2 * NQK_S)
        dl_col = jnp.broadcast_to(jnp.repeat(dl, DK_SMALL, axis=1)[:, :, None], (2, NQK_S, LANES))

        (gq, gk, ge, gv, sg_gla, gdec, mq, mk, mw, mv, sz_ml, mp, mcol, mc,
         rq, rk, re, rv, sg_ret) = _proj(ctx_src, x_src, off, mods, g1, w_proj, l, a2s, abias, ml_conv[l],
                                         gate_bias, cos_t, sin_t, dl_row)
        o_f, o_b = _mix((gq, gk, ge, gv, gdec), (rq, rk, re, rv), (mq, mk, mw, mv, mp, mcol, mc), dl_col)
        norm_g = jnp.concatenate([gla_norm_g[l], ml_norm_g[l], ret_norm_g[l]])[None, :]
        tok = _merge(ctx_src, x_src, off, mods, g1, w_gate, o_f, o_b, sg_gla, sz_ml, sg_ret, norm_g,
                     w_branch_b, w_o_b, l, skip)
        tok = _mlp(tok, mods, g2, w_up_b, w_down_b, l, final_g[None, :], skip, last)
        ctx_src, x_src, off = tok, tok, 0
    return tok
```

```python
import functools

import jax
import jax.numpy as jnp
from jax import lax
from jax.experimental import pallas as pl
from jax.experimental.pallas import tpu as pltpu

F32 = jnp.float32
BF16 = jnp.bfloat16

D_MODEL = 1024
DEPTH = 2
GRID_W = 64
CHUNK = 64
EPS = 1e-6
M_INIT = -1e30
HEADS = 4
DK_SMALL = 64
DK_ML = 128
DV = 128
BRANCH_W = HEADS * DV
GLA_RANK = 16
GLA_TAU = 16.0
ROPE_BASE = 10000.0
D_FF = 4 * D_MODEL
LANES = 128

TILE = 256
CHUNKS_PER_TILE = TILE // CHUNK
HALO = 8
VMEM_LIMIT = 56 * 1024 * 1024
VMEM_LIMIT_PROJ = 60 * 1024 * 1024

NQK_S = HEADS * DK_SMALL
NQK_M = HEADS * DK_ML
NHD = 2 * HEADS
W_GLA = 2 * NQK_S + 2 * BRANCH_W + 2 * GLA_RANK
W_ML = 2 * NQK_M + 2 * BRANCH_W + 2 * NHD
W_RET = 2 * NQK_S + 2 * BRANCH_W
W_GATE = 3 * D_MODEL

PW_MLQK = 0
PW_MLR = PW_MLQK + 2 * NQK_M
PW_MLR_WIDTH = 2 * BRANCH_W
PW_GR = PW_MLR + PW_MLR_WIDTH
PW_GR_WIDTH = 2 * (2 * NQK_S + 2 * BRANCH_W)
W_PROJ = PW_GR + PW_GR_WIDTH
MLR_V, MLR_Z = 0, BRANCH_W
GR_GLA_Q, GR_GLA_K, GR_GLA_V, GR_GLA_G = 0, NQK_S, 2 * NQK_S, 2 * NQK_S + BRANCH_W
GR_RET_Q = 2 * NQK_S + 2 * BRANCH_W
GR_RET_K, GR_RET_V, GR_RET_G = GR_RET_Q + NQK_S, GR_RET_Q + 2 * NQK_S, GR_RET_Q + 2 * NQK_S + BRANCH_W
W_SMALL = LANES
SM_GATES, SM_GLA_A = 0, 2 * NHD
WPREP_BLOCK = 256

MC_B, MC_W, MC_U = 0, NHD, 2 * NHD
MC_WIDTH = 3 * NHD

PAIRS = HEADS // 2
PAIR_K = 2 * DK_SMALL
PAIR_V = 2 * DV
MIX_LOOKAHEAD = 23


def _cparams(n_axes, vmem_limit=VMEM_LIMIT):
    return pltpu.CompilerParams(dimension_semantics=("arbitrary",) * n_axes,
                                vmem_limit_bytes=vmem_limit)


def _log_sigmoid(z):
    return jnp.minimum(z, 0.0) - jnp.log(1.0 + jnp.exp(-jnp.abs(z)))


def _sigmoid(z):
    return 1.0 / (1.0 + jnp.exp(-z))


def _silu(z):
    return z * _sigmoid(z)


def _norm_mod(x, g, shift, scale):
    y = x * lax.rsqrt(jnp.mean(x * x, axis=-1, keepdims=True) + EPS) * g
    return y * (1.0 + scale) + shift


def _dot(a, b):
    return jnp.dot(a, b, preferred_element_type=F32)


def _dot_nt(a, b):
    return lax.dot_general(a, b, (((1,), (1,)), ((), ())), preferred_element_type=F32)


def _dot_tn(a, b):
    return lax.dot_general(a, b, (((0,), (0,)), ((), ())), preferred_element_type=F32)


def _split2(x):
    hi = x.astype(BF16)
    return hi, (x - hi.astype(F32)).astype(BF16)


def _chunk_tri(lower):
    r = lax.broadcasted_iota(jnp.int32, (TILE, TILE), 0)
    c = lax.broadcasted_iota(jnp.int32, (TILE, TILE), 1)
    same = (r // CHUNK) == (c // CHUNK)
    tri = (c <= r) if lower else (c >= r)
    return jnp.where(same & tri, 1.0, 0.0).astype(BF16)


def _chunk_total(x):
    w = x.shape[-1]
    tot = jnp.sum(x.reshape(CHUNKS_PER_TILE, CHUNK, w), axis=1, keepdims=True)
    return jnp.broadcast_to(tot, (CHUNKS_PER_TILE, CHUNK, w)).reshape(TILE, w)


def _ada_kernel(c_ref, w_ref, b_ref, o_ref):
    s = _silu(c_ref[...])
    s_hi = s.astype(BF16).astype(F32)
    s2 = jnp.concatenate([s_hi, s - s_hi], axis=0).astype(BF16)
    w_hi, w_lo = _split2(w_ref[0])
    r = _dot(s2, w_hi) + _dot(s2, w_lo)
    o_ref[0] = r[0:8] + r[8:16] + b_ref[0]


def _ada(c_rows, w_ada, b_ada):
    tn = 1536
    return pl.pallas_call(
        _ada_kernel,
        out_shape=jax.ShapeDtypeStruct((DEPTH, 8, 6 * D_MODEL), F32),
        grid=(DEPTH, 6 * D_MODEL // tn),
        in_specs=[pl.BlockSpec((8, D_MODEL), lambda l, n: (0, 0)),
                  pl.BlockSpec((1, D_MODEL, tn), lambda l, n: (l, 0, n)),
                  pl.BlockSpec((1, 1, tn), lambda l, n: (l, 0, n))],
        out_specs=pl.BlockSpec((1, 8, tn), lambda l, n: (l, 0, n)),
        compiler_params=_cparams(2), name="ada",
    )(c_rows, w_ada, b_ada.reshape(DEPTH, 1, 6 * D_MODEL))


def _rope_kernel(ang_ref, cos_ref, sin_ref, nsin_ref):
    a = ang_ref[...]
    s = jnp.sin(a)
    cos_ref[...] = jnp.cos(a)
    sin_ref[...] = s
    nsin_ref[...] = -s


def _rope_tables(seq, ctx_len):
    rows = seq // GRID_W
    nf = DK_SMALL // 4
    inv = ROPE_BASE ** (-jnp.arange(nf, dtype=F32) / nf)
    ang = jnp.concatenate([jnp.arange(rows, dtype=F32)[:, None] * inv,
                           jnp.arange(GRID_W, dtype=F32)[:, None] * inv], axis=0)
    n = rows + GRID_W
    cos, sin, nsin = pl.pallas_call(
        _rope_kernel,
        out_shape=(jax.ShapeDtypeStruct((n, nf), F32),) * 3,
        name="rope_tables",
    )(ang)

    def per_token(first, second):
        def half(t):
            r = jnp.broadcast_to(t[:rows, None, :], (rows, GRID_W, nf))
            c = jnp.broadcast_to(t[None, rows:, :], (rows, GRID_W, nf))
            return jnp.concatenate([r, c], axis=-1)
        head = jnp.concatenate([half(first), half(second)], axis=-1).reshape(seq, DK_SMALL)
        return jnp.tile(head, (1, HEADS))

    ones = jnp.ones((ctx_len, NQK_S), F32)
    cos_t = jnp.concatenate([ones, per_token(cos, cos)], axis=0)
    sin_t = jnp.concatenate([0.0 * ones, per_token(nsin, sin)], axis=0)
    return cos_t, sin_t


def _wprep_kernel(n_proj_blocks, src_ref, wt_ref, eye_ref, wp_ref, wg_ref):
    del src_ref
    c = pl.program_id(1)
    cols = _dot_nt(eye_ref[...], wt_ref[0].astype(BF16)).astype(BF16)

    @pl.when(c < n_proj_blocks)
    def _():
        wp_ref[0] = cols

    @pl.when(c >= n_proj_blocks)
    def _():
        wg_ref[0] = cols


def _wprep(w_in):
    o0, o1, o2 = W_GLA, W_GLA + W_ML, W_GLA + W_ML + W_RET
    n_sg = 2 * NQK_S + 2 * BRANCH_W
    segments = [(o0, 2 * NQK_M),
                (o0 + 2 * NQK_M, 2 * BRANCH_W),
                (0, n_sg),
                (o1, W_RET),
                (o2, W_GATE)]
    starts = [s + k for s, n in segments for k in range(0, n, WPREP_BLOCK)]
    n_proj, n_gate = W_PROJ // WPREP_BLOCK, W_GATE // WPREP_BLOCK
    assert len(starts) == n_proj + n_gate
    w_t = jnp.swapaxes(w_in, 1, 2)
    eye = jnp.eye(D_MODEL, dtype=BF16)
    return pl.pallas_call(
        functools.partial(_wprep_kernel, n_proj),
        out_shape=(jax.ShapeDtypeStruct((DEPTH, D_MODEL, W_PROJ), BF16),
                   jax.ShapeDtypeStruct((DEPTH, D_MODEL, W_GATE), BF16)),
        grid_spec=pltpu.PrefetchScalarGridSpec(
            num_scalar_prefetch=1,
            grid=(DEPTH, n_proj + n_gate),
            in_specs=[pl.BlockSpec((pl.Element(1), pl.Element(WPREP_BLOCK), pl.Element(D_MODEL)),
                                   lambda l, c, src: (l, src[c] * 8, 0)),
                      pl.BlockSpec((D_MODEL, D_MODEL), lambda l, c, src: (0, 0))],
            out_specs=(pl.BlockSpec((1, D_MODEL, WPREP_BLOCK),
                                    lambda l, c, src: (l, 0, jnp.minimum(c, n_proj - 1))),
                       pl.BlockSpec((1, D_MODEL, WPREP_BLOCK),
                                    lambda l, c, src: (l, 0, jnp.maximum(c - n_proj, 0))))),
        compiler_params=_cparams(2), name="wprep",
    )(jnp.asarray([s // 8 for s in starts], jnp.int32), w_t, eye)


def _proj_kernel(nt, xa_ref, xap_ref, xan_ref, xb_ref, xbp_ref, xbn_ref, ctx_ref,
                 mod_ref, g_ref, w_ref, ws_ref, a2_ref, ab_ref, cw_ref, gb_ref, cos_ref, sin_ref, dl_ref,
                 gq_ref, gk_ref, ge_ref, gv_ref, gsg_ref, gdec_ref,
                 mq_ref, mk_ref, mkw_ref, mv_ref, msz_ref, mp0_ref, mcol_ref, mcs_ref,
                 rq_ref, rk_ref, re_ref, rv_ref, rsg_ref,
                 ext0_ref, ext1_ref, prml0_ref, prml1_ref, prgr0_ref, prgr1_ref, rdec_ref, tri_ref,
                 h0_ref, h1_ref, small0_ref, small1_ref):
    i = pl.program_id(0)
    ext_refs = (ext0_ref, ext1_ref)
    prml_refs = (prml0_ref, prml1_ref)
    prgr_refs = (prgr0_ref, prgr1_ref)
    h_refs = (h0_ref, h1_ref)
    small_refs = (small0_ref, small1_ref)
    row8 = lax.broadcasted_iota(jnp.int32, (NHD, TILE), 0)
    lane8 = lax.broadcasted_iota(jnp.int32, (NHD, TILE), 1)
    lane_chunk = lane8 // CHUNK
    lane_in_chunk = lane8 % CHUNK
    ti = lax.broadcasted_iota(jnp.int32, (CHUNK, CHUNK), 0)
    si = lax.broadcasted_iota(jnp.int32, (CHUNK, CHUNK), 1)
    ext_row = lax.broadcasted_iota(jnp.int32, (TILE + 2 * HALO, 1), 0)

    def project_stages(x, x_prev, x_next, tile_idx, mod_row, slot):
        st = {}

        def mlstm_qk():
            mod = mod_ref[mod_row]
            x_all = jnp.concatenate([x_prev, x, x_next], axis=0)
            h_all = _norm_mod(x_all, g_ref[...], mod[0:1, :], mod[1:2, :])
            h_refs[slot][...] = h_all[HALO:HALO + TILE, :].astype(BF16)
            prev_ok = tile_idx >= 2
            next_ok = jnp.logical_and(tile_idx >= 1, tile_idx <= nt - 2)
            dead = jnp.logical_or(jnp.logical_and(ext_row < HALO, jnp.logical_not(prev_ok)),
                                  jnp.logical_and(ext_row >= TILE + HALO, jnp.logical_not(next_ok)))
            ext = _dot(h_all.astype(BF16), w_ref[0, :, PW_MLQK:PW_MLR])
            ext_refs[slot][...] = jnp.where(dead, 0.0, ext)

        def mlstm_rest():
            prml_refs[slot][...] = _dot(h_refs[slot][...], w_ref[0, :, PW_MLR:PW_GR])
            small_refs[slot][...] = _dot(h_refs[slot][...], ws_ref[0])

        def gla():
            prgr_refs[slot][:, 0:GR_RET_Q] = _dot(h_refs[slot][...], w_ref[0, :, PW_GR:PW_GR + GR_RET_Q])

        def ret():
            prgr_refs[slot][:, GR_RET_Q:] = _dot(h_refs[slot][...], w_ref[0, :, PW_GR + GR_RET_Q:])

        return [mlstm_qk, mlstm_rest, gla, ret]

    def project(*args):
        for stage in project_stages(*args):
            stage()

    @pl.when(i == 0)
    def _():
        tri_ref[0] = _chunk_tri(True)
        tri_ref[1] = _chunk_tri(False)
        lg = _log_sigmoid(dl_ref[...])
        t_in_chunk = (lax.broadcasted_iota(jnp.int32, (TILE, NQK_S), 0) % CHUNK).astype(F32)
        for d in range(2):
            lg_d = lg[:, d * NQK_S:(d + 1) * NQK_S]
            steps = (t_in_chunk + 1.0) if d == 0 else (float(CHUNK) - t_in_chunk)
            b = steps * lg_d
            rdec_ref[3 * d] = jnp.exp(b)
            rdec_ref[3 * d + 1] = jnp.exp(-b)
            rdec_ref[3 * d + 2] = jnp.exp(float(CHUNK) * lg_d - b)
        halo0 = jnp.zeros((HALO, D_MODEL), F32)
        project(ctx_ref[0], halo0, halo0, 0, 2, 0)

    def rope(x):
        lane = lax.broadcasted_iota(jnp.int32, (TILE, NQK_S), 1)
        first_half = (lane % DK_SMALL) < (DK_SMALL // 2)
        swapped = jnp.where(first_half, pltpu.roll(x, NQK_S - DK_SMALL // 2, 1),
                            pltpu.roll(x, DK_SMALL // 2, 1))
        return x * cos_ref[...] + swapped * sin_ref[...]

    def seg_cummax(x, reverse):
        sh = 1
        while sh < CHUNK:
            if reverse:
                y, ok = pltpu.roll(x, TILE - sh, 1), lane_in_chunk < CHUNK - sh
            else:
                y, ok = pltpu.roll(x, sh, 1), lane_in_chunk >= sh
            x = jnp.where(ok, jnp.maximum(x, y), x)
            sh *= 2
        return x

    def prepare_stages(slot, bb):
        ext_ref, prml_ref, prgr_ref = ext_refs[slot], prml_refs[slot], prgr_refs[slot]
        small_ref = small_refs[slot]
        st = {}

        def conv():
            qk = _silu(cw_ref[0:1, :] * ext_ref[HALO - 1:HALO - 1 + TILE, :]
                       + cw_ref[1:2, :] * ext_ref[HALO:HALO + TILE, :]
                       + cw_ref[2:3, :] * ext_ref[HALO + 1:HALO + 1 + TILE, :])
            mq_ref[bb] = qk[:, 0:NQK_M].astype(BF16)
            st["k_ml"] = qk[:, NQK_M:] * (DK_ML ** -0.5)
            mk_ref[bb] = st["k_ml"].astype(BF16)

        def gates_in():
            mv_ref[bb] = prml_ref[:,MLR_V:MLR_V + BRANCH_W].astype(BF16)
            msz_ref[bb] = _silu(prml_ref[:,MLR_Z:MLR_Z + BRANCH_W]).astype(BF16)
            gates_t = (small_ref[:, SM_GATES:SM_GATES + 2 * NHD] + gb_ref[...]).T
            st["li_t"] = gates_t[0:NHD]
            st["lf_t"] = _log_sigmoid(gates_t[NHD:])
            lf_hi, lf_lo = _split2(st["lf_t"])
            st["lf2"] = jnp.concatenate([lf_hi, lf_lo], axis=0)

        def gates_mm():
            st["cum_f"] = _dot(st["lf2"], tri_ref[1])
            st["cum_b"] = _dot(st["lf2"], tri_ref[0])

        def gates_out():
            mlstm_gates_body(st["li_t"], st["lf_t"], st["cum_f"], st["cum_b"], st["k_ml"], bb)

        def gla_in():
            gv_ref[bb] = prgr_ref[:,GR_GLA_V:GR_GLA_V + BRANCH_W].astype(BF16)
            gsg_ref[bb] = _silu(prgr_ref[:,GR_GLA_G:GR_GLA_G + BRANCH_W]).astype(BF16)
            a_hi, a_lo = _split2(small_ref[:, SM_GLA_A:SM_GLA_A + 2 * GLA_RANK])
            st["a2"] = jnp.concatenate([a_hi, a_lo], axis=1)

        def gla_z_mm():
            st["z"] = _dot(st["a2"], a2_ref[...])

        def gla_la():
            la = _log_sigmoid(st["z"] + ab_ref[...]) * (1.0 / GLA_TAU)
            st["la"] = la
            st["la_split"] = [_split2(la[:, d * NQK_S:(d + 1) * NQK_S]) for d in range(2)]

        def gla_cum_mm():
            st["b"] = []
            for d in range(2):
                tri = tri_ref[d]
                la_hi, la_lo = st["la_split"][d]
                st["b"].append(_dot(tri, la_hi) + _dot(tri, la_lo))

        def gla_out():
            gla_body(prgr_ref, st["la"], st["b"], bb)

        def ret():
            ret_body(prgr_ref, bb)

        return dict(conv=conv, gates_in=gates_in, gates_mm=gates_mm, gates_out=gates_out,
                    gla_in=gla_in, gla_z_mm=gla_z_mm, gla_la=gla_la, gla_cum_mm=gla_cum_mm,
                    gla_out=gla_out, ret=ret)

    def mlstm_gates_body(li_t, lf_t, cum_f, cum_b, k_ml, bb):
        b_t = jnp.where(row8 < HEADS, cum_f[0:NHD] + cum_f[NHD:], cum_b[0:NHD] + cum_b[NHD:])
        tot_t = jnp.zeros((NHD, TILE), F32)
        mloc_t = jnp.zeros((NHD, TILE), F32)
        for c in range(CHUNKS_PER_TILE):
            tot_c = jnp.sum(lf_t[:, c * CHUNK:(c + 1) * CHUNK], axis=-1, keepdims=True)
            tot_t = jnp.where(lane_chunk == c, tot_c, tot_t)
            mcs_ref[bb, c, 0:NHD, :] = jnp.broadcast_to(tot_c, (NHD, LANES))
        g_t = tot_t - b_t + li_t
        for c in range(CHUNKS_PER_TILE):
            mloc_c = jnp.max(g_t[:, c * CHUNK:(c + 1) * CHUNK], axis=-1, keepdims=True)
            mloc_t = jnp.where(lane_chunk == c, mloc_c, mloc_t)
            mcs_ref[bb, c, NHD:2 * NHD, :] = jnp.broadcast_to(mloc_c, (NHD, LANES))
        w_t = jnp.exp(g_t - mloc_t)
        r_t = li_t - b_t
        u_t = jnp.where(row8 < HEADS, seg_cummax(r_t, False), seg_cummax(r_t, True))
        cols = jnp.concatenate([b_t, w_t, u_t], axis=0).T
        mcol_ref[bb] = cols
        for j in range(NHD):
            d, hd = j // HEADS, j % HEADS
            mask = (si <= ti) if d == 0 else (si >= ti)
            mkw_ref[bb, :, d * NQK_M + hd * DK_ML:d * NQK_M + (hd + 1) * DK_ML] = (
                k_ml[:, hd * DK_ML:(hd + 1) * DK_ML] * cols[:, MC_W + j:MC_W + j + 1]).astype(BF16)
            for c in range(CHUNKS_PER_TILE):
                rows = slice(c * CHUNK, (c + 1) * CHUNK)
                a = r_t[j:j + 1, rows] - cols[rows, MC_U + j:MC_U + j + 1]
                mp0_ref[bb, j // 2, rows, (j % 2) * CHUNK:(j % 2 + 1) * CHUNK] = jnp.where(mask, jnp.exp(a), 0.0)

    def gla_body(prgr_ref, la, cums, bb):
        half_t = jnp.sum(la.reshape(2 * CHUNKS_PER_TILE, CHUNK // 2, 2 * NQK_S), axis=1).T
        q_gla = prgr_ref[:,GR_GLA_Q:GR_GLA_Q + NQK_S] * (DK_SMALL ** -0.5)
        k_gla = prgr_ref[:,GR_GLA_K:GR_GLA_K + NQK_S]
        for d in range(2):
            la_d = la[:, d * NQK_S:(d + 1) * NQK_S]
            b = cums[d]
            tot = _chunk_total(la_d)
            gq_ref[bb, :, d * NQK_S:(d + 1) * NQK_S] = (q_gla * jnp.exp(b)).astype(BF16)
            gk_ref[bb, :, d * NQK_S:(d + 1) * NQK_S] = (k_gla * jnp.exp(-b)).astype(BF16)
            ge_ref[bb, :, d * NQK_S:(d + 1) * NQK_S] = (k_gla * jnp.exp(tot - b)).astype(BF16)
            ht = half_t[d * NQK_S:(d + 1) * NQK_S, :]
            for c in range(CHUNKS_PER_TILE):
                tot_col = ht[:, 2 * c:2 * c + 1] + ht[:, 2 * c + 1:2 * c + 2]
                gdec_ref[bb, c, d * NQK_S:(d + 1) * NQK_S, :] = jnp.broadcast_to(jnp.exp(tot_col), (NQK_S, LANES))

    def ret_body(prgr_ref, bb):
        q_ret = rope(prgr_ref[:,GR_RET_Q:GR_RET_Q + NQK_S])
        k_ret = rope(prgr_ref[:,GR_RET_K:GR_RET_K + NQK_S] * (DK_SMALL ** -0.5))
        rv_ref[bb] = prgr_ref[:,GR_RET_V:GR_RET_V + BRANCH_W].astype(BF16)
        rsg_ref[bb] = _silu(prgr_ref[:,GR_RET_G:GR_RET_G + BRANCH_W]).astype(BF16)
        for d in range(2):
            rq_ref[bb, :, d * NQK_S:(d + 1) * NQK_S] = (q_ret * rdec_ref[3 * d]).astype(BF16)
            rk_ref[bb, :, d * NQK_S:(d + 1) * NQK_S] = (k_ret * rdec_ref[3 * d + 1]).astype(BF16)
            re_ref[bb, :, d * NQK_S:(d + 1) * NQK_S] = (k_ret * rdec_ref[3 * d + 2]).astype(BF16)

    next_tile = jnp.minimum(i + 1, nt - 1)
    xa = jnp.where(i == 0, ctx_ref[1], xa_ref[0])
    a_ext, a_mlr, a_gla, a_ret = project_stages(xa, xap_ref[0], xan_ref[0], i, jnp.where(i == 0, 2, 1), 1)
    b_ext, b_mlr, b_gla, b_ret = project_stages(xb_ref[0], xbp_ref[0], xbn_ref[0], next_tile, 0, 0)
    x = prepare_stages(0, 0)
    y = prepare_stages(1, 1)
    order = [a_ext, x["conv"], x["gates_in"],
             a_mlr, x["gates_mm"], x["gates_out"], x["gla_in"],
             a_gla, x["gla_z_mm"], x["gla_la"],
             a_ret, x["gla_cum_mm"],
             b_ext, x["gla_out"], x["ret"], y["conv"], y["gates_in"],
             b_mlr, y["gates_mm"], y["gates_out"], y["gla_in"],
             b_gla, y["gla_z_mm"], y["gla_la"],
             b_ret, y["gla_cum_mm"], y["gla_out"], y["ret"]]
    for stage in order:
        stage()


def _proj(ctx_src, x_src, off, mods, g1, w, w_small, layer, a2s, abias, conv_w, gate_bias, cos_t, sin_t,
          dl_row):
    B = x_src.shape[0]
    nt = x_src.shape[1] // TILE + off
    T = nt * TILE
    tile = lambda width: pl.BlockSpec((B, TILE, width), lambda i: (0, i, 0))
    const = lambda shape: pl.BlockSpec(shape, lambda i: (0,) * len(shape))
    per_halo = TILE // HALO
    n_halo = x_src.shape[1] // HALO
    nxt = lambda i: jnp.minimum(i + 1, nt - 1)
    blk = lambda t: jnp.maximum(t - off, 0)
    prev_halo = lambda t: jnp.maximum((t - off) * per_halo - 1, 0)
    next_halo = lambda t: jnp.clip((t - off + 1) * per_halo, 0, n_halo - 1)
    tok_shape = lambda width, dt: jax.ShapeDtypeStruct((B, T, width), dt)
    return pl.pallas_call(
        functools.partial(_proj_kernel, nt),
        out_shape=(tok_shape(2 * NQK_S, BF16), tok_shape(2 * NQK_S, BF16), tok_shape(2 * NQK_S, BF16),
                   tok_shape(BRANCH_W, BF16), tok_shape(BRANCH_W, BF16),
                   jax.ShapeDtypeStruct((B, T // CHUNK, 2 * NQK_S, LANES), F32),
                   tok_shape(NQK_M, BF16), tok_shape(NQK_M, BF16), tok_shape(2 * NQK_M, BF16),
                   tok_shape(BRANCH_W, BF16), tok_shape(BRANCH_W, BF16),
                   jax.ShapeDtypeStruct((B, 2 * PAIRS, T, 2 * CHUNK), F32),
                   tok_shape(MC_WIDTH, F32),
                   jax.ShapeDtypeStruct((B, T // CHUNK, 2 * NHD, LANES), F32),
                   tok_shape(2 * NQK_S, BF16), tok_shape(2 * NQK_S, BF16), tok_shape(2 * NQK_S, BF16),
                   tok_shape(BRANCH_W, BF16), tok_shape(BRANCH_W, BF16)),
        grid=(nt,),
        in_specs=[pl.BlockSpec((1, TILE, D_MODEL), lambda i: (1, blk(i), 0)),
                  pl.BlockSpec((1, HALO, D_MODEL), lambda i: (1, prev_halo(i), 0)),
                  pl.BlockSpec((1, HALO, D_MODEL), lambda i: (1, next_halo(i), 0)),
                  pl.BlockSpec((1, TILE, D_MODEL), lambda i: (0, blk(nxt(i)), 0)),
                  pl.BlockSpec((1, HALO, D_MODEL), lambda i: (0, prev_halo(nxt(i)), 0)),
                  pl.BlockSpec((1, HALO, D_MODEL), lambda i: (0, next_halo(nxt(i)), 0)),
                  pl.BlockSpec((B, TILE, D_MODEL), lambda i: (0, 0, 0)),
                  const((B + 1, 6, D_MODEL)), const((1, D_MODEL)),
                  pl.BlockSpec((1, D_MODEL, W_PROJ), lambda i: (layer, 0, 0), pipeline_mode=pl.Buffered(1)),
                  pl.BlockSpec((1, D_MODEL, W_SMALL), lambda i: (layer, 0, 0)),
                  const((4 * GLA_RANK, 2 * NQK_S)), const((1, 2 * NQK_S)),
                  const((3, 2 * NQK_M)), const((1, 2 * NHD)),
                  pl.BlockSpec((TILE, NQK_S), lambda i: (i, 0)),
                  pl.BlockSpec((TILE, NQK_S), lambda i: (i, 0)),
                  const((1, 2 * NQK_S))],
        out_specs=(tile(2 * NQK_S), tile(2 * NQK_S), tile(2 * NQK_S), tile(BRANCH_W), tile(BRANCH_W),
                   pl.BlockSpec((B, CHUNKS_PER_TILE, 2 * NQK_S, LANES), lambda i: (0, i, 0, 0)),
                   tile(NQK_M), tile(NQK_M), tile(2 * NQK_M), tile(BRANCH_W), tile(BRANCH_W),
                   pl.BlockSpec((B, 2 * PAIRS, TILE, 2 * CHUNK), lambda i: (0, 0, i, 0)),
                   tile(MC_WIDTH),
                   pl.BlockSpec((B, CHUNKS_PER_TILE, 2 * NHD, LANES), lambda i: (0, i, 0, 0)),
                   tile(2 * NQK_S), tile(2 * NQK_S), tile(2 * NQK_S), tile(BRANCH_W), tile(BRANCH_W)),
        scratch_shapes=[pltpu.VMEM((TILE + 2 * HALO, 2 * NQK_M), F32)] * 2
                       + [pltpu.VMEM((TILE, PW_MLR_WIDTH), F32)] * 2
                       + [pltpu.VMEM((TILE, PW_GR_WIDTH), F32)] * 2
                       + [pltpu.VMEM((6, TILE, NQK_S), F32), pltpu.VMEM((2, TILE, TILE), BF16)]
                       + [pltpu.VMEM((TILE, D_MODEL), BF16)] * 2
                       + [pltpu.VMEM((TILE, W_SMALL), F32)] * 2,
        compiler_params=_cparams(1, VMEM_LIMIT_PROJ), name="proj",
    )(x_src, x_src, x_src, x_src, x_src, x_src, ctx_src,
      mods, g1, w, w_small, a2s, abias, conv_w, gate_bias, cos_t, sin_t, dl_row)


def _mix_kernel(batch, *refs):
    (gq_f, gq_b, gk_f, gk_b, ge_f, ge_b, gv_f, gv_b, gd_f, gd_b,
     rq_f, rq_b, rk_f, rk_b, re_f, re_b, rv_f, rv_b, rdl,
     mq_f, mq_b, mk_f, mk_b, mw_f, mw_b, mv_f, mv_b, mp_f, mp_b, mcol_f, mcol_b, mc_f, mc_b,
     of_ref, ob_ref, sg_ref, sr_ref, sm_ref, m_ref) = refs
    j = pl.program_id(0)

    @pl.when(j == 0)
    def _():
        sg_ref[...] = jnp.zeros_like(sg_ref)
        sr_ref[...] = jnp.zeros_like(sr_ref)
        sm_ref[...] = jnp.zeros_like(sm_ref)
        m_ref[...] = jnp.full_like(m_ref, M_INIT)

    row_head = lax.broadcasted_iota(jnp.int32, (2 * CHUNK, PAIR_V), 0) // CHUNK
    diag_kv = row_head == lax.broadcasted_iota(jnp.int32, (2 * CHUNK, PAIR_V), 1) // DV
    diag_kk = (lax.broadcasted_iota(jnp.int32, (2 * CHUNK, PAIR_K), 0) // CHUNK
               == lax.broadcasted_iota(jnp.int32, (2 * CHUNK, PAIR_K), 1) // DK_SMALL)
    t_idx = lax.broadcasted_iota(jnp.int32, (CHUNK, 2 * CHUNK), 0)
    s_idx = lax.broadcasted_iota(jnp.int32, (CHUNK, 2 * CHUNK), 1) % CHUNK
    causal = (s_idx <= t_idx, s_idx >= t_idx)
    ones_v = jnp.ones((CHUNK, DV), BF16)
    zeros_vext = jnp.zeros((CHUNK, PAIR_V), BF16)
    ret_dec = jnp.exp(float(CHUNK) * _log_sigmoid(rdl[...]))

    decay_sets = (
        ((gq_f, gk_f, ge_f, gv_f), (gq_b, gk_b, ge_b, gv_b), sg_ref, 0),
        ((rq_f, rk_f, re_f, rv_f), (rq_b, rk_b, re_b, rv_b), sr_ref, 2 * BRANCH_W),
    )
    ml_sets = ((mq_f, mk_f, mw_f, mv_f, mp_f, mcol_f, mc_f),
               (mq_b, mk_b, mw_b, mv_b, mp_b, mcol_b, mc_b))
    outs = (of_ref, ob_ref)


    def decay_phase1(bb, d, c, rows, bi, p, refs):
        q_r, k_r, e_r, v_r = refs
        ks = slice(p * PAIR_K, (p + 1) * PAIR_K)
        q2 = q_r[bb, rows, ks]
        k2 = k_r[bb, rows, ks]
        v2 = v_r[bb, rows, p * PAIR_V:(p + 1) * PAIR_V]
        k_bd = jnp.where(diag_kk, jnp.concatenate([k2, k2], axis=0), 0.0)
        dec = (gd_f if d == 0 else gd_b)[bb, c, ks, :] if bi == 0 else ret_dec[d, ks, :]
        upd = jnp.where(diag_kv, _dot_tn(e_r[bb, rows, ks], v2), 0.0)
        return dict(q2=q2, v2=v2, dec=jnp.concatenate([dec, dec], axis=1), upd=upd,
                    att=_dot_nt(q2, k_bd))

    def decay_phase2(bb, d, p, rows, o_ref, s_ref, base, w):
        s_prev = s_ref[bb, d, p]
        att = jnp.where(causal[d], w["att"], 0.0).astype(BF16)
        v_bd = jnp.where(diag_kv, jnp.concatenate([w["v2"], w["v2"]], axis=0), 0.0)
        lhs = jnp.concatenate([att, w["q2"]], axis=1)
        rhs = jnp.concatenate([v_bd, s_prev.astype(BF16)], axis=0)
        o_ref[bb, rows, base + p * PAIR_V:base + (p + 1) * PAIR_V] = _dot(lhs, rhs)
        s_ref[bb, d, p] = w["dec"] * s_prev + w["upd"]

    def ml_phase1(bb, d, c, rows, p, refs):
        q_r, k_r, w_r, v_r, p_r, col_r, c_r = refs
        cols = col_r[bb, rows, :]
        q2 = q_r[bb, rows, p * 2 * DK_ML:(p + 1) * 2 * DK_ML]
        k2 = k_r[bb, rows, p * 2 * DK_ML:(p + 1) * 2 * DK_ML]
        k_bd = jnp.where(diag_kv, jnp.concatenate([k2, k2], axis=0), 0.0)
        att = _dot_nt(q2, k_bd)
        heads = []
        for hl in range(2):
            hd = 2 * p + hl
            jj = d * HEADS + hd
            vext = jnp.concatenate([v_r[bb, rows, hd * DV:(hd + 1) * DV], ones_v], axis=1)
            st = sm_ref[bb, d, hd]
            heads.append(dict(
                hd=hd, st=st, vext=vext,
                inter=_dot(q2[:, hl * DK_ML:(hl + 1) * DK_ML], st.astype(BF16)),
                upd=_dot_tn(w_r[bb, rows, hd * DK_ML:(hd + 1) * DK_ML], vext),
                u=jnp.broadcast_to(cols[:, MC_U + jj:MC_U + jj + 1], (CHUNK, LANES)),
                b=jnp.broadcast_to(cols[:, MC_B + jj:MC_B + jj + 1], (CHUNK, LANES))))
        return dict(att=att, heads=heads, p0=p_r[bb, p, rows, :], cs=c_r[bb, c])

    def ml_phase2(bb, d, rows, o_ref, w):
        sc = (w["att"] * w["p0"]).astype(BF16)
        for hl, hw in enumerate(w["heads"]):
            hd = hw["hd"]
            jj = d * HEADS + hd
            vz = (jnp.concatenate([hw["vext"], zeros_vext], axis=0) if hl == 0
                  else jnp.concatenate([zeros_vext, hw["vext"]], axis=0))
            intra = _dot(sc, vz)
            m_prev = m_ref[bb, jj:jj + 1, :]
            u = hw["u"]
            mx = jnp.maximum(u, m_prev)
            f = jnp.exp(u - mx)
            e = jnp.exp(m_prev - mx)
            num = f * intra[:, 0:DV] + e * hw["inter"][:, 0:DV]
            den = f * intra[:, DV:] + e * hw["inter"][:, DV:]
            den = jnp.maximum(jnp.abs(den), jnp.exp(-(hw["b"] + mx)))
            o_ref[bb, rows, BRANCH_W + hd * DV:BRANCH_W + (hd + 1) * DV] = num / den
            tot = w["cs"][jj:jj + 1, :]
            mloc = w["cs"][NHD + jj:NHD + jj + 1, :]
            m_new = jnp.maximum(tot + m_prev, mloc)
            keep = jnp.exp(tot + m_prev - m_new)
            gain = jnp.exp(mloc - m_new)
            sm_ref[bb, d, hd] = (jnp.concatenate([keep, keep], axis=1) * hw["st"]
                                 + jnp.concatenate([gain, gain], axis=1) * hw["upd"])
            m_ref[bb, jj:jj + 1, :] = m_new

    units = []
    for step in range(CHUNKS_PER_TILE):
        for bb, d in ((b_, d_) for b_ in range(batch) for d_ in range(2)):
            c = step if d == 0 else CHUNKS_PER_TILE - 1 - step
            rows = pl.ds(c * CHUNK, CHUNK)
            for bi, (fset, bset, s_ref, base) in enumerate(decay_sets):
                for p in range(PAIRS):
                    units.append((functools.partial(decay_phase1, bb, d, c, rows, bi, p, fset if d == 0 else bset),
                                  functools.partial(decay_phase2, bb, d, p, rows, outs[d], s_ref, base)))
            for p in range(PAIRS):
                units.append((functools.partial(ml_phase1, bb, d, c, rows, p, ml_sets[d]),
                              functools.partial(ml_phase2, bb, d, rows, outs[d])))

    assert MIX_LOOKAHEAD < len(units) // CHUNKS_PER_TILE
    pending = []
    for phase1, phase2 in units:
        pending.append((phase2, phase1()))
        if len(pending) > MIX_LOOKAHEAD:
            fn, w = pending.pop(0)
            fn(w)
    for fn, w in pending:
        fn(w)


def _mix(gla, ret, ml, ret_dl_col):
    gq, gk, ge, gv, gdec = gla
    rq, rk, re, rv = ret
    mq, mk, mw, mv, mp, mcol, mc = ml
    B, T, _ = gv.shape
    nt = T // TILE

    def tf(j):
        return j

    def tb(j):
        return jnp.where(j == 0, 0, nt - j)

    def pair(width, lane_block_b=0):
        return [pl.BlockSpec((B, TILE, width), lambda j: (0, tf(j), 0)),
                pl.BlockSpec((B, TILE, width), lambda j: (0, tb(j), lane_block_b))]

    dec_pair = [pl.BlockSpec((B, CHUNKS_PER_TILE, NQK_S, LANES), lambda j: (0, tf(j), 0, 0)),
                pl.BlockSpec((B, CHUNKS_PER_TILE, NQK_S, LANES), lambda j: (0, tb(j), 1, 0))]
    p0_pair = [pl.BlockSpec((B, PAIRS, TILE, 2 * CHUNK), lambda j: (0, 0, tf(j), 0)),
               pl.BlockSpec((B, PAIRS, TILE, 2 * CHUNK), lambda j: (0, 1, tb(j), 0))]
    cs_pair = [pl.BlockSpec((B, CHUNKS_PER_TILE, 2 * NHD, LANES), lambda j: (0, tf(j), 0, 0)),
               pl.BlockSpec((B, CHUNKS_PER_TILE, 2 * NHD, LANES), lambda j: (0, tb(j), 0, 0))]
    decay_specs = pair(NQK_S, 1) + pair(NQK_S, 1) + pair(NQK_S, 1) + pair(BRANCH_W)
    in_specs = (decay_specs + dec_pair + decay_specs
                + [pl.BlockSpec((2, NQK_S, LANES), lambda j: (0, 0, 0))]
                + pair(NQK_M) + pair(NQK_M) + pair(NQK_M, 1) + pair(BRANCH_W) + p0_pair
                + pair(MC_WIDTH) + cs_pair)
    args = (gq, gq, gk, gk, ge, ge, gv, gv, gdec, gdec,
            rq, rq, rk, rk, re, re, rv, rv, ret_dl_col,
            mq, mq, mk, mk, mw, mw, mv, mv, mp, mp, mcol, mcol, mc, mc)
    return pl.pallas_call(
        functools.partial(_mix_kernel, B),
        out_shape=(jax.ShapeDtypeStruct((B, T, 3 * BRANCH_W), F32),) * 2,
        grid=(nt,),
        in_specs=in_specs,
        out_specs=(pl.BlockSpec((B, TILE, 3 * BRANCH_W), lambda j: (0, tf(j), 0)),
                   pl.BlockSpec((B, TILE, 3 * BRANCH_W), lambda j: (0, tb(j), 0))),
        scratch_shapes=[pltpu.VMEM((B, 2, PAIRS, PAIR_K, PAIR_V), F32),
                        pltpu.VMEM((B, 2, PAIRS, PAIR_K, PAIR_V), F32),
                        pltpu.VMEM((B, 2, HEADS, DK_ML, 2 * DV), F32),
                        pltpu.VMEM((B, NHD, LANES), F32)],
        compiler_params=_cparams(1), name="mix",
    )(*args)


def _merge_kernel(batch, first_tile, x_ref, ctx_ref, mod_ref, g_ref, wg_ref, of_ref, ob_ref,
                  sgg_ref, sgm_ref, sgr_ref, ng_ref, wb_ref, wo_ref, out_ref):
    tile_idx = pl.program_id(0) + first_tile
    side = (sgg_ref, sgm_ref, sgr_ref)
    mods = [mod_ref[jnp.where(tile_idx == 0, batch, bb)] for bb in range(batch)]
    xs = [jnp.where(tile_idx == 0, ctx_ref[bb], x_ref[bb]) for bb in range(batch)]
    hs = [_norm_mod(xs[bb], g_ref[...], mods[bb][0:1, :], mods[bb][1:2, :]).astype(BF16)
          for bb in range(batch)]
    accs = [jnp.zeros((TILE, D_MODEL), F32) for _ in range(batch)]
    for br in range(3):
        for bb in range(batch):
            parts = []
            for hd in range(HEADS):
                cols = slice(br * BRANCH_W + hd * DV, br * BRANCH_W + (hd + 1) * DV)
                oh = of_ref[bb, :, cols] + ob_ref[bb, :, cols]
                if br > 0:
                    oh = oh - jnp.mean(oh, axis=-1, keepdims=True)
                parts.append(oh * lax.rsqrt(jnp.mean(oh * oh, axis=-1, keepdims=True) + EPS))
            y = jnp.concatenate(parts, axis=1) * ng_ref[:, br * BRANCH_W:(br + 1) * BRANCH_W]
            y = (y * side[br][bb]).astype(BF16)
            gate = _sigmoid(_dot(hs[bb], wg_ref[0, :, br * D_MODEL:(br + 1) * D_MODEL]))
            accs[bb] = accs[bb] + gate * _dot(y, wb_ref[0, br])
    for bb in range(batch):
        out = _dot(accs[bb].astype(BF16), wo_ref[0])
        out_ref[bb] = xs[bb] + mods[bb][2:3, :] * out


def _merge(ctx_src, x_src, off, mods, g1, wgate, o_f, o_b, sg_gla, sz_ml, sg_ret, norm_g, wb, wo, layer, skip):
    B = x_src.shape[0]
    nt = x_src.shape[1] // TILE + off - skip
    tile = lambda width: pl.BlockSpec((B, TILE, width), lambda i: (0, i + skip, 0))
    const = lambda shape: pl.BlockSpec(shape, lambda i: (0,) * len(shape))
    per_layer = lambda shape: pl.BlockSpec((1,) + shape, lambda i: (layer,) + (0,) * len(shape),
                                           pipeline_mode=pl.Buffered(1))
    return pl.pallas_call(
        functools.partial(_merge_kernel, B, skip),
        out_shape=jax.ShapeDtypeStruct((B, nt * TILE, D_MODEL), F32),
        grid=(nt,),
        in_specs=[pl.BlockSpec((B, TILE, D_MODEL), lambda i: (0, jnp.maximum(i + skip - off, 0), 0)),
                  pl.BlockSpec((B, TILE, D_MODEL), lambda i: (0, 0, 0)),
                  const((B + 1, 6, D_MODEL)),
                  const((1, D_MODEL)), per_layer((D_MODEL, W_GATE)),
                  tile(3 * BRANCH_W), tile(3 * BRANCH_W),
                  tile(BRANCH_W), tile(BRANCH_W), tile(BRANCH_W),
                  const((1, 3 * BRANCH_W)), per_layer((3, BRANCH_W, D_MODEL)),
                  per_layer((D_MODEL, D_MODEL))],
        out_specs=pl.BlockSpec((B, TILE, D_MODEL), lambda i: (0, i, 0)),
        compiler_params=_cparams(1), name="merge",
    )(x_src, ctx_src, mods, g1, wgate, o_f, o_b, sg_gla, sz_ml, sg_ret, norm_g, wb, wo)


def _mlp_kernel(batch, first_tile, final, x_ref, mod_ref, g_ref, wu_ref, wd_ref, fg_ref, out_ref):
    tile_idx = pl.program_id(0) + first_tile
    mods = [mod_ref[jnp.where(tile_idx == 0, batch, bb)] for bb in range(batch)]
    ups = []
    for bb in range(batch):
        h = _norm_mod(x_ref[bb], g_ref[...], mods[bb][3:4, :], mods[bb][4:5, :]).astype(BF16)
        ups.append(_dot(h, wu_ref[0]))
    for bb in range(batch):
        u = jnp.maximum(ups[bb], 0.0)
        y = x_ref[bb] + mods[bb][5:6, :] * _dot((u * u).astype(BF16), wd_ref[0])
        if final:
            y = y * lax.rsqrt(jnp.mean(y * y, axis=-1, keepdims=True) + EPS) * fg_ref[...]
        out_ref[bb] = y


def _mlp(tok, mods, g2, wu, wd, layer, final_g, first_tile, final):
    B, T, _ = tok.shape
    const = lambda shape: pl.BlockSpec(shape, lambda i: (0,) * len(shape))
    per_layer = lambda shape: pl.BlockSpec((1,) + shape, lambda i: (layer,) + (0,) * len(shape),
                                           pipeline_mode=pl.Buffered(1))
    return pl.pallas_call(
        functools.partial(_mlp_kernel, B, first_tile, final),
        out_shape=jax.ShapeDtypeStruct((B, T, D_MODEL), F32),
        grid=(T // TILE,),
        in_specs=[pl.BlockSpec((B, TILE, D_MODEL), lambda i: (0, i, 0)),
                  const((B + 1, 6, D_MODEL)),
                  const((1, D_MODEL)), per_layer((D_MODEL, D_FF)), per_layer((D_FF, D_MODEL)),
                  const((1, D_MODEL))],
        out_specs=pl.BlockSpec((B, TILE, D_MODEL), lambda i: (0, i, 0)),
        compiler_params=_cparams(1), name="mlp",
    )(tok, mods, g2, wu, wd, final_g)


def kernel(x, c, ctx, c_ctx, norm1_g, norm2_g, w_ada, b_ada, w_in, gla_a2, gla_a_bias, gla_norm_g,
           ml_conv, ml_i_bias, ml_f_bias, ml_norm_g, ret_decay_logit, ret_norm_g, w_branch, w_o,
           w_up, w_down, final_g):
    B, seq, _ = x.shape
    ctx_len = ctx.shape[1]
    assert B == 2 and ctx_len == TILE and seq % TILE == 0 and seq % GRID_W == 0
    ctx_src, x_src, off = ctx, x, 1

    c_rows = jnp.concatenate([c, c_ctx[None, :], jnp.zeros((8 - B - 1, D_MODEL), F32)], axis=0)
    mods_all = _ada(c_rows, w_ada, b_ada)[:, 0:3].reshape(DEPTH, 3, 6, D_MODEL)

    cos_t, sin_t = _rope_tables(seq, ctx_len)

    w_proj, w_gate = _wprep(w_in)
    o0, o1 = W_GLA, W_GLA + W_ML
    w_small = jnp.concatenate([
        w_in[:, :, o1 - 2 * NHD:o1],
        w_in[:, :, o0 - 2 * GLA_RANK:o0],
        jnp.zeros((DEPTH, D_MODEL, W_SMALL - 2 * NHD - 2 * GLA_RANK), F32),
    ], axis=2).astype(BF16)
    w_branch_b = w_branch.astype(BF16)
    w_o_b = w_o.astype(BF16)
    w_up_b = w_up.astype(BF16)
    w_down_b = w_down.astype(BF16)
    for l in range(DEPTH):
        last = l == DEPTH - 1
        skip = 1 if last else 0
        mods = mods_all[l]
        g1 = norm1_g[l][None, :]
        g2 = norm2_g[l][None, :]
        zeros = jnp.zeros((GLA_RANK, NQK_S), F32)
        a2blk = jnp.concatenate([jnp.concatenate([gla_a2[l, 0], zeros], axis=1),
                                 jnp.concatenate([zeros, gla_a2[l, 1]], axis=1)], axis=0)
        a2s = jnp.concatenate([a2blk, a2blk], axis=0).astype(BF16)
        abias = gla_a_bias[l].reshape(1, 2 * NQK_S)
        gate_bias = jnp.concatenate([ml_i_bias[l].reshape(1, NHD), ml_f_bias[l].reshape(1, NHD)], axis=1)
        dl = ret_decay_logit[l]
        dl_row = jnp.repeat(dl, DK_SMALL, axis=1).reshape(1, 2 * NQK_S)
        dl_col = jnp.broadcast_to(jnp.repeat(dl, DK_SMALL, axis=1)[:, :, None], (2, NQK_S, LANES))

        (gq, gk, ge, gv, sg_gla, gdec, mq, mk, mw, mv, sz_ml, mp, mcol, mc,
         rq, rk, re, rv, sg_ret) = _proj(ctx_src, x_src, off, mods, g1, w_proj, w_small, l, a2s, abias,
                                         ml_conv[l], gate_bias, cos_t, sin_t, dl_row)
        o_f, o_b = _mix((gq, gk, ge, gv, gdec), (rq, rk, re, rv), (mq, mk, mw, mv, mp, mcol, mc), dl_col)
        norm_g = jnp.concatenate([gla_norm_g[l], ml_norm_g[l], ret_norm_g[l]])[None, :]
        tok = _merge(ctx_src, x_src, off, mods, g1, w_gate, o_f, o_b, sg_gla, sz_ml, sg_ret, norm_g,
                     w_branch_b, w_o_b, l, skip)
        tok = _mlp(tok, mods, g2, w_up_b, w_down_b, l, final_g[None, :], skip, last)
        ctx_src, x_src, off = tok, tok, 0
    return tok
```

```python
import functools

import jax
import jax.numpy as jnp
from jax import lax
from jax.experimental import pallas as pl
from jax.experimental.pallas import tpu as pltpu

F32 = jnp.float32
BF16 = jnp.bfloat16

D_MODEL = 1024
DEPTH = 2
GRID_W = 64
CHUNK = 64
EPS = 1e-6
M_INIT = -1e30
HEADS = 4
DK_SMALL = 64
DK_ML = 128
DV = 128
BRANCH_W = HEADS * DV
GLA_RANK = 16
GLA_TAU = 16.0
ROPE_BASE = 10000.0
D_FF = 4 * D_MODEL
LANES = 128

TILE = 256
CHUNKS_PER_TILE = TILE // CHUNK
HALO = 8
VMEM_LIMIT = 56 * 1024 * 1024
VMEM_LIMIT_PROJ = 60 * 1024 * 1024

NQK_S = HEADS * DK_SMALL
NQK_M = HEADS * DK_ML
NHD = 2 * HEADS
W_GLA = 2 * NQK_S + 2 * BRANCH_W + 2 * GLA_RANK
W_ML = 2 * NQK_M + 2 * BRANCH_W + 2 * NHD
W_RET = 2 * NQK_S + 2 * BRANCH_W
W_GATE = 3 * D_MODEL

PW_MLQK = 0
PW_MLR = PW_MLQK + 2 * NQK_M
PW_MLR_WIDTH = 2 * BRANCH_W + LANES
PW_GR = PW_MLR + PW_MLR_WIDTH
PW_GR_WIDTH = 2 * (2 * NQK_S + 2 * BRANCH_W) + LANES
W_PROJ = PW_GR + PW_GR_WIDTH
MLR_V, MLR_Z, MLR_GATES = 0, BRANCH_W, 2 * BRANCH_W
GR_GLA_Q, GR_GLA_K, GR_GLA_V, GR_GLA_G = 0, NQK_S, 2 * NQK_S, 2 * NQK_S + BRANCH_W
GR_RET_Q = 2 * NQK_S + 2 * BRANCH_W
GR_RET_K, GR_RET_V, GR_RET_G = GR_RET_Q + NQK_S, GR_RET_Q + 2 * NQK_S, GR_RET_Q + 2 * NQK_S + BRANCH_W
GR_GLA_A = 2 * GR_RET_Q

MC_B, MC_W, MC_U = 0, NHD, 2 * NHD
MC_WIDTH = 3 * NHD

PAIRS = HEADS // 2
PAIR_K = 2 * DK_SMALL
PAIR_V = 2 * DV
MIX_LOOKAHEAD = 23


def _cparams(n_axes, vmem_limit=VMEM_LIMIT):
    return pltpu.CompilerParams(dimension_semantics=("arbitrary",) * n_axes,
                                vmem_limit_bytes=vmem_limit)


def _log_sigmoid(z):
    return jnp.minimum(z, 0.0) - jnp.log(1.0 + jnp.exp(-jnp.abs(z)))


def _sigmoid(z):
    return 1.0 / (1.0 + jnp.exp(-z))


def _silu(z):
    return z * _sigmoid(z)


def _norm_mod(x, g, shift, scale):
    y = x * lax.rsqrt(jnp.mean(x * x, axis=-1, keepdims=True) + EPS) * g
    return y * (1.0 + scale) + shift


def _dot(a, b):
    return jnp.dot(a, b, preferred_element_type=F32)


def _dot_nt(a, b):
    return lax.dot_general(a, b, (((1,), (1,)), ((), ())), preferred_element_type=F32)


def _dot_tn(a, b):
    return lax.dot_general(a, b, (((0,), (0,)), ((), ())), preferred_element_type=F32)


def _split2(x):
    hi = x.astype(BF16)
    return hi, (x - hi.astype(F32)).astype(BF16)


def _chunk_tri(lower):
    r = lax.broadcasted_iota(jnp.int32, (TILE, TILE), 0)
    c = lax.broadcasted_iota(jnp.int32, (TILE, TILE), 1)
    same = (r // CHUNK) == (c // CHUNK)
    tri = (c <= r) if lower else (c >= r)
    return jnp.where(same & tri, 1.0, 0.0).astype(BF16)


def _chunk_total(x):
    w = x.shape[-1]
    tot = jnp.sum(x.reshape(CHUNKS_PER_TILE, CHUNK, w), axis=1, keepdims=True)
    return jnp.broadcast_to(tot, (CHUNKS_PER_TILE, CHUNK, w)).reshape(TILE, w)


def _ada_kernel(c_ref, w_ref, b_ref, o_ref):
    s = _silu(c_ref[...])
    s_hi = s.astype(BF16).astype(F32)
    s2 = jnp.concatenate([s_hi, s - s_hi], axis=0).astype(BF16)
    w_hi, w_lo = _split2(w_ref[0])
    r = _dot(s2, w_hi) + _dot(s2, w_lo)
    o_ref[0] = r[0:8] + r[8:16] + b_ref[0]


def _ada(c_rows, w_ada, b_ada):
    tn = 1536
    return pl.pallas_call(
        _ada_kernel,
        out_shape=jax.ShapeDtypeStruct((DEPTH, 8, 6 * D_MODEL), F32),
        grid=(DEPTH, 6 * D_MODEL // tn),
        in_specs=[pl.BlockSpec((8, D_MODEL), lambda l, n: (0, 0)),
                  pl.BlockSpec((1, D_MODEL, tn), lambda l, n: (l, 0, n)),
                  pl.BlockSpec((1, 1, tn), lambda l, n: (l, 0, n))],
        out_specs=pl.BlockSpec((1, 8, tn), lambda l, n: (l, 0, n)),
        compiler_params=_cparams(2), name="ada",
    )(c_rows, w_ada, b_ada.reshape(DEPTH, 1, 6 * D_MODEL))


def _rope_kernel(ang_ref, cos_ref, sin_ref, nsin_ref):
    a = ang_ref[...]
    s = jnp.sin(a)
    cos_ref[...] = jnp.cos(a)
    sin_ref[...] = s
    nsin_ref[...] = -s


def _rope_tables(seq, ctx_len):
    rows = seq // GRID_W
    nf = DK_SMALL // 4
    inv = ROPE_BASE ** (-jnp.arange(nf, dtype=F32) / nf)
    ang = jnp.concatenate([jnp.arange(rows, dtype=F32)[:, None] * inv,
                           jnp.arange(GRID_W, dtype=F32)[:, None] * inv], axis=0)
    n = rows + GRID_W
    cos, sin, nsin = pl.pallas_call(
        _rope_kernel,
        out_shape=(jax.ShapeDtypeStruct((n, nf), F32),) * 3,
        name="rope_tables",
    )(ang)

    def per_token(first, second):
        def half(t):
            r = jnp.broadcast_to(t[:rows, None, :], (rows, GRID_W, nf))
            c = jnp.broadcast_to(t[None, rows:, :], (rows, GRID_W, nf))
            return jnp.concatenate([r, c], axis=-1)
        head = jnp.concatenate([half(first), half(second)], axis=-1).reshape(seq, DK_SMALL)
        return jnp.tile(head, (1, HEADS))

    ones = jnp.ones((ctx_len, NQK_S), F32)
    cos_t = jnp.concatenate([ones, per_token(cos, cos)], axis=0)
    sin_t = jnp.concatenate([0.0 * ones, per_token(nsin, sin)], axis=0)
    return cos_t, sin_t


def _wprep_kernel(w_ref, wp_ref, wg_ref):
    o0, o1, o2 = W_GLA, W_GLA + W_ML, W_GLA + W_ML + W_RET
    n_sg = 2 * NQK_S + 2 * BRANCH_W
    rows = w_ref.shape[1]

    def copy(dst_ref, dst, src, n):
        dst_ref[0, :, dst:dst + n] = w_ref[0, :, src:src + n].astype(BF16)

    copy(wp_ref, PW_MLQK, o0, 2 * NQK_M)
    n_mlr = 2 * BRANCH_W + 2 * NHD
    copy(wp_ref, PW_MLR, o0 + 2 * NQK_M, n_mlr)
    wp_ref[0, :, PW_MLR + n_mlr:PW_GR] = jnp.zeros((rows, PW_GR - PW_MLR - n_mlr), BF16)
    copy(wp_ref, PW_GR + GR_GLA_Q, 0, n_sg)
    copy(wp_ref, PW_GR + GR_RET_Q, o1, W_RET)
    copy(wp_ref, PW_GR + GR_GLA_A, n_sg, 2 * GLA_RANK)
    pad0 = PW_GR + GR_GLA_A + 2 * GLA_RANK
    wp_ref[0, :, pad0:W_PROJ] = jnp.zeros((rows, W_PROJ - pad0), BF16)
    copy(wg_ref, 0, o2, W_GATE)


def _wprep(w_in):
    rows = 256
    return pl.pallas_call(
        _wprep_kernel,
        out_shape=(jax.ShapeDtypeStruct((DEPTH, D_MODEL, W_PROJ), BF16),
                   jax.ShapeDtypeStruct((DEPTH, D_MODEL, W_GATE), BF16)),
        grid=(DEPTH, D_MODEL // rows),
        in_specs=[pl.BlockSpec((1, rows, w_in.shape[2]), lambda l, r: (l, r, 0))],
        out_specs=(pl.BlockSpec((1, rows, W_PROJ), lambda l, r: (l, r, 0)),
                   pl.BlockSpec((1, rows, W_GATE), lambda l, r: (l, r, 0))),
        compiler_params=_cparams(2), name="wprep",
    )(w_in)


def _proj_kernel(nt, xa_ref, xap_ref, xan_ref, xb_ref, xbp_ref, xbn_ref, ctx_ref,
                 mod_ref, g_ref, w_ref, a2_ref, ab_ref, cw_ref, gb_ref, cos_ref, sin_ref, dl_ref,
                 gqke_ref, gvs_ref, gdec_ref, mqk_ref, mkw_ref, mvs_ref, mp0_ref, mcol_ref, mcs_ref,
                 rqke_ref, rvs_ref,
                 ext0_ref, ext1_ref, prml0_ref, prml1_ref, prgr0_ref, prgr1_ref, rdec_ref, tri_ref,
                 h0_ref, h1_ref):
    i = pl.program_id(0)
    ext_refs = (ext0_ref, ext1_ref)
    prml_refs = (prml0_ref, prml1_ref)
    prgr_refs = (prgr0_ref, prgr1_ref)
    h_refs = (h0_ref, h1_ref)
    row8 = lax.broadcasted_iota(jnp.int32, (NHD, TILE), 0)
    lane8 = lax.broadcasted_iota(jnp.int32, (NHD, TILE), 1)
    lane_chunk = lane8 // CHUNK
    lane_in_chunk = lane8 % CHUNK
    ti = lax.broadcasted_iota(jnp.int32, (CHUNK, CHUNK), 0)
    si = lax.broadcasted_iota(jnp.int32, (CHUNK, CHUNK), 1)
    ext_row = lax.broadcasted_iota(jnp.int32, (TILE + 2 * HALO, 1), 0)

    def project_stages(x, x_prev, x_next, tile_idx, mod_row, slot):
        st = {}

        def mlstm_qk():
            mod = mod_ref[mod_row]
            x_all = jnp.concatenate([x_prev, x, x_next], axis=0)
            h_all = _norm_mod(x_all, g_ref[...], mod[0:1, :], mod[1:2, :])
            h_refs[slot][...] = h_all[HALO:HALO + TILE, :].astype(BF16)
            prev_ok = tile_idx >= 2
            next_ok = jnp.logical_and(tile_idx >= 1, tile_idx <= nt - 2)
            dead = jnp.logical_or(jnp.logical_and(ext_row < HALO, jnp.logical_not(prev_ok)),
                                  jnp.logical_and(ext_row >= TILE + HALO, jnp.logical_not(next_ok)))
            ext = _dot(h_all.astype(BF16), w_ref[0, :, PW_MLQK:PW_MLR])
            ext_refs[slot][...] = jnp.where(dead, 0.0, ext)

        def mlstm_rest():
            prml_refs[slot][...] = _dot(h_refs[slot][...], w_ref[0, :, PW_MLR:PW_GR])

        def gla():
            prgr_refs[slot][:, 0:GR_RET_Q] = _dot(h_refs[slot][...], w_ref[0, :, PW_GR:PW_GR + GR_RET_Q])

        def ret():
            prgr_refs[slot][:, GR_RET_Q:] = _dot(h_refs[slot][...], w_ref[0, :, PW_GR + GR_RET_Q:])

        return [mlstm_qk, mlstm_rest, gla, ret]

    def project(*args):
        for stage in project_stages(*args):
            stage()

    @pl.when(i == 0)
    def _():
        tri_ref[0] = _chunk_tri(True)
        tri_ref[1] = _chunk_tri(False)
        lg = _log_sigmoid(dl_ref[...])
        t_in_chunk = (lax.broadcasted_iota(jnp.int32, (TILE, NQK_S), 0) % CHUNK).astype(F32)
        for d in range(2):
            lg_d = lg[:, d * NQK_S:(d + 1) * NQK_S]
            steps = (t_in_chunk + 1.0) if d == 0 else (float(CHUNK) - t_in_chunk)
            b = steps * lg_d
            rdec_ref[3 * d] = jnp.exp(b)
            rdec_ref[3 * d + 1] = jnp.exp(-b)
            rdec_ref[3 * d + 2] = jnp.exp(float(CHUNK) * lg_d - b)
        halo0 = jnp.zeros((HALO, D_MODEL), F32)
        project(ctx_ref[0], halo0, halo0, 0, 2, 0)

    def rope(x):
        lane = lax.broadcasted_iota(jnp.int32, (TILE, NQK_S), 1)
        first_half = (lane % DK_SMALL) < (DK_SMALL // 2)
        swapped = jnp.where(first_half, pltpu.roll(x, NQK_S - DK_SMALL // 2, 1),
                            pltpu.roll(x, DK_SMALL // 2, 1))
        return x * cos_ref[...] + swapped * sin_ref[...]

    def seg_cummax(x, reverse):
        sh = 1
        while sh < CHUNK:
            if reverse:
                y, ok = pltpu.roll(x, TILE - sh, 1), lane_in_chunk < CHUNK - sh
            else:
                y, ok = pltpu.roll(x, sh, 1), lane_in_chunk >= sh
            x = jnp.where(ok, jnp.maximum(x, y), x)
            sh *= 2
        return x

    def prepare_stages(slot, bb):
        ext_ref, prml_ref, prgr_ref = ext_refs[slot], prml_refs[slot], prgr_refs[slot]
        st = {}

        def conv():
            qk = _silu(cw_ref[0:1, :] * ext_ref[HALO - 1:HALO - 1 + TILE, :]
                       + cw_ref[1:2, :] * ext_ref[HALO:HALO + TILE, :]
                       + cw_ref[2:3, :] * ext_ref[HALO + 1:HALO + 1 + TILE, :])
            mqk_ref[bb, :, 0:NQK_M] = qk[:, 0:NQK_M].astype(BF16)
            st["k_ml"] = qk[:, NQK_M:] * (DK_ML ** -0.5)
            mqk_ref[bb, :, NQK_M:] = st["k_ml"].astype(BF16)

        def gates_in():
            mvs_ref[bb, :, 0:BRANCH_W] = prml_ref[:,MLR_V:MLR_V + BRANCH_W].astype(BF16)
            mvs_ref[bb, :, BRANCH_W:] = _silu(prml_ref[:,MLR_Z:MLR_Z + BRANCH_W]).astype(BF16)
            gates_t = (prml_ref[:,MLR_GATES:MLR_GATES + 2 * NHD] + gb_ref[...]).T
            st["li_t"] = gates_t[0:NHD]
            st["lf_t"] = _log_sigmoid(gates_t[NHD:])
            lf_hi, lf_lo = _split2(st["lf_t"])
            st["lf2"] = jnp.concatenate([lf_hi, lf_lo], axis=0)

        def gates_mm():
            st["cum_f"] = _dot(st["lf2"], tri_ref[1])
            st["cum_b"] = _dot(st["lf2"], tri_ref[0])

        def gates_out():
            mlstm_gates_body(st["li_t"], st["lf_t"], st["cum_f"], st["cum_b"], st["k_ml"], bb)

        def gla_in():
            gvs_ref[bb, :, 0:BRANCH_W] = prgr_ref[:,GR_GLA_V:GR_GLA_V + BRANCH_W].astype(BF16)
            gvs_ref[bb, :, BRANCH_W:] = _silu(prgr_ref[:,GR_GLA_G:GR_GLA_G + BRANCH_W]).astype(BF16)
            a_hi, a_lo = _split2(prgr_ref[:,GR_GLA_A:GR_GLA_A + 2 * GLA_RANK])
            st["a2"] = jnp.concatenate([a_hi, a_lo], axis=1)

        def gla_z_mm():
            st["z"] = _dot(st["a2"], a2_ref[...])

        def gla_la():
            la = _log_sigmoid(st["z"] + ab_ref[...]) * (1.0 / GLA_TAU)
            st["la"] = la
            st["la_split"] = [_split2(la[:, d * NQK_S:(d + 1) * NQK_S]) for d in range(2)]

        def gla_cum_mm():
            st["b"] = []
            for d in range(2):
                tri = tri_ref[d]
                la_hi, la_lo = st["la_split"][d]
                st["b"].append(_dot(tri, la_hi) + _dot(tri, la_lo))

        def gla_out():
            gla_body(prgr_ref, st["la"], st["b"], bb)

        def ret():
            ret_body(prgr_ref, bb)

        return dict(conv=conv, gates_in=gates_in, gates_mm=gates_mm, gates_out=gates_out,
                    gla_in=gla_in, gla_z_mm=gla_z_mm, gla_la=gla_la, gla_cum_mm=gla_cum_mm,
                    gla_out=gla_out, ret=ret)

    def mlstm_gates_body(li_t, lf_t, cum_f, cum_b, k_ml, bb):
        b_t = jnp.where(row8 < HEADS, cum_f[0:NHD] + cum_f[NHD:], cum_b[0:NHD] + cum_b[NHD:])
        tot_t = jnp.zeros((NHD, TILE), F32)
        mloc_t = jnp.zeros((NHD, TILE), F32)
        for c in range(CHUNKS_PER_TILE):
            tot_c = jnp.sum(lf_t[:, c * CHUNK:(c + 1) * CHUNK], axis=-1, keepdims=True)
            tot_t = jnp.where(lane_chunk == c, tot_c, tot_t)
            mcs_ref[bb, c, 0:NHD, :] = jnp.broadcast_to(tot_c, (NHD, LANES))
        g_t = tot_t - b_t + li_t
        for c in range(CHUNKS_PER_TILE):
            mloc_c = jnp.max(g_t[:, c * CHUNK:(c + 1) * CHUNK], axis=-1, keepdims=True)
            mloc_t = jnp.where(lane_chunk == c, mloc_c, mloc_t)
            mcs_ref[bb, c, NHD:2 * NHD, :] = jnp.broadcast_to(mloc_c, (NHD, LANES))
        w_t = jnp.exp(g_t - mloc_t)
        r_t = li_t - b_t
        u_t = jnp.where(row8 < HEADS, seg_cummax(r_t, False), seg_cummax(r_t, True))
        cols = jnp.concatenate([b_t, w_t, u_t], axis=0).T
        mcol_ref[bb] = cols
        for j in range(NHD):
            d, hd = j // HEADS, j % HEADS
            mask = (si <= ti) if d == 0 else (si >= ti)
            mkw_ref[bb, :, d * NQK_M + hd * DK_ML:d * NQK_M + (hd + 1) * DK_ML] = (
                k_ml[:, hd * DK_ML:(hd + 1) * DK_ML] * cols[:, MC_W + j:MC_W + j + 1]).astype(BF16)
            for c in range(CHUNKS_PER_TILE):
                rows = slice(c * CHUNK, (c + 1) * CHUNK)
                a = r_t[j:j + 1, rows] - cols[rows, MC_U + j:MC_U + j + 1]
                mp0_ref[bb, j // 2, rows, (j % 2) * CHUNK:(j % 2 + 1) * CHUNK] = jnp.where(mask, jnp.exp(a), 0.0)

    def gla_body(prgr_ref, la, cums, bb):
        half_t = jnp.sum(la.reshape(2 * CHUNKS_PER_TILE, CHUNK // 2, 2 * NQK_S), axis=1).T
        q_gla = prgr_ref[:,GR_GLA_Q:GR_GLA_Q + NQK_S] * (DK_SMALL ** -0.5)
        k_gla = prgr_ref[:,GR_GLA_K:GR_GLA_K + NQK_S]
        for d in range(2):
            la_d = la[:, d * NQK_S:(d + 1) * NQK_S]
            b = cums[d]
            tot = _chunk_total(la_d)
            gqke_ref[bb, :, d * NQK_S:(d + 1) * NQK_S] = (q_gla * jnp.exp(b)).astype(BF16)
            gqke_ref[bb, :, (2 + d) * NQK_S:(3 + d) * NQK_S] = (k_gla * jnp.exp(-b)).astype(BF16)
            gqke_ref[bb, :, (4 + d) * NQK_S:(5 + d) * NQK_S] = (k_gla * jnp.exp(tot - b)).astype(BF16)
            ht = half_t[d * NQK_S:(d + 1) * NQK_S, :]
            for c in range(CHUNKS_PER_TILE):
                tot_col = ht[:, 2 * c:2 * c + 1] + ht[:, 2 * c + 1:2 * c + 2]
                gdec_ref[bb, c, d * NQK_S:(d + 1) * NQK_S, :] = jnp.broadcast_to(jnp.exp(tot_col), (NQK_S, LANES))

    def ret_body(prgr_ref, bb):
        q_ret = rope(prgr_ref[:,GR_RET_Q:GR_RET_Q + NQK_S])
        k_ret = rope(prgr_ref[:,GR_RET_K:GR_RET_K + NQK_S] * (DK_SMALL ** -0.5))
        rvs_ref[bb, :, 0:BRANCH_W] = prgr_ref[:,GR_RET_V:GR_RET_V + BRANCH_W].astype(BF16)
        rvs_ref[bb, :, BRANCH_W:] = _silu(prgr_ref[:,GR_RET_G:GR_RET_G + BRANCH_W]).astype(BF16)
        for d in range(2):
            rqke_ref[bb, :, d * NQK_S:(d + 1) * NQK_S] = (q_ret * rdec_ref[3 * d]).astype(BF16)
            rqke_ref[bb, :, (2 + d) * NQK_S:(3 + d) * NQK_S] = (k_ret * rdec_ref[3 * d + 1]).astype(BF16)
            rqke_ref[bb, :, (4 + d) * NQK_S:(5 + d) * NQK_S] = (k_ret * rdec_ref[3 * d + 2]).astype(BF16)

    next_tile = jnp.minimum(i + 1, nt - 1)
    xa = jnp.where(i == 0, ctx_ref[1], xa_ref[0])
    a_ext, a_mlr, a_gla, a_ret = project_stages(xa, xap_ref[0], xan_ref[0], i, jnp.where(i == 0, 2, 1), 1)
    b_ext, b_mlr, b_gla, b_ret = project_stages(xb_ref[0], xbp_ref[0], xbn_ref[0], next_tile, 0, 0)
    x = prepare_stages(0, 0)
    y = prepare_stages(1, 1)
    order = [a_ext, x["conv"], x["gates_in"],
             a_mlr, x["gates_mm"], x["gates_out"], x["gla_in"],
             a_gla, x["gla_z_mm"], x["gla_la"],
             a_ret, x["gla_cum_mm"],
             b_ext, x["gla_out"], x["ret"], y["conv"], y["gates_in"],
             b_mlr, y["gates_mm"], y["gates_out"], y["gla_in"],
             b_gla, y["gla_z_mm"], y["gla_la"],
             b_ret, y["gla_cum_mm"], y["gla_out"], y["ret"]]
    for stage in order:
        stage()


def _proj(ctx_src, x_src, off, mods, g1, w, layer, a2s, abias, conv_w, gate_bias, cos_t, sin_t, dl_row):
    B = x_src.shape[0]
    nt = x_src.shape[1] // TILE + off
    T = nt * TILE
    tile = lambda width: pl.BlockSpec((B, TILE, width), lambda i: (0, i, 0))
    const = lambda shape: pl.BlockSpec(shape, lambda i: (0,) * len(shape))
    per_halo = TILE // HALO
    n_halo = x_src.shape[1] // HALO
    nxt = lambda i: jnp.minimum(i + 1, nt - 1)
    blk = lambda t: jnp.maximum(t - off, 0)
    prev_halo = lambda t: jnp.maximum((t - off) * per_halo - 1, 0)
    next_halo = lambda t: jnp.clip((t - off + 1) * per_halo, 0, n_halo - 1)
    tok_shape = lambda width, dt: jax.ShapeDtypeStruct((B, T, width), dt)
    return pl.pallas_call(
        functools.partial(_proj_kernel, nt),
        out_shape=(tok_shape(6 * NQK_S, BF16), tok_shape(2 * BRANCH_W, BF16),
                   jax.ShapeDtypeStruct((B, T // CHUNK, 2 * NQK_S, LANES), F32),
                   tok_shape(2 * NQK_M, BF16), tok_shape(2 * NQK_M, BF16), tok_shape(2 * BRANCH_W, BF16),
                   jax.ShapeDtypeStruct((B, 2 * PAIRS, T, 2 * CHUNK), F32),
                   tok_shape(MC_WIDTH, F32),
                   jax.ShapeDtypeStruct((B, T // CHUNK, 2 * NHD, LANES), F32),
                   tok_shape(6 * NQK_S, BF16), tok_shape(2 * BRANCH_W, BF16)),
        grid=(nt,),
        in_specs=[pl.BlockSpec((1, TILE, D_MODEL), lambda i: (1, blk(i), 0)),
                  pl.BlockSpec((1, HALO, D_MODEL), lambda i: (1, prev_halo(i), 0)),
                  pl.BlockSpec((1, HALO, D_MODEL), lambda i: (1, next_halo(i), 0)),
                  pl.BlockSpec((1, TILE, D_MODEL), lambda i: (0, blk(nxt(i)), 0)),
                  pl.BlockSpec((1, HALO, D_MODEL), lambda i: (0, prev_halo(nxt(i)), 0)),
                  pl.BlockSpec((1, HALO, D_MODEL), lambda i: (0, next_halo(nxt(i)), 0)),
                  pl.BlockSpec((B, TILE, D_MODEL), lambda i: (0, 0, 0)),
                  const((B + 1, 6, D_MODEL)), const((1, D_MODEL)),
                  pl.BlockSpec((1, D_MODEL, W_PROJ), lambda i: (layer, 0, 0), pipeline_mode=pl.Buffered(1)),
                  const((4 * GLA_RANK, 2 * NQK_S)), const((1, 2 * NQK_S)),
                  const((3, 2 * NQK_M)), const((1, 2 * NHD)),
                  pl.BlockSpec((TILE, NQK_S), lambda i: (i, 0)),
                  pl.BlockSpec((TILE, NQK_S), lambda i: (i, 0)),
                  const((1, 2 * NQK_S))],
        out_specs=(tile(6 * NQK_S), tile(2 * BRANCH_W),
                   pl.BlockSpec((B, CHUNKS_PER_TILE, 2 * NQK_S, LANES), lambda i: (0, i, 0, 0)),
                   tile(2 * NQK_M), tile(2 * NQK_M), tile(2 * BRANCH_W),
                   pl.BlockSpec((B, 2 * PAIRS, TILE, 2 * CHUNK), lambda i: (0, 0, i, 0)),
                   tile(MC_WIDTH),
                   pl.BlockSpec((B, CHUNKS_PER_TILE, 2 * NHD, LANES), lambda i: (0, i, 0, 0)),
                   tile(6 * NQK_S), tile(2 * BRANCH_W)),
        scratch_shapes=[pltpu.VMEM((TILE + 2 * HALO, 2 * NQK_M), F32)] * 2
                       + [pltpu.VMEM((TILE, PW_MLR_WIDTH), F32)] * 2
                       + [pltpu.VMEM((TILE, PW_GR_WIDTH), F32)] * 2
                       + [pltpu.VMEM((6, TILE, NQK_S), F32), pltpu.VMEM((2, TILE, TILE), BF16)]
                       + [pltpu.VMEM((TILE, D_MODEL), BF16)] * 2,
        compiler_params=_cparams(1, VMEM_LIMIT_PROJ), name="proj",
    )(x_src, x_src, x_src, x_src, x_src, x_src, ctx_src,
      mods, g1, w, a2s, abias, conv_w, gate_bias, cos_t, sin_t, dl_row)


def _mix_kernel(batch, *refs):
    (gq_f, gq_b, gk_f, gk_b, ge_f, ge_b, gv_f, gv_b, gd_f, gd_b,
     rq_f, rq_b, rk_f, rk_b, re_f, re_b, rv_f, rv_b, rdl,
     mq_f, mq_b, mk_f, mk_b, mw_f, mw_b, mv_f, mv_b, mp_f, mp_b, mcol_f, mcol_b, mc_f, mc_b,
     of_ref, ob_ref, sg_ref, sr_ref, sm_ref, m_ref) = refs
    j = pl.program_id(0)

    @pl.when(j == 0)
    def _():
        sg_ref[...] = jnp.zeros_like(sg_ref)
        sr_ref[...] = jnp.zeros_like(sr_ref)
        sm_ref[...] = jnp.zeros_like(sm_ref)
        m_ref[...] = jnp.full_like(m_ref, M_INIT)

    row_head = lax.broadcasted_iota(jnp.int32, (2 * CHUNK, PAIR_V), 0) // CHUNK
    diag_kv = row_head == lax.broadcasted_iota(jnp.int32, (2 * CHUNK, PAIR_V), 1) // DV
    diag_kk = (lax.broadcasted_iota(jnp.int32, (2 * CHUNK, PAIR_K), 0) // CHUNK
               == lax.broadcasted_iota(jnp.int32, (2 * CHUNK, PAIR_K), 1) // DK_SMALL)
    t_idx = lax.broadcasted_iota(jnp.int32, (CHUNK, 2 * CHUNK), 0)
    s_idx = lax.broadcasted_iota(jnp.int32, (CHUNK, 2 * CHUNK), 1) % CHUNK
    causal = (s_idx <= t_idx, s_idx >= t_idx)
    ones_v = jnp.ones((CHUNK, DV), BF16)
    zeros_vext = jnp.zeros((CHUNK, PAIR_V), BF16)
    ret_dec = jnp.exp(float(CHUNK) * _log_sigmoid(rdl[...]))

    decay_sets = (
        ((gq_f, gk_f, ge_f, gv_f), (gq_b, gk_b, ge_b, gv_b), sg_ref, 0),
        ((rq_f, rk_f, re_f, rv_f), (rq_b, rk_b, re_b, rv_b), sr_ref, 2 * BRANCH_W),
    )
    ml_sets = ((mq_f, mk_f, mw_f, mv_f, mp_f, mcol_f, mc_f),
               (mq_b, mk_b, mw_b, mv_b, mp_b, mcol_b, mc_b))
    outs = (of_ref, ob_ref)


    def decay_phase1(bb, d, c, rows, bi, p, refs):
        q_r, k_r, e_r, v_r = refs
        ks = slice(p * PAIR_K, (p + 1) * PAIR_K)
        q2 = q_r[bb, rows, ks]
        k2 = k_r[bb, rows, ks]
        v2 = v_r[bb, rows, p * PAIR_V:(p + 1) * PAIR_V]
        k_bd = jnp.where(diag_kk, jnp.concatenate([k2, k2], axis=0), 0.0)
        dec = (gd_f if d == 0 else gd_b)[bb, c, ks, :] if bi == 0 else ret_dec[d, ks, :]
        upd = jnp.where(diag_kv, _dot_tn(e_r[bb, rows, ks], v2), 0.0)
        return dict(q2=q2, v2=v2, dec=jnp.concatenate([dec, dec], axis=1), upd=upd,
                    att=_dot_nt(q2, k_bd))

    def decay_phase2(bb, d, p, rows, o_ref, s_ref, base, w):
        s_prev = s_ref[bb, d, p]
        att = jnp.where(causal[d], w["att"], 0.0).astype(BF16)
        v_bd = jnp.where(diag_kv, jnp.concatenate([w["v2"], w["v2"]], axis=0), 0.0)
        lhs = jnp.concatenate([att, w["q2"]], axis=1)
        rhs = jnp.concatenate([v_bd, s_prev.astype(BF16)], axis=0)
        o_ref[bb, rows, base + p * PAIR_V:base + (p + 1) * PAIR_V] = _dot(lhs, rhs)
        s_ref[bb, d, p] = w["dec"] * s_prev + w["upd"]

    def ml_phase1(bb, d, c, rows, p, refs):
        q_r, k_r, w_r, v_r, p_r, col_r, c_r = refs
        cols = col_r[bb, rows, :]
        q2 = q_r[bb, rows, p * 2 * DK_ML:(p + 1) * 2 * DK_ML]
        k2 = k_r[bb, rows, p * 2 * DK_ML:(p + 1) * 2 * DK_ML]
        k_bd = jnp.where(diag_kv, jnp.concatenate([k2, k2], axis=0), 0.0)
        att = _dot_nt(q2, k_bd)
        heads = []
        for hl in range(2):
            hd = 2 * p + hl
            jj = d * HEADS + hd
            vext = jnp.concatenate([v_r[bb, rows, hd * DV:(hd + 1) * DV], ones_v], axis=1)
            st = sm_ref[bb, d, hd]
            heads.append(dict(
                hd=hd, st=st, vext=vext,
                inter=_dot(q2[:, hl * DK_ML:(hl + 1) * DK_ML], st.astype(BF16)),
                upd=_dot_tn(w_r[bb, rows, hd * DK_ML:(hd + 1) * DK_ML], vext),
                u=jnp.broadcast_to(cols[:, MC_U + jj:MC_U + jj + 1], (CHUNK, LANES)),
                b=jnp.broadcast_to(cols[:, MC_B + jj:MC_B + jj + 1], (CHUNK, LANES))))
        return dict(att=att, heads=heads, p0=p_r[bb, p, rows, :], cs=c_r[bb, c])

    def ml_phase2(bb, d, rows, o_ref, w):
        sc = (w["att"] * w["p0"]).astype(BF16)
        for hl, hw in enumerate(w["heads"]):
            hd = hw["hd"]
            jj = d * HEADS + hd
            vz = (jnp.concatenate([hw["vext"], zeros_vext], axis=0) if hl == 0
                  else jnp.concatenate([zeros_vext, hw["vext"]], axis=0))
            intra = _dot(sc, vz)
            m_prev = m_ref[bb, jj:jj + 1, :]
            u = hw["u"]
            mx = jnp.maximum(u, m_prev)
            f = jnp.exp(u - mx)
            e = jnp.exp(m_prev - mx)
            num = f * intra[:, 0:DV] + e * hw["inter"][:, 0:DV]
            den = f * intra[:, DV:] + e * hw["inter"][:, DV:]
            den = jnp.maximum(jnp.abs(den), jnp.exp(-(hw["b"] + mx)))
            o_ref[bb, rows, BRANCH_W + hd * DV:BRANCH_W + (hd + 1) * DV] = num / den
            tot = w["cs"][jj:jj + 1, :]
            mloc = w["cs"][NHD + jj:NHD + jj + 1, :]
            m_new = jnp.maximum(tot + m_prev, mloc)
            keep = jnp.exp(tot + m_prev - m_new)
            gain = jnp.exp(mloc - m_new)
            sm_ref[bb, d, hd] = (jnp.concatenate([keep, keep], axis=1) * hw["st"]
                                 + jnp.concatenate([gain, gain], axis=1) * hw["upd"])
            m_ref[bb, jj:jj + 1, :] = m_new

    units = []
    for step in range(CHUNKS_PER_TILE):
        for bb, d in ((b_, d_) for b_ in range(batch) for d_ in range(2)):
            c = step if d == 0 else CHUNKS_PER_TILE - 1 - step
            rows = pl.ds(c * CHUNK, CHUNK)
            for bi, (fset, bset, s_ref, base) in enumerate(decay_sets):
                for p in range(PAIRS):
                    units.append((functools.partial(decay_phase1, bb, d, c, rows, bi, p, fset if d == 0 else bset),
                                  functools.partial(decay_phase2, bb, d, p, rows, outs[d], s_ref, base)))
            for p in range(PAIRS):
                units.append((functools.partial(ml_phase1, bb, d, c, rows, p, ml_sets[d]),
                              functools.partial(ml_phase2, bb, d, rows, outs[d])))

    assert MIX_LOOKAHEAD < len(units) // CHUNKS_PER_TILE
    pending = []
    for phase1, phase2 in units:
        pending.append((phase2, phase1()))
        if len(pending) > MIX_LOOKAHEAD:
            fn, w = pending.pop(0)
            fn(w)
    for fn, w in pending:
        fn(w)


def _mix(gla, ret, ml, ret_dl_col):
    g_qke, g_vs, gdec = gla
    r_qke, r_vs = ret
    m_qk, mw, m_vs, mp, mcol, mc = ml
    B, T, _ = g_vs.shape
    nt = T // TILE

    def tf(j):
        return j

    def tb(j):
        return jnp.where(j == 0, 0, nt - j)

    def pair(width, lane_block_b=0, lane_block_f=0):
        return [pl.BlockSpec((B, TILE, width), lambda j: (0, tf(j), lane_block_f)),
                pl.BlockSpec((B, TILE, width), lambda j: (0, tb(j), lane_block_b))]

    dec_pair = [pl.BlockSpec((B, CHUNKS_PER_TILE, NQK_S, LANES), lambda j: (0, tf(j), 0, 0)),
                pl.BlockSpec((B, CHUNKS_PER_TILE, NQK_S, LANES), lambda j: (0, tb(j), 1, 0))]
    p0_pair = [pl.BlockSpec((B, PAIRS, TILE, 2 * CHUNK), lambda j: (0, 0, tf(j), 0)),
               pl.BlockSpec((B, PAIRS, TILE, 2 * CHUNK), lambda j: (0, 1, tb(j), 0))]
    cs_pair = [pl.BlockSpec((B, CHUNKS_PER_TILE, 2 * NHD, LANES), lambda j: (0, tf(j), 0, 0)),
               pl.BlockSpec((B, CHUNKS_PER_TILE, 2 * NHD, LANES), lambda j: (0, tb(j), 0, 0))]
    decay_specs = pair(NQK_S, 1, 0) + pair(NQK_S, 3, 2) + pair(NQK_S, 5, 4) + pair(BRANCH_W)
    in_specs = (decay_specs + dec_pair + decay_specs
                + [pl.BlockSpec((2, NQK_S, LANES), lambda j: (0, 0, 0))]
                + pair(NQK_M) + pair(NQK_M, 1, 1) + pair(NQK_M, 1) + pair(BRANCH_W) + p0_pair
                + pair(MC_WIDTH) + cs_pair)
    args = (g_qke, g_qke, g_qke, g_qke, g_qke, g_qke, g_vs, g_vs, gdec, gdec,
            r_qke, r_qke, r_qke, r_qke, r_qke, r_qke, r_vs, r_vs, ret_dl_col,
            m_qk, m_qk, m_qk, m_qk, mw, mw, m_vs, m_vs, mp, mp, mcol, mcol, mc, mc)
    return pl.pallas_call(
        functools.partial(_mix_kernel, B),
        out_shape=(jax.ShapeDtypeStruct((B, T, 3 * BRANCH_W), F32),) * 2,
        grid=(nt,),
        in_specs=in_specs,
        out_specs=(pl.BlockSpec((B, TILE, 3 * BRANCH_W), lambda j: (0, tf(j), 0)),
                   pl.BlockSpec((B, TILE, 3 * BRANCH_W), lambda j: (0, tb(j), 0))),
        scratch_shapes=[pltpu.VMEM((B, 2, PAIRS, PAIR_K, PAIR_V), F32),
                        pltpu.VMEM((B, 2, PAIRS, PAIR_K, PAIR_V), F32),
                        pltpu.VMEM((B, 2, HEADS, DK_ML, 2 * DV), F32),
                        pltpu.VMEM((B, NHD, LANES), F32)],
        compiler_params=_cparams(1), name="mix",
    )(*args)


def _merge_kernel(batch, first_tile, x_ref, ctx_ref, mod_ref, g_ref, wg_ref, of_ref, ob_ref,
                  sgg_ref, sgm_ref, sgr_ref, ng_ref, wb_ref, wo_ref, out_ref):
    tile_idx = pl.program_id(0) + first_tile
    side = (sgg_ref, sgm_ref, sgr_ref)
    mods = [mod_ref[jnp.where(tile_idx == 0, batch, bb)] for bb in range(batch)]
    xs = [jnp.where(tile_idx == 0, ctx_ref[bb], x_ref[bb]) for bb in range(batch)]
    hs = [_norm_mod(xs[bb], g_ref[...], mods[bb][0:1, :], mods[bb][1:2, :]).astype(BF16)
          for bb in range(batch)]
    accs = [jnp.zeros((TILE, D_MODEL), F32) for _ in range(batch)]
    for br in range(3):
        for bb in range(batch):
            parts = []
            for hd in range(HEADS):
                cols = slice(br * BRANCH_W + hd * DV, br * BRANCH_W + (hd + 1) * DV)
                oh = of_ref[bb, :, cols] + ob_ref[bb, :, cols]
                if br > 0:
                    oh = oh - jnp.mean(oh, axis=-1, keepdims=True)
                parts.append(oh * lax.rsqrt(jnp.mean(oh * oh, axis=-1, keepdims=True) + EPS))
            y = jnp.concatenate(parts, axis=1) * ng_ref[:, br * BRANCH_W:(br + 1) * BRANCH_W]
            y = (y * side[br][bb]).astype(BF16)
            gate = _sigmoid(_dot(hs[bb], wg_ref[0, :, br * D_MODEL:(br + 1) * D_MODEL]))
            accs[bb] = accs[bb] + gate * _dot(y, wb_ref[0, br])
    for bb in range(batch):
        out = _dot(accs[bb].astype(BF16), wo_ref[0])
        out_ref[bb] = xs[bb] + mods[bb][2:3, :] * out


def _merge(ctx_src, x_src, off, mods, g1, wgate, o_f, o_b, sg_gla, sz_ml, sg_ret, norm_g, wb, wo, layer, skip):
    B = x_src.shape[0]
    nt = x_src.shape[1] // TILE + off - skip
    tile = lambda width: pl.BlockSpec((B, TILE, width), lambda i: (0, i + skip, 0))
    gate_tile = pl.BlockSpec((B, TILE, BRANCH_W), lambda i: (0, i + skip, 1))
    const = lambda shape: pl.BlockSpec(shape, lambda i: (0,) * len(shape))
    per_layer = lambda shape: pl.BlockSpec((1,) + shape, lambda i: (layer,) + (0,) * len(shape),
                                           pipeline_mode=pl.Buffered(1))
    return pl.pallas_call(
        functools.partial(_merge_kernel, B, skip),
        out_shape=jax.ShapeDtypeStruct((B, nt * TILE, D_MODEL), F32),
        grid=(nt,),
        in_specs=[pl.BlockSpec((B, TILE, D_MODEL), lambda i: (0, jnp.maximum(i + skip - off, 0), 0)),
                  pl.BlockSpec((B, TILE, D_MODEL), lambda i: (0, 0, 0)),
                  const((B + 1, 6, D_MODEL)),
                  const((1, D_MODEL)), per_layer((D_MODEL, W_GATE)),
                  tile(3 * BRANCH_W), tile(3 * BRANCH_W),
                  gate_tile, gate_tile, gate_tile,
                  const((1, 3 * BRANCH_W)), per_layer((3, BRANCH_W, D_MODEL)),
                  per_layer((D_MODEL, D_MODEL))],
        out_specs=pl.BlockSpec((B, TILE, D_MODEL), lambda i: (0, i, 0)),
        compiler_params=_cparams(1), name="merge",
    )(x_src, ctx_src, mods, g1, wgate, o_f, o_b, sg_gla, sz_ml, sg_ret, norm_g, wb, wo)


def _mlp_kernel(batch, first_tile, final, x_ref, mod_ref, g_ref, wu_ref, wd_ref, fg_ref, out_ref):
    tile_idx = pl.program_id(0) + first_tile
    mods = [mod_ref[jnp.where(tile_idx == 0, batch, bb)] for bb in range(batch)]
    ups = []
    for bb in range(batch):
        h = _norm_mod(x_ref[bb], g_ref[...], mods[bb][3:4, :], mods[bb][4:5, :]).astype(BF16)
        ups.append(_dot(h, wu_ref[0]))
    for bb in range(batch):
        u = jnp.maximum(ups[bb], 0.0)
        y = x_ref[bb] + mods[bb][5:6, :] * _dot((u * u).astype(BF16), wd_ref[0])
        if final:
            y = y * lax.rsqrt(jnp.mean(y * y, axis=-1, keepdims=True) + EPS) * fg_ref[...]
        out_ref[bb] = y


def _mlp(tok, mods, g2, wu, wd, layer, final_g, first_tile, final):
    B, T, _ = tok.shape
    const = lambda shape: pl.BlockSpec(shape, lambda i: (0,) * len(shape))
    per_layer = lambda shape: pl.BlockSpec((1,) + shape, lambda i: (layer,) + (0,) * len(shape),
                                           pipeline_mode=pl.Buffered(1))
    return pl.pallas_call(
        functools.partial(_mlp_kernel, B, first_tile, final),
        out_shape=jax.ShapeDtypeStruct((B, T, D_MODEL), F32),
        grid=(T // TILE,),
        in_specs=[pl.BlockSpec((B, TILE, D_MODEL), lambda i: (0, i, 0)),
                  const((B + 1, 6, D_MODEL)),
                  const((1, D_MODEL)), per_layer((D_MODEL, D_FF)), per_layer((D_FF, D_MODEL)),
                  const((1, D_MODEL))],
        out_specs=pl.BlockSpec((B, TILE, D_MODEL), lambda i: (0, i, 0)),
        compiler_params=_cparams(1), name="mlp",
    )(tok, mods, g2, wu, wd, final_g)


def kernel(x, c, ctx, c_ctx, norm1_g, norm2_g, w_ada, b_ada, w_in, gla_a2, gla_a_bias, gla_norm_g,
           ml_conv, ml_i_bias, ml_f_bias, ml_norm_g, ret_decay_logit, ret_norm_g, w_branch, w_o,
           w_up, w_down, final_g):
    B, seq, _ = x.shape
    ctx_len = ctx.shape[1]
    assert B == 2 and ctx_len == TILE and seq % TILE == 0 and seq % GRID_W == 0
    ctx_src, x_src, off = ctx, x, 1

    c_rows = jnp.concatenate([c, c_ctx[None, :], jnp.zeros((8 - B - 1, D_MODEL), F32)], axis=0)
    mods_all = _ada(c_rows, w_ada, b_ada)[:, 0:3].reshape(DEPTH, 3, 6, D_MODEL)

    cos_t, sin_t = _rope_tables(seq, ctx_len)

    w_proj, w_gate = _wprep(w_in)
    w_branch_b = w_branch.astype(BF16)
    w_o_b = w_o.astype(BF16)
    w_up_b = w_up.astype(BF16)
    w_down_b = w_down.astype(BF16)
    for l in range(DEPTH):
        last = l == DEPTH - 1
        skip = 1 if last else 0
        mods = mods_all[l]
        g1 = norm1_g[l][None, :]
        g2 = norm2_g[l][None, :]
        zeros = jnp.zeros((GLA_RANK, NQK_S), F32)
        a2blk = jnp.concatenate([jnp.concatenate([gla_a2[l, 0], zeros], axis=1),
                                 jnp.concatenate([zeros, gla_a2[l, 1]], axis=1)], axis=0)
        a2s = jnp.concatenate([a2blk, a2blk], axis=0).astype(BF16)
        abias = gla_a_bias[l].reshape(1, 2 * NQK_S)
        gate_bias = jnp.concatenate([ml_i_bias[l].reshape(1, NHD), ml_f_bias[l].reshape(1, NHD)], axis=1)
        dl = ret_decay_logit[l]
        dl_row = jnp.repeat(dl, DK_SMALL, axis=1).reshape(1, 2 * NQK_S)
        dl_col = jnp.broadcast_to(jnp.repeat(dl, DK_SMALL, axis=1)[:, :, None], (2, NQK_S, LANES))

        (g_qke, g_vs, gdec, m_qk, mw, m_vs, mp, mcol, mc, r_qke, r_vs) = _proj(
            ctx_src, x_src, off, mods, g1, w_proj, l, a2s, abias, ml_conv[l], gate_bias, cos_t, sin_t, dl_row)
        o_f, o_b = _mix((g_qke, g_vs, gdec), (r_qke, r_vs), (m_qk, mw, m_vs, mp, mcol, mc), dl_col)
        norm_g = jnp.concatenate([gla_norm_g[l], ml_norm_g[l], ret_norm_g[l]])[None, :]
        tok = _merge(ctx_src, x_src, off, mods, g1, w_gate, o_f, o_b, g_vs, m_vs, r_vs, norm_g,
                     w_branch_b, w_o_b, l, skip)
        tok = _mlp(tok, mods, g2, w_up_b, w_down_b, l, final_g[None, :], skip, last)
        ctx_src, x_src, off = tok, tok, 0
    return tok
```

```python
import functools

import jax
import jax.numpy as jnp
from jax import lax
from jax.experimental import pallas as pl
from jax.experimental.pallas import tpu as pltpu

F32 = jnp.float32
BF16 = jnp.bfloat16

D_MODEL = 1024
DEPTH = 2
GRID_W = 64
CHUNK = 64
EPS = 1e-6
M_INIT = -1e30
HEADS = 4
DK_SMALL = 64
DK_ML = 128
DV = 128
BRANCH_W = HEADS * DV
GLA_RANK = 16
GLA_TAU = 16.0
ROPE_BASE = 10000.0
D_FF = 4 * D_MODEL
LANES = 128

TILE = 256
CHUNKS_PER_TILE = TILE // CHUNK
HALO = 8
VMEM_LIMIT = 56 * 1024 * 1024
VMEM_LIMIT_PROJ = 60 * 1024 * 1024

NQK_S = HEADS * DK_SMALL
NQK_M = HEADS * DK_ML
NHD = 2 * HEADS
W_GLA = 2 * NQK_S + 2 * BRANCH_W + 2 * GLA_RANK
W_ML = 2 * NQK_M + 2 * BRANCH_W + 2 * NHD
W_RET = 2 * NQK_S + 2 * BRANCH_W
W_GATE = 3 * D_MODEL

PW_MLQK = 0
PW_MLR = PW_MLQK + 2 * NQK_M
PW_MLR_WIDTH = 2 * BRANCH_W + LANES
PW_GR = PW_MLR + PW_MLR_WIDTH
PW_GR_WIDTH = 2 * (2 * NQK_S + 2 * BRANCH_W) + LANES
W_PROJ = PW_GR + PW_GR_WIDTH
MLR_V, MLR_Z, MLR_GATES = 0, BRANCH_W, 2 * BRANCH_W
GR_GLA_Q, GR_GLA_K, GR_GLA_V, GR_GLA_G = 0, NQK_S, 2 * NQK_S, 2 * NQK_S + BRANCH_W
GR_RET_Q = 2 * NQK_S + 2 * BRANCH_W
GR_RET_K, GR_RET_V, GR_RET_G = GR_RET_Q + NQK_S, GR_RET_Q + 2 * NQK_S, GR_RET_Q + 2 * NQK_S + BRANCH_W
GR_GLA_A = 2 * GR_RET_Q

MC_B, MC_W, MC_U = 0, NHD, 2 * NHD
MC_WIDTH = 3 * NHD

PAIRS = HEADS // 2
PAIR_K = 2 * DK_SMALL
PAIR_V = 2 * DV
MIX_LOOKAHEAD = 23


def _cparams(n_axes, vmem_limit=VMEM_LIMIT):
    return pltpu.CompilerParams(dimension_semantics=("arbitrary",) * n_axes,
                                vmem_limit_bytes=vmem_limit)


def _log_sigmoid(z):
    return jnp.minimum(z, 0.0) - jnp.log(1.0 + jnp.exp(-jnp.abs(z)))


def _sigmoid(z):
    return 1.0 / (1.0 + jnp.exp(-z))


def _silu(z):
    return z * _sigmoid(z)


def _norm_mod(x, g, shift, scale):
    y = x * lax.rsqrt(jnp.mean(x * x, axis=-1, keepdims=True) + EPS) * g
    return y * (1.0 + scale) + shift


def _dot(a, b):
    return jnp.dot(a, b, preferred_element_type=F32)


def _dot_nt(a, b):
    return lax.dot_general(a, b, (((1,), (1,)), ((), ())), preferred_element_type=F32)


def _dot_tn(a, b):
    return lax.dot_general(a, b, (((0,), (0,)), ((), ())), preferred_element_type=F32)


def _split2(x):
    hi = x.astype(BF16)
    return hi, (x - hi.astype(F32)).astype(BF16)


def _chunk_tri(lower):
    r = lax.broadcasted_iota(jnp.int32, (TILE, TILE), 0)
    c = lax.broadcasted_iota(jnp.int32, (TILE, TILE), 1)
    same = (r // CHUNK) == (c // CHUNK)
    tri = (c <= r) if lower else (c >= r)
    return jnp.where(same & tri, 1.0, 0.0).astype(BF16)


def _chunk_total(x):
    w = x.shape[-1]
    tot = jnp.sum(x.reshape(CHUNKS_PER_TILE, CHUNK, w), axis=1, keepdims=True)
    return jnp.broadcast_to(tot, (CHUNKS_PER_TILE, CHUNK, w)).reshape(TILE, w)


def _ada_kernel(c_ref, w_ref, b_ref, o_ref):
    s = _silu(c_ref[...])
    s_hi = s.astype(BF16).astype(F32)
    s2 = jnp.concatenate([s_hi, s - s_hi], axis=0).astype(BF16)
    w_hi, w_lo = _split2(w_ref[0])
    r = _dot(s2, w_hi) + _dot(s2, w_lo)
    o_ref[0] = r[0:8] + r[8:16] + b_ref[0]


def _ada(c_rows, w_ada, b_ada):
    tn = 1536
    return pl.pallas_call(
        _ada_kernel,
        out_shape=jax.ShapeDtypeStruct((DEPTH, 8, 6 * D_MODEL), F32),
        grid=(DEPTH, 6 * D_MODEL // tn),
        in_specs=[pl.BlockSpec((8, D_MODEL), lambda l, n: (0, 0)),
                  pl.BlockSpec((1, D_MODEL, tn), lambda l, n: (l, 0, n)),
                  pl.BlockSpec((1, 1, tn), lambda l, n: (l, 0, n))],
        out_specs=pl.BlockSpec((1, 8, tn), lambda l, n: (l, 0, n)),
        compiler_params=_cparams(2), name="ada",
    )(c_rows, w_ada, b_ada.reshape(DEPTH, 1, 6 * D_MODEL))


def _rope_kernel(ang_ref, cos_ref, sin_ref, nsin_ref):
    a = ang_ref[...]
    s = jnp.sin(a)
    cos_ref[...] = jnp.cos(a)
    sin_ref[...] = s
    nsin_ref[...] = -s


def _rope_tables(seq, ctx_len):
    rows = seq // GRID_W
    nf = DK_SMALL // 4
    inv = ROPE_BASE ** (-jnp.arange(nf, dtype=F32) / nf)
    ang = jnp.concatenate([jnp.arange(rows, dtype=F32)[:, None] * inv,
                           jnp.arange(GRID_W, dtype=F32)[:, None] * inv], axis=0)
    n = rows + GRID_W
    cos, sin, nsin = pl.pallas_call(
        _rope_kernel,
        out_shape=(jax.ShapeDtypeStruct((n, nf), F32),) * 3,
        name="rope_tables",
    )(ang)

    def per_token(first, second):
        def half(t):
            r = jnp.broadcast_to(t[:rows, None, :], (rows, GRID_W, nf))
            c = jnp.broadcast_to(t[None, rows:, :], (rows, GRID_W, nf))
            return jnp.concatenate([r, c], axis=-1)
        head = jnp.concatenate([half(first), half(second)], axis=-1).reshape(seq, DK_SMALL)
        return jnp.tile(head, (1, HEADS))

    ones = jnp.ones((ctx_len, NQK_S), F32)
    cos_t = jnp.concatenate([ones, per_token(cos, cos)], axis=0)
    sin_t = jnp.concatenate([0.0 * ones, per_token(nsin, sin)], axis=0)
    return cos_t, sin_t


def _wprep_kernel(w_ref, wp_ref, wg_ref):
    o0, o1, o2 = W_GLA, W_GLA + W_ML, W_GLA + W_ML + W_RET
    n_sg = 2 * NQK_S + 2 * BRANCH_W
    rows = w_ref.shape[1]

    def copy(dst_ref, dst, src, n):
        dst_ref[0, :, dst:dst + n] = w_ref[0, :, src:src + n].astype(BF16)

    copy(wp_ref, PW_MLQK, o0, 2 * NQK_M)
    n_mlr = 2 * BRANCH_W + 2 * NHD
    copy(wp_ref, PW_MLR, o0 + 2 * NQK_M, n_mlr)
    wp_ref[0, :, PW_MLR + n_mlr:PW_GR] = jnp.zeros((rows, PW_GR - PW_MLR - n_mlr), BF16)
    copy(wp_ref, PW_GR + GR_GLA_Q, 0, n_sg)
    copy(wp_ref, PW_GR + GR_RET_Q, o1, W_RET)
    copy(wp_ref, PW_GR + GR_GLA_A, n_sg, 2 * GLA_RANK)
    pad0 = PW_GR + GR_GLA_A + 2 * GLA_RANK
    wp_ref[0, :, pad0:W_PROJ] = jnp.zeros((rows, W_PROJ - pad0), BF16)
    copy(wg_ref, 0, o2, W_GATE)


def _wprep(w_in):
    rows = 256
    return pl.pallas_call(
        _wprep_kernel,
        out_shape=(jax.ShapeDtypeStruct((DEPTH, D_MODEL, W_PROJ), BF16),
                   jax.ShapeDtypeStruct((DEPTH, D_MODEL, W_GATE), BF16)),
        grid=(DEPTH, D_MODEL // rows),
        in_specs=[pl.BlockSpec((1, rows, w_in.shape[2]), lambda l, r: (l, r, 0))],
        out_specs=(pl.BlockSpec((1, rows, W_PROJ), lambda l, r: (l, r, 0)),
                   pl.BlockSpec((1, rows, W_GATE), lambda l, r: (l, r, 0))),
        compiler_params=_cparams(2), name="wprep",
    )(w_in)


def _proj_kernel(nt, xa_ref, xap_ref, xan_ref, xb_ref, xbp_ref, xbn_ref, ctx_ref,
                 mod_ref, g_ref, w_ref, a2_ref, ab_ref, cw_ref, gb_ref, cos_ref, sin_ref, dl_ref,
                 gqke_ref, gvs_ref, gdec_ref, mqk_ref, mkw_ref, mvs_ref, mp0_ref, mcol_ref, mcs_ref,
                 rqke_ref, rvs_ref,
                 ext0_ref, ext1_ref, prml0_ref, prml1_ref, prgr0_ref, prgr1_ref, rdec_ref, tri_ref,
                 h0_ref, h1_ref):
    i = pl.program_id(0)
    ext_refs = (ext0_ref, ext1_ref)
    prml_refs = (prml0_ref, prml1_ref)
    prgr_refs = (prgr0_ref, prgr1_ref)
    h_refs = (h0_ref, h1_ref)
    row8 = lax.broadcasted_iota(jnp.int32, (NHD, TILE), 0)
    lane8 = lax.broadcasted_iota(jnp.int32, (NHD, TILE), 1)
    lane_chunk = lane8 // CHUNK
    lane_in_chunk = lane8 % CHUNK
    ti = lax.broadcasted_iota(jnp.int32, (CHUNK, CHUNK), 0)
    si = lax.broadcasted_iota(jnp.int32, (CHUNK, CHUNK), 1)
    ext_row = lax.broadcasted_iota(jnp.int32, (TILE + 2 * HALO, 1), 0)

    def project_stages(x, x_prev, x_next, tile_idx, mod_row, slot):
        st = {}

        def mlstm_qk():
            mod = mod_ref[mod_row]
            x_all = jnp.concatenate([x_prev, x, x_next], axis=0)
            h_all = _norm_mod(x_all, g_ref[...], mod[0:1, :], mod[1:2, :])
            h_refs[slot][...] = h_all[HALO:HALO + TILE, :].astype(BF16)
            prev_ok = tile_idx >= 2
            next_ok = jnp.logical_and(tile_idx >= 1, tile_idx <= nt - 2)
            dead = jnp.logical_or(jnp.logical_and(ext_row < HALO, jnp.logical_not(prev_ok)),
                                  jnp.logical_and(ext_row >= TILE + HALO, jnp.logical_not(next_ok)))
            ext = _dot(h_all.astype(BF16), w_ref[0, :, PW_MLQK:PW_MLR])
            ext_refs[slot][...] = jnp.where(dead, 0.0, ext)

        def mlstm_rest():
            prml_refs[slot][...] = _dot(h_refs[slot][...], w_ref[0, :, PW_MLR:PW_GR])

        def gla():
            prgr_refs[slot][:, 0:GR_RET_Q] = _dot(h_refs[slot][...], w_ref[0, :, PW_GR:PW_GR + GR_RET_Q])

        def ret():
            prgr_refs[slot][:, GR_RET_Q:] = _dot(h_refs[slot][...], w_ref[0, :, PW_GR + GR_RET_Q:])

        return [mlstm_qk, mlstm_rest, gla, ret]

    def project(*args):
        for stage in project_stages(*args):
            stage()

    @pl.when(i == 0)
    def _():
        tri_ref[0] = _chunk_tri(True)
        tri_ref[1] = _chunk_tri(False)
        lg = _log_sigmoid(dl_ref[...])
        t_in_chunk = (lax.broadcasted_iota(jnp.int32, (TILE, NQK_S), 0) % CHUNK).astype(F32)
        for d in range(2):
            lg_d = lg[:, d * NQK_S:(d + 1) * NQK_S]
            steps = (t_in_chunk + 1.0) if d == 0 else (float(CHUNK) - t_in_chunk)
            b = steps * lg_d
            rdec_ref[3 * d] = jnp.exp(b)
            rdec_ref[3 * d + 1] = jnp.exp(-b)
            rdec_ref[3 * d + 2] = jnp.exp(float(CHUNK) * lg_d - b)
        halo0 = jnp.zeros((HALO, D_MODEL), F32)
        project(ctx_ref[0], halo0, halo0, 0, 2, 0)

    def rope(x):
        lane = lax.broadcasted_iota(jnp.int32, (TILE, NQK_S), 1)
        first_half = (lane % DK_SMALL) < (DK_SMALL // 2)
        swapped = jnp.where(first_half, pltpu.roll(x, NQK_S - DK_SMALL // 2, 1),
                            pltpu.roll(x, DK_SMALL // 2, 1))
        return x * cos_ref[...] + swapped * sin_ref[...]

    def seg_cummax(x, reverse):
        sh = 1
        while sh < CHUNK:
            if reverse:
                y, ok = pltpu.roll(x, TILE - sh, 1), lane_in_chunk < CHUNK - sh
            else:
                y, ok = pltpu.roll(x, sh, 1), lane_in_chunk >= sh
            x = jnp.where(ok, jnp.maximum(x, y), x)
            sh *= 2
        return x

    def prepare_stages(slot, bb):
        ext_ref, prml_ref, prgr_ref = ext_refs[slot], prml_refs[slot], prgr_refs[slot]
        st = {}

        def conv():
            ext = ext_ref[...]
            n_ext = TILE + 2 * HALO
            prev = pltpu.roll(ext, 1, 0)[HALO:HALO + TILE, :]
            nxt = pltpu.roll(ext, n_ext - 1, 0)[HALO:HALO + TILE, :]
            qk = _silu(cw_ref[0:1, :] * prev + cw_ref[1:2, :] * ext[HALO:HALO + TILE, :]
                       + cw_ref[2:3, :] * nxt)
            mqk_ref[bb, :, 0:NQK_M] = qk[:, 0:NQK_M].astype(BF16)
            st["k_ml"] = qk[:, NQK_M:] * (DK_ML ** -0.5)
            mqk_ref[bb, :, NQK_M:] = st["k_ml"].astype(BF16)

        def gates_in():
            mvs_ref[bb, :, 0:BRANCH_W] = prml_ref[:,MLR_V:MLR_V + BRANCH_W].astype(BF16)
            mvs_ref[bb, :, BRANCH_W:] = _silu(prml_ref[:,MLR_Z:MLR_Z + BRANCH_W]).astype(BF16)
            gates_t = (prml_ref[:,MLR_GATES:MLR_GATES + 2 * NHD] + gb_ref[...]).T
            st["li_t"] = gates_t[0:NHD]
            st["lf_t"] = _log_sigmoid(gates_t[NHD:])
            lf_hi, lf_lo = _split2(st["lf_t"])
            st["lf2"] = jnp.concatenate([lf_hi, lf_lo], axis=0)

        def gates_mm():
            st["cum_f"] = _dot(st["lf2"], tri_ref[1])
            st["cum_b"] = _dot(st["lf2"], tri_ref[0])

        def gates_out():
            mlstm_gates_body(st["li_t"], st["lf_t"], st["cum_f"], st["cum_b"], st["k_ml"], bb)

        def gla_in():
            gvs_ref[bb, :, 0:BRANCH_W] = prgr_ref[:,GR_GLA_V:GR_GLA_V + BRANCH_W].astype(BF16)
            gvs_ref[bb, :, BRANCH_W:] = _silu(prgr_ref[:,GR_GLA_G:GR_GLA_G + BRANCH_W]).astype(BF16)
            a_hi, a_lo = _split2(prgr_ref[:,GR_GLA_A:GR_GLA_A + 2 * GLA_RANK])
            st["a2"] = jnp.concatenate([a_hi, a_lo], axis=1)

        def gla_z_mm():
            st["z"] = _dot(st["a2"], a2_ref[...])

        def gla_la():
            la = _log_sigmoid(st["z"] + ab_ref[...]) * (1.0 / GLA_TAU)
            st["la"] = la
            st["la_split"] = [_split2(la[:, d * NQK_S:(d + 1) * NQK_S]) for d in range(2)]

        def gla_cum_mm():
            st["b"] = []
            for d in range(2):
                tri = tri_ref[d]
                la_hi, la_lo = st["la_split"][d]
                st["b"].append(_dot(tri, la_hi) + _dot(tri, la_lo))

        def gla_out():
            gla_body(prgr_ref, st["la"], st["b"], bb)

        def ret():
            ret_body(prgr_ref, bb)

        return dict(conv=conv, gates_in=gates_in, gates_mm=gates_mm, gates_out=gates_out,
                    gla_in=gla_in, gla_z_mm=gla_z_mm, gla_la=gla_la, gla_cum_mm=gla_cum_mm,
                    gla_out=gla_out, ret=ret)

    def mlstm_gates_body(li_t, lf_t, cum_f, cum_b, k_ml, bb):
        b_t = jnp.where(row8 < HEADS, cum_f[0:NHD] + cum_f[NHD:], cum_b[0:NHD] + cum_b[NHD:])
        tot_t = jnp.zeros((NHD, TILE), F32)
        mloc_t = jnp.zeros((NHD, TILE), F32)
        for c in range(CHUNKS_PER_TILE):
            tot_c = jnp.sum(lf_t[:, c * CHUNK:(c + 1) * CHUNK], axis=-1, keepdims=True)
            tot_t = jnp.where(lane_chunk == c, tot_c, tot_t)
            mcs_ref[bb, c, 0:NHD, :] = jnp.broadcast_to(tot_c, (NHD, LANES))
        g_t = tot_t - b_t + li_t
        for c in range(CHUNKS_PER_TILE):
            mloc_c = jnp.max(g_t[:, c * CHUNK:(c + 1) * CHUNK], axis=-1, keepdims=True)
            mloc_t = jnp.where(lane_chunk == c, mloc_c, mloc_t)
            mcs_ref[bb, c, NHD:2 * NHD, :] = jnp.broadcast_to(mloc_c, (NHD, LANES))
        w_t = jnp.exp(g_t - mloc_t)
        r_t = li_t - b_t
        u_t = jnp.where(row8 < HEADS, seg_cummax(r_t, False), seg_cummax(r_t, True))
        cols = jnp.concatenate([b_t, w_t, u_t], axis=0).T
        mcol_ref[bb] = cols
        for j in range(NHD):
            d, hd = j // HEADS, j % HEADS
            mask = (si <= ti) if d == 0 else (si >= ti)
            mkw_ref[bb, :, d * NQK_M + hd * DK_ML:d * NQK_M + (hd + 1) * DK_ML] = (
                k_ml[:, hd * DK_ML:(hd + 1) * DK_ML] * cols[:, MC_W + j:MC_W + j + 1]).astype(BF16)
            for c in range(CHUNKS_PER_TILE):
                rows = slice(c * CHUNK, (c + 1) * CHUNK)
                a = r_t[j:j + 1, rows] - cols[rows, MC_U + j:MC_U + j + 1]
                mp0_ref[bb, j // 2, rows, (j % 2) * CHUNK:(j % 2 + 1) * CHUNK] = jnp.where(mask, jnp.exp(a), 0.0)

    def gla_body(prgr_ref, la, cums, bb):
        half_t = jnp.sum(la.reshape(2 * CHUNKS_PER_TILE, CHUNK // 2, 2 * NQK_S), axis=1).T
        q_gla = prgr_ref[:,GR_GLA_Q:GR_GLA_Q + NQK_S] * (DK_SMALL ** -0.5)
        k_gla = prgr_ref[:,GR_GLA_K:GR_GLA_K + NQK_S]
        for d in range(2):
            la_d = la[:, d * NQK_S:(d + 1) * NQK_S]
            b = cums[d]
            tot = _chunk_total(la_d)
            gqke_ref[bb, :, d * NQK_S:(d + 1) * NQK_S] = (q_gla * jnp.exp(b)).astype(BF16)
            gqke_ref[bb, :, (2 + d) * NQK_S:(3 + d) * NQK_S] = (k_gla * jnp.exp(-b)).astype(BF16)
            gqke_ref[bb, :, (4 + d) * NQK_S:(5 + d) * NQK_S] = (k_gla * jnp.exp(tot - b)).astype(BF16)
            ht = half_t[d * NQK_S:(d + 1) * NQK_S, :]
            for c in range(CHUNKS_PER_TILE):
                tot_col = ht[:, 2 * c:2 * c + 1] + ht[:, 2 * c + 1:2 * c + 2]
                gdec_ref[bb, c, d * NQK_S:(d + 1) * NQK_S, :] = jnp.broadcast_to(jnp.exp(tot_col), (NQK_S, LANES))

    def ret_body(prgr_ref, bb):
        q_ret = rope(prgr_ref[:,GR_RET_Q:GR_RET_Q + NQK_S])
        k_ret = rope(prgr_ref[:,GR_RET_K:GR_RET_K + NQK_S] * (DK_SMALL ** -0.5))
        rvs_ref[bb, :, 0:BRANCH_W] = prgr_ref[:,GR_RET_V:GR_RET_V + BRANCH_W].astype(BF16)
        rvs_ref[bb, :, BRANCH_W:] = _silu(prgr_ref[:,GR_RET_G:GR_RET_G + BRANCH_W]).astype(BF16)
        for d in range(2):
            rqke_ref[bb, :, d * NQK_S:(d + 1) * NQK_S] = (q_ret * rdec_ref[3 * d]).astype(BF16)
            rqke_ref[bb, :, (2 + d) * NQK_S:(3 + d) * NQK_S] = (k_ret * rdec_ref[3 * d + 1]).astype(BF16)
            rqke_ref[bb, :, (4 + d) * NQK_S:(5 + d) * NQK_S] = (k_ret * rdec_ref[3 * d + 2]).astype(BF16)

    next_tile = jnp.minimum(i + 1, nt - 1)
    xa = jnp.where(i == 0, ctx_ref[1], xa_ref[0])
    a_ext, a_mlr, a_gla, a_ret = project_stages(xa, xap_ref[0], xan_ref[0], i, jnp.where(i == 0, 2, 1), 1)
    b_ext, b_mlr, b_gla, b_ret = project_stages(xb_ref[0], xbp_ref[0], xbn_ref[0], next_tile, 0, 0)
    x = prepare_stages(0, 0)
    y = prepare_stages(1, 1)
    order = [a_ext, x["conv"], x["gates_in"],
             a_mlr, x["gates_mm"], x["gates_out"], x["gla_in"],
             a_gla, x["gla_z_mm"], x["gla_la"],
             a_ret, x["gla_cum_mm"],
             b_ext, x["gla_out"], x["ret"], y["conv"], y["gates_in"],
             b_mlr, y["gates_mm"], y["gates_out"], y["gla_in"],
             b_gla, y["gla_z_mm"], y["gla_la"],
             b_ret, y["gla_cum_mm"], y["gla_out"], y["ret"]]
    for stage in order:
        stage()


def _proj(ctx_src, x_src, off, mods, g1, w, layer, a2s, abias, conv_w, gate_bias, cos_t, sin_t, dl_row):
    B = x_src.shape[0]
    nt = x_src.shape[1] // TILE + off
    T = nt * TILE
    tile = lambda width: pl.BlockSpec((B, TILE, width), lambda i: (0, i, 0))
    const = lambda shape: pl.BlockSpec(shape, lambda i: (0,) * len(shape))
    per_halo = TILE // HALO
    n_halo = x_src.shape[1] // HALO
    nxt = lambda i: jnp.minimum(i + 1, nt - 1)
    blk = lambda t: jnp.maximum(t - off, 0)
    prev_halo = lambda t: jnp.maximum((t - off) * per_halo - 1, 0)
    next_halo = lambda t: jnp.clip((t - off + 1) * per_halo, 0, n_halo - 1)
    tok_shape = lambda width, dt: jax.ShapeDtypeStruct((B, T, width), dt)
    return pl.pallas_call(
        functools.partial(_proj_kernel, nt),
        out_shape=(tok_shape(6 * NQK_S, BF16), tok_shape(2 * BRANCH_W, BF16),
                   jax.ShapeDtypeStruct((B, T // CHUNK, 2 * NQK_S, LANES), F32),
                   tok_shape(2 * NQK_M, BF16), tok_shape(2 * NQK_M, BF16), tok_shape(2 * BRANCH_W, BF16),
                   jax.ShapeDtypeStruct((B, 2 * PAIRS, T, 2 * CHUNK), F32),
                   tok_shape(MC_WIDTH, F32),
                   jax.ShapeDtypeStruct((B, T // CHUNK, 2 * NHD, LANES), F32),
                   tok_shape(6 * NQK_S, BF16), tok_shape(2 * BRANCH_W, BF16)),
        grid=(nt,),
        in_specs=[pl.BlockSpec((1, TILE, D_MODEL), lambda i: (1, blk(i), 0)),
                  pl.BlockSpec((1, HALO, D_MODEL), lambda i: (1, prev_halo(i), 0)),
                  pl.BlockSpec((1, HALO, D_MODEL), lambda i: (1, next_halo(i), 0)),
                  pl.BlockSpec((1, TILE, D_MODEL), lambda i: (0, blk(nxt(i)), 0)),
                  pl.BlockSpec((1, HALO, D_MODEL), lambda i: (0, prev_halo(nxt(i)), 0)),
                  pl.BlockSpec((1, HALO, D_MODEL), lambda i: (0, next_halo(nxt(i)), 0)),
                  pl.BlockSpec((B, TILE, D_MODEL), lambda i: (0, 0, 0)),
                  const((B + 1, 6, D_MODEL)), const((1, D_MODEL)),
                  pl.BlockSpec((1, D_MODEL, W_PROJ), lambda i: (layer, 0, 0), pipeline_mode=pl.Buffered(1)),
                  const((4 * GLA_RANK, 2 * NQK_S)), const((1, 2 * NQK_S)),
                  const((3, 2 * NQK_M)), const((1, 2 * NHD)),
                  pl.BlockSpec((TILE, NQK_S), lambda i: (i, 0)),
                  pl.BlockSpec((TILE, NQK_S), lambda i: (i, 0)),
                  const((1, 2 * NQK_S))],
        out_specs=(tile(6 * NQK_S), tile(2 * BRANCH_W),
                   pl.BlockSpec((B, CHUNKS_PER_TILE, 2 * NQK_S, LANES), lambda i: (0, i, 0, 0)),
                   tile(2 * NQK_M), tile(2 * NQK_M), tile(2 * BRANCH_W),
                   pl.BlockSpec((B, 2 * PAIRS, TILE, 2 * CHUNK), lambda i: (0, 0, i, 0)),
                   tile(MC_WIDTH),
                   pl.BlockSpec((B, CHUNKS_PER_TILE, 2 * NHD, LANES), lambda i: (0, i, 0, 0)),
                   tile(6 * NQK_S), tile(2 * BRANCH_W)),
        scratch_shapes=[pltpu.VMEM((TILE + 2 * HALO, 2 * NQK_M), F32)] * 2
                       + [pltpu.VMEM((TILE, PW_MLR_WIDTH), F32)] * 2
                       + [pltpu.VMEM((TILE, PW_GR_WIDTH), F32)] * 2
                       + [pltpu.VMEM((6, TILE, NQK_S), F32), pltpu.VMEM((2, TILE, TILE), BF16)]
                       + [pltpu.VMEM((TILE, D_MODEL), BF16)] * 2,
        compiler_params=_cparams(1, VMEM_LIMIT_PROJ), name="proj",
    )(x_src, x_src, x_src, x_src, x_src, x_src, ctx_src,
      mods, g1, w, a2s, abias, conv_w, gate_bias, cos_t, sin_t, dl_row)


def _mix_kernel(batch, *refs):
    (gq_f, gq_b, gk_f, gk_b, ge_f, ge_b, gv_f, gv_b, gd_f, gd_b,
     rq_f, rq_b, rk_f, rk_b, re_f, re_b, rv_f, rv_b, rdl,
     mq_f, mq_b, mk_f, mk_b, mw_f, mw_b, mv_f, mv_b, mp_f, mp_b, mcol_f, mcol_b, mc_f, mc_b,
     of_ref, ob_ref, sg_ref, sr_ref, sm_ref, m_ref) = refs
    j = pl.program_id(0)

    @pl.when(j == 0)
    def _():
        sg_ref[...] = jnp.zeros_like(sg_ref)
        sr_ref[...] = jnp.zeros_like(sr_ref)
        sm_ref[...] = jnp.zeros_like(sm_ref)
        m_ref[...] = jnp.full_like(m_ref, M_INIT)

    row_head = lax.broadcasted_iota(jnp.int32, (2 * CHUNK, PAIR_V), 0) // CHUNK
    diag_kv = row_head == lax.broadcasted_iota(jnp.int32, (2 * CHUNK, PAIR_V), 1) // DV
    diag_kk = (lax.broadcasted_iota(jnp.int32, (2 * CHUNK, PAIR_K), 0) // CHUNK
               == lax.broadcasted_iota(jnp.int32, (2 * CHUNK, PAIR_K), 1) // DK_SMALL)
    t_idx = lax.broadcasted_iota(jnp.int32, (CHUNK, 2 * CHUNK), 0)
    s_idx = lax.broadcasted_iota(jnp.int32, (CHUNK, 2 * CHUNK), 1) % CHUNK
    causal = (s_idx <= t_idx, s_idx >= t_idx)
    ones_v = jnp.ones((CHUNK, DV), BF16)
    zeros_vext = jnp.zeros((CHUNK, PAIR_V), BF16)
    ret_dec = jnp.exp(float(CHUNK) * _log_sigmoid(rdl[...]))

    decay_sets = (
        ((gq_f, gk_f, ge_f, gv_f), (gq_b, gk_b, ge_b, gv_b), sg_ref, 0),
        ((rq_f, rk_f, re_f, rv_f), (rq_b, rk_b, re_b, rv_b), sr_ref, 2 * BRANCH_W),
    )
    ml_sets = ((mq_f, mk_f, mw_f, mv_f, mp_f, mcol_f, mc_f),
               (mq_b, mk_b, mw_b, mv_b, mp_b, mcol_b, mc_b))
    outs = (of_ref, ob_ref)


    def decay_phase1(bb, d, c, rows, bi, p, refs):
        q_r, k_r, e_r, v_r = refs
        ks = slice(p * PAIR_K, (p + 1) * PAIR_K)
        q2 = q_r[bb, rows, ks]
        k2 = k_r[bb, rows, ks]
        v2 = v_r[bb, rows, p * PAIR_V:(p + 1) * PAIR_V]
        k_bd = jnp.where(diag_kk, jnp.concatenate([k2, k2], axis=0), 0.0)
        dec = (gd_f if d == 0 else gd_b)[bb, c, ks, :] if bi == 0 else ret_dec[d, ks, :]
        upd = jnp.where(diag_kv, _dot_tn(e_r[bb, rows, ks], v2), 0.0)
        return dict(q2=q2, v2=v2, dec=jnp.concatenate([dec, dec], axis=1), upd=upd,
                    att=_dot_nt(q2, k_bd))

    def decay_phase2(bb, d, p, rows, o_ref, s_ref, base, w):
        s_prev = s_ref[bb, d, p]
        att = jnp.where(causal[d], w["att"], 0.0).astype(BF16)
        v_bd = jnp.where(diag_kv, jnp.concatenate([w["v2"], w["v2"]], axis=0), 0.0)
        lhs = jnp.concatenate([att, w["q2"]], axis=1)
        rhs = jnp.concatenate([v_bd, s_prev.astype(BF16)], axis=0)
        o_ref[bb, rows, base + p * PAIR_V:base + (p + 1) * PAIR_V] = _dot(lhs, rhs)
        s_ref[bb, d, p] = w["dec"] * s_prev + w["upd"]

    def ml_phase1(bb, d, c, rows, p, refs):
        q_r, k_r, w_r, v_r, p_r, col_r, c_r = refs
        cols = col_r[bb, rows, :]
        q2 = q_r[bb, rows, p * 2 * DK_ML:(p + 1) * 2 * DK_ML]
        k2 = k_r[bb, rows, p * 2 * DK_ML:(p + 1) * 2 * DK_ML]
        k_bd = jnp.where(diag_kv, jnp.concatenate([k2, k2], axis=0), 0.0)
        att = _dot_nt(q2, k_bd)
        heads = []
        for hl in range(2):
            hd = 2 * p + hl
            jj = d * HEADS + hd
            vext = jnp.concatenate([v_r[bb, rows, hd * DV:(hd + 1) * DV], ones_v], axis=1)
            st = sm_ref[bb, d, hd]
            heads.append(dict(
                hd=hd, st=st, vext=vext,
                inter=_dot(q2[:, hl * DK_ML:(hl + 1) * DK_ML], st.astype(BF16)),
                upd=_dot_tn(w_r[bb, rows, hd * DK_ML:(hd + 1) * DK_ML], vext),
                u=jnp.broadcast_to(cols[:, MC_U + jj:MC_U + jj + 1], (CHUNK, LANES)),
                b=jnp.broadcast_to(cols[:, MC_B + jj:MC_B + jj + 1], (CHUNK, LANES))))
        return dict(att=att, heads=heads, p0=p_r[bb, p, rows, :], cs=c_r[bb, c])

    def ml_phase2(bb, d, rows, o_ref, w):
        sc = (w["att"] * w["p0"]).astype(BF16)
        for hl, hw in enumerate(w["heads"]):
            hd = hw["hd"]
            jj = d * HEADS + hd
            vz = (jnp.concatenate([hw["vext"], zeros_vext], axis=0) if hl == 0
                  else jnp.concatenate([zeros_vext, hw["vext"]], axis=0))
            intra = _dot(sc, vz)
            m_prev = m_ref[bb, jj:jj + 1, :]
            u = hw["u"]
            mx = jnp.maximum(u, m_prev)
            f = jnp.exp(u - mx)
            e = jnp.exp(m_prev - mx)
            num = f * intra[:, 0:DV] + e * hw["inter"][:, 0:DV]
            den = f * intra[:, DV:] + e * hw["inter"][:, DV:]
            den = jnp.maximum(jnp.abs(den), jnp.exp(-(hw["b"] + mx)))
            o_ref[bb, rows, BRANCH_W + hd * DV:BRANCH_W + (hd + 1) * DV] = num / den
            tot = w["cs"][jj:jj + 1, :]
            mloc = w["cs"][NHD + jj:NHD + jj + 1, :]
            m_new = jnp.maximum(tot + m_prev, mloc)
            keep = jnp.exp(tot + m_prev - m_new)
            gain = jnp.exp(mloc - m_new)
            sm_ref[bb, d, hd] = (jnp.concatenate([keep, keep], axis=1) * hw["st"]
                                 + jnp.concatenate([gain, gain], axis=1) * hw["upd"])
            m_ref[bb, jj:jj + 1, :] = m_new

    units = []
    for step in range(CHUNKS_PER_TILE):
        for bb, d in ((b_, d_) for b_ in range(batch) for d_ in range(2)):
            c = step if d == 0 else CHUNKS_PER_TILE - 1 - step
            rows = pl.ds(c * CHUNK, CHUNK)
            for bi, (fset, bset, s_ref, base) in enumerate(decay_sets):
                for p in range(PAIRS):
                    units.append((functools.partial(decay_phase1, bb, d, c, rows, bi, p, fset if d == 0 else bset),
                                  functools.partial(decay_phase2, bb, d, p, rows, outs[d], s_ref, base)))
            for p in range(PAIRS):
                units.append((functools.partial(ml_phase1, bb, d, c, rows, p, ml_sets[d]),
                              functools.partial(ml_phase2, bb, d, rows, outs[d])))

    assert MIX_LOOKAHEAD < len(units) // CHUNKS_PER_TILE
    pending = []
    for phase1, phase2 in units:
        pending.append((phase2, phase1()))
        if len(pending) > MIX_LOOKAHEAD:
            fn, w = pending.pop(0)
            fn(w)
    for fn, w in pending:
        fn(w)


def _mix(gla, ret, ml, ret_dl_col):
    g_qke, g_vs, gdec = gla
    r_qke, r_vs = ret
    m_qk, mw, m_vs, mp, mcol, mc = ml
    B, T, _ = g_vs.shape
    nt = T // TILE

    def tf(j):
        return j

    def tb(j):
        return jnp.where(j == 0, 0, nt - j)

    def pair(width, lane_block_b=0, lane_block_f=0):
        return [pl.BlockSpec((B, TILE, width), lambda j: (0, tf(j), lane_block_f)),
                pl.BlockSpec((B, TILE, width), lambda j: (0, tb(j), lane_block_b))]

    dec_pair = [pl.BlockSpec((B, CHUNKS_PER_TILE, NQK_S, LANES), lambda j: (0, tf(j), 0, 0)),
                pl.BlockSpec((B, CHUNKS_PER_TILE, NQK_S, LANES), lambda j: (0, tb(j), 1, 0))]
    p0_pair = [pl.BlockSpec((B, PAIRS, TILE, 2 * CHUNK), lambda j: (0, 0, tf(j), 0)),
               pl.BlockSpec((B, PAIRS, TILE, 2 * CHUNK), lambda j: (0, 1, tb(j), 0))]
    cs_pair = [pl.BlockSpec((B, CHUNKS_PER_TILE, 2 * NHD, LANES), lambda j: (0, tf(j), 0, 0)),
               pl.BlockSpec((B, CHUNKS_PER_TILE, 2 * NHD, LANES), lambda j: (0, tb(j), 0, 0))]
    decay_specs = pair(NQK_S, 1, 0) + pair(NQK_S, 3, 2) + pair(NQK_S, 5, 4) + pair(BRANCH_W)
    in_specs = (decay_specs + dec_pair + decay_specs
                + [pl.BlockSpec((2, NQK_S, LANES), lambda j: (0, 0, 0))]
                + pair(NQK_M) + pair(NQK_M, 1, 1) + pair(NQK_M, 1) + pair(BRANCH_W) + p0_pair
                + pair(MC_WIDTH) + cs_pair)
    args = (g_qke, g_qke, g_qke, g_qke, g_qke, g_qke, g_vs, g_vs, gdec, gdec,
            r_qke, r_qke, r_qke, r_qke, r_qke, r_qke, r_vs, r_vs, ret_dl_col,
            m_qk, m_qk, m_qk, m_qk, mw, mw, m_vs, m_vs, mp, mp, mcol, mcol, mc, mc)
    return pl.pallas_call(
        functools.partial(_mix_kernel, B),
        out_shape=(jax.ShapeDtypeStruct((B, T, 3 * BRANCH_W), F32),) * 2,
        grid=(nt,),
        in_specs=in_specs,
        out_specs=(pl.BlockSpec((B, TILE, 3 * BRANCH_W), lambda j: (0, tf(j), 0)),
                   pl.BlockSpec((B, TILE, 3 * BRANCH_W), lambda j: (0, tb(j), 0))),
        scratch_shapes=[pltpu.VMEM((B, 2, PAIRS, PAIR_K, PAIR_V), F32),
                        pltpu.VMEM((B, 2, PAIRS, PAIR_K, PAIR_V), F32),
                        pltpu.VMEM((B, 2, HEADS, DK_ML, 2 * DV), F32),
                        pltpu.VMEM((B, NHD, LANES), F32)],
        compiler_params=_cparams(1), name="mix",
    )(*args)


def _merge_kernel(batch, first_tile, x_ref, ctx_ref, mod_ref, g_ref, wg_ref, of_ref, ob_ref,
                  sgg_ref, sgm_ref, sgr_ref, ng_ref, wb_ref, wo_ref, out_ref):
    tile_idx = pl.program_id(0) + first_tile
    side = (sgg_ref, sgm_ref, sgr_ref)
    mods = [mod_ref[jnp.where(tile_idx == 0, batch, bb)] for bb in range(batch)]
    xs = [jnp.where(tile_idx == 0, ctx_ref[bb], x_ref[bb]) for bb in range(batch)]
    hs = [_norm_mod(xs[bb], g_ref[...], mods[bb][0:1, :], mods[bb][1:2, :]).astype(BF16)
          for bb in range(batch)]
    accs = [jnp.zeros((TILE, D_MODEL), F32) for _ in range(batch)]
    for br in range(3):
        for bb in range(batch):
            parts = []
            for hd in range(HEADS):
                cols = slice(br * BRANCH_W + hd * DV, br * BRANCH_W + (hd + 1) * DV)
                oh = of_ref[bb, :, cols] + ob_ref[bb, :, cols]
                if br > 0:
                    oh = oh - jnp.mean(oh, axis=-1, keepdims=True)
                parts.append(oh * lax.rsqrt(jnp.mean(oh * oh, axis=-1, keepdims=True) + EPS))
            y = jnp.concatenate(parts, axis=1) * ng_ref[:, br * BRANCH_W:(br + 1) * BRANCH_W]
            y = (y * side[br][bb]).astype(BF16)
            gate = _sigmoid(_dot(hs[bb], wg_ref[0, :, br * D_MODEL:(br + 1) * D_MODEL]))
            accs[bb] = accs[bb] + gate * _dot(y, wb_ref[0, br])
    for bb in range(batch):
        out = _dot(accs[bb].astype(BF16), wo_ref[0])
        out_ref[bb] = xs[bb] + mods[bb][2:3, :] * out


def _merge(ctx_src, x_src, off, mods, g1, wgate, o_f, o_b, sg_gla, sz_ml, sg_ret, norm_g, wb, wo, layer, skip):
    B = x_src.shape[0]
    nt = x_src.shape[1] // TILE + off - skip
    tile = lambda width: pl.BlockSpec((B, TILE, width), lambda i: (0, i + skip, 0))
    gate_tile = pl.BlockSpec((B, TILE, BRANCH_W), lambda i: (0, i + skip, 1))
    const = lambda shape: pl.BlockSpec(shape, lambda i: (0,) * len(shape))
    per_layer = lambda shape: pl.BlockSpec((1,) + shape, lambda i: (layer,) + (0,) * len(shape),
                                           pipeline_mode=pl.Buffered(1))
    return pl.pallas_call(
        functools.partial(_merge_kernel, B, skip),
        out_shape=jax.ShapeDtypeStruct((B, nt * TILE, D_MODEL), F32),
        grid=(nt,),
        in_specs=[pl.BlockSpec((B, TILE, D_MODEL), lambda i: (0, jnp.maximum(i + skip - off, 0), 0)),
                  pl.BlockSpec((B, TILE, D_MODEL), lambda i: (0, 0, 0)),
                  const((B + 1, 6, D_MODEL)),
                  const((1, D_MODEL)), per_layer((D_MODEL, W_GATE)),
                  tile(3 * BRANCH_W), tile(3 * BRANCH_W),
                  gate_tile, gate_tile, gate_tile,
                  const((1, 3 * BRANCH_W)), per_layer((3, BRANCH_W, D_MODEL)),
                  per_layer((D_MODEL, D_MODEL))],
        out_specs=pl.BlockSpec((B, TILE, D_MODEL), lambda i: (0, i, 0)),
        compiler_params=_cparams(1), name="merge",
    )(x_src, ctx_src, mods, g1, wgate, o_f, o_b, sg_gla, sz_ml, sg_ret, norm_g, wb, wo)


def _mlp_kernel(batch, first_tile, final, x_ref, mod_ref, g_ref, wu_ref, wd_ref, fg_ref, out_ref):
    tile_idx = pl.program_id(0) + first_tile
    mods = [mod_ref[jnp.where(tile_idx == 0, batch, bb)] for bb in range(batch)]
    ups = []
    for bb in range(batch):
        h = _norm_mod(x_ref[bb], g_ref[...], mods[bb][3:4, :], mods[bb][4:5, :]).astype(BF16)
        ups.append(_dot(h, wu_ref[0]))
    for bb in range(batch):
        u = jnp.maximum(ups[bb], 0.0)
        y = x_ref[bb] + mods[bb][5:6, :] * _dot((u * u).astype(BF16), wd_ref[0])
        if final:
            y = y * lax.rsqrt(jnp.mean(y * y, axis=-1, keepdims=True) + EPS) * fg_ref[...]
        out_ref[bb] = y


def _mlp(tok, mods, g2, wu, wd, layer, final_g, first_tile, final):
    B, T, _ = tok.shape
    const = lambda shape: pl.BlockSpec(shape, lambda i: (0,) * len(shape))
    per_layer = lambda shape: pl.BlockSpec((1,) + shape, lambda i: (layer,) + (0,) * len(shape),
                                           pipeline_mode=pl.Buffered(1))
    return pl.pallas_call(
        functools.partial(_mlp_kernel, B, first_tile, final),
        out_shape=jax.ShapeDtypeStruct((B, T, D_MODEL), F32),
        grid=(T // TILE,),
        in_specs=[pl.BlockSpec((B, TILE, D_MODEL), lambda i: (0, i, 0)),
                  const((B + 1, 6, D_MODEL)),
                  const((1, D_MODEL)), per_layer((D_MODEL, D_FF)), per_layer((D_FF, D_MODEL)),
                  const((1, D_MODEL))],
        out_specs=pl.BlockSpec((B, TILE, D_MODEL), lambda i: (0, i, 0)),
        compiler_params=_cparams(1), name="mlp",
    )(tok, mods, g2, wu, wd, final_g)


def kernel(x, c, ctx, c_ctx, norm1_g, norm2_g, w_ada, b_ada, w_in, gla_a2, gla_a_bias, gla_norm_g,
           ml_conv, ml_i_bias, ml_f_bias, ml_norm_g, ret_decay_logit, ret_norm_g, w_branch, w_o,
           w_up, w_down, final_g):
    B, seq, _ = x.shape
    ctx_len = ctx.shape[1]
    assert B == 2 and ctx_len == TILE and seq % TILE == 0 and seq % GRID_W == 0
    ctx_src, x_src, off = ctx, x, 1

    c_rows = jnp.concatenate([c, c_ctx[None, :], jnp.zeros((8 - B - 1, D_MODEL), F32)], axis=0)
    mods_all = _ada(c_rows, w_ada, b_ada)[:, 0:3].reshape(DEPTH, 3, 6, D_MODEL)

    cos_t, sin_t = _rope_tables(seq, ctx_len)

    w_proj, w_gate = _wprep(w_in)
    w_branch_b = w_branch.astype(BF16)
    w_o_b = w_o.astype(BF16)
    w_up_b = w_up.astype(BF16)
    w_down_b = w_down.astype(BF16)
    for l in range(DEPTH):
        last = l == DEPTH - 1
        skip = 1 if last else 0
        mods = mods_all[l]
        g1 = norm1_g[l][None, :]
        g2 = norm2_g[l][None, :]
        zeros = jnp.zeros((GLA_RANK, NQK_S), F32)
        a2blk = jnp.concatenate([jnp.concatenate([gla_a2[l, 0], zeros], axis=1),
                                 jnp.concatenate([zeros, gla_a2[l, 1]], axis=1)], axis=0)
        a2s = jnp.concatenate([a2blk, a2blk], axis=0).astype(BF16)
        abias = gla_a_bias[l].reshape(1, 2 * NQK_S)
        gate_bias = jnp.concatenate([ml_i_bias[l].reshape(1, NHD), ml_f_bias[l].reshape(1, NHD)], axis=1)
        dl = ret_decay_logit[l]
        dl_row = jnp.repeat(dl, DK_SMALL, axis=1).reshape(1, 2 * NQK_S)
        dl_col = jnp.broadcast_to(jnp.repeat(dl, DK_SMALL, axis=1)[:, :, None], (2, NQK_S, LANES))

        (g_qke, g_vs, gdec, m_qk, mw, m_vs, mp, mcol, mc, r_qke, r_vs) = _proj(
            ctx_src, x_src, off, mods, g1, w_proj, l, a2s, abias, ml_conv[l], gate_bias, cos_t, sin_t, dl_row)
        o_f, o_b = _mix((g_qke, g_vs, gdec), (r_qke, r_vs), (m_qk, mw, m_vs, mp, mcol, mc), dl_col)
        norm_g = jnp.concatenate([gla_norm_g[l], ml_norm_g[l], ret_norm_g[l]])[None, :]
        tok = _merge(ctx_src, x_src, off, mods, g1, w_gate, o_f, o_b, g_vs, m_vs, r_vs, norm_g,
                     w_branch_b, w_o_b, l, skip)
        tok = _mlp(tok, mods, g2, w_up_b, w_down_b, l, final_g[None, :], skip, last)
        ctx_src, x_src, off = tok, tok, 0
    return tok
```

```python
import functools

import jax
import jax.numpy as jnp
from jax import lax
from jax.experimental import pallas as pl
from jax.experimental.pallas import tpu as pltpu

F32 = jnp.float32
BF16 = jnp.bfloat16

D_MODEL = 1024
DEPTH = 2
GRID_W = 64
CHUNK = 64
EPS = 1e-6
M_INIT = -1e30
HEADS = 4
DK_SMALL = 64
DK_ML = 128
DV = 128
BRANCH_W = HEADS * DV
GLA_RANK = 16
GLA_TAU = 16.0
ROPE_BASE = 10000.0
D_FF = 4 * D_MODEL
LANES = 128

TILE = 256
CHUNKS_PER_TILE = TILE // CHUNK
HALO = 8
VMEM_LIMIT = 56 * 1024 * 1024
VMEM_LIMIT_PROJ = 60 * 1024 * 1024

NQK_S = HEADS * DK_SMALL
NQK_M = HEADS * DK_ML
NHD = 2 * HEADS
W_GLA = 2 * NQK_S + 2 * BRANCH_W + 2 * GLA_RANK
W_ML = 2 * NQK_M + 2 * BRANCH_W + 2 * NHD
W_RET = 2 * NQK_S + 2 * BRANCH_W
W_GATE = 3 * D_MODEL

PW_MLQK = 0
PW_MLR = PW_MLQK + 2 * NQK_M
PW_MLR_WIDTH = 2 * BRANCH_W + LANES
PW_GR = PW_MLR + PW_MLR_WIDTH
PW_GR_WIDTH = 2 * (2 * NQK_S + 2 * BRANCH_W) + LANES
W_PROJ = PW_GR + PW_GR_WIDTH
MLR_V, MLR_Z, MLR_GATES = 0, BRANCH_W, 2 * BRANCH_W
GR_GLA_Q, GR_GLA_K, GR_GLA_V, GR_GLA_G = 0, NQK_S, 2 * NQK_S, 2 * NQK_S + BRANCH_W
GR_RET_Q = 2 * NQK_S + 2 * BRANCH_W
GR_RET_K, GR_RET_V, GR_RET_G = GR_RET_Q + NQK_S, GR_RET_Q + 2 * NQK_S, GR_RET_Q + 2 * NQK_S + BRANCH_W
GR_GLA_A = 2 * GR_RET_Q

MC_B, MC_W, MC_U = 0, NHD, 2 * NHD
MC_WIDTH = 3 * NHD

PAIRS = HEADS // 2
PAIR_K = 2 * DK_SMALL
PAIR_V = 2 * DV
MIX_LOOKAHEAD = 23


def _cparams(n_axes, vmem_limit=VMEM_LIMIT):
    return pltpu.CompilerParams(dimension_semantics=("arbitrary",) * n_axes,
                                vmem_limit_bytes=vmem_limit)


def _log_sigmoid(z):
    return jnp.minimum(z, 0.0) - jnp.log(1.0 + jnp.exp(-jnp.abs(z)))


def _sigmoid(z):
    return 1.0 / (1.0 + jnp.exp(-z))


def _silu(z):
    return z * _sigmoid(z)


def _norm_mod(x, g, shift, scale):
    return x * lax.rsqrt(jnp.mean(x * x, axis=-1, keepdims=True) + EPS) * (g * (1.0 + scale)) + shift


def _dot(a, b):
    return jnp.dot(a, b, preferred_element_type=F32)


def _dot_nt(a, b):
    return lax.dot_general(a, b, (((1,), (1,)), ((), ())), preferred_element_type=F32)


def _dot_tn(a, b):
    return lax.dot_general(a, b, (((0,), (0,)), ((), ())), preferred_element_type=F32)


def _split2(x):
    hi = x.astype(BF16)
    return hi, (x - hi.astype(F32)).astype(BF16)


def _chunk_tri(lower):
    r = lax.broadcasted_iota(jnp.int32, (TILE, TILE), 0)
    c = lax.broadcasted_iota(jnp.int32, (TILE, TILE), 1)
    same = (r // CHUNK) == (c // CHUNK)
    tri = (c <= r) if lower else (c >= r)
    return jnp.where(same & tri, 1.0, 0.0).astype(BF16)


def _chunk_total(x):
    w = x.shape[-1]
    tot = jnp.sum(x.reshape(CHUNKS_PER_TILE, CHUNK, w), axis=1, keepdims=True)
    return jnp.broadcast_to(tot, (CHUNKS_PER_TILE, CHUNK, w)).reshape(TILE, w)


def _ada_kernel(c_ref, w_ref, b_ref, o_ref):
    s = _silu(c_ref[...])
    s_hi = s.astype(BF16).astype(F32)
    s2 = jnp.concatenate([s_hi, s - s_hi], axis=0).astype(BF16)
    w_hi, w_lo = _split2(w_ref[0])
    r = _dot(s2, w_hi) + _dot(s2, w_lo)
    o_ref[0] = r[0:8] + r[8:16] + b_ref[0]


def _ada(c_rows, w_ada, b_ada):
    tn = 1536
    return pl.pallas_call(
        _ada_kernel,
        out_shape=jax.ShapeDtypeStruct((DEPTH, 8, 6 * D_MODEL), F32),
        grid=(DEPTH, 6 * D_MODEL // tn),
        in_specs=[pl.BlockSpec((8, D_MODEL), lambda l, n: (0, 0)),
                  pl.BlockSpec((1, D_MODEL, tn), lambda l, n: (l, 0, n)),
                  pl.BlockSpec((1, 1, tn), lambda l, n: (l, 0, n))],
        out_specs=pl.BlockSpec((1, 8, tn), lambda l, n: (l, 0, n)),
        compiler_params=_cparams(2), name="ada",
    )(c_rows, w_ada, b_ada.reshape(DEPTH, 1, 6 * D_MODEL))


def _rope_kernel(ang_ref, cos_ref, sin_ref, nsin_ref):
    a = ang_ref[...]
    s = jnp.sin(a)
    cos_ref[...] = jnp.cos(a)
    sin_ref[...] = s
    nsin_ref[...] = -s


def _rope_tables(seq, ctx_len):
    rows = seq // GRID_W
    nf = DK_SMALL // 4
    inv = ROPE_BASE ** (-jnp.arange(nf, dtype=F32) / nf)
    ang = jnp.concatenate([jnp.arange(rows, dtype=F32)[:, None] * inv,
                           jnp.arange(GRID_W, dtype=F32)[:, None] * inv], axis=0)
    n = rows + GRID_W
    cos, sin, nsin = pl.pallas_call(
        _rope_kernel,
        out_shape=(jax.ShapeDtypeStruct((n, nf), F32),) * 3,
        name="rope_tables",
    )(ang)

    def per_token(first, second):
        def half(t):
            r = jnp.broadcast_to(t[:rows, None, :], (rows, GRID_W, nf))
            c = jnp.broadcast_to(t[None, rows:, :], (rows, GRID_W, nf))
            return jnp.concatenate([r, c], axis=-1)
        head = jnp.concatenate([half(first), half(second)], axis=-1).reshape(seq, DK_SMALL)
        return jnp.tile(head, (1, HEADS))

    ones = jnp.ones((ctx_len, NQK_S), F32)
    cos_t = jnp.concatenate([ones, per_token(cos, cos)], axis=0)
    sin_t = jnp.concatenate([0.0 * ones, per_token(nsin, sin)], axis=0)
    return cos_t, sin_t


def _wprep_kernel(w_ref, wp_ref, wg_ref):
    o0, o1, o2 = W_GLA, W_GLA + W_ML, W_GLA + W_ML + W_RET
    n_sg = 2 * NQK_S + 2 * BRANCH_W
    rows = w_ref.shape[1]

    def copy(dst_ref, dst, src, n):
        dst_ref[0, :, dst:dst + n] = w_ref[0, :, src:src + n].astype(BF16)

    copy(wp_ref, PW_MLQK, o0, 2 * NQK_M)
    n_mlr = 2 * BRANCH_W + 2 * NHD
    copy(wp_ref, PW_MLR, o0 + 2 * NQK_M, n_mlr)
    wp_ref[0, :, PW_MLR + n_mlr:PW_GR] = jnp.zeros((rows, PW_GR - PW_MLR - n_mlr), BF16)
    copy(wp_ref, PW_GR + GR_GLA_Q, 0, n_sg)
    copy(wp_ref, PW_GR + GR_RET_Q, o1, W_RET)
    copy(wp_ref, PW_GR + GR_GLA_A, n_sg, 2 * GLA_RANK)
    pad0 = PW_GR + GR_GLA_A + 2 * GLA_RANK
    wp_ref[0, :, pad0:W_PROJ] = jnp.zeros((rows, W_PROJ - pad0), BF16)
    copy(wg_ref, 0, o2, W_GATE)


def _wprep(w_in):
    rows = 256
    return pl.pallas_call(
        _wprep_kernel,
        out_shape=(jax.ShapeDtypeStruct((DEPTH, D_MODEL, W_PROJ), BF16),
                   jax.ShapeDtypeStruct((DEPTH, D_MODEL, W_GATE), BF16)),
        grid=(DEPTH, D_MODEL // rows),
        in_specs=[pl.BlockSpec((1, rows, w_in.shape[2]), lambda l, r: (l, r, 0))],
        out_specs=(pl.BlockSpec((1, rows, W_PROJ), lambda l, r: (l, r, 0)),
                   pl.BlockSpec((1, rows, W_GATE), lambda l, r: (l, r, 0))),
        compiler_params=_cparams(2), name="wprep",
    )(w_in)


def _proj_kernel(nt, xa_ref, xap_ref, xan_ref, xb_ref, xbp_ref, xbn_ref, ctx_ref,
                 mod_ref, g_ref, w_ref, a2_ref, ab_ref, cw_ref, gb_ref, cos_ref, sin_ref, dl_ref,
                 gqke_ref, gvs_ref, gdec_ref, mqk_ref, mkw_ref, mvs_ref, mp0_ref, mcol_ref, mcs_ref,
                 rqke_ref, rvs_ref,
                 ext0_ref, ext1_ref, prml0_ref, prml1_ref, prgr0_ref, prgr1_ref, rdec_ref, tri_ref,
                 h0_ref, h1_ref):
    i = pl.program_id(0)
    ext_refs = (ext0_ref, ext1_ref)
    prml_refs = (prml0_ref, prml1_ref)
    prgr_refs = (prgr0_ref, prgr1_ref)
    h_refs = (h0_ref, h1_ref)
    row8 = lax.broadcasted_iota(jnp.int32, (NHD, TILE), 0)
    lane8 = lax.broadcasted_iota(jnp.int32, (NHD, TILE), 1)
    lane_chunk = lane8 // CHUNK
    lane_in_chunk = lane8 % CHUNK
    ti2 = lax.broadcasted_iota(jnp.int32, (CHUNK, 2 * CHUNK), 0)
    lane2 = lax.broadcasted_iota(jnp.int32, (CHUNK, 2 * CHUNK), 1)
    si2 = lane2 % CHUNK
    lane_first = lane2 < CHUNK

    def project_stages(x, x_prev, x_next, tile_idx, mod_row, slot):
        st = {}

        def mlstm_qk():
            mod = mod_ref[mod_row]
            x_all = jnp.concatenate([x_prev, x, x_next], axis=0)
            h_all = _norm_mod(x_all, g_ref[...], mod[0:1, :], mod[1:2, :])
            h_refs[slot][...] = h_all[HALO:HALO + TILE, :].astype(BF16)
            prev_ok = tile_idx >= 2
            next_ok = jnp.logical_and(tile_idx >= 1, tile_idx <= nt - 2)
            ext = _dot(h_all.astype(BF16), w_ref[0, :, PW_MLQK:PW_MLR])
            ext_refs[slot][HALO:HALO + TILE, :] = ext[HALO:HALO + TILE, :]
            ext_refs[slot][0:HALO, :] = jnp.where(prev_ok, ext[0:HALO, :], 0.0)
            ext_refs[slot][HALO + TILE:, :] = jnp.where(next_ok, ext[HALO + TILE:, :], 0.0)

        def mlstm_rest():
            prml_refs[slot][...] = _dot(h_refs[slot][...], w_ref[0, :, PW_MLR:PW_GR])

        def gla():
            prgr_refs[slot][:, 0:GR_RET_Q] = _dot(h_refs[slot][...], w_ref[0, :, PW_GR:PW_GR + GR_RET_Q])

        def ret():
            prgr_refs[slot][:, GR_RET_Q:] = _dot(h_refs[slot][...], w_ref[0, :, PW_GR + GR_RET_Q:])

        return [mlstm_qk, mlstm_rest, gla, ret]

    def project(*args):
        for stage in project_stages(*args):
            stage()

    @pl.when(i == 0)
    def _():
        tri_ref[0] = _chunk_tri(True)
        tri_ref[1] = _chunk_tri(False)
        lg = _log_sigmoid(dl_ref[...])
        t_in_chunk = (lax.broadcasted_iota(jnp.int32, (TILE, NQK_S), 0) % CHUNK).astype(F32)
        for d in range(2):
            lg_d = lg[:, d * NQK_S:(d + 1) * NQK_S]
            steps = (t_in_chunk + 1.0) if d == 0 else (float(CHUNK) - t_in_chunk)
            b = steps * lg_d
            rdec_ref[3 * d] = jnp.exp(b)
            rdec_ref[3 * d + 1] = jnp.exp(-b)
            rdec_ref[3 * d + 2] = jnp.exp(float(CHUNK) * lg_d - b)
        halo0 = jnp.zeros((HALO, D_MODEL), F32)
        project(ctx_ref[0], halo0, halo0, 0, 2, 0)

    def rope(x):
        lane = lax.broadcasted_iota(jnp.int32, (TILE, NQK_S), 1)
        first_half = (lane % DK_SMALL) < (DK_SMALL // 2)
        swapped = jnp.where(first_half, pltpu.roll(x, NQK_S - DK_SMALL // 2, 1),
                            pltpu.roll(x, DK_SMALL // 2, 1))
        return x * cos_ref[...] + swapped * sin_ref[...]

    def seg_cummax(x, reverse):
        sh = 1
        while sh < CHUNK:
            if reverse:
                y, ok = pltpu.roll(x, TILE - sh, 1), lane_in_chunk < CHUNK - sh
            else:
                y, ok = pltpu.roll(x, sh, 1), lane_in_chunk >= sh
            x = jnp.where(ok, jnp.maximum(x, y), x)
            sh *= 2
        return x

    def prepare_stages(slot, bb):
        ext_ref, prml_ref, prgr_ref = ext_refs[slot], prml_refs[slot], prgr_refs[slot]
        st = {}

        def conv():
            ext = ext_ref[...]
            n_ext = TILE + 2 * HALO
            prev = pltpu.roll(ext, 1, 0)[HALO:HALO + TILE, :]
            nxt = pltpu.roll(ext, n_ext - 1, 0)[HALO:HALO + TILE, :]
            qk = _silu(cw_ref[0:1, :] * prev + cw_ref[1:2, :] * ext[HALO:HALO + TILE, :]
                       + cw_ref[2:3, :] * nxt)
            mqk_ref[bb, :, 0:NQK_M] = qk[:, 0:NQK_M].astype(BF16)
            st["k_ml"] = qk[:, NQK_M:] * (DK_ML ** -0.5)
            mqk_ref[bb, :, NQK_M:] = st["k_ml"].astype(BF16)

        def gates_in():
            mvs_ref[bb, :, 0:BRANCH_W] = prml_ref[:,MLR_V:MLR_V + BRANCH_W].astype(BF16)
            mvs_ref[bb, :, BRANCH_W:] = _silu(prml_ref[:,MLR_Z:MLR_Z + BRANCH_W]).astype(BF16)
            gates_t = (prml_ref[:,MLR_GATES:MLR_GATES + 2 * NHD] + gb_ref[...]).T
            st["li_t"] = gates_t[0:NHD]
            st["lf_t"] = _log_sigmoid(gates_t[NHD:])
            lf_hi, lf_lo = _split2(st["lf_t"])
            st["lf2"] = jnp.concatenate([lf_hi, lf_lo], axis=0)

        def gates_mm():
            st["cum_f"] = _dot(st["lf2"], tri_ref[1])
            st["cum_b"] = _dot(st["lf2"], tri_ref[0])

        def gates_out():
            mlstm_gates_body(st["li_t"], st["lf_t"], st["cum_f"], st["cum_b"], st["k_ml"], bb)

        def gla_in():
            gvs_ref[bb, :, 0:BRANCH_W] = prgr_ref[:,GR_GLA_V:GR_GLA_V + BRANCH_W].astype(BF16)
            gvs_ref[bb, :, BRANCH_W:] = _silu(prgr_ref[:,GR_GLA_G:GR_GLA_G + BRANCH_W]).astype(BF16)
            a_hi, a_lo = _split2(prgr_ref[:,GR_GLA_A:GR_GLA_A + 2 * GLA_RANK])
            st["a2"] = jnp.concatenate([a_hi, a_lo], axis=1)

        def gla_z_mm():
            st["z"] = _dot(st["a2"], a2_ref[...])

        def gla_la():
            la = _log_sigmoid(st["z"] + ab_ref[...]) * (1.0 / GLA_TAU)
            st["la"] = la
            st["la_split"] = [_split2(la[:, d * NQK_S:(d + 1) * NQK_S]) for d in range(2)]

        def gla_cum_mm():
            st["b"] = []
            for d in range(2):
                tri = tri_ref[d]
                la_hi, la_lo = st["la_split"][d]
                st["b"].append(_dot(tri, la_hi) + _dot(tri, la_lo))

        def gla_out():
            gla_body(prgr_ref, st["la"], st["b"], bb)

        def ret():
            ret_body(prgr_ref, bb)

        return dict(conv=conv, gates_in=gates_in, gates_mm=gates_mm, gates_out=gates_out,
                    gla_in=gla_in, gla_z_mm=gla_z_mm, gla_la=gla_la, gla_cum_mm=gla_cum_mm,
                    gla_out=gla_out, ret=ret)

    def mlstm_gates_body(li_t, lf_t, cum_f, cum_b, k_ml, bb):
        b_t = jnp.where(row8 < HEADS, cum_f[0:NHD] + cum_f[NHD:], cum_b[0:NHD] + cum_b[NHD:])
        tot_t = jnp.zeros((NHD, TILE), F32)
        mloc_t = jnp.zeros((NHD, TILE), F32)
        for c in range(CHUNKS_PER_TILE):
            tot_c = jnp.sum(lf_t[:, c * CHUNK:(c + 1) * CHUNK], axis=-1, keepdims=True)
            tot_t = jnp.where(lane_chunk == c, tot_c, tot_t)
            mcs_ref[bb, c, 0:NHD, :] = jnp.broadcast_to(tot_c, (NHD, LANES))
        g_t = tot_t - b_t + li_t
        for c in range(CHUNKS_PER_TILE):
            mloc_c = jnp.max(g_t[:, c * CHUNK:(c + 1) * CHUNK], axis=-1, keepdims=True)
            mloc_t = jnp.where(lane_chunk == c, mloc_c, mloc_t)
            mcs_ref[bb, c, NHD:2 * NHD, :] = jnp.broadcast_to(mloc_c, (NHD, LANES))
        w_t = jnp.exp(g_t - mloc_t)
        r_t = li_t - b_t
        u_t = jnp.where(row8 < HEADS, seg_cummax(r_t, False), seg_cummax(r_t, True))
        cols = jnp.concatenate([b_t, w_t, u_t], axis=0).T
        mcol_ref[bb] = cols
        for j in range(NHD):
            d, hd = j // HEADS, j % HEADS
            mkw_ref[bb, :, d * NQK_M + hd * DK_ML:d * NQK_M + (hd + 1) * DK_ML] = (
                k_ml[:, hd * DK_ML:(hd + 1) * DK_ML] * cols[:, MC_W + j:MC_W + j + 1]).astype(BF16)
        for pp in range(NHD // 2):
            j0, j1 = 2 * pp, 2 * pp + 1
            mask = (si2 <= ti2) if pp < PAIRS else (si2 >= ti2)
            for c in range(CHUNKS_PER_TILE):
                rows = slice(c * CHUNK, (c + 1) * CHUNK)
                r2 = jnp.concatenate([r_t[j0:j0 + 1, rows], r_t[j1:j1 + 1, rows]], axis=1)
                u2 = jnp.where(lane_first, cols[rows, MC_U + j0:MC_U + j0 + 1],
                               cols[rows, MC_U + j1:MC_U + j1 + 1])
                mp0_ref[bb, pp, rows, :] = jnp.where(mask, jnp.exp(r2 - u2), 0.0)

    def gla_body(prgr_ref, la, cums, bb):
        half_t = jnp.sum(la.reshape(2 * CHUNKS_PER_TILE, CHUNK // 2, 2 * NQK_S), axis=1).T
        q_gla = prgr_ref[:,GR_GLA_Q:GR_GLA_Q + NQK_S] * (DK_SMALL ** -0.5)
        k_gla = prgr_ref[:,GR_GLA_K:GR_GLA_K + NQK_S]
        for d in range(2):
            la_d = la[:, d * NQK_S:(d + 1) * NQK_S]
            b = cums[d]
            tot = _chunk_total(la_d)
            decay = jnp.exp(b)
            gqke_ref[bb, :, d * NQK_S:(d + 1) * NQK_S] = (q_gla * decay).astype(BF16)
            gqke_ref[bb, :, (2 + d) * NQK_S:(3 + d) * NQK_S] = (k_gla * (1.0 / decay)).astype(BF16)
            gqke_ref[bb, :, (4 + d) * NQK_S:(5 + d) * NQK_S] = (k_gla * jnp.exp(tot - b)).astype(BF16)
            ht = half_t[d * NQK_S:(d + 1) * NQK_S, :]
            for c in range(CHUNKS_PER_TILE):
                tot_col = ht[:, 2 * c:2 * c + 1] + ht[:, 2 * c + 1:2 * c + 2]
                gdec_ref[bb, c, d * NQK_S:(d + 1) * NQK_S, :] = jnp.broadcast_to(jnp.exp(tot_col), (NQK_S, LANES))

    def ret_body(prgr_ref, bb):
        q_ret = rope(prgr_ref[:,GR_RET_Q:GR_RET_Q + NQK_S])
        k_ret = rope(prgr_ref[:,GR_RET_K:GR_RET_K + NQK_S] * (DK_SMALL ** -0.5))
        rvs_ref[bb, :, 0:BRANCH_W] = prgr_ref[:,GR_RET_V:GR_RET_V + BRANCH_W].astype(BF16)
        rvs_ref[bb, :, BRANCH_W:] = _silu(prgr_ref[:,GR_RET_G:GR_RET_G + BRANCH_W]).astype(BF16)
        for d in range(2):
            rqke_ref[bb, :, d * NQK_S:(d + 1) * NQK_S] = (q_ret * rdec_ref[3 * d]).astype(BF16)
            rqke_ref[bb, :, (2 + d) * NQK_S:(3 + d) * NQK_S] = (k_ret * rdec_ref[3 * d + 1]).astype(BF16)
            rqke_ref[bb, :, (4 + d) * NQK_S:(5 + d) * NQK_S] = (k_ret * rdec_ref[3 * d + 2]).astype(BF16)

    next_tile = jnp.minimum(i + 1, nt - 1)
    xa = jnp.where(i == 0, ctx_ref[1], xa_ref[0])
    a_ext, a_mlr, a_gla, a_ret = project_stages(xa, xap_ref[0], xan_ref[0], i, jnp.where(i == 0, 2, 1), 1)
    b_ext, b_mlr, b_gla, b_ret = project_stages(xb_ref[0], xbp_ref[0], xbn_ref[0], next_tile, 0, 0)
    x = prepare_stages(0, 0)
    y = prepare_stages(1, 1)
    order = [a_ext, x["conv"], x["gates_in"],
             a_mlr, x["gates_mm"], x["gates_out"], x["gla_in"],
             a_gla, x["gla_z_mm"], x["gla_la"],
             a_ret, x["gla_cum_mm"],
             b_ext, x["gla_out"], x["ret"], y["conv"], y["gates_in"],
             b_mlr, y["gates_mm"], y["gates_out"], y["gla_in"],
             b_gla, y["gla_z_mm"], y["gla_la"],
             b_ret, y["gla_cum_mm"], y["gla_out"], y["ret"]]
    for stage in order:
        stage()


def _proj(ctx_src, x_src, off, mods, g1, w, layer, a2s, abias, conv_w, gate_bias, cos_t, sin_t, dl_row):
    B = x_src.shape[0]
    nt = x_src.shape[1] // TILE + off
    T = nt * TILE
    tile = lambda width: pl.BlockSpec((B, TILE, width), lambda i: (0, i, 0))
    const = lambda shape: pl.BlockSpec(shape, lambda i: (0,) * len(shape))
    per_halo = TILE // HALO
    n_halo = x_src.shape[1] // HALO
    nxt = lambda i: jnp.minimum(i + 1, nt - 1)
    blk = lambda t: jnp.maximum(t - off, 0)
    prev_halo = lambda t: jnp.maximum((t - off) * per_halo - 1, 0)
    next_halo = lambda t: jnp.clip((t - off + 1) * per_halo, 0, n_halo - 1)
    tok_shape = lambda width, dt: jax.ShapeDtypeStruct((B, T, width), dt)
    return pl.pallas_call(
        functools.partial(_proj_kernel, nt),
        out_shape=(tok_shape(6 * NQK_S, BF16), tok_shape(2 * BRANCH_W, BF16),
                   jax.ShapeDtypeStruct((B, T // CHUNK, 2 * NQK_S, LANES), F32),
                   tok_shape(2 * NQK_M, BF16), tok_shape(2 * NQK_M, BF16), tok_shape(2 * BRANCH_W, BF16),
                   jax.ShapeDtypeStruct((B, 2 * PAIRS, T, 2 * CHUNK), F32),
                   tok_shape(MC_WIDTH, F32),
                   jax.ShapeDtypeStruct((B, T // CHUNK, 2 * NHD, LANES), F32),
                   tok_shape(6 * NQK_S, BF16), tok_shape(2 * BRANCH_W, BF16)),
        grid=(nt,),
        in_specs=[pl.BlockSpec((1, TILE, D_MODEL), lambda i: (1, blk(i), 0)),
                  pl.BlockSpec((1, HALO, D_MODEL), lambda i: (1, prev_halo(i), 0)),
                  pl.BlockSpec((1, HALO, D_MODEL), lambda i: (1, next_halo(i), 0)),
                  pl.BlockSpec((1, TILE, D_MODEL), lambda i: (0, blk(nxt(i)), 0)),
                  pl.BlockSpec((1, HALO, D_MODEL), lambda i: (0, prev_halo(nxt(i)), 0)),
                  pl.BlockSpec((1, HALO, D_MODEL), lambda i: (0, next_halo(nxt(i)), 0)),
                  pl.BlockSpec((B, TILE, D_MODEL), lambda i: (0, 0, 0)),
                  const((B + 1, 6, D_MODEL)), const((1, D_MODEL)),
                  pl.BlockSpec((1, D_MODEL, W_PROJ), lambda i: (layer, 0, 0), pipeline_mode=pl.Buffered(1)),
                  const((4 * GLA_RANK, 2 * NQK_S)), const((1, 2 * NQK_S)),
                  const((3, 2 * NQK_M)), const((1, 2 * NHD)),
                  pl.BlockSpec((TILE, NQK_S), lambda i: (i, 0)),
                  pl.BlockSpec((TILE, NQK_S), lambda i: (i, 0)),
                  const((1, 2 * NQK_S))],
        out_specs=(tile(6 * NQK_S), tile(2 * BRANCH_W),
                   pl.BlockSpec((B, CHUNKS_PER_TILE, 2 * NQK_S, LANES), lambda i: (0, i, 0, 0)),
                   tile(2 * NQK_M), tile(2 * NQK_M), tile(2 * BRANCH_W),
                   pl.BlockSpec((B, 2 * PAIRS, TILE, 2 * CHUNK), lambda i: (0, 0, i, 0)),
                   tile(MC_WIDTH),
                   pl.BlockSpec((B, CHUNKS_PER_TILE, 2 * NHD, LANES), lambda i: (0, i, 0, 0)),
                   tile(6 * NQK_S), tile(2 * BRANCH_W)),
        scratch_shapes=[pltpu.VMEM((TILE + 2 * HALO, 2 * NQK_M), F32)] * 2
                       + [pltpu.VMEM((TILE, PW_MLR_WIDTH), F32)] * 2
                       + [pltpu.VMEM((TILE, PW_GR_WIDTH), F32)] * 2
                       + [pltpu.VMEM((6, TILE, NQK_S), F32), pltpu.VMEM((2, TILE, TILE), BF16)]
                       + [pltpu.VMEM((TILE, D_MODEL), BF16)] * 2,
        compiler_params=_cparams(1, VMEM_LIMIT_PROJ), name="proj",
    )(x_src, x_src, x_src, x_src, x_src, x_src, ctx_src,
      mods, g1, w, a2s, abias, conv_w, gate_bias, cos_t, sin_t, dl_row)


def _mix_kernel(batch, *refs):
    (gq_f, gq_b, gk_f, gk_b, ge_f, ge_b, gv_f, gv_b, gd_f, gd_b,
     rq_f, rq_b, rk_f, rk_b, re_f, re_b, rv_f, rv_b, rdl,
     mq_f, mq_b, mk_f, mk_b, mw_f, mw_b, mv_f, mv_b, mp_f, mp_b, mcol_f, mcol_b, mc_f, mc_b,
     of_ref, ob_ref, sg_ref, sr_ref, sm_ref, m_ref) = refs
    j = pl.program_id(0)

    @pl.when(j == 0)
    def _():
        sg_ref[...] = jnp.zeros_like(sg_ref)
        sr_ref[...] = jnp.zeros_like(sr_ref)
        sm_ref[...] = jnp.zeros_like(sm_ref)
        m_ref[...] = jnp.full_like(m_ref, M_INIT)

    row_head = lax.broadcasted_iota(jnp.int32, (2 * CHUNK, PAIR_V), 0) // CHUNK
    diag_kv = row_head == lax.broadcasted_iota(jnp.int32, (2 * CHUNK, PAIR_V), 1) // DV
    diag_kk = (lax.broadcasted_iota(jnp.int32, (2 * CHUNK, PAIR_K), 0) // CHUNK
               == lax.broadcasted_iota(jnp.int32, (2 * CHUNK, PAIR_K), 1) // DK_SMALL)
    t_idx = lax.broadcasted_iota(jnp.int32, (CHUNK, 2 * CHUNK), 0)
    s_idx = lax.broadcasted_iota(jnp.int32, (CHUNK, 2 * CHUNK), 1) % CHUNK
    causal = (s_idx <= t_idx, s_idx >= t_idx)
    ones_v = jnp.ones((CHUNK, DV), BF16)
    zeros_vext = jnp.zeros((CHUNK, PAIR_V), BF16)
    ret_dec = jnp.exp(float(CHUNK) * _log_sigmoid(rdl[...]))

    decay_sets = (
        ((gq_f, gk_f, ge_f, gv_f), (gq_b, gk_b, ge_b, gv_b), sg_ref, 0),
        ((rq_f, rk_f, re_f, rv_f), (rq_b, rk_b, re_b, rv_b), sr_ref, 2 * BRANCH_W),
    )
    ml_sets = ((mq_f, mk_f, mw_f, mv_f, mp_f, mcol_f, mc_f),
               (mq_b, mk_b, mw_b, mv_b, mp_b, mcol_b, mc_b))
    outs = (of_ref, ob_ref)


    def decay_phase1(bb, d, c, rows, bi, p, refs):
        q_r, k_r, e_r, v_r = refs
        ks = slice(p * PAIR_K, (p + 1) * PAIR_K)
        q2 = q_r[bb, rows, ks]
        k2 = k_r[bb, rows, ks]
        v2 = v_r[bb, rows, p * PAIR_V:(p + 1) * PAIR_V]
        k_bd = jnp.where(diag_kk, jnp.concatenate([k2, k2], axis=0), 0.0)
        dec = (gd_f if d == 0 else gd_b)[bb, c, ks, :] if bi == 0 else ret_dec[d, ks, :]
        upd = jnp.where(diag_kv, _dot_tn(e_r[bb, rows, ks], v2), 0.0)
        return dict(q2=q2, v2=v2, dec=jnp.concatenate([dec, dec], axis=1), upd=upd,
                    att=_dot_nt(q2, k_bd))

    def decay_phase2(bb, d, p, rows, o_ref, s_ref, base, w):
        s_prev = s_ref[bb, d, p]
        att = jnp.where(causal[d], w["att"], 0.0).astype(BF16)
        v_bd = jnp.where(diag_kv, jnp.concatenate([w["v2"], w["v2"]], axis=0), 0.0)
        lhs = jnp.concatenate([att, w["q2"]], axis=1)
        rhs = jnp.concatenate([v_bd, s_prev.astype(BF16)], axis=0)
        o_ref[bb, rows, base + p * PAIR_V:base + (p + 1) * PAIR_V] = _dot(lhs, rhs)
        s_ref[bb, d, p] = w["dec"] * s_prev + w["upd"]

    def ml_phase1(bb, d, c, rows, p, refs):
        q_r, k_r, w_r, v_r, p_r, col_r, c_r = refs
        cols = col_r[bb, rows, :]
        q2 = q_r[bb, rows, p * 2 * DK_ML:(p + 1) * 2 * DK_ML]
        k2 = k_r[bb, rows, p * 2 * DK_ML:(p + 1) * 2 * DK_ML]
        k_bd = jnp.where(diag_kv, jnp.concatenate([k2, k2], axis=0), 0.0)
        att = _dot_nt(q2, k_bd)
        heads = []
        for hl in range(2):
            hd = 2 * p + hl
            jj = d * HEADS + hd
            vext = jnp.concatenate([v_r[bb, rows, hd * DV:(hd + 1) * DV], ones_v], axis=1)
            st = sm_ref[bb, d, hd]
            heads.append(dict(
                hd=hd, st=st, vext=vext,
                inter=_dot(q2[:, hl * DK_ML:(hl + 1) * DK_ML], st.astype(BF16)),
                upd=_dot_tn(w_r[bb, rows, hd * DK_ML:(hd + 1) * DK_ML], vext),
                u=jnp.broadcast_to(cols[:, MC_U + jj:MC_U + jj + 1], (CHUNK, LANES)),
                b=jnp.broadcast_to(cols[:, MC_B + jj:MC_B + jj + 1], (CHUNK, LANES))))
        return dict(att=att, heads=heads, p0=p_r[bb, p, rows, :], cs=c_r[bb, c])

    def ml_phase2(bb, d, rows, o_ref, w):
        sc = (w["att"] * w["p0"]).astype(BF16)
        for hl, hw in enumerate(w["heads"]):
            hd = hw["hd"]
            jj = d * HEADS + hd
            vz = (jnp.concatenate([hw["vext"], zeros_vext], axis=0) if hl == 0
                  else jnp.concatenate([zeros_vext, hw["vext"]], axis=0))
            intra = _dot(sc, vz)
            m_prev = m_ref[bb, jj:jj + 1, :]
            u = hw["u"]
            mx = jnp.maximum(u, m_prev)
            f = jnp.exp(u - mx)
            e = jnp.exp(m_prev - mx)
            num = f * intra[:, 0:DV] + e * hw["inter"][:, 0:DV]
            den = f * intra[:, DV:] + e * hw["inter"][:, DV:]
            den = jnp.maximum(jnp.abs(den), jnp.exp(-(hw["b"] + mx)))
            o_ref[bb, rows, BRANCH_W + hd * DV:BRANCH_W + (hd + 1) * DV] = num / den
            tot = w["cs"][jj:jj + 1, :]
            mloc = w["cs"][NHD + jj:NHD + jj + 1, :]
            m_new = jnp.maximum(tot + m_prev, mloc)
            keep = jnp.exp(tot + m_prev - m_new)
            gain = jnp.exp(mloc - m_new)
            sm_ref[bb, d, hd] = (jnp.concatenate([keep, keep], axis=1) * hw["st"]
                                 + jnp.concatenate([gain, gain], axis=1) * hw["upd"])
            m_ref[bb, jj:jj + 1, :] = m_new

    units = []
    for step in range(CHUNKS_PER_TILE):
        for bb, d in ((b_, d_) for b_ in range(batch) for d_ in range(2)):
            c = step if d == 0 else CHUNKS_PER_TILE - 1 - step
            rows = pl.ds(c * CHUNK, CHUNK)
            for bi, (fset, bset, s_ref, base) in enumerate(decay_sets):
                for p in range(PAIRS):
                    units.append((functools.partial(decay_phase1, bb, d, c, rows, bi, p, fset if d == 0 else bset),
                                  functools.partial(decay_phase2, bb, d, p, rows, outs[d], s_ref, base)))
            for p in range(PAIRS):
                units.append((functools.partial(ml_phase1, bb, d, c, rows, p, ml_sets[d]),
                              functools.partial(ml_phase2, bb, d, rows, outs[d])))

    assert MIX_LOOKAHEAD < len(units) // CHUNKS_PER_TILE
    pending = []
    for phase1, phase2 in units:
        pending.append((phase2, phase1()))
        if len(pending) > MIX_LOOKAHEAD:
            fn, w = pending.pop(0)
            fn(w)
    for fn, w in pending:
        fn(w)


def _mix(gla, ret, ml, ret_dl_col):
    g_qke, g_vs, gdec = gla
    r_qke, r_vs = ret
    m_qk, mw, m_vs, mp, mcol, mc = ml
    B, T, _ = g_vs.shape
    nt = T // TILE

    def tf(j):
        return j

    def tb(j):
        return jnp.where(j == 0, 0, nt - j)

    def pair(width, lane_block_b=0, lane_block_f=0):
        return [pl.BlockSpec((B, TILE, width), lambda j: (0, tf(j), lane_block_f)),
                pl.BlockSpec((B, TILE, width), lambda j: (0, tb(j), lane_block_b))]

    dec_pair = [pl.BlockSpec((B, CHUNKS_PER_TILE, NQK_S, LANES), lambda j: (0, tf(j), 0, 0)),
                pl.BlockSpec((B, CHUNKS_PER_TILE, NQK_S, LANES), lambda j: (0, tb(j), 1, 0))]
    p0_pair = [pl.BlockSpec((B, PAIRS, TILE, 2 * CHUNK), lambda j: (0, 0, tf(j), 0)),
               pl.BlockSpec((B, PAIRS, TILE, 2 * CHUNK), lambda j: (0, 1, tb(j), 0))]
    cs_pair = [pl.BlockSpec((B, CHUNKS_PER_TILE, 2 * NHD, LANES), lambda j: (0, tf(j), 0, 0)),
               pl.BlockSpec((B, CHUNKS_PER_TILE, 2 * NHD, LANES), lambda j: (0, tb(j), 0, 0))]
    decay_specs = pair(NQK_S, 1, 0) + pair(NQK_S, 3, 2) + pair(NQK_S, 5, 4) + pair(BRANCH_W)
    in_specs = (decay_specs + dec_pair + decay_specs
                + [pl.BlockSpec((2, NQK_S, LANES), lambda j: (0, 0, 0))]
                + pair(NQK_M) + pair(NQK_M, 1, 1) + pair(NQK_M, 1) + pair(BRANCH_W) + p0_pair
                + pair(MC_WIDTH) + cs_pair)
    args = (g_qke, g_qke, g_qke, g_qke, g_qke, g_qke, g_vs, g_vs, gdec, gdec,
            r_qke, r_qke, r_qke, r_qke, r_qke, r_qke, r_vs, r_vs, ret_dl_col,
            m_qk, m_qk, m_qk, m_qk, mw, mw, m_vs, m_vs, mp, mp, mcol, mcol, mc, mc)
    return pl.pallas_call(
        functools.partial(_mix_kernel, B),
        out_shape=(jax.ShapeDtypeStruct((B, T, 3 * BRANCH_W), F32),) * 2,
        grid=(nt,),
        in_specs=in_specs,
        out_specs=(pl.BlockSpec((B, TILE, 3 * BRANCH_W), lambda j: (0, tf(j), 0)),
                   pl.BlockSpec((B, TILE, 3 * BRANCH_W), lambda j: (0, tb(j), 0))),
        scratch_shapes=[pltpu.VMEM((B, 2, PAIRS, PAIR_K, PAIR_V), F32),
                        pltpu.VMEM((B, 2, PAIRS, PAIR_K, PAIR_V), F32),
                        pltpu.VMEM((B, 2, HEADS, DK_ML, 2 * DV), F32),
                        pltpu.VMEM((B, NHD, LANES), F32)],
        compiler_params=_cparams(1), name="mix",
    )(*args)


def _merge_kernel(batch, first_tile, x_ref, ctx_ref, mod_ref, g_ref, wg_ref, of_ref, ob_ref,
                  sgg_ref, sgm_ref, sgr_ref, ng_ref, wb_ref, wo_ref, out_ref):
    tile_idx = pl.program_id(0) + first_tile
    side = (sgg_ref, sgm_ref, sgr_ref)
    mods = [mod_ref[jnp.where(tile_idx == 0, batch, bb)] for bb in range(batch)]
    xs = [jnp.where(tile_idx == 0, ctx_ref[bb], x_ref[bb]) for bb in range(batch)]
    hs = [_norm_mod(xs[bb], g_ref[...], mods[bb][0:1, :], mods[bb][1:2, :]).astype(BF16)
          for bb in range(batch)]
    accs = [jnp.zeros((TILE, D_MODEL), F32) for _ in range(batch)]
    for br in range(3):
        for bb in range(batch):
            parts = []
            for hd in range(HEADS):
                cols = slice(br * BRANCH_W + hd * DV, br * BRANCH_W + (hd + 1) * DV)
                oh = of_ref[bb, :, cols] + ob_ref[bb, :, cols]
                if br > 0:
                    oh = oh - jnp.mean(oh, axis=-1, keepdims=True)
                parts.append(oh * lax.rsqrt(jnp.mean(oh * oh, axis=-1, keepdims=True) + EPS))
            y = jnp.concatenate(parts, axis=1) * ng_ref[:, br * BRANCH_W:(br + 1) * BRANCH_W]
            y = (y * side[br][bb]).astype(BF16)
            gate = _sigmoid(_dot(hs[bb], wg_ref[0, :, br * D_MODEL:(br + 1) * D_MODEL]))
            accs[bb] = accs[bb] + gate * _dot(y, wb_ref[0, br])
    for bb in range(batch):
        out = _dot(accs[bb].astype(BF16), wo_ref[0])
        out_ref[bb] = xs[bb] + mods[bb][2:3, :] * out


def _merge(ctx_src, x_src, off, mods, g1, wgate, o_f, o_b, sg_gla, sz_ml, sg_ret, norm_g, wb, wo, layer, skip):
    B = x_src.shape[0]
    nt = x_src.shape[1] // TILE + off - skip
    tile = lambda width: pl.BlockSpec((B, TILE, width), lambda i: (0, i + skip, 0))
    gate_tile = pl.BlockSpec((B, TILE, BRANCH_W), lambda i: (0, i + skip, 1))
    const = lambda shape: pl.BlockSpec(shape, lambda i: (0,) * len(shape))
    per_layer = lambda shape: pl.BlockSpec((1,) + shape, lambda i: (layer,) + (0,) * len(shape),
                                           pipeline_mode=pl.Buffered(1))
    return pl.pallas_call(
        functools.partial(_merge_kernel, B, skip),
        out_shape=jax.ShapeDtypeStruct((B, nt * TILE, D_MODEL), F32),
        grid=(nt,),
        in_specs=[pl.BlockSpec((B, TILE, D_MODEL), lambda i: (0, jnp.maximum(i + skip - off, 0), 0)),
                  pl.BlockSpec((B, TILE, D_MODEL), lambda i: (0, 0, 0)),
                  const((B + 1, 6, D_MODEL)),
                  const((1, D_MODEL)), per_layer((D_MODEL, W_GATE)),
                  tile(3 * BRANCH_W), tile(3 * BRANCH_W),
                  gate_tile, gate_tile, gate_tile,
                  const((1, 3 * BRANCH_W)), per_layer((3, BRANCH_W, D_MODEL)),
                  per_layer((D_MODEL, D_MODEL))],
        out_specs=pl.BlockSpec((B, TILE, D_MODEL), lambda i: (0, i, 0)),
        compiler_params=_cparams(1), name="merge",
    )(x_src, ctx_src, mods, g1, wgate, o_f, o_b, sg_gla, sz_ml, sg_ret, norm_g, wb, wo)


def _mlp_kernel(batch, first_tile, final, x_ref, mod_ref, g_ref, wu_ref, wd_ref, fg_ref, out_ref):
    tile_idx = pl.program_id(0) + first_tile
    mods = [mod_ref[jnp.where(tile_idx == 0, batch, bb)] for bb in range(batch)]
    ups = []
    for bb in range(batch):
        h = _norm_mod(x_ref[bb], g_ref[...], mods[bb][3:4, :], mods[bb][4:5, :]).astype(BF16)
        ups.append(_dot(h, wu_ref[0]))
    for bb in range(batch):
        u = jnp.maximum(ups[bb], 0.0)
        y = x_ref[bb] + mods[bb][5:6, :] * _dot((u * u).astype(BF16), wd_ref[0])
        if final:
            y = y * lax.rsqrt(jnp.mean(y * y, axis=-1, keepdims=True) + EPS) * fg_ref[...]
        out_ref[bb] = y


def _mlp(tok, mods, g2, wu, wd, layer, final_g, first_tile, final):
    B, T, _ = tok.shape
    const = lambda shape: pl.BlockSpec(shape, lambda i: (0,) * len(shape))
    per_layer = lambda shape: pl.BlockSpec((1,) + shape, lambda i: (layer,) + (0,) * len(shape),
                                           pipeline_mode=pl.Buffered(1))
    return pl.pallas_call(
        functools.partial(_mlp_kernel, B, first_tile, final),
        out_shape=jax.ShapeDtypeStruct((B, T, D_MODEL), F32),
        grid=(T // TILE,),
        in_specs=[pl.BlockSpec((B, TILE, D_MODEL), lambda i: (0, i, 0)),
                  const((B + 1, 6, D_MODEL)),
                  const((1, D_MODEL)), per_layer((D_MODEL, D_FF)), per_layer((D_FF, D_MODEL)),
                  const((1, D_MODEL))],
        out_specs=pl.BlockSpec((B, TILE, D_MODEL), lambda i: (0, i, 0)),
        compiler_params=_cparams(1), name="mlp",
    )(tok, mods, g2, wu, wd, final_g)


def kernel(x, c, ctx, c_ctx, norm1_g, norm2_g, w_ada, b_ada, w_in, gla_a2, gla_a_bias, gla_norm_g,
           ml_conv, ml_i_bias, ml_f_bias, ml_norm_g, ret_decay_logit, ret_norm_g, w_branch, w_o,
           w_up, w_down, final_g):
    B, seq, _ = x.shape
    ctx_len = ctx.shape[1]
    assert B == 2 and ctx_len == TILE and seq % TILE == 0 and seq % GRID_W == 0
    ctx_src, x_src, off = ctx, x, 1

    c_rows = jnp.concatenate([c, c_ctx[None, :], jnp.zeros((8 - B - 1, D_MODEL), F32)], axis=0)
    mods_all = _ada(c_rows, w_ada, b_ada)[:, 0:3].reshape(DEPTH, 3, 6, D_MODEL)

    cos_t, sin_t = _rope_tables(seq, ctx_len)

    w_proj, w_gate = _wprep(w_in)
    w_branch_b = w_branch.astype(BF16)
    w_o_b = w_o.astype(BF16)
    w_up_b = w_up.astype(BF16)
    w_down_b = w_down.astype(BF16)
    for l in range(DEPTH):
        last = l == DEPTH - 1
        skip = 1 if last else 0
        mods = mods_all[l]
        g1 = norm1_g[l][None, :]
        g2 = norm2_g[l][None, :]
        zeros = jnp.zeros((GLA_RANK, NQK_S), F32)
        a2blk = jnp.concatenate([jnp.concatenate([gla_a2[l, 0], zeros], axis=1),
                                 jnp.concatenate([zeros, gla_a2[l, 1]], axis=1)], axis=0)
        a2s = jnp.concatenate([a2blk, a2blk], axis=0).astype(BF16)
        abias = gla_a_bias[l].reshape(1, 2 * NQK_S)
        gate_bias = jnp.concatenate([ml_i_bias[l].reshape(1, NHD), ml_f_bias[l].reshape(1, NHD)], axis=1)
        dl = ret_decay_logit[l]
        dl_row = jnp.repeat(dl, DK_SMALL, axis=1).reshape(1, 2 * NQK_S)
        dl_col = jnp.broadcast_to(jnp.repeat(dl, DK_SMALL, axis=1)[:, :, None], (2, NQK_S, LANES))

        (g_qke, g_vs, gdec, m_qk, mw, m_vs, mp, mcol, mc, r_qke, r_vs) = _proj(
            ctx_src, x_src, off, mods, g1, w_proj, l, a2s, abias, ml_conv[l], gate_bias, cos_t, sin_t, dl_row)
        o_f, o_b = _mix((g_qke, g_vs, gdec), (r_qke, r_vs), (m_qk, mw, m_vs, mp, mcol, mc), dl_col)
        norm_g = jnp.concatenate([gla_norm_g[l], ml_norm_g[l], ret_norm_g[l]])[None, :]
        tok = _merge(ctx_src, x_src, off, mods, g1, w_gate, o_f, o_b, g_vs, m_vs, r_vs, norm_g,
                     w_branch_b, w_o_b, l, skip)
        tok = _mlp(tok, mods, g2, w_up_b, w_down_b, l, final_g[None, :], skip, last)
        ctx_src, x_src, off = tok, tok, 0
    return tok
```

```python
import functools

import jax
import jax.numpy as jnp
from jax import lax
from jax.experimental import pallas as pl
from jax.experimental.pallas import tpu as pltpu

F32 = jnp.float32
BF16 = jnp.bfloat16

D_MODEL = 1024
DEPTH = 2
GRID_W = 64
CHUNK = 64
EPS = 1e-6
M_INIT = -1e30
HEADS = 4
DK_SMALL = 64
DK_ML = 128
DV = 128
BRANCH_W = HEADS * DV
GLA_RANK = 16
GLA_TAU = 16.0
ROPE_BASE = 10000.0
D_FF = 4 * D_MODEL
LANES = 128

TILE = 256
CHUNKS_PER_TILE = TILE // CHUNK
HALO = 8
VMEM_LIMIT = 56 * 1024 * 1024
VMEM_LIMIT_PROJ = 60 * 1024 * 1024

NQK_S = HEADS * DK_SMALL
NQK_M = HEADS * DK_ML
NHD = 2 * HEADS
W_GLA = 2 * NQK_S + 2 * BRANCH_W + 2 * GLA_RANK
W_ML = 2 * NQK_M + 2 * BRANCH_W + 2 * NHD
W_RET = 2 * NQK_S + 2 * BRANCH_W
W_GATE = 3 * D_MODEL

PW_MLQK = 0
PW_MLR = PW_MLQK + 2 * NQK_M
PW_MLR_WIDTH = 2 * BRANCH_W + LANES
PW_GR = PW_MLR + PW_MLR_WIDTH
PW_GR_WIDTH = 2 * (2 * NQK_S + 2 * BRANCH_W) + LANES
W_PROJ = PW_GR + PW_GR_WIDTH
MLR_V, MLR_Z, MLR_GATES = 0, BRANCH_W, 2 * BRANCH_W
GR_GLA_Q, GR_GLA_K, GR_GLA_V, GR_GLA_G = 0, NQK_S, 2 * NQK_S, 2 * NQK_S + BRANCH_W
GR_RET_Q = 2 * NQK_S + 2 * BRANCH_W
GR_RET_K, GR_RET_V, GR_RET_G = GR_RET_Q + NQK_S, GR_RET_Q + 2 * NQK_S, GR_RET_Q + 2 * NQK_S + BRANCH_W
GR_GLA_A = 2 * GR_RET_Q

MC_B, MC_W, MC_U = 0, NHD, 2 * NHD
MC_WIDTH = 3 * NHD

PAIRS = HEADS // 2
PAIR_K = 2 * DK_SMALL
PAIR_V = 2 * DV
MIX_LOOKAHEAD = 23


def _cparams(n_axes, vmem_limit=VMEM_LIMIT):
    return pltpu.CompilerParams(dimension_semantics=("arbitrary",) * n_axes,
                                vmem_limit_bytes=vmem_limit)


def _log_sigmoid(z):
    return jnp.minimum(z, 0.0) - jnp.log(1.0 + jnp.exp(-jnp.abs(z)))


def _sigmoid(z):
    return 1.0 / (1.0 + jnp.exp(-z))


def _silu(z):
    return z * _sigmoid(z)


def _norm_mod(x, g, shift, scale):
    return x * lax.rsqrt(jnp.mean(x * x, axis=-1, keepdims=True) + EPS) * (g * (1.0 + scale)) + shift


def _dot(a, b):
    return jnp.dot(a, b, preferred_element_type=F32)


def _dot_nt(a, b):
    return lax.dot_general(a, b, (((1,), (1,)), ((), ())), preferred_element_type=F32)


def _dot_tn(a, b):
    return lax.dot_general(a, b, (((0,), (0,)), ((), ())), preferred_element_type=F32)


def _split2(x):
    hi = x.astype(BF16)
    return hi, (x - hi.astype(F32)).astype(BF16)


def _chunk_tri(lower):
    r = lax.broadcasted_iota(jnp.int32, (TILE, TILE), 0)
    c = lax.broadcasted_iota(jnp.int32, (TILE, TILE), 1)
    same = (r // CHUNK) == (c // CHUNK)
    tri = (c <= r) if lower else (c >= r)
    return jnp.where(same & tri, 1.0, 0.0).astype(BF16)


def _chunk_total(x):
    w = x.shape[-1]
    tot = jnp.sum(x.reshape(CHUNKS_PER_TILE, CHUNK, w), axis=1, keepdims=True)
    return jnp.broadcast_to(tot, (CHUNKS_PER_TILE, CHUNK, w)).reshape(TILE, w)


def _ada_kernel(c_ref, w_ref, b_ref, o_ref):
    s = _silu(c_ref[...])
    s_hi = s.astype(BF16).astype(F32)
    s2 = jnp.concatenate([s_hi, s - s_hi], axis=0).astype(BF16)
    w_hi, w_lo = _split2(w_ref[0])
    r = _dot(s2, w_hi) + _dot(s2, w_lo)
    o_ref[0] = r[0:8] + r[8:16] + b_ref[0]


def _ada(c_rows, w_ada, b_ada):
    tn = 1536
    return pl.pallas_call(
        _ada_kernel,
        out_shape=jax.ShapeDtypeStruct((DEPTH, 8, 6 * D_MODEL), F32),
        grid=(DEPTH, 6 * D_MODEL // tn),
        in_specs=[pl.BlockSpec((8, D_MODEL), lambda l, n: (0, 0)),
                  pl.BlockSpec((1, D_MODEL, tn), lambda l, n: (l, 0, n)),
                  pl.BlockSpec((1, 1, tn), lambda l, n: (l, 0, n))],
        out_specs=pl.BlockSpec((1, 8, tn), lambda l, n: (l, 0, n)),
        compiler_params=_cparams(2), name="ada",
    )(c_rows, w_ada, b_ada.reshape(DEPTH, 1, 6 * D_MODEL))


def _rope_kernel(ang_ref, cos_ref, sin_ref, nsin_ref):
    a = ang_ref[...]
    s = jnp.sin(a)
    cos_ref[...] = jnp.cos(a)
    sin_ref[...] = s
    nsin_ref[...] = -s


def _rope_tables(seq, ctx_len):
    rows = seq // GRID_W
    nf = DK_SMALL // 4
    inv = ROPE_BASE ** (-jnp.arange(nf, dtype=F32) / nf)
    ang = jnp.concatenate([jnp.arange(rows, dtype=F32)[:, None] * inv,
                           jnp.arange(GRID_W, dtype=F32)[:, None] * inv], axis=0)
    n = rows + GRID_W
    cos, sin, nsin = pl.pallas_call(
        _rope_kernel,
        out_shape=(jax.ShapeDtypeStruct((n, nf), F32),) * 3,
        name="rope_tables",
    )(ang)

    def per_token(first, second):
        def half(t):
            r = jnp.broadcast_to(t[:rows, None, :], (rows, GRID_W, nf))
            c = jnp.broadcast_to(t[None, rows:, :], (rows, GRID_W, nf))
            return jnp.concatenate([r, c], axis=-1)
        head = jnp.concatenate([half(first), half(second)], axis=-1).reshape(seq, DK_SMALL)
        return jnp.tile(head, (1, HEADS))

    ones = jnp.ones((ctx_len, NQK_S), F32)
    cos_t = jnp.concatenate([ones, per_token(cos, cos)], axis=0)
    sin_t = jnp.concatenate([0.0 * ones, per_token(nsin, sin)], axis=0)
    return cos_t, sin_t


def _wprep_kernel(w_ref, wp_ref, wg_ref):
    o0, o1, o2 = W_GLA, W_GLA + W_ML, W_GLA + W_ML + W_RET
    n_sg = 2 * NQK_S + 2 * BRANCH_W
    rows = w_ref.shape[1]

    def copy(dst_ref, dst, src, n):
        dst_ref[0, :, dst:dst + n] = w_ref[0, :, src:src + n].astype(BF16)

    copy(wp_ref, PW_MLQK, o0, 2 * NQK_M)
    n_mlr = 2 * BRANCH_W + 2 * NHD
    copy(wp_ref, PW_MLR, o0 + 2 * NQK_M, n_mlr)
    wp_ref[0, :, PW_MLR + n_mlr:PW_GR] = jnp.zeros((rows, PW_GR - PW_MLR - n_mlr), BF16)
    copy(wp_ref, PW_GR + GR_GLA_Q, 0, n_sg)
    copy(wp_ref, PW_GR + GR_RET_Q, o1, W_RET)
    copy(wp_ref, PW_GR + GR_GLA_A, n_sg, 2 * GLA_RANK)
    pad0 = PW_GR + GR_GLA_A + 2 * GLA_RANK
    wp_ref[0, :, pad0:W_PROJ] = jnp.zeros((rows, W_PROJ - pad0), BF16)
    copy(wg_ref, 0, o2, W_GATE)


def _wprep(w_in):
    rows = 256
    return pl.pallas_call(
        _wprep_kernel,
        out_shape=(jax.ShapeDtypeStruct((DEPTH, D_MODEL, W_PROJ), BF16),
                   jax.ShapeDtypeStruct((DEPTH, D_MODEL, W_GATE), BF16)),
        grid=(DEPTH, D_MODEL // rows),
        in_specs=[pl.BlockSpec((1, rows, w_in.shape[2]), lambda l, r: (l, r, 0))],
        out_specs=(pl.BlockSpec((1, rows, W_PROJ), lambda l, r: (l, r, 0)),
                   pl.BlockSpec((1, rows, W_GATE), lambda l, r: (l, r, 0))),
        compiler_params=_cparams(2), name="wprep",
    )(w_in)


def _proj_kernel(nt, xa_ref, xap_ref, xan_ref, xb_ref, xbp_ref, xbn_ref, ctx_ref,
                 mod_ref, g_ref, w_ref, a2_ref, ab_ref, cw_ref, gb_ref, cos_ref, sin_ref, dl_ref,
                 gqke_ref, gvs_ref, gdec_ref, mqk_ref, mkw_ref, mvs_ref, mp0_ref, mcol_ref, mcs_ref,
                 rqke_ref, rvs_ref,
                 ext0_ref, ext1_ref, prml0_ref, prml1_ref, prgr0_ref, prgr1_ref, rdec_ref, tri_ref,
                 h0_ref, h1_ref):
    i = pl.program_id(0)
    ext_refs = (ext0_ref, ext1_ref)
    prml_refs = (prml0_ref, prml1_ref)
    prgr_refs = (prgr0_ref, prgr1_ref)
    h_refs = (h0_ref, h1_ref)
    row8 = lax.broadcasted_iota(jnp.int32, (NHD, TILE), 0)
    lane8 = lax.broadcasted_iota(jnp.int32, (NHD, TILE), 1)
    lane_chunk = lane8 // CHUNK
    lane_in_chunk = lane8 % CHUNK
    ti2 = lax.broadcasted_iota(jnp.int32, (CHUNK, 2 * CHUNK), 0)
    lane2 = lax.broadcasted_iota(jnp.int32, (CHUNK, 2 * CHUNK), 1)
    si2 = lane2 % CHUNK
    lane_first = lane2 < CHUNK

    def project_stages(x, x_prev, x_next, tile_idx, mod_row, slot):
        st = {}

        def mlstm_qk():
            mod = mod_ref[mod_row]
            x_all = jnp.concatenate([x_prev, x, x_next], axis=0)
            h_all = _norm_mod(x_all, g_ref[...], mod[0:1, :], mod[1:2, :])
            h_refs[slot][...] = h_all[HALO:HALO + TILE, :].astype(BF16)
            prev_ok = tile_idx >= 2
            next_ok = jnp.logical_and(tile_idx >= 1, tile_idx <= nt - 2)
            ext = _dot(h_all.astype(BF16), w_ref[0, :, PW_MLQK:PW_MLR])
            ext_refs[slot][HALO:HALO + TILE, :] = ext[HALO:HALO + TILE, :]
            ext_refs[slot][0:HALO, :] = jnp.where(prev_ok, ext[0:HALO, :], 0.0)
            ext_refs[slot][HALO + TILE:, :] = jnp.where(next_ok, ext[HALO + TILE:, :], 0.0)

        def mlstm_rest():
            prml_refs[slot][...] = _dot(h_refs[slot][...], w_ref[0, :, PW_MLR:PW_GR])

        def gla():
            prgr_refs[slot][:, 0:GR_RET_Q] = _dot(h_refs[slot][...], w_ref[0, :, PW_GR:PW_GR + GR_RET_Q])

        def ret():
            prgr_refs[slot][:, GR_RET_Q:] = _dot(h_refs[slot][...], w_ref[0, :, PW_GR + GR_RET_Q:])

        return [mlstm_qk, mlstm_rest, gla, ret]

    def project(*args):
        for stage in project_stages(*args):
            stage()

    @pl.when(i == 0)
    def _():
        tri_ref[0] = _chunk_tri(True)
        tri_ref[1] = _chunk_tri(False)
        lg = _log_sigmoid(dl_ref[...])
        t_in_chunk = (lax.broadcasted_iota(jnp.int32, (TILE, NQK_S), 0) % CHUNK).astype(F32)
        for d in range(2):
            lg_d = lg[:, d * NQK_S:(d + 1) * NQK_S]
            steps = (t_in_chunk + 1.0) if d == 0 else (float(CHUNK) - t_in_chunk)
            b = steps * lg_d
            rdec_ref[3 * d] = jnp.exp(b)
            rdec_ref[3 * d + 1] = jnp.exp(-b)
            rdec_ref[3 * d + 2] = jnp.exp(float(CHUNK) * lg_d - b)
        halo0 = jnp.zeros((HALO, D_MODEL), F32)
        project(ctx_ref[0], halo0, halo0, 0, 2, 0)

    def rope(x):
        lane = lax.broadcasted_iota(jnp.int32, (TILE, NQK_S), 1)
        first_half = (lane % DK_SMALL) < (DK_SMALL // 2)
        swapped = jnp.where(first_half, pltpu.roll(x, NQK_S - DK_SMALL // 2, 1),
                            pltpu.roll(x, DK_SMALL // 2, 1))
        return x * cos_ref[...] + swapped * sin_ref[...]

    def seg_cummax(x, reverse):
        sh = 1
        while sh < CHUNK:
            if reverse:
                y, ok = pltpu.roll(x, TILE - sh, 1), lane_in_chunk < CHUNK - sh
            else:
                y, ok = pltpu.roll(x, sh, 1), lane_in_chunk >= sh
            x = jnp.where(ok, jnp.maximum(x, y), x)
            sh *= 2
        return x

    def prepare_stages(slot, bb):
        ext_ref, prml_ref, prgr_ref = ext_refs[slot], prml_refs[slot], prgr_refs[slot]
        st = {}

        def conv():
            ext = ext_ref[...]
            n_ext = TILE + 2 * HALO
            prev = pltpu.roll(ext, 1, 0)[HALO:HALO + TILE, :]
            nxt = pltpu.roll(ext, n_ext - 1, 0)[HALO:HALO + TILE, :]
            qk = _silu(cw_ref[0:1, :] * prev + cw_ref[1:2, :] * ext[HALO:HALO + TILE, :]
                       + cw_ref[2:3, :] * nxt)
            mqk_ref[bb, :, 0:NQK_M] = qk[:, 0:NQK_M].astype(BF16)
            st["k_ml"] = qk[:, NQK_M:] * (DK_ML ** -0.5)
            mqk_ref[bb, :, NQK_M:] = st["k_ml"].astype(BF16)

        def gates_in():
            mvs_ref[bb, :, 0:BRANCH_W] = prml_ref[:,MLR_V:MLR_V + BRANCH_W].astype(BF16)
            mvs_ref[bb, :, BRANCH_W:] = _silu(prml_ref[:,MLR_Z:MLR_Z + BRANCH_W]).astype(BF16)
            gates_t = (prml_ref[:,MLR_GATES:MLR_GATES + 2 * NHD] + gb_ref[...]).T
            st["li_t"] = gates_t[0:NHD]
            st["lf_t"] = _log_sigmoid(gates_t[NHD:])
            lf_hi, lf_lo = _split2(st["lf_t"])
            st["lf2"] = jnp.concatenate([lf_hi, lf_lo], axis=0)

        def gates_mm():
            st["cum_f"] = _dot(st["lf2"], tri_ref[1])
            st["cum_b"] = _dot(st["lf2"], tri_ref[0])

        def gates_out():
            mlstm_gates_body(st["li_t"], st["lf_t"], st["cum_f"], st["cum_b"], st["k_ml"], bb)

        def gla_in():
            gvs_ref[bb, :, 0:BRANCH_W] = prgr_ref[:,GR_GLA_V:GR_GLA_V + BRANCH_W].astype(BF16)
            gvs_ref[bb, :, BRANCH_W:] = _silu(prgr_ref[:,GR_GLA_G:GR_GLA_G + BRANCH_W]).astype(BF16)
            a_hi, a_lo = _split2(prgr_ref[:,GR_GLA_A:GR_GLA_A + 2 * GLA_RANK])
            st["a2"] = jnp.concatenate([a_hi, a_lo], axis=1)

        def gla_z_mm():
            st["z"] = _dot(st["a2"], a2_ref[...])

        def gla_la():
            la = _log_sigmoid(st["z"] + ab_ref[...]) * (1.0 / GLA_TAU)
            st["la"] = la
            st["la_split"] = [_split2(la[:, d * NQK_S:(d + 1) * NQK_S]) for d in range(2)]

        def gla_cum_mm():
            st["b"] = []
            for d in range(2):
                tri = tri_ref[d]
                la_hi, la_lo = st["la_split"][d]
                st["b"].append(_dot(tri, la_hi) + _dot(tri, la_lo))

        def gla_out():
            gla_body(prgr_ref, st["la"], st["b"], bb)

        def ret():
            ret_body(prgr_ref, bb)

        return dict(conv=conv, gates_in=gates_in, gates_mm=gates_mm, gates_out=gates_out,
                    gla_in=gla_in, gla_z_mm=gla_z_mm, gla_la=gla_la, gla_cum_mm=gla_cum_mm,
                    gla_out=gla_out, ret=ret)

    def mlstm_gates_body(li_t, lf_t, cum_f, cum_b, k_ml, bb):
        b_t = jnp.where(row8 < HEADS, cum_f[0:NHD] + cum_f[NHD:], cum_b[0:NHD] + cum_b[NHD:])
        tot_t = jnp.zeros((NHD, TILE), F32)
        mloc_t = jnp.zeros((NHD, TILE), F32)
        for c in range(CHUNKS_PER_TILE):
            tot_c = jnp.sum(lf_t[:, c * CHUNK:(c + 1) * CHUNK], axis=-1, keepdims=True)
            tot_t = jnp.where(lane_chunk == c, tot_c, tot_t)
            mcs_ref[bb, c, 0:NHD, :] = jnp.broadcast_to(tot_c, (NHD, LANES))
        g_t = tot_t - b_t + li_t
        for c in range(CHUNKS_PER_TILE):
            mloc_c = jnp.max(g_t[:, c * CHUNK:(c + 1) * CHUNK], axis=-1, keepdims=True)
            mloc_t = jnp.where(lane_chunk == c, mloc_c, mloc_t)
            mcs_ref[bb, c, NHD:2 * NHD, :] = jnp.broadcast_to(mloc_c, (NHD, LANES))
        w_t = jnp.exp(g_t - mloc_t)
        r_t = li_t - b_t
        u_t = jnp.where(row8 < HEADS, seg_cummax(r_t, False), seg_cummax(r_t, True))
        cols = jnp.concatenate([b_t, w_t, u_t], axis=0).T
        mcol_ref[bb] = cols
        for j in range(NHD):
            d, hd = j // HEADS, j % HEADS
            mkw_ref[bb, :, d * NQK_M + hd * DK_ML:d * NQK_M + (hd + 1) * DK_ML] = (
                k_ml[:, hd * DK_ML:(hd + 1) * DK_ML] * cols[:, MC_W + j:MC_W + j + 1]).astype(BF16)
        for pp in range(NHD // 2):
            j0, j1 = 2 * pp, 2 * pp + 1
            mask = (si2 <= ti2) if pp < PAIRS else (si2 >= ti2)
            for c in range(CHUNKS_PER_TILE):
                rows = slice(c * CHUNK, (c + 1) * CHUNK)
                r2 = jnp.concatenate([r_t[j0:j0 + 1, rows], r_t[j1:j1 + 1, rows]], axis=1)
                u2 = jnp.where(lane_first, cols[rows, MC_U + j0:MC_U + j0 + 1],
                               cols[rows, MC_U + j1:MC_U + j1 + 1])
                mp0_ref[bb, pp, rows, :] = jnp.where(mask, jnp.exp(r2 - u2), 0.0)

    def gla_body(prgr_ref, la, cums, bb):
        half_t = jnp.sum(la.reshape(2 * CHUNKS_PER_TILE, CHUNK // 2, 2 * NQK_S), axis=1).T
        q_gla = prgr_ref[:,GR_GLA_Q:GR_GLA_Q + NQK_S] * (DK_SMALL ** -0.5)
        k_gla = prgr_ref[:,GR_GLA_K:GR_GLA_K + NQK_S]
        for d in range(2):
            la_d = la[:, d * NQK_S:(d + 1) * NQK_S]
            b = cums[d]
            tot = _chunk_total(la_d)
            decay = jnp.exp(b)
            gqke_ref[bb, :, d * NQK_S:(d + 1) * NQK_S] = (q_gla * decay).astype(BF16)
            gqke_ref[bb, :, (2 + d) * NQK_S:(3 + d) * NQK_S] = (k_gla * (1.0 / decay)).astype(BF16)
            gqke_ref[bb, :, (4 + d) * NQK_S:(5 + d) * NQK_S] = (k_gla * jnp.exp(tot - b)).astype(BF16)
            ht = half_t[d * NQK_S:(d + 1) * NQK_S, :]
            for c in range(CHUNKS_PER_TILE):
                tot_col = ht[:, 2 * c:2 * c + 1] + ht[:, 2 * c + 1:2 * c + 2]
                gdec_ref[bb, c, d * NQK_S:(d + 1) * NQK_S, :] = jnp.broadcast_to(jnp.exp(tot_col), (NQK_S, LANES))

    def ret_body(prgr_ref, bb):
        q_ret = rope(prgr_ref[:,GR_RET_Q:GR_RET_Q + NQK_S])
        k_ret = rope(prgr_ref[:,GR_RET_K:GR_RET_K + NQK_S] * (DK_SMALL ** -0.5))
        rvs_ref[bb, :, 0:BRANCH_W] = prgr_ref[:,GR_RET_V:GR_RET_V + BRANCH_W].astype(BF16)
        rvs_ref[bb, :, BRANCH_W:] = _silu(prgr_ref[:,GR_RET_G:GR_RET_G + BRANCH_W]).astype(BF16)
        for d in range(2):
            rqke_ref[bb, :, d * NQK_S:(d + 1) * NQK_S] = (q_ret * rdec_ref[3 * d]).astype(BF16)
            rqke_ref[bb, :, (2 + d) * NQK_S:(3 + d) * NQK_S] = (k_ret * rdec_ref[3 * d + 1]).astype(BF16)
            rqke_ref[bb, :, (4 + d) * NQK_S:(5 + d) * NQK_S] = (k_ret * rdec_ref[3 * d + 2]).astype(BF16)

    next_tile = jnp.minimum(i + 1, nt - 1)
    xa = jnp.where(i == 0, ctx_ref[1], xa_ref[0])
    a_ext, a_mlr, a_gla, a_ret = project_stages(xa, xap_ref[0], xan_ref[0], i, jnp.where(i == 0, 2, 1), 1)
    b_ext, b_mlr, b_gla, b_ret = project_stages(xb_ref[0], xbp_ref[0], xbn_ref[0], next_tile, 0, 0)
    x = prepare_stages(0, 0)
    y = prepare_stages(1, 1)
    order = [a_ext, x["conv"], x["gates_in"],
             a_mlr, x["gates_mm"], x["gates_out"], x["gla_in"],
             a_gla, x["gla_z_mm"], x["gla_la"],
             a_ret, x["gla_cum_mm"],
             b_ext, x["gla_out"], x["ret"], y["conv"], y["gates_in"],
             b_mlr, y["gates_mm"], y["gates_out"], y["gla_in"],
             b_gla, y["gla_z_mm"], y["gla_la"],
             b_ret, y["gla_cum_mm"], y["gla_out"], y["ret"]]
    for stage in order:
        stage()


def _proj(ctx_src, x_src, off, mods, g1, w, layer, a2s, abias, conv_w, gate_bias, cos_t, sin_t, dl_row):
    B = x_src.shape[0]
    nt = x_src.shape[1] // TILE + off
    T = nt * TILE
    tile = lambda width: pl.BlockSpec((B, TILE, width), lambda i: (0, i, 0))
    const = lambda shape: pl.BlockSpec(shape, lambda i: (0,) * len(shape))
    per_halo = TILE // HALO
    n_halo = x_src.shape[1] // HALO
    nxt = lambda i: jnp.minimum(i + 1, nt - 1)
    blk = lambda t: jnp.maximum(t - off, 0)
    prev_halo = lambda t: jnp.maximum((t - off) * per_halo - 1, 0)
    next_halo = lambda t: jnp.clip((t - off + 1) * per_halo, 0, n_halo - 1)
    tok_shape = lambda width, dt: jax.ShapeDtypeStruct((B, T, width), dt)
    return pl.pallas_call(
        functools.partial(_proj_kernel, nt),
        out_shape=(tok_shape(6 * NQK_S, BF16), tok_shape(2 * BRANCH_W, BF16),
                   jax.ShapeDtypeStruct((B, T // CHUNK, 2 * NQK_S, LANES), F32),
                   tok_shape(2 * NQK_M, BF16), tok_shape(2 * NQK_M, BF16), tok_shape(2 * BRANCH_W, BF16),
                   jax.ShapeDtypeStruct((B, 2 * PAIRS, T, 2 * CHUNK), F32),
                   tok_shape(MC_WIDTH, F32),
                   jax.ShapeDtypeStruct((B, T // CHUNK, 2 * NHD, LANES), F32),
                   tok_shape(6 * NQK_S, BF16), tok_shape(2 * BRANCH_W, BF16)),
        grid=(nt,),
        in_specs=[pl.BlockSpec((1, TILE, D_MODEL), lambda i: (1, blk(i), 0)),
                  pl.BlockSpec((1, HALO, D_MODEL), lambda i: (1, prev_halo(i), 0)),
                  pl.BlockSpec((1, HALO, D_MODEL), lambda i: (1, next_halo(i), 0)),
                  pl.BlockSpec((1, TILE, D_MODEL), lambda i: (0, blk(nxt(i)), 0)),
                  pl.BlockSpec((1, HALO, D_MODEL), lambda i: (0, prev_halo(nxt(i)), 0)),
                  pl.BlockSpec((1, HALO, D_MODEL), lambda i: (0, next_halo(nxt(i)), 0)),
                  pl.BlockSpec((B, TILE, D_MODEL), lambda i: (0, 0, 0)),
                  const((B + 1, 6, D_MODEL)), const((1, D_MODEL)),
                  pl.BlockSpec((1, D_MODEL, W_PROJ), lambda i: (layer, 0, 0), pipeline_mode=pl.Buffered(1)),
                  const((4 * GLA_RANK, 2 * NQK_S)), const((1, 2 * NQK_S)),
                  const((3, 2 * NQK_M)), const((1, 2 * NHD)),
                  pl.BlockSpec((TILE, NQK_S), lambda i: (i, 0)),
                  pl.BlockSpec((TILE, NQK_S), lambda i: (i, 0)),
                  const((1, 2 * NQK_S))],
        out_specs=(tile(6 * NQK_S), tile(2 * BRANCH_W),
                   pl.BlockSpec((B, CHUNKS_PER_TILE, 2 * NQK_S, LANES), lambda i: (0, i, 0, 0)),
                   tile(2 * NQK_M), tile(2 * NQK_M), tile(2 * BRANCH_W),
                   pl.BlockSpec((B, 2 * PAIRS, TILE, 2 * CHUNK), lambda i: (0, 0, i, 0)),
                   tile(MC_WIDTH),
                   pl.BlockSpec((B, CHUNKS_PER_TILE, 2 * NHD, LANES), lambda i: (0, i, 0, 0)),
                   tile(6 * NQK_S), tile(2 * BRANCH_W)),
        scratch_shapes=[pltpu.VMEM((TILE + 2 * HALO, 2 * NQK_M), F32)] * 2
                       + [pltpu.VMEM((TILE, PW_MLR_WIDTH), F32)] * 2
                       + [pltpu.VMEM((TILE, PW_GR_WIDTH), F32)] * 2
                       + [pltpu.VMEM((6, TILE, NQK_S), F32), pltpu.VMEM((2, TILE, TILE), BF16)]
                       + [pltpu.VMEM((TILE, D_MODEL), BF16)] * 2,
        compiler_params=_cparams(1, VMEM_LIMIT_PROJ), name="proj",
    )(x_src, x_src, x_src, x_src, x_src, x_src, ctx_src,
      mods, g1, w, a2s, abias, conv_w, gate_bias, cos_t, sin_t, dl_row)


def _mix_kernel(batch, *refs):
    (gq_f, gq_b, gk_f, gk_b, ge_f, ge_b, gv_f, gv_b, gd_f, gd_b,
     rq_f, rq_b, rk_f, rk_b, re_f, re_b, rv_f, rv_b, rdl,
     mq_f, mq_b, mk_f, mk_b, mw_f, mw_b, mv_f, mv_b, mp_f, mp_b, mcol_f, mcol_b, mc_f, mc_b,
     of_ref, ob_ref, sg_ref, sr_ref, sm_ref, m_ref) = refs
    j = pl.program_id(0)

    @pl.when(j == 0)
    def _():
        sg_ref[...] = jnp.zeros_like(sg_ref)
        sr_ref[...] = jnp.zeros_like(sr_ref)
        sm_ref[...] = jnp.zeros_like(sm_ref)
        m_ref[...] = jnp.full_like(m_ref, M_INIT)

    row_head = lax.broadcasted_iota(jnp.int32, (2 * CHUNK, PAIR_V), 0) // CHUNK
    diag_kv = row_head == lax.broadcasted_iota(jnp.int32, (2 * CHUNK, PAIR_V), 1) // DV
    diag_kk = (lax.broadcasted_iota(jnp.int32, (2 * CHUNK, PAIR_K), 0) // CHUNK
               == lax.broadcasted_iota(jnp.int32, (2 * CHUNK, PAIR_K), 1) // DK_SMALL)
    upper_rows = lax.broadcasted_iota(jnp.int32, (PAIR_K, DV), 0) < DK_SMALL
    t_idx = lax.broadcasted_iota(jnp.int32, (CHUNK, 2 * CHUNK), 0)
    s_idx = lax.broadcasted_iota(jnp.int32, (CHUNK, 2 * CHUNK), 1) % CHUNK
    causal = (s_idx <= t_idx, s_idx >= t_idx)
    ones_v = jnp.ones((CHUNK, DV), BF16)
    zeros_vext = jnp.zeros((CHUNK, PAIR_V), BF16)
    ret_dec = jnp.exp(float(CHUNK) * _log_sigmoid(rdl[...]))

    decay_sets = (
        ((gq_f, gk_f, ge_f, gv_f), (gq_b, gk_b, ge_b, gv_b), sg_ref, 0),
        ((rq_f, rk_f, re_f, rv_f), (rq_b, rk_b, re_b, rv_b), sr_ref, 2 * BRANCH_W),
    )
    ml_sets = ((mq_f, mk_f, mw_f, mv_f, mp_f, mcol_f, mc_f),
               (mq_b, mk_b, mw_b, mv_b, mp_b, mcol_b, mc_b))
    outs = (of_ref, ob_ref)


    def decay_phase1(bb, d, c, rows, bi, p, refs):
        q_r, k_r, e_r, v_r = refs
        ks = slice(p * PAIR_K, (p + 1) * PAIR_K)
        q2 = q_r[bb, rows, ks]
        k2 = k_r[bb, rows, ks]
        v2 = v_r[bb, rows, p * PAIR_V:(p + 1) * PAIR_V]
        k_bd = jnp.where(diag_kk, jnp.concatenate([k2, k2], axis=0), 0.0)
        dec = (gd_f if d == 0 else gd_b)[bb, c, ks, :] if bi == 0 else ret_dec[d, ks, :]
        full = _dot_tn(e_r[bb, rows, ks], v2)
        upd = jnp.where(upper_rows, full[:, 0:DV], full[:, DV:])
        return dict(q2=q2, v2=v2, dec=dec, upd=upd, att=_dot_nt(q2, k_bd))

    def decay_phase2(bb, d, p, rows, o_ref, s_ref, base, w):
        s_prev = s_ref[bb, d, p]
        s_b = s_prev.astype(BF16)
        s_bd = jnp.where(diag_kv, jnp.concatenate([s_b, s_b], axis=1), 0.0)
        att = jnp.where(causal[d], w["att"], 0.0).astype(BF16)
        v_bd = jnp.where(diag_kv, jnp.concatenate([w["v2"], w["v2"]], axis=0), 0.0)
        lhs = jnp.concatenate([att, w["q2"]], axis=1)
        rhs = jnp.concatenate([v_bd, s_bd], axis=0)
        o_ref[bb, rows, base + p * PAIR_V:base + (p + 1) * PAIR_V] = _dot(lhs, rhs)
        s_ref[bb, d, p] = w["dec"] * s_prev + w["upd"]

    def ml_phase1(bb, d, c, rows, p, refs):
        q_r, k_r, w_r, v_r, p_r, col_r, c_r = refs
        cols = col_r[bb, rows, :]
        q2 = q_r[bb, rows, p * 2 * DK_ML:(p + 1) * 2 * DK_ML]
        k2 = k_r[bb, rows, p * 2 * DK_ML:(p + 1) * 2 * DK_ML]
        k_bd = jnp.where(diag_kv, jnp.concatenate([k2, k2], axis=0), 0.0)
        att = _dot_nt(q2, k_bd)
        heads = []
        for hl in range(2):
            hd = 2 * p + hl
            jj = d * HEADS + hd
            vext = jnp.concatenate([v_r[bb, rows, hd * DV:(hd + 1) * DV], ones_v], axis=1)
            st = sm_ref[bb, d, hd]
            heads.append(dict(
                hd=hd, st=st, vext=vext,
                inter=_dot(q2[:, hl * DK_ML:(hl + 1) * DK_ML], st.astype(BF16)),
                upd=_dot_tn(w_r[bb, rows, hd * DK_ML:(hd + 1) * DK_ML], vext),
                u=jnp.broadcast_to(cols[:, MC_U + jj:MC_U + jj + 1], (CHUNK, LANES)),
                b=jnp.broadcast_to(cols[:, MC_B + jj:MC_B + jj + 1], (CHUNK, LANES))))
        return dict(att=att, heads=heads, p0=p_r[bb, p, rows, :], cs=c_r[bb, c])

    def ml_phase2(bb, d, rows, o_ref, w):
        sc = (w["att"] * w["p0"]).astype(BF16)
        for hl, hw in enumerate(w["heads"]):
            hd = hw["hd"]
            jj = d * HEADS + hd
            vz = (jnp.concatenate([hw["vext"], zeros_vext], axis=0) if hl == 0
                  else jnp.concatenate([zeros_vext, hw["vext"]], axis=0))
            intra = _dot(sc, vz)
            m_prev = m_ref[bb, jj:jj + 1, :]
            u = hw["u"]
            mx = jnp.maximum(u, m_prev)
            f = jnp.exp(u - mx)
            e = jnp.exp(m_prev - mx)
            num = f * intra[:, 0:DV] + e * hw["inter"][:, 0:DV]
            den = f * intra[:, DV:] + e * hw["inter"][:, DV:]
            den = jnp.maximum(jnp.abs(den), jnp.exp(-(hw["b"] + mx)))
            o_ref[bb, rows, BRANCH_W + hd * DV:BRANCH_W + (hd + 1) * DV] = num / den
            tot = w["cs"][jj:jj + 1, :]
            mloc = w["cs"][NHD + jj:NHD + jj + 1, :]
            m_new = jnp.maximum(tot + m_prev, mloc)
            keep = jnp.exp(tot + m_prev - m_new)
            gain = jnp.exp(mloc - m_new)
            sm_ref[bb, d, hd] = (jnp.concatenate([keep, keep], axis=1) * hw["st"]
                                 + jnp.concatenate([gain, gain], axis=1) * hw["upd"])
            m_ref[bb, jj:jj + 1, :] = m_new

    units = []
    for step in range(CHUNKS_PER_TILE):
        for bb, d in ((b_, d_) for b_ in range(batch) for d_ in range(2)):
            c = step if d == 0 else CHUNKS_PER_TILE - 1 - step
            rows = pl.ds(c * CHUNK, CHUNK)
            for bi, (fset, bset, s_ref, base) in enumerate(decay_sets):
                for p in range(PAIRS):
                    units.append((functools.partial(decay_phase1, bb, d, c, rows, bi, p, fset if d == 0 else bset),
                                  functools.partial(decay_phase2, bb, d, p, rows, outs[d], s_ref, base)))
            for p in range(PAIRS):
                units.append((functools.partial(ml_phase1, bb, d, c, rows, p, ml_sets[d]),
                              functools.partial(ml_phase2, bb, d, rows, outs[d])))

    assert MIX_LOOKAHEAD < len(units) // CHUNKS_PER_TILE
    pending = []
    for phase1, phase2 in units:
        pending.append((phase2, phase1()))
        if len(pending) > MIX_LOOKAHEAD:
            fn, w = pending.pop(0)
            fn(w)
    for fn, w in pending:
        fn(w)


def _mix(gla, ret, ml, ret_dl_col):
    g_qke, g_vs, gdec = gla
    r_qke, r_vs = ret
    m_qk, mw, m_vs, mp, mcol, mc = ml
    B, T, _ = g_vs.shape
    nt = T // TILE

    def tf(j):
        return j

    def tb(j):
        return jnp.where(j == 0, 0, nt - j)

    def pair(width, lane_block_b=0, lane_block_f=0):
        return [pl.BlockSpec((B, TILE, width), lambda j: (0, tf(j), lane_block_f)),
                pl.BlockSpec((B, TILE, width), lambda j: (0, tb(j), lane_block_b))]

    dec_pair = [pl.BlockSpec((B, CHUNKS_PER_TILE, NQK_S, LANES), lambda j: (0, tf(j), 0, 0)),
                pl.BlockSpec((B, CHUNKS_PER_TILE, NQK_S, LANES), lambda j: (0, tb(j), 1, 0))]
    p0_pair = [pl.BlockSpec((B, PAIRS, TILE, 2 * CHUNK), lambda j: (0, 0, tf(j), 0)),
               pl.BlockSpec((B, PAIRS, TILE, 2 * CHUNK), lambda j: (0, 1, tb(j), 0))]
    cs_pair = [pl.BlockSpec((B, CHUNKS_PER_TILE, 2 * NHD, LANES), lambda j: (0, tf(j), 0, 0)),
               pl.BlockSpec((B, CHUNKS_PER_TILE, 2 * NHD, LANES), lambda j: (0, tb(j), 0, 0))]
    decay_specs = pair(NQK_S, 1, 0) + pair(NQK_S, 3, 2) + pair(NQK_S, 5, 4) + pair(BRANCH_W)
    in_specs = (decay_specs + dec_pair + decay_specs
                + [pl.BlockSpec((2, NQK_S, LANES), lambda j: (0, 0, 0))]
                + pair(NQK_M) + pair(NQK_M, 1, 1) + pair(NQK_M, 1) + pair(BRANCH_W) + p0_pair
                + pair(MC_WIDTH) + cs_pair)
    args = (g_qke, g_qke, g_qke, g_qke, g_qke, g_qke, g_vs, g_vs, gdec, gdec,
            r_qke, r_qke, r_qke, r_qke, r_qke, r_qke, r_vs, r_vs, ret_dl_col,
            m_qk, m_qk, m_qk, m_qk, mw, mw, m_vs, m_vs, mp, mp, mcol, mcol, mc, mc)
    return pl.pallas_call(
        functools.partial(_mix_kernel, B),
        out_shape=(jax.ShapeDtypeStruct((B, T, 3 * BRANCH_W), F32),) * 2,
        grid=(nt,),
        in_specs=in_specs,
        out_specs=(pl.BlockSpec((B, TILE, 3 * BRANCH_W), lambda j: (0, tf(j), 0)),
                   pl.BlockSpec((B, TILE, 3 * BRANCH_W), lambda j: (0, tb(j), 0))),
        scratch_shapes=[pltpu.VMEM((B, 2, PAIRS, PAIR_K, DV), F32),
                        pltpu.VMEM((B, 2, PAIRS, PAIR_K, DV), F32),
                        pltpu.VMEM((B, 2, HEADS, DK_ML, 2 * DV), F32),
                        pltpu.VMEM((B, NHD, LANES), F32)],
        compiler_params=_cparams(1), name="mix",
    )(*args)


def _merge_kernel(batch, first_tile, x_ref, ctx_ref, mod_ref, g_ref, wg_ref, of_ref, ob_ref,
                  sgg_ref, sgm_ref, sgr_ref, ng_ref, wb_ref, wo_ref, out_ref):
    tile_idx = pl.program_id(0) + first_tile
    side = (sgg_ref, sgm_ref, sgr_ref)
    mods = [mod_ref[jnp.where(tile_idx == 0, batch, bb)] for bb in range(batch)]
    xs = [jnp.where(tile_idx == 0, ctx_ref[bb], x_ref[bb]) for bb in range(batch)]
    hs = [_norm_mod(xs[bb], g_ref[...], mods[bb][0:1, :], mods[bb][1:2, :]).astype(BF16)
          for bb in range(batch)]
    accs = [jnp.zeros((TILE, D_MODEL), F32) for _ in range(batch)]
    for br in range(3):
        for bb in range(batch):
            parts = []
            for hd in range(HEADS):
                cols = slice(br * BRANCH_W + hd * DV, br * BRANCH_W + (hd + 1) * DV)
                oh = of_ref[bb, :, cols] + ob_ref[bb, :, cols]
                if br > 0:
                    oh = oh - jnp.mean(oh, axis=-1, keepdims=True)
                parts.append(oh * lax.rsqrt(jnp.mean(oh * oh, axis=-1, keepdims=True) + EPS))
            y = jnp.concatenate(parts, axis=1) * ng_ref[:, br * BRANCH_W:(br + 1) * BRANCH_W]
            y = (y * side[br][bb]).astype(BF16)
            gate = _sigmoid(_dot(hs[bb], wg_ref[0, :, br * D_MODEL:(br + 1) * D_MODEL]))
            accs[bb] = accs[bb] + gate * _dot(y, wb_ref[0, br])
    for bb in range(batch):
        out = _dot(accs[bb].astype(BF16), wo_ref[0])
        out_ref[bb] = xs[bb] + mods[bb][2:3, :] * out


def _merge(ctx_src, x_src, off, mods, g1, wgate, o_f, o_b, sg_gla, sz_ml, sg_ret, norm_g, wb, wo, layer, skip):
    B = x_src.shape[0]
    nt = x_src.shape[1] // TILE + off - skip
    tile = lambda width: pl.BlockSpec((B, TILE, width), lambda i: (0, i + skip, 0))
    gate_tile = pl.BlockSpec((B, TILE, BRANCH_W), lambda i: (0, i + skip, 1))
    const = lambda shape: pl.BlockSpec(shape, lambda i: (0,) * len(shape))
    per_layer = lambda shape: pl.BlockSpec((1,) + shape, lambda i: (layer,) + (0,) * len(shape),
                                           pipeline_mode=pl.Buffered(1))
    return pl.pallas_call(
        functools.partial(_merge_kernel, B, skip),
        out_shape=jax.ShapeDtypeStruct((B, nt * TILE, D_MODEL), F32),
        grid=(nt,),
        in_specs=[pl.BlockSpec((B, TILE, D_MODEL), lambda i: (0, jnp.maximum(i + skip - off, 0), 0)),
                  pl.BlockSpec((B, TILE, D_MODEL), lambda i: (0, 0, 0)),
                  const((B + 1, 6, D_MODEL)),
                  const((1, D_MODEL)), per_layer((D_MODEL, W_GATE)),
                  tile(3 * BRANCH_W), tile(3 * BRANCH_W),
                  gate_tile, gate_tile, gate_tile,
                  const((1, 3 * BRANCH_W)), per_layer((3, BRANCH_W, D_MODEL)),
                  per_layer((D_MODEL, D_MODEL))],
        out_specs=pl.BlockSpec((B, TILE, D_MODEL), lambda i: (0, i, 0)),
        compiler_params=_cparams(1), name="merge",
    )(x_src, ctx_src, mods, g1, wgate, o_f, o_b, sg_gla, sz_ml, sg_ret, norm_g, wb, wo)


def _mlp_kernel(batch, first_tile, final, x_ref, mod_ref, g_ref, wu_ref, wd_ref, fg_ref, out_ref):
    tile_idx = pl.program_id(0) + first_tile
    mods = [mod_ref[jnp.where(tile_idx == 0, batch, bb)] for bb in range(batch)]
    ups = []
    for bb in range(batch):
        h = _norm_mod(x_ref[bb], g_ref[...], mods[bb][3:4, :], mods[bb][4:5, :]).astype(BF16)
        ups.append(_dot(h, wu_ref[0]))
    for bb in range(batch):
        u = jnp.maximum(ups[bb], 0.0)
        y = x_ref[bb] + mods[bb][5:6, :] * _dot((u * u).astype(BF16), wd_ref[0])
        if final:
            y = y * lax.rsqrt(jnp.mean(y * y, axis=-1, keepdims=True) + EPS) * fg_ref[...]
        out_ref[bb] = y


def _mlp(tok, mods, g2, wu, wd, layer, final_g, first_tile, final):
    B, T, _ = tok.shape
    const = lambda shape: pl.BlockSpec(shape, lambda i: (0,) * len(shape))
    per_layer = lambda shape: pl.BlockSpec((1,) + shape, lambda i: (layer,) + (0,) * len(shape),
                                           pipeline_mode=pl.Buffered(1))
    return pl.pallas_call(
        functools.partial(_mlp_kernel, B, first_tile, final),
        out_shape=jax.ShapeDtypeStruct((B, T, D_MODEL), F32),
        grid=(T // TILE,),
        in_specs=[pl.BlockSpec((B, TILE, D_MODEL), lambda i: (0, i, 0)),
                  const((B + 1, 6, D_MODEL)),
                  const((1, D_MODEL)), per_layer((D_MODEL, D_FF)), per_layer((D_FF, D_MODEL)),
                  const((1, D_MODEL))],
        out_specs=pl.BlockSpec((B, TILE, D_MODEL), lambda i: (0, i, 0)),
        compiler_params=_cparams(1), name="mlp",
    )(tok, mods, g2, wu, wd, final_g)


def kernel(x, c, ctx, c_ctx, norm1_g, norm2_g, w_ada, b_ada, w_in, gla_a2, gla_a_bias, gla_norm_g,
           ml_conv, ml_i_bias, ml_f_bias, ml_norm_g, ret_decay_logit, ret_norm_g, w_branch, w_o,
           w_up, w_down, final_g):
    B, seq, _ = x.shape
    ctx_len = ctx.shape[1]
    assert B == 2 and ctx_len == TILE and seq % TILE == 0 and seq % GRID_W == 0
    ctx_src, x_src, off = ctx, x, 1

    c_rows = jnp.concatenate([c, c_ctx[None, :], jnp.zeros((8 - B - 1, D_MODEL), F32)], axis=0)
    mods_all = _ada(c_rows, w_ada, b_ada)[:, 0:3].reshape(DEPTH, 3, 6, D_MODEL)

    cos_t, sin_t = _rope_tables(seq, ctx_len)

    w_proj, w_gate = _wprep(w_in)
    w_branch_b = w_branch.astype(BF16)
    w_o_b = w_o.astype(BF16)
    w_up_b = w_up.astype(BF16)
    w_down_b = w_down.astype(BF16)
    for l in range(DEPTH):
        last = l == DEPTH - 1
        skip = 1 if last else 0
        mods = mods_all[l]
        g1 = norm1_g[l][None, :]
        g2 = norm2_g[l][None, :]
        zeros = jnp.zeros((GLA_RANK, NQK_S), F32)
        a2blk = jnp.concatenate([jnp.concatenate([gla_a2[l, 0], zeros], axis=1),
                                 jnp.concatenate([zeros, gla_a2[l, 1]], axis=1)], axis=0)
        a2s = jnp.concatenate([a2blk, a2blk], axis=0).astype(BF16)
        abias = gla_a_bias[l].reshape(1, 2 * NQK_S)
        gate_bias = jnp.concatenate([ml_i_bias[l].reshape(1, NHD), ml_f_bias[l].reshape(1, NHD)], axis=1)
        dl = ret_decay_logit[l]
        dl_row = jnp.repeat(dl, DK_SMALL, axis=1).reshape(1, 2 * NQK_S)
        dl_col = jnp.broadcast_to(jnp.repeat(dl, DK_SMALL, axis=1)[:, :, None], (2, NQK_S, LANES))

        (g_qke, g_vs, gdec, m_qk, mw, m_vs, mp, mcol, mc, r_qke, r_vs) = _proj(
            ctx_src, x_src, off, mods, g1, w_proj, l, a2s, abias, ml_conv[l], gate_bias, cos_t, sin_t, dl_row)
        o_f, o_b = _mix((g_qke, g_vs, gdec), (r_qke, r_vs), (m_qk, mw, m_vs, mp, mcol, mc), dl_col)
        norm_g = jnp.concatenate([gla_norm_g[l], ml_norm_g[l], ret_norm_g[l]])[None, :]
        tok = _merge(ctx_src, x_src, off, mods, g1, w_gate, o_f, o_b, g_vs, m_vs, r_vs, norm_g,
                     w_branch_b, w_o_b, l, skip)
        tok = _mlp(tok, mods, g2, w_up_b, w_down_b, l, final_g[None, :], skip, last)
        ctx_src, x_src, off = tok, tok, 0
    return tok
```

```python
import functools

import jax
import jax.numpy as jnp
from jax import lax
from jax.experimental import pallas as pl
from jax.experimental.pallas import tpu as pltpu

F32 = jnp.float32
BF16 = jnp.bfloat16

D_MODEL = 1024
DEPTH = 2
GRID_W = 64
CHUNK = 64
EPS = 1e-6
M_INIT = -1e30
HEADS = 4
DK_SMALL = 64
DK_ML = 128
DV = 128
BRANCH_W = HEADS * DV
GLA_RANK = 16
GLA_TAU = 16.0
ROPE_BASE = 10000.0
D_FF = 4 * D_MODEL
LANES = 128

TILE = 256
CHUNKS_PER_TILE = TILE // CHUNK
HALO = 8
VMEM_LIMIT = 56 * 1024 * 1024
VMEM_LIMIT_PROJ = 60 * 1024 * 1024

NQK_S = HEADS * DK_SMALL
NQK_M = HEADS * DK_ML
NHD = 2 * HEADS
W_GLA = 2 * NQK_S + 2 * BRANCH_W + 2 * GLA_RANK
W_ML = 2 * NQK_M + 2 * BRANCH_W + 2 * NHD
W_RET = 2 * NQK_S + 2 * BRANCH_W
W_GATE = 3 * D_MODEL

PW_MLQK = 0
PW_MLR = PW_MLQK + 2 * NQK_M
PW_MLR_WIDTH = 2 * BRANCH_W + LANES
PW_GR = PW_MLR + PW_MLR_WIDTH
PW_GR_WIDTH = 2 * (2 * NQK_S + 2 * BRANCH_W) + LANES
W_PROJ = PW_GR + PW_GR_WIDTH
MLR_V, MLR_Z, MLR_GATES = 0, BRANCH_W, 2 * BRANCH_W
GR_GLA_Q, GR_GLA_K, GR_GLA_V, GR_GLA_G = 0, NQK_S, 2 * NQK_S, 2 * NQK_S + BRANCH_W
GR_RET_Q = 2 * NQK_S + 2 * BRANCH_W
GR_RET_K, GR_RET_V, GR_RET_G = GR_RET_Q + NQK_S, GR_RET_Q + 2 * NQK_S, GR_RET_Q + 2 * NQK_S + BRANCH_W
GR_GLA_A = 2 * GR_RET_Q

MC_B, MC_W, MC_U = 0, NHD, 2 * NHD
MC_WIDTH = 3 * NHD

PAIRS = HEADS // 2
PAIR_K = 2 * DK_SMALL
PAIR_V = 2 * DV
MIX_LOOKAHEAD = 23


def _cparams(n_axes, vmem_limit=VMEM_LIMIT):
    return pltpu.CompilerParams(dimension_semantics=("arbitrary",) * n_axes,
                                vmem_limit_bytes=vmem_limit)


def _log_sigmoid(z):
    return jnp.minimum(z, 0.0) - jnp.log(1.0 + jnp.exp(-jnp.abs(z)))


def _sigmoid(z):
    return 1.0 / (1.0 + jnp.exp(-z))


def _silu(z):
    return z * _sigmoid(z)


def _norm_mod(x, g, shift, scale):
    return x * lax.rsqrt(jnp.mean(x * x, axis=-1, keepdims=True) + EPS) * (g * (1.0 + scale)) + shift


def _dot(a, b):
    return jnp.dot(a, b, preferred_element_type=F32)


def _dot_nt(a, b):
    return lax.dot_general(a, b, (((1,), (1,)), ((), ())), preferred_element_type=F32)


def _dot_tn(a, b):
    return lax.dot_general(a, b, (((0,), (0,)), ((), ())), preferred_element_type=F32)


def _split2(x):
    hi = x.astype(BF16)
    return hi, (x - hi.astype(F32)).astype(BF16)


def _chunk_tri(lower):
    r = lax.broadcasted_iota(jnp.int32, (TILE, TILE), 0)
    c = lax.broadcasted_iota(jnp.int32, (TILE, TILE), 1)
    same = (r // CHUNK) == (c // CHUNK)
    tri = (c <= r) if lower else (c >= r)
    return jnp.where(same & tri, 1.0, 0.0).astype(BF16)


def _chunk_total(x):
    w = x.shape[-1]
    tot = jnp.sum(x.reshape(CHUNKS_PER_TILE, CHUNK, w), axis=1, keepdims=True)
    return jnp.broadcast_to(tot, (CHUNKS_PER_TILE, CHUNK, w)).reshape(TILE, w)


def _ada_kernel(c_ref, w_ref, b_ref, o_ref):
    s = _silu(c_ref[...])
    s_hi = s.astype(BF16).astype(F32)
    s2 = jnp.concatenate([s_hi, s - s_hi], axis=0).astype(BF16)
    w_hi, w_lo = _split2(w_ref[0])
    r = _dot(s2, w_hi) + _dot(s2, w_lo)
    o_ref[0] = r[0:8] + r[8:16] + b_ref[0]


def _ada(c_rows, w_ada, b_ada):
    tn = 1536
    return pl.pallas_call(
        _ada_kernel,
        out_shape=jax.ShapeDtypeStruct((DEPTH, 8, 6 * D_MODEL), F32),
        grid=(DEPTH, 6 * D_MODEL // tn),
        in_specs=[pl.BlockSpec((8, D_MODEL), lambda l, n: (0, 0)),
                  pl.BlockSpec((1, D_MODEL, tn), lambda l, n: (l, 0, n)),
                  pl.BlockSpec((1, 1, tn), lambda l, n: (l, 0, n))],
        out_specs=pl.BlockSpec((1, 8, tn), lambda l, n: (l, 0, n)),
        compiler_params=_cparams(2), name="ada",
    )(c_rows, w_ada, b_ada.reshape(DEPTH, 1, 6 * D_MODEL))


def _rope_kernel(ang_ref, cos_ref, sin_ref, nsin_ref):
    a = ang_ref[...]
    s = jnp.sin(a)
    cos_ref[...] = jnp.cos(a)
    sin_ref[...] = s
    nsin_ref[...] = -s


def _rope_tables(seq, ctx_len):
    rows = seq // GRID_W
    nf = DK_SMALL // 4
    inv = ROPE_BASE ** (-jnp.arange(nf, dtype=F32) / nf)
    ang = jnp.concatenate([jnp.arange(rows, dtype=F32)[:, None] * inv,
                           jnp.arange(GRID_W, dtype=F32)[:, None] * inv], axis=0)
    n = rows + GRID_W
    cos, sin, nsin = pl.pallas_call(
        _rope_kernel,
        out_shape=(jax.ShapeDtypeStruct((n, nf), F32),) * 3,
        name="rope_tables",
    )(ang)

    def per_token(first, second):
        def half(t):
            r = jnp.broadcast_to(t[:rows, None, :], (rows, GRID_W, nf))
            c = jnp.broadcast_to(t[None, rows:, :], (rows, GRID_W, nf))
            return jnp.concatenate([r, c], axis=-1)
        head = jnp.concatenate([half(first), half(second)], axis=-1).reshape(seq, DK_SMALL)
        return jnp.tile(head, (1, HEADS))

    ones = jnp.ones((ctx_len, NQK_S), F32)
    cos_t = jnp.concatenate([ones, per_token(cos, cos)], axis=0)
    sin_t = jnp.concatenate([0.0 * ones, per_token(nsin, sin)], axis=0)
    return cos_t, sin_t


def _wprep_kernel(w_ref, wp_ref, wg_ref):
    o0, o1, o2 = W_GLA, W_GLA + W_ML, W_GLA + W_ML + W_RET
    n_sg = 2 * NQK_S + 2 * BRANCH_W
    rows = w_ref.shape[1]

    def copy(dst_ref, dst, src, n):
        dst_ref[0, :, dst:dst + n] = w_ref[0, :, src:src + n].astype(BF16)

    copy(wp_ref, PW_MLQK, o0, 2 * NQK_M)
    n_mlr = 2 * BRANCH_W + 2 * NHD
    copy(wp_ref, PW_MLR, o0 + 2 * NQK_M, n_mlr)
    wp_ref[0, :, PW_MLR + n_mlr:PW_GR] = jnp.zeros((rows, PW_GR - PW_MLR - n_mlr), BF16)
    copy(wp_ref, PW_GR + GR_GLA_Q, 0, n_sg)
    copy(wp_ref, PW_GR + GR_RET_Q, o1, W_RET)
    copy(wp_ref, PW_GR + GR_GLA_A, n_sg, 2 * GLA_RANK)
    pad0 = PW_GR + GR_GLA_A + 2 * GLA_RANK
    wp_ref[0, :, pad0:W_PROJ] = jnp.zeros((rows, W_PROJ - pad0), BF16)
    copy(wg_ref, 0, o2, W_GATE)


def _wprep(w_in):
    rows = 256
    return pl.pallas_call(
        _wprep_kernel,
        out_shape=(jax.ShapeDtypeStruct((DEPTH, D_MODEL, W_PROJ), BF16),
                   jax.ShapeDtypeStruct((DEPTH, D_MODEL, W_GATE), BF16)),
        grid=(DEPTH, D_MODEL // rows),
        in_specs=[pl.BlockSpec((1, rows, w_in.shape[2]), lambda l, r: (l, r, 0))],
        out_specs=(pl.BlockSpec((1, rows, W_PROJ), lambda l, r: (l, r, 0)),
                   pl.BlockSpec((1, rows, W_GATE), lambda l, r: (l, r, 0))),
        compiler_params=_cparams(2), name="wprep",
    )(w_in)


def _proj_kernel(nt, xa_ref, xap_ref, xan_ref, xb_ref, xbp_ref, xbn_ref, ctx_ref,
                 mod_ref, g_ref, w_ref, a2_ref, ab_ref, cw_ref, gb_ref, cos_ref, sin_ref, dl_ref,
                 gqke_ref, gvs_ref, gdec_ref, mqk_ref, mkw_ref, mvs_ref, mp0_ref, mcol_ref, mcs_ref,
                 rqke_ref, rvs_ref,
                 ext0_ref, ext1_ref, prml0_ref, prml1_ref, prgr0_ref, prgr1_ref, rdec_ref, tri_ref,
                 h0_ref, h1_ref):
    i = pl.program_id(0)
    ext_refs = (ext0_ref, ext1_ref)
    prml_refs = (prml0_ref, prml1_ref)
    prgr_refs = (prgr0_ref, prgr1_ref)
    h_refs = (h0_ref, h1_ref)
    row8 = lax.broadcasted_iota(jnp.int32, (NHD, TILE), 0)
    lane8 = lax.broadcasted_iota(jnp.int32, (NHD, TILE), 1)
    lane_chunk = lane8 // CHUNK
    lane_in_chunk = lane8 % CHUNK
    ti2 = lax.broadcasted_iota(jnp.int32, (CHUNK, 2 * CHUNK), 0)
    lane2 = lax.broadcasted_iota(jnp.int32, (CHUNK, 2 * CHUNK), 1)
    si2 = lane2 % CHUNK
    lane_first = lane2 < CHUNK

    def project_stages(x, x_prev, x_next, tile_idx, mod_row, slot):
        st = {}

        def mlstm_qk():
            mod = mod_ref[mod_row]
            x_all = jnp.concatenate([x_prev, x, x_next], axis=0)
            h_all = _norm_mod(x_all, g_ref[...], mod[0:1, :], mod[1:2, :])
            h_refs[slot][...] = h_all[HALO:HALO + TILE, :].astype(BF16)
            prev_ok = tile_idx >= 2
            next_ok = jnp.logical_and(tile_idx >= 1, tile_idx <= nt - 2)
            ext = _dot(h_all.astype(BF16), w_ref[0, :, PW_MLQK:PW_MLR])
            ext_refs[slot][HALO:HALO + TILE, :] = ext[HALO:HALO + TILE, :]
            ext_refs[slot][0:HALO, :] = jnp.where(prev_ok, ext[0:HALO, :], 0.0)
            ext_refs[slot][HALO + TILE:, :] = jnp.where(next_ok, ext[HALO + TILE:, :], 0.0)

        def mlstm_rest():
            prml_refs[slot][...] = _dot(h_refs[slot][...], w_ref[0, :, PW_MLR:PW_GR])

        def gla():
            prgr_refs[slot][:, 0:GR_RET_Q] = _dot(h_refs[slot][...], w_ref[0, :, PW_GR:PW_GR + GR_RET_Q])

        def ret():
            prgr_refs[slot][:, GR_RET_Q:] = _dot(h_refs[slot][...], w_ref[0, :, PW_GR + GR_RET_Q:])

        return [mlstm_qk, mlstm_rest, gla, ret]

    def project(*args):
        for stage in project_stages(*args):
            stage()

    @pl.when(i == 0)
    def _():
        tri_ref[0] = _chunk_tri(True)
        tri_ref[1] = _chunk_tri(False)
        lg = _log_sigmoid(dl_ref[...])
        t_in_chunk = (lax.broadcasted_iota(jnp.int32, (TILE, NQK_S), 0) % CHUNK).astype(F32)
        for d in range(2):
            lg_d = lg[:, d * NQK_S:(d + 1) * NQK_S]
            steps = (t_in_chunk + 1.0) if d == 0 else (float(CHUNK) - t_in_chunk)
            b = steps * lg_d
            rdec_ref[3 * d] = jnp.exp(b)
            rdec_ref[3 * d + 1] = jnp.exp(-b)
            rdec_ref[3 * d + 2] = jnp.exp(float(CHUNK) * lg_d - b)
        halo0 = jnp.zeros((HALO, D_MODEL), F32)
        project(ctx_ref[0], halo0, halo0, 0, 2, 0)

    def rope(x):
        lane = lax.broadcasted_iota(jnp.int32, (TILE, NQK_S), 1)
        first_half = (lane % DK_SMALL) < (DK_SMALL // 2)
        swapped = jnp.where(first_half, pltpu.roll(x, NQK_S - DK_SMALL // 2, 1),
                            pltpu.roll(x, DK_SMALL // 2, 1))
        return x * cos_ref[...] + swapped * sin_ref[...]

    def seg_cummax(x, reverse):
        sh = 1
        while sh < CHUNK:
            if reverse:
                y, ok = pltpu.roll(x, TILE - sh, 1), lane_in_chunk < CHUNK - sh
            else:
                y, ok = pltpu.roll(x, sh, 1), lane_in_chunk >= sh
            x = jnp.where(ok, jnp.maximum(x, y), x)
            sh *= 2
        return x

    def prepare_stages(slot, bb):
        ext_ref, prml_ref, prgr_ref = ext_refs[slot], prml_refs[slot], prgr_refs[slot]
        st = {}

        def conv():
            ext = ext_ref[...]
            n_ext = TILE + 2 * HALO
            prev = pltpu.roll(ext, 1, 0)[HALO:HALO + TILE, :]
            nxt = pltpu.roll(ext, n_ext - 1, 0)[HALO:HALO + TILE, :]
            qk = _silu(cw_ref[0:1, :] * prev + cw_ref[1:2, :] * ext[HALO:HALO + TILE, :]
                       + cw_ref[2:3, :] * nxt)
            mqk_ref[bb, :, 0:NQK_M] = qk[:, 0:NQK_M].astype(BF16)
            st["k_ml"] = qk[:, NQK_M:] * (DK_ML ** -0.5)
            mqk_ref[bb, :, NQK_M:] = st["k_ml"].astype(BF16)

        def gates_in():
            mvs_ref[bb, :, 0:BRANCH_W] = prml_ref[:,MLR_V:MLR_V + BRANCH_W].astype(BF16)
            mvs_ref[bb, :, BRANCH_W:] = _silu(prml_ref[:,MLR_Z:MLR_Z + BRANCH_W]).astype(BF16)
            gates_t = (prml_ref[:,MLR_GATES:MLR_GATES + 2 * NHD] + gb_ref[...]).T
            st["li_t"] = gates_t[0:NHD]
            st["lf_t"] = _log_sigmoid(gates_t[NHD:])
            lf_hi, lf_lo = _split2(st["lf_t"])
            st["lf2"] = jnp.concatenate([lf_hi, lf_lo], axis=0)

        def gates_mm():
            st["cum_f"] = _dot(st["lf2"], tri_ref[1])
            st["cum_b"] = _dot(st["lf2"], tri_ref[0])

        def gates_out():
            mlstm_gates_body(st["li_t"], st["lf_t"], st["cum_f"], st["cum_b"], st["k_ml"], bb)

        def gla_in():
            gvs_ref[bb, :, 0:BRANCH_W] = prgr_ref[:,GR_GLA_V:GR_GLA_V + BRANCH_W].astype(BF16)
            gvs_ref[bb, :, BRANCH_W:] = _silu(prgr_ref[:,GR_GLA_G:GR_GLA_G + BRANCH_W]).astype(BF16)
            a_hi, a_lo = _split2(prgr_ref[:,GR_GLA_A:GR_GLA_A + 2 * GLA_RANK])
            st["a2"] = jnp.concatenate([a_hi, a_lo], axis=1)

        def gla_z_mm():
            st["z"] = _dot(st["a2"], a2_ref[...])

        def gla_la():
            la = _log_sigmoid(st["z"] + ab_ref[...]) * (1.0 / GLA_TAU)
            st["la"] = la
            st["la_split"] = [_split2(la[:, d * NQK_S:(d + 1) * NQK_S]) for d in range(2)]

        def gla_cum_mm():
            st["b"] = []
            for d in range(2):
                tri = tri_ref[d]
                la_hi, la_lo = st["la_split"][d]
                st["b"].append(_dot(tri, la_hi) + _dot(tri, la_lo))

        def gla_out():
            gla_body(prgr_ref, st["la"], st["b"], bb)

        def ret():
            ret_body(prgr_ref, bb)

        return dict(conv=conv, gates_in=gates_in, gates_mm=gates_mm, gates_out=gates_out,
                    gla_in=gla_in, gla_z_mm=gla_z_mm, gla_la=gla_la, gla_cum_mm=gla_cum_mm,
                    gla_out=gla_out, ret=ret)

    def mlstm_gates_body(li_t, lf_t, cum_f, cum_b, k_ml, bb):
        b_t = jnp.where(row8 < HEADS, cum_f[0:NHD] + cum_f[NHD:], cum_b[0:NHD] + cum_b[NHD:])
        tot_t = jnp.zeros((NHD, TILE), F32)
        mloc_t = jnp.zeros((NHD, TILE), F32)
        for c in range(CHUNKS_PER_TILE):
            tot_c = jnp.sum(lf_t[:, c * CHUNK:(c + 1) * CHUNK], axis=-1, keepdims=True)
            tot_t = jnp.where(lane_chunk == c, tot_c, tot_t)
            mcs_ref[bb, c, 0:NHD, :] = jnp.broadcast_to(tot_c, (NHD, LANES))
        g_t = tot_t - b_t + li_t
        for c in range(CHUNKS_PER_TILE):
            mloc_c = jnp.max(g_t[:, c * CHUNK:(c + 1) * CHUNK], axis=-1, keepdims=True)
            mloc_t = jnp.where(lane_chunk == c, mloc_c, mloc_t)
            mcs_ref[bb, c, NHD:2 * NHD, :] = jnp.broadcast_to(mloc_c, (NHD, LANES))
        w_t = jnp.exp(g_t - mloc_t)
        r_t = li_t - b_t
        u_t = jnp.where(row8 < HEADS, seg_cummax(r_t, False), seg_cummax(r_t, True))
        cols = jnp.concatenate([b_t, w_t, u_t], axis=0).T
        mcol_ref[bb] = cols
        for j in range(NHD):
            d, hd = j // HEADS, j % HEADS
            mkw_ref[bb, :, d * NQK_M + hd * DK_ML:d * NQK_M + (hd + 1) * DK_ML] = (
                k_ml[:, hd * DK_ML:(hd + 1) * DK_ML] * cols[:, MC_W + j:MC_W + j + 1]).astype(BF16)
        for pp in range(NHD // 2):
            j0, j1 = 2 * pp, 2 * pp + 1
            mask = (si2 <= ti2) if pp < PAIRS else (si2 >= ti2)
            for c in range(CHUNKS_PER_TILE):
                rows = slice(c * CHUNK, (c + 1) * CHUNK)
                r2 = jnp.concatenate([r_t[j0:j0 + 1, rows], r_t[j1:j1 + 1, rows]], axis=1)
                u2 = jnp.where(lane_first, cols[rows, MC_U + j0:MC_U + j0 + 1],
                               cols[rows, MC_U + j1:MC_U + j1 + 1])
                mp0_ref[bb, pp, rows, :] = jnp.where(mask, jnp.exp(r2 - u2), 0.0)

    def gla_body(prgr_ref, la, cums, bb):
        half_t = jnp.sum(la.reshape(2 * CHUNKS_PER_TILE, CHUNK // 2, 2 * NQK_S), axis=1).T
        q_gla = prgr_ref[:,GR_GLA_Q:GR_GLA_Q + NQK_S] * (DK_SMALL ** -0.5)
        k_gla = prgr_ref[:,GR_GLA_K:GR_GLA_K + NQK_S]
        for d in range(2):
            la_d = la[:, d * NQK_S:(d + 1) * NQK_S]
            b = cums[d]
            tot = _chunk_total(la_d)
            decay = jnp.exp(b)
            gqke_ref[bb, :, d * NQK_S:(d + 1) * NQK_S] = (q_gla * decay).astype(BF16)
            gqke_ref[bb, :, (2 + d) * NQK_S:(3 + d) * NQK_S] = (k_gla * (1.0 / decay)).astype(BF16)
            gqke_ref[bb, :, (4 + d) * NQK_S:(5 + d) * NQK_S] = (k_gla * jnp.exp(tot - b)).astype(BF16)
            ht = half_t[d * NQK_S:(d + 1) * NQK_S, :]
            for c in range(CHUNKS_PER_TILE):
                tot_col = ht[:, 2 * c:2 * c + 1] + ht[:, 2 * c + 1:2 * c + 2]
                gdec_ref[bb, c, d * NQK_S:(d + 1) * NQK_S, :] = jnp.broadcast_to(jnp.exp(tot_col), (NQK_S, LANES))

    def ret_body(prgr_ref, bb):
        q_ret = rope(prgr_ref[:,GR_RET_Q:GR_RET_Q + NQK_S])
        k_ret = rope(prgr_ref[:,GR_RET_K:GR_RET_K + NQK_S] * (DK_SMALL ** -0.5))
        rvs_ref[bb, :, 0:BRANCH_W] = prgr_ref[:,GR_RET_V:GR_RET_V + BRANCH_W].astype(BF16)
        rvs_ref[bb, :, BRANCH_W:] = _silu(prgr_ref[:,GR_RET_G:GR_RET_G + BRANCH_W]).astype(BF16)
        for d in range(2):
            rqke_ref[bb, :, d * NQK_S:(d + 1) * NQK_S] = (q_ret * rdec_ref[3 * d]).astype(BF16)
            rqke_ref[bb, :, (2 + d) * NQK_S:(3 + d) * NQK_S] = (k_ret * rdec_ref[3 * d + 1]).astype(BF16)
            rqke_ref[bb, :, (4 + d) * NQK_S:(5 + d) * NQK_S] = (k_ret * rdec_ref[3 * d + 2]).astype(BF16)

    next_tile = jnp.minimum(i + 1, nt - 1)
    xa = jnp.where(i == 0, ctx_ref[1], xa_ref[0])
    a_ext, a_mlr, a_gla, a_ret = project_stages(xa, xap_ref[0], xan_ref[0], i, jnp.where(i == 0, 2, 1), 1)
    b_ext, b_mlr, b_gla, b_ret = project_stages(xb_ref[0], xbp_ref[0], xbn_ref[0], next_tile, 0, 0)
    x = prepare_stages(0, 0)
    y = prepare_stages(1, 1)
    order = [a_ext, x["conv"], x["gates_in"],
             a_mlr, x["gates_mm"], x["gates_out"], x["gla_in"],
             a_gla, x["gla_z_mm"], x["gla_la"],
             a_ret, x["gla_cum_mm"],
             b_ext, x["gla_out"], x["ret"], y["conv"], y["gates_in"],
             b_mlr, y["gates_mm"], y["gates_out"], y["gla_in"],
             b_gla, y["gla_z_mm"], y["gla_la"],
             b_ret, y["gla_cum_mm"], y["gla_out"], y["ret"]]
    for stage in order:
        stage()


def _proj(ctx_src, x_src, off, mods, g1, w, layer, a2s, abias, conv_w, gate_bias, cos_t, sin_t, dl_row):
    B = x_src.shape[0]
    nt = x_src.shape[1] // TILE + off
    T = nt * TILE
    tile = lambda width: pl.BlockSpec((B, TILE, width), lambda i: (0, i, 0))
    const = lambda shape: pl.BlockSpec(shape, lambda i: (0,) * len(shape))
    per_halo = TILE // HALO
    n_halo = x_src.shape[1] // HALO
    nxt = lambda i: jnp.minimum(i + 1, nt - 1)
    blk = lambda t: jnp.maximum(t - off, 0)
    prev_halo = lambda t: jnp.maximum((t - off) * per_halo - 1, 0)
    next_halo = lambda t: jnp.clip((t - off + 1) * per_halo, 0, n_halo - 1)
    tok_shape = lambda width, dt: jax.ShapeDtypeStruct((B, T, width), dt)
    return pl.pallas_call(
        functools.partial(_proj_kernel, nt),
        out_shape=(tok_shape(6 * NQK_S, BF16), tok_shape(2 * BRANCH_W, BF16),
                   jax.ShapeDtypeStruct((B, T // CHUNK, 2 * NQK_S, LANES), F32),
                   tok_shape(2 * NQK_M, BF16), tok_shape(2 * NQK_M, BF16), tok_shape(2 * BRANCH_W, BF16),
                   jax.ShapeDtypeStruct((B, 2 * PAIRS, T, 2 * CHUNK), F32),
                   tok_shape(MC_WIDTH, F32),
                   jax.ShapeDtypeStruct((B, T // CHUNK, 2 * NHD, LANES), F32),
                   tok_shape(6 * NQK_S, BF16), tok_shape(2 * BRANCH_W, BF16)),
        grid=(nt,),
        in_specs=[pl.BlockSpec((1, TILE, D_MODEL), lambda i: (1, blk(i), 0)),
                  pl.BlockSpec((1, HALO, D_MODEL), lambda i: (1, prev_halo(i), 0)),
                  pl.BlockSpec((1, HALO, D_MODEL), lambda i: (1, next_halo(i), 0)),
                  pl.BlockSpec((1, TILE, D_MODEL), lambda i: (0, blk(nxt(i)), 0)),
                  pl.BlockSpec((1, HALO, D_MODEL), lambda i: (0, prev_halo(nxt(i)), 0)),
                  pl.BlockSpec((1, HALO, D_MODEL), lambda i: (0, next_halo(nxt(i)), 0)),
                  pl.BlockSpec((B, TILE, D_MODEL), lambda i: (0, 0, 0)),
                  const((B + 1, 6, D_MODEL)), const((1, D_MODEL)),
                  pl.BlockSpec((1, D_MODEL, W_PROJ), lambda i: (layer, 0, 0), pipeline_mode=pl.Buffered(1)),
                  const((4 * GLA_RANK, 2 * NQK_S)), const((1, 2 * NQK_S)),
                  const((3, 2 * NQK_M)), const((1, 2 * NHD)),
                  pl.BlockSpec((TILE, NQK_S), lambda i: (i, 0)),
                  pl.BlockSpec((TILE, NQK_S), lambda i: (i, 0)),
                  const((1, 2 * NQK_S))],
        out_specs=(tile(6 * NQK_S), tile(2 * BRANCH_W),
                   pl.BlockSpec((B, CHUNKS_PER_TILE, 2 * NQK_S, LANES), lambda i: (0, i, 0, 0)),
                   tile(2 * NQK_M), tile(2 * NQK_M), tile(2 * BRANCH_W),
                   pl.BlockSpec((B, 2 * PAIRS, TILE, 2 * CHUNK), lambda i: (0, 0, i, 0)),
                   tile(MC_WIDTH),
                   pl.BlockSpec((B, CHUNKS_PER_TILE, 2 * NHD, LANES), lambda i: (0, i, 0, 0)),
                   tile(6 * NQK_S), tile(2 * BRANCH_W)),
        scratch_shapes=[pltpu.VMEM((TILE + 2 * HALO, 2 * NQK_M), F32)] * 2
                       + [pltpu.VMEM((TILE, PW_MLR_WIDTH), F32)] * 2
                       + [pltpu.VMEM((TILE, PW_GR_WIDTH), F32)] * 2
                       + [pltpu.VMEM((6, TILE, NQK_S), F32), pltpu.VMEM((2, TILE, TILE), BF16)]
                       + [pltpu.VMEM((TILE, D_MODEL), BF16)] * 2,
        compiler_params=_cparams(1, VMEM_LIMIT_PROJ), name="proj",
    )(x_src, x_src, x_src, x_src, x_src, x_src, ctx_src,
      mods, g1, w, a2s, abias, conv_w, gate_bias, cos_t, sin_t, dl_row)


def _mix_kernel(batch, *refs):
    (gq_f, gq_b, gk_f, gk_b, ge_f, ge_b, gv_f, gv_b, gd_f, gd_b,
     rq_f, rq_b, rk_f, rk_b, re_f, re_b, rv_f, rv_b, rdl,
     mq_f, mq_b, mk_f, mk_b, mw_f, mw_b, mv_f, mv_b, mp_f, mp_b, mcol_f, mcol_b, mc_f, mc_b,
     of_ref, ob_ref, sg_ref, sr_ref, sm_ref, m_ref) = refs
    j = pl.program_id(0)

    @pl.when(j == 0)
    def _():
        sg_ref[...] = jnp.zeros_like(sg_ref)
        sr_ref[...] = jnp.zeros_like(sr_ref)
        sm_ref[...] = jnp.zeros_like(sm_ref)
        m_ref[...] = jnp.full_like(m_ref, M_INIT)

    row_head = lax.broadcasted_iota(jnp.int32, (2 * CHUNK, PAIR_V), 0) // CHUNK
    diag_kv = row_head == lax.broadcasted_iota(jnp.int32, (2 * CHUNK, PAIR_V), 1) // DV
    diag_kk = (lax.broadcasted_iota(jnp.int32, (2 * CHUNK, PAIR_K), 0) // CHUNK
               == lax.broadcasted_iota(jnp.int32, (2 * CHUNK, PAIR_K), 1) // DK_SMALL)
    upper_rows = lax.broadcasted_iota(jnp.int32, (PAIR_K, DV), 0) < DK_SMALL
    t_idx = lax.broadcasted_iota(jnp.int32, (CHUNK, 2 * CHUNK), 0)
    s_idx = lax.broadcasted_iota(jnp.int32, (CHUNK, 2 * CHUNK), 1) % CHUNK
    causal = (s_idx <= t_idx, s_idx >= t_idx)
    ones_v = jnp.ones((CHUNK, DV), BF16)
    zeros_vext = jnp.zeros((CHUNK, PAIR_V), BF16)
    ret_dec = jnp.exp(float(CHUNK) * _log_sigmoid(rdl[...]))

    decay_sets = (
        ((gq_f, gk_f, ge_f, gv_f), (gq_b, gk_b, ge_b, gv_b), sg_ref, 0),
        ((rq_f, rk_f, re_f, rv_f), (rq_b, rk_b, re_b, rv_b), sr_ref, 2 * BRANCH_W),
    )
    ml_sets = ((mq_f, mk_f, mw_f, mv_f, mp_f, mcol_f, mc_f),
               (mq_b, mk_b, mw_b, mv_b, mp_b, mcol_b, mc_b))
    outs = (of_ref, ob_ref)


    def decay_phase1(bb, d, c, rows, bi, p, refs):
        q_r, k_r, e_r, v_r = refs
        ks = slice(p * PAIR_K, (p + 1) * PAIR_K)
        q2 = q_r[bb, rows, ks]
        k2 = k_r[bb, rows, ks]
        v2 = v_r[bb, rows, p * PAIR_V:(p + 1) * PAIR_V]
        k_bd = jnp.where(diag_kk, jnp.concatenate([k2, k2], axis=0), 0.0)
        dec = (gd_f if d == 0 else gd_b)[bb, c, ks, :] if bi == 0 else ret_dec[d, ks, :]
        full = _dot_tn(e_r[bb, rows, ks], v2)
        upd = jnp.where(upper_rows, full[:, 0:DV], full[:, DV:])
        return dict(q2=q2, v2=v2, dec=dec, upd=upd, att=_dot_nt(q2, k_bd))

    def decay_phase2(bb, d, p, rows, o_ref, s_ref, base, w):
        s_prev = s_ref[bb, d, p]
        s_b = s_prev.astype(BF16)
        s_bd = jnp.where(diag_kv, jnp.concatenate([s_b, s_b], axis=1), 0.0)
        att = jnp.where(causal[d], w["att"], 0.0).astype(BF16)
        v_bd = jnp.where(diag_kv, jnp.concatenate([w["v2"], w["v2"]], axis=0), 0.0)
        lhs = jnp.concatenate([att, w["q2"]], axis=1)
        rhs = jnp.concatenate([v_bd, s_bd], axis=0)
        o_ref[bb, rows, base + p * PAIR_V:base + (p + 1) * PAIR_V] = _dot(lhs, rhs).astype(BF16)
        s_ref[bb, d, p] = w["dec"] * s_prev + w["upd"]

    def ml_phase1(bb, d, c, rows, p, refs):
        q_r, k_r, w_r, v_r, p_r, col_r, c_r = refs
        cols = col_r[bb, rows, :]
        q2 = q_r[bb, rows, p * 2 * DK_ML:(p + 1) * 2 * DK_ML]
        k2 = k_r[bb, rows, p * 2 * DK_ML:(p + 1) * 2 * DK_ML]
        k_bd = jnp.where(diag_kv, jnp.concatenate([k2, k2], axis=0), 0.0)
        att = _dot_nt(q2, k_bd)
        heads = []
        for hl in range(2):
            hd = 2 * p + hl
            jj = d * HEADS + hd
            vext = jnp.concatenate([v_r[bb, rows, hd * DV:(hd + 1) * DV], ones_v], axis=1)
            st = sm_ref[bb, d, hd]
            heads.append(dict(
                hd=hd, st=st, vext=vext,
                inter=_dot(q2[:, hl * DK_ML:(hl + 1) * DK_ML], st.astype(BF16)),
                upd=_dot_tn(w_r[bb, rows, hd * DK_ML:(hd + 1) * DK_ML], vext),
                u=jnp.broadcast_to(cols[:, MC_U + jj:MC_U + jj + 1], (CHUNK, LANES)),
                b=jnp.broadcast_to(cols[:, MC_B + jj:MC_B + jj + 1], (CHUNK, LANES))))
        return dict(att=att, heads=heads, p0=p_r[bb, p, rows, :], cs=c_r[bb, c])

    def ml_phase2(bb, d, rows, o_ref, w):
        sc = (w["att"] * w["p0"]).astype(BF16)
        for hl, hw in enumerate(w["heads"]):
            hd = hw["hd"]
            jj = d * HEADS + hd
            vz = (jnp.concatenate([hw["vext"], zeros_vext], axis=0) if hl == 0
                  else jnp.concatenate([zeros_vext, hw["vext"]], axis=0))
            intra = _dot(sc, vz)
            m_prev = m_ref[bb, jj:jj + 1, :]
            u = hw["u"]
            mx = jnp.maximum(u, m_prev)
            f = jnp.exp(u - mx)
            e = jnp.exp(m_prev - mx)
            num = f * intra[:, 0:DV] + e * hw["inter"][:, 0:DV]
            den = f * intra[:, DV:] + e * hw["inter"][:, DV:]
            den = jnp.maximum(jnp.abs(den), jnp.exp(-(hw["b"] + mx)))
            o_ref[bb, rows, BRANCH_W + hd * DV:BRANCH_W + (hd + 1) * DV] = (num / den).astype(BF16)
            tot = w["cs"][jj:jj + 1, :]
            mloc = w["cs"][NHD + jj:NHD + jj + 1, :]
            m_new = jnp.maximum(tot + m_prev, mloc)
            keep = jnp.exp(tot + m_prev - m_new)
            gain = jnp.exp(mloc - m_new)
            sm_ref[bb, d, hd] = (jnp.concatenate([keep, keep], axis=1) * hw["st"]
                                 + jnp.concatenate([gain, gain], axis=1) * hw["upd"])
            m_ref[bb, jj:jj + 1, :] = m_new

    units = []
    for step in range(CHUNKS_PER_TILE):
        for bb, d in ((b_, d_) for b_ in range(batch) for d_ in range(2)):
            c = step if d == 0 else CHUNKS_PER_TILE - 1 - step
            rows = pl.ds(c * CHUNK, CHUNK)
            for bi, (fset, bset, s_ref, base) in enumerate(decay_sets):
                for p in range(PAIRS):
                    units.append((functools.partial(decay_phase1, bb, d, c, rows, bi, p, fset if d == 0 else bset),
                                  functools.partial(decay_phase2, bb, d, p, rows, outs[d], s_ref, base)))
            for p in range(PAIRS):
                units.append((functools.partial(ml_phase1, bb, d, c, rows, p, ml_sets[d]),
                              functools.partial(ml_phase2, bb, d, rows, outs[d])))

    assert MIX_LOOKAHEAD < len(units) // CHUNKS_PER_TILE
    pending = []
    for phase1, phase2 in units:
        pending.append((phase2, phase1()))
        if len(pending) > MIX_LOOKAHEAD:
            fn, w = pending.pop(0)
            fn(w)
    for fn, w in pending:
        fn(w)


def _mix(gla, ret, ml, ret_dl_col):
    g_qke, g_vs, gdec = gla
    r_qke, r_vs = ret
    m_qk, mw, m_vs, mp, mcol, mc = ml
    B, T, _ = g_vs.shape
    nt = T // TILE

    def tf(j):
        return j

    def tb(j):
        return jnp.where(j == 0, 0, nt - j)

    def pair(width, lane_block_b=0, lane_block_f=0):
        return [pl.BlockSpec((B, TILE, width), lambda j: (0, tf(j), lane_block_f)),
                pl.BlockSpec((B, TILE, width), lambda j: (0, tb(j), lane_block_b))]

    dec_pair = [pl.BlockSpec((B, CHUNKS_PER_TILE, NQK_S, LANES), lambda j: (0, tf(j), 0, 0)),
                pl.BlockSpec((B, CHUNKS_PER_TILE, NQK_S, LANES), lambda j: (0, tb(j), 1, 0))]
    p0_pair = [pl.BlockSpec((B, PAIRS, TILE, 2 * CHUNK), lambda j: (0, 0, tf(j), 0)),
               pl.BlockSpec((B, PAIRS, TILE, 2 * CHUNK), lambda j: (0, 1, tb(j), 0))]
    cs_pair = [pl.BlockSpec((B, CHUNKS_PER_TILE, 2 * NHD, LANES), lambda j: (0, tf(j), 0, 0)),
               pl.BlockSpec((B, CHUNKS_PER_TILE, 2 * NHD, LANES), lambda j: (0, tb(j), 0, 0))]
    decay_specs = pair(NQK_S, 1, 0) + pair(NQK_S, 3, 2) + pair(NQK_S, 5, 4) + pair(BRANCH_W)
    in_specs = (decay_specs + dec_pair + decay_specs
                + [pl.BlockSpec((2, NQK_S, LANES), lambda j: (0, 0, 0))]
                + pair(NQK_M) + pair(NQK_M, 1, 1) + pair(NQK_M, 1) + pair(BRANCH_W) + p0_pair
                + pair(MC_WIDTH) + cs_pair)
    args = (g_qke, g_qke, g_qke, g_qke, g_qke, g_qke, g_vs, g_vs, gdec, gdec,
            r_qke, r_qke, r_qke, r_qke, r_qke, r_qke, r_vs, r_vs, ret_dl_col,
            m_qk, m_qk, m_qk, m_qk, mw, mw, m_vs, m_vs, mp, mp, mcol, mcol, mc, mc)
    return pl.pallas_call(
        functools.partial(_mix_kernel, B),
        out_shape=(jax.ShapeDtypeStruct((B, T, 3 * BRANCH_W), BF16),) * 2,
        grid=(nt,),
        in_specs=in_specs,
        out_specs=(pl.BlockSpec((B, TILE, 3 * BRANCH_W), lambda j: (0, tf(j), 0)),
                   pl.BlockSpec((B, TILE, 3 * BRANCH_W), lambda j: (0, tb(j), 0))),
        scratch_shapes=[pltpu.VMEM((B, 2, PAIRS, PAIR_K, DV), F32),
                        pltpu.VMEM((B, 2, PAIRS, PAIR_K, DV), F32),
                        pltpu.VMEM((B, 2, HEADS, DK_ML, 2 * DV), F32),
                        pltpu.VMEM((B, NHD, LANES), F32)],
        compiler_params=_cparams(1), name="mix",
    )(*args)


def _merge_kernel(batch, first_tile, x_ref, ctx_ref, mod_ref, g_ref, wg_ref, of_ref, ob_ref,
                  sgg_ref, sgm_ref, sgr_ref, ng_ref, wb_ref, wo_ref, out_ref):
    tile_idx = pl.program_id(0) + first_tile
    side = (sgg_ref, sgm_ref, sgr_ref)
    mods = [mod_ref[jnp.where(tile_idx == 0, batch, bb)] for bb in range(batch)]
    xs = [jnp.where(tile_idx == 0, ctx_ref[bb], x_ref[bb]) for bb in range(batch)]
    hs = [_norm_mod(xs[bb], g_ref[...], mods[bb][0:1, :], mods[bb][1:2, :]).astype(BF16)
          for bb in range(batch)]
    accs = [jnp.zeros((TILE, D_MODEL), F32) for _ in range(batch)]
    for br in range(3):
        for bb in range(batch):
            parts = []
            for hd in range(HEADS):
                cols = slice(br * BRANCH_W + hd * DV, br * BRANCH_W + (hd + 1) * DV)
                oh = of_ref[bb, :, cols].astype(F32) + ob_ref[bb, :, cols].astype(F32)
                if br > 0:
                    oh = oh - jnp.mean(oh, axis=-1, keepdims=True)
                parts.append(oh * lax.rsqrt(jnp.mean(oh * oh, axis=-1, keepdims=True) + EPS))
            y = jnp.concatenate(parts, axis=1) * ng_ref[:, br * BRANCH_W:(br + 1) * BRANCH_W]
            y = (y * side[br][bb]).astype(BF16)
            gate = _sigmoid(_dot(hs[bb], wg_ref[0, :, br * D_MODEL:(br + 1) * D_MODEL]))
            accs[bb] = accs[bb] + gate * _dot(y, wb_ref[0, br])
    for bb in range(batch):
        out = _dot(accs[bb].astype(BF16), wo_ref[0])
        out_ref[bb] = xs[bb] + mods[bb][2:3, :] * out


def _merge(ctx_src, x_src, off, mods, g1, wgate, o_f, o_b, sg_gla, sz_ml, sg_ret, norm_g, wb, wo, layer, skip):
    B = x_src.shape[0]
    nt = x_src.shape[1] // TILE + off - skip
    tile = lambda width: pl.BlockSpec((B, TILE, width), lambda i: (0, i + skip, 0))
    gate_tile = pl.BlockSpec((B, TILE, BRANCH_W), lambda i: (0, i + skip, 1))
    const = lambda shape: pl.BlockSpec(shape, lambda i: (0,) * len(shape))
    per_layer = lambda shape: pl.BlockSpec((1,) + shape, lambda i: (layer,) + (0,) * len(shape),
                                           pipeline_mode=pl.Buffered(1))
    return pl.pallas_call(
        functools.partial(_merge_kernel, B, skip),
        out_shape=jax.ShapeDtypeStruct((B, nt * TILE, D_MODEL), F32),
        grid=(nt,),
        in_specs=[pl.BlockSpec((B, TILE, D_MODEL), lambda i: (0, jnp.maximum(i + skip - off, 0), 0)),
                  pl.BlockSpec((B, TILE, D_MODEL), lambda i: (0, 0, 0)),
                  const((B + 1, 6, D_MODEL)),
                  const((1, D_MODEL)), per_layer((D_MODEL, W_GATE)),
                  tile(3 * BRANCH_W), tile(3 * BRANCH_W),
                  gate_tile, gate_tile, gate_tile,
                  const((1, 3 * BRANCH_W)), per_layer((3, BRANCH_W, D_MODEL)),
                  per_layer((D_MODEL, D_MODEL))],
        out_specs=pl.BlockSpec((B, TILE, D_MODEL), lambda i: (0, i, 0)),
        compiler_params=_cparams(1), name="merge",
    )(x_src, ctx_src, mods, g1, wgate, o_f, o_b, sg_gla, sz_ml, sg_ret, norm_g, wb, wo)


def _mlp_kernel(batch, first_tile, final, x_ref, mod_ref, g_ref, wu_ref, wd_ref, fg_ref, out_ref):
    tile_idx = pl.program_id(0) + first_tile
    mods = [mod_ref[jnp.where(tile_idx == 0, batch, bb)] for bb in range(batch)]
    ups = []
    for bb in range(batch):
        h = _norm_mod(x_ref[bb], g_ref[...], mods[bb][3:4, :], mods[bb][4:5, :]).astype(BF16)
        ups.append(_dot(h, wu_ref[0]))
    for bb in range(batch):
        u = jnp.maximum(ups[bb], 0.0)
        y = x_ref[bb] + mods[bb][5:6, :] * _dot((u * u).astype(BF16), wd_ref[0])
        if final:
            y = y * lax.rsqrt(jnp.mean(y * y, axis=-1, keepdims=True) + EPS) * fg_ref[...]
        out_ref[bb] = y


def _mlp(tok, mods, g2, wu, wd, layer, final_g, first_tile, final):
    B, T, _ = tok.shape
    const = lambda shape: pl.BlockSpec(shape, lambda i: (0,) * len(shape))
    per_layer = lambda shape: pl.BlockSpec((1,) + shape, lambda i: (layer,) + (0,) * len(shape),
                                           pipeline_mode=pl.Buffered(1))
    return pl.pallas_call(
        functools.partial(_mlp_kernel, B, first_tile, final),
        out_shape=jax.ShapeDtypeStruct((B, T, D_MODEL), F32),
        grid=(T // TILE,),
        in_specs=[pl.BlockSpec((B, TILE, D_MODEL), lambda i: (0, i, 0)),
                  const((B + 1, 6, D_MODEL)),
                  const((1, D_MODEL)), per_layer((D_MODEL, D_FF)), per_layer((D_FF, D_MODEL)),
                  const((1, D_MODEL))],
        out_specs=pl.BlockSpec((B, TILE, D_MODEL), lambda i: (0, i, 0)),
        compiler_params=_cparams(1), name="mlp",
    )(tok, mods, g2, wu, wd, final_g)


def kernel(x, c, ctx, c_ctx, norm1_g, norm2_g, w_ada, b_ada, w_in, gla_a2, gla_a_bias, gla_norm_g,
           ml_conv, ml_i_bias, ml_f_bias, ml_norm_g, ret_decay_logit, ret_norm_g, w_branch, w_o,
           w_up, w_down, final_g):
    B, seq, _ = x.shape
    ctx_len = ctx.shape[1]
    assert B == 2 and ctx_len == TILE and seq % TILE == 0 and seq % GRID_W == 0
    ctx_src, x_src, off = ctx, x, 1

    c_rows = jnp.concatenate([c, c_ctx[None, :], jnp.zeros((8 - B - 1, D_MODEL), F32)], axis=0)
    mods_all = _ada(c_rows, w_ada, b_ada)[:, 0:3].reshape(DEPTH, 3, 6, D_MODEL)

    cos_t, sin_t = _rope_tables(seq, ctx_len)

    w_proj, w_gate = _wprep(w_in)
    w_branch_b = w_branch.astype(BF16)
    w_o_b = w_o.astype(BF16)
    w_up_b = w_up.astype(BF16)
    w_down_b = w_down.astype(BF16)
    for l in range(DEPTH):
        last = l == DEPTH - 1
        skip = 1 if last else 0
        mods = mods_all[l]
        g1 = norm1_g[l][None, :]
        g2 = norm2_g[l][None, :]
        zeros = jnp.zeros((GLA_RANK, NQK_S), F32)
        a2blk = jnp.concatenate([jnp.concatenate([gla_a2[l, 0], zeros], axis=1),
                                 jnp.concatenate([zeros, gla_a2[l, 1]], axis=1)], axis=0)
        a2s = jnp.concatenate([a2blk, a2blk], axis=0).astype(BF16)
        abias = gla_a_bias[l].reshape(1, 2 * NQK_S)
        gate_bias = jnp.concatenate([ml_i_bias[l].reshape(1, NHD), ml_f_bias[l].reshape(1, NHD)], axis=1)
        dl = ret_decay_logit[l]
        dl_row = jnp.repeat(dl, DK_SMALL, axis=1).reshape(1, 2 * NQK_S)
        dl_col = jnp.broadcast_to(jnp.repeat(dl, DK_SMALL, axis=1)[:, :, None], (2, NQK_S, LANES))

        (g_qke, g_vs, gdec, m_qk, mw, m_vs, mp, mcol, mc, r_qke, r_vs) = _proj(
            ctx_src, x_src, off, mods, g1, w_proj, l, a2s, abias, ml_conv[l], gate_bias, cos_t, sin_t, dl_row)
        o_f, o_b = _mix((g_qke, g_vs, gdec), (r_qke, r_vs), (m_qk, mw, m_vs, mp, mcol, mc), dl_col)
        norm_g = jnp.concatenate([gla_norm_g[l], ml_norm_g[l], ret_norm_g[l]])[None, :]
        tok = _merge(ctx_src, x_src, off, mods, g1, w_gate, o_f, o_b, g_vs, m_vs, r_vs, norm_g,
                     w_branch_b, w_o_b, l, skip)
        tok = _mlp(tok, mods, g2, w_up_b, w_down_b, l, final_g[None, :], skip, last)
        ctx_src, x_src, off = tok, tok, 0
    return tok
```

```python
import functools

import jax
import jax.numpy as jnp
from jax import lax
from jax.experimental import pallas as pl
from jax.experimental.pallas import tpu as pltpu

F32 = jnp.float32
BF16 = jnp.bfloat16

D_MODEL = 1024
DEPTH = 2
GRID_W = 64
CHUNK = 64
EPS = 1e-6
M_INIT = -1e30
HEADS = 4
DK_SMALL = 64
DK_ML = 128
DV = 128
BRANCH_W = HEADS * DV
GLA_RANK = 16
GLA_TAU = 16.0
ROPE_BASE = 10000.0
D_FF = 4 * D_MODEL
LANES = 128

TILE = 256
CHUNKS_PER_TILE = TILE // CHUNK
HALO = 8
VMEM_LIMIT = 56 * 1024 * 1024
VMEM_LIMIT_PROJ = 60 * 1024 * 1024

NQK_S = HEADS * DK_SMALL
NQK_M = HEADS * DK_ML
NHD = 2 * HEADS
W_GLA = 2 * NQK_S + 2 * BRANCH_W + 2 * GLA_RANK
W_ML = 2 * NQK_M + 2 * BRANCH_W + 2 * NHD
W_RET = 2 * NQK_S + 2 * BRANCH_W
W_GATE = 3 * D_MODEL

PW_MLQK = 0
PW_MLR = PW_MLQK + 2 * NQK_M
PW_MLR_WIDTH = 2 * BRANCH_W + LANES
PW_GR = PW_MLR + PW_MLR_WIDTH
PW_GR_WIDTH = 2 * (2 * NQK_S + 2 * BRANCH_W) + LANES
W_PROJ = PW_GR + PW_GR_WIDTH
MLR_V, MLR_Z, MLR_GATES = 0, BRANCH_W, 2 * BRANCH_W
GR_GLA_Q, GR_GLA_K, GR_GLA_V, GR_GLA_G = 0, NQK_S, 2 * NQK_S, 2 * NQK_S + BRANCH_W
GR_RET_Q = 2 * NQK_S + 2 * BRANCH_W
GR_RET_K, GR_RET_V, GR_RET_G = GR_RET_Q + NQK_S, GR_RET_Q + 2 * NQK_S, GR_RET_Q + 2 * NQK_S + BRANCH_W
GR_GLA_A = 2 * GR_RET_Q

MC_B, MC_W, MC_U = 0, NHD, 2 * NHD
MC_WIDTH = 3 * NHD

PAIRS = HEADS // 2
PAIR_K = 2 * DK_SMALL
PAIR_V = 2 * DV
MIX_LOOKAHEAD = 23


def _cparams(n_axes, vmem_limit=VMEM_LIMIT):
    return pltpu.CompilerParams(dimension_semantics=("arbitrary",) * n_axes,
                                vmem_limit_bytes=vmem_limit)


def _log_sigmoid(z):
    return jnp.minimum(z, 0.0) - jnp.log(1.0 + jnp.exp(-jnp.abs(z)))


def _sigmoid(z):
    return 1.0 / (1.0 + jnp.exp(-z))


def _silu(z):
    return z * _sigmoid(z)


def _norm_mod(x, g, shift, scale):
    return x * lax.rsqrt(jnp.mean(x * x, axis=-1, keepdims=True) + EPS) * (g * (1.0 + scale)) + shift


def _dot(a, b):
    return jnp.dot(a, b, preferred_element_type=F32)


def _dot_nt(a, b):
    return lax.dot_general(a, b, (((1,), (1,)), ((), ())), preferred_element_type=F32)


def _dot_tn(a, b):
    return lax.dot_general(a, b, (((0,), (0,)), ((), ())), preferred_element_type=F32)


def _split2(x):
    hi = x.astype(BF16)
    return hi, (x - hi.astype(F32)).astype(BF16)


def _chunk_tri(lower):
    r = lax.broadcasted_iota(jnp.int32, (TILE, TILE), 0)
    c = lax.broadcasted_iota(jnp.int32, (TILE, TILE), 1)
    same = (r // CHUNK) == (c // CHUNK)
    tri = (c <= r) if lower else (c >= r)
    return jnp.where(same & tri, 1.0, 0.0).astype(BF16)


def _chunk_total(x):
    w = x.shape[-1]
    tot = jnp.sum(x.reshape(CHUNKS_PER_TILE, CHUNK, w), axis=1, keepdims=True)
    return jnp.broadcast_to(tot, (CHUNKS_PER_TILE, CHUNK, w)).reshape(TILE, w)


def _ada_kernel(c_ref, w_ref, b_ref, o_ref):
    s = _silu(c_ref[...])
    s_hi = s.astype(BF16).astype(F32)
    s2 = jnp.concatenate([s_hi, s - s_hi], axis=0).astype(BF16)
    w_hi, w_lo = _split2(w_ref[0])
    r = _dot(s2, w_hi) + _dot(s2, w_lo)
    o_ref[0] = r[0:8] + r[8:16] + b_ref[0]


def _ada(c_rows, w_ada, b_ada):
    tn = 1536
    return pl.pallas_call(
        _ada_kernel,
        out_shape=jax.ShapeDtypeStruct((DEPTH, 8, 6 * D_MODEL), F32),
        grid=(DEPTH, 6 * D_MODEL // tn),
        in_specs=[pl.BlockSpec((8, D_MODEL), lambda l, n: (0, 0)),
                  pl.BlockSpec((1, D_MODEL, tn), lambda l, n: (l, 0, n)),
                  pl.BlockSpec((1, 1, tn), lambda l, n: (l, 0, n))],
        out_specs=pl.BlockSpec((1, 8, tn), lambda l, n: (l, 0, n)),
        compiler_params=_cparams(2), name="ada",
    )(c_rows, w_ada, b_ada.reshape(DEPTH, 1, 6 * D_MODEL))


def _rope_kernel(ang_ref, cos_ref, sin_ref, nsin_ref):
    a = ang_ref[...]
    s = jnp.sin(a)
    cos_ref[...] = jnp.cos(a)
    sin_ref[...] = s
    nsin_ref[...] = -s


def _rope_tables(seq, ctx_len):
    rows = seq // GRID_W
    nf = DK_SMALL // 4
    inv = ROPE_BASE ** (-jnp.arange(nf, dtype=F32) / nf)
    ang = jnp.concatenate([jnp.arange(rows, dtype=F32)[:, None] * inv,
                           jnp.arange(GRID_W, dtype=F32)[:, None] * inv], axis=0)
    n = rows + GRID_W
    cos, sin, nsin = pl.pallas_call(
        _rope_kernel,
        out_shape=(jax.ShapeDtypeStruct((n, nf), F32),) * 3,
        name="rope_tables",
    )(ang)

    def per_token(first, second):
        def half(t):
            r = jnp.broadcast_to(t[:rows, None, :], (rows, GRID_W, nf))
            c = jnp.broadcast_to(t[None, rows:, :], (rows, GRID_W, nf))
            return jnp.concatenate([r, c], axis=-1)
        head = jnp.concatenate([half(first), half(second)], axis=-1).reshape(seq, DK_SMALL)
        return jnp.tile(head, (1, HEADS))

    ones = jnp.ones((ctx_len, NQK_S), F32)
    cos_t = jnp.concatenate([ones, per_token(cos, cos)], axis=0)
    sin_t = jnp.concatenate([0.0 * ones, per_token(nsin, sin)], axis=0)
    return cos_t, sin_t


def _wprep_kernel(w_ref, wp_ref, wg_ref):
    o0, o1, o2 = W_GLA, W_GLA + W_ML, W_GLA + W_ML + W_RET
    n_sg = 2 * NQK_S + 2 * BRANCH_W
    rows = w_ref.shape[1]

    def copy(dst_ref, dst, src, n):
        dst_ref[0, :, dst:dst + n] = w_ref[0, :, src:src + n].astype(BF16)

    copy(wp_ref, PW_MLQK, o0, 2 * NQK_M)
    n_mlr = 2 * BRANCH_W + 2 * NHD
    copy(wp_ref, PW_MLR, o0 + 2 * NQK_M, n_mlr)
    wp_ref[0, :, PW_MLR + n_mlr:PW_GR] = jnp.zeros((rows, PW_GR - PW_MLR - n_mlr), BF16)
    copy(wp_ref, PW_GR + GR_GLA_Q, 0, n_sg)
    copy(wp_ref, PW_GR + GR_RET_Q, o1, W_RET)
    copy(wp_ref, PW_GR + GR_GLA_A, n_sg, 2 * GLA_RANK)
    pad0 = PW_GR + GR_GLA_A + 2 * GLA_RANK
    wp_ref[0, :, pad0:W_PROJ] = jnp.zeros((rows, W_PROJ - pad0), BF16)
    copy(wg_ref, 0, o2, W_GATE)


def _wprep(w_in):
    rows = 256
    return pl.pallas_call(
        _wprep_kernel,
        out_shape=(jax.ShapeDtypeStruct((DEPTH, D_MODEL, W_PROJ), BF16),
                   jax.ShapeDtypeStruct((DEPTH, D_MODEL, W_GATE), BF16)),
        grid=(DEPTH, D_MODEL // rows),
        in_specs=[pl.BlockSpec((1, rows, w_in.shape[2]), lambda l, r: (l, r, 0))],
        out_specs=(pl.BlockSpec((1, rows, W_PROJ), lambda l, r: (l, r, 0)),
                   pl.BlockSpec((1, rows, W_GATE), lambda l, r: (l, r, 0))),
        compiler_params=_cparams(2), name="wprep",
    )(w_in)


def _proj_kernel(nt, xa_ref, xap_ref, xan_ref, xb_ref, xbp_ref, xbn_ref, ctx_ref,
                 mod_ref, g_ref, w_ref, a2_ref, ab_ref, cw_ref, gb_ref, cos_ref, sin_ref, dl_ref,
                 gqke_ref, gvs_ref, gdec_ref, mqk_ref, mkw_ref, mvs_ref, mp0_ref, mcol_ref, mcs_ref,
                 rqke_ref, rvs_ref,
                 ext0_ref, ext1_ref, prml0_ref, prml1_ref, prgr0_ref, prgr1_ref, rdec_ref, tri_ref,
                 h0_ref, h1_ref):
    i = pl.program_id(0)
    ext_refs = (ext0_ref, ext1_ref)
    prml_refs = (prml0_ref, prml1_ref)
    prgr_refs = (prgr0_ref, prgr1_ref)
    h_refs = (h0_ref, h1_ref)
    row8 = lax.broadcasted_iota(jnp.int32, (NHD, TILE), 0)
    lane8 = lax.broadcasted_iota(jnp.int32, (NHD, TILE), 1)
    lane_chunk = lane8 // CHUNK
    lane_in_chunk = lane8 % CHUNK
    ti2 = lax.broadcasted_iota(jnp.int32, (CHUNK, 2 * CHUNK), 0)
    lane2 = lax.broadcasted_iota(jnp.int32, (CHUNK, 2 * CHUNK), 1)
    si2 = lane2 % CHUNK
    lane_first = lane2 < CHUNK

    def project_stages(x, x_prev, x_next, tile_idx, mod_row, slot):
        st = {}

        def mlstm_qk():
            mod = mod_ref[mod_row]
            x_all = jnp.concatenate([x_prev, x, x_next], axis=0)
            h_all = _norm_mod(x_all, g_ref[...], mod[0:1, :], mod[1:2, :])
            h_refs[slot][...] = h_all[HALO:HALO + TILE, :].astype(BF16)
            prev_ok = tile_idx >= 2
            next_ok = jnp.logical_and(tile_idx >= 1, tile_idx <= nt - 2)
            ext = _dot(h_all.astype(BF16), w_ref[0, :, PW_MLQK:PW_MLR])
            ext_refs[slot][HALO:HALO + TILE, :] = ext[HALO:HALO + TILE, :]
            ext_refs[slot][0:HALO, :] = jnp.where(prev_ok, ext[0:HALO, :], 0.0)
            ext_refs[slot][HALO + TILE:, :] = jnp.where(next_ok, ext[HALO + TILE:, :], 0.0)

        def mlstm_rest():
            prml_refs[slot][...] = _dot(h_refs[slot][...], w_ref[0, :, PW_MLR:PW_GR])

        def gla():
            prgr_refs[slot][:, 0:GR_RET_Q] = _dot(h_refs[slot][...], w_ref[0, :, PW_GR:PW_GR + GR_RET_Q])

        def ret():
            prgr_refs[slot][:, GR_RET_Q:] = _dot(h_refs[slot][...], w_ref[0, :, PW_GR + GR_RET_Q:])

        return [mlstm_qk, mlstm_rest, gla, ret]

    def project(*args):
        for stage in project_stages(*args):
            stage()

    @pl.when(i == 0)
    def _():
        tri_ref[0] = _chunk_tri(True)
        tri_ref[1] = _chunk_tri(False)
        lg = _log_sigmoid(dl_ref[...])
        t_in_chunk = (lax.broadcasted_iota(jnp.int32, (TILE, NQK_S), 0) % CHUNK).astype(F32)
        for d in range(2):
            lg_d = lg[:, d * NQK_S:(d + 1) * NQK_S]
            steps = (t_in_chunk + 1.0) if d == 0 else (float(CHUNK) - t_in_chunk)
            b = steps * lg_d
            rdec_ref[3 * d] = jnp.exp(b)
            rdec_ref[3 * d + 1] = jnp.exp(-b)
            rdec_ref[3 * d + 2] = jnp.exp(float(CHUNK) * lg_d - b)
        halo0 = jnp.zeros((HALO, D_MODEL), F32)
        project(ctx_ref[0], halo0, halo0, 0, 2, 0)

    def rope(x):
        lane = lax.broadcasted_iota(jnp.int32, (TILE, NQK_S), 1)
        first_half = (lane % DK_SMALL) < (DK_SMALL // 2)
        swapped = jnp.where(first_half, pltpu.roll(x, NQK_S - DK_SMALL // 2, 1),
                            pltpu.roll(x, DK_SMALL // 2, 1))
        return x * cos_ref[...] + swapped * sin_ref[...]

    def seg_cummax(x, reverse):
        sh = 1
        while sh < CHUNK:
            if reverse:
                y, ok = pltpu.roll(x, TILE - sh, 1), lane_in_chunk < CHUNK - sh
            else:
                y, ok = pltpu.roll(x, sh, 1), lane_in_chunk >= sh
            x = jnp.where(ok, jnp.maximum(x, y), x)
            sh *= 2
        return x

    def prepare_stages(slot, bb):
        ext_ref, prml_ref, prgr_ref = ext_refs[slot], prml_refs[slot], prgr_refs[slot]
        st = {}

        def conv():
            ext = ext_ref[...]
            n_ext = TILE + 2 * HALO
            prev = pltpu.roll(ext, 1, 0)[HALO:HALO + TILE, :]
            nxt = pltpu.roll(ext, n_ext - 1, 0)[HALO:HALO + TILE, :]
            qk = _silu(cw_ref[0:1, :] * prev + cw_ref[1:2, :] * ext[HALO:HALO + TILE, :]
                       + cw_ref[2:3, :] * nxt)
            mqk_ref[bb, :, 0:NQK_M] = qk[:, 0:NQK_M].astype(BF16)
            st["k_ml"] = qk[:, NQK_M:] * (DK_ML ** -0.5)
            mqk_ref[bb, :, NQK_M:] = st["k_ml"].astype(BF16)

        def gates_in():
            mvs_ref[bb, :, 0:BRANCH_W] = prml_ref[:,MLR_V:MLR_V + BRANCH_W].astype(BF16)
            mvs_ref[bb, :, BRANCH_W:] = _silu(prml_ref[:,MLR_Z:MLR_Z + BRANCH_W]).astype(BF16)
            gates_t = (prml_ref[:,MLR_GATES:MLR_GATES + 2 * NHD] + gb_ref[...]).T
            st["li_t"] = gates_t[0:NHD]
            st["lf_t"] = _log_sigmoid(gates_t[NHD:])
            lf_hi, lf_lo = _split2(st["lf_t"])
            st["lf2"] = jnp.concatenate([lf_hi, lf_lo], axis=0)

        def gates_mm():
            st["cum_f"] = _dot(st["lf2"], tri_ref[1])
            st["cum_b"] = _dot(st["lf2"], tri_ref[0])

        def gates_out():
            mlstm_gates_body(st["li_t"], st["lf_t"], st["cum_f"], st["cum_b"], st["k_ml"], bb)

        def gla_in():
            gvs_ref[bb, :, 0:BRANCH_W] = prgr_ref[:,GR_GLA_V:GR_GLA_V + BRANCH_W].astype(BF16)
            gvs_ref[bb, :, BRANCH_W:] = _silu(prgr_ref[:,GR_GLA_G:GR_GLA_G + BRANCH_W]).astype(BF16)
            a_hi, a_lo = _split2(prgr_ref[:,GR_GLA_A:GR_GLA_A + 2 * GLA_RANK])
            st["a2"] = jnp.concatenate([a_hi, a_lo], axis=1)

        def gla_z_mm():
            st["z"] = _dot(st["a2"], a2_ref[...])

        def gla_la():
            la = _log_sigmoid(st["z"] + ab_ref[...]) * (1.0 / GLA_TAU)
            st["la"] = la
            st["la_split"] = [_split2(la[:, d * NQK_S:(d + 1) * NQK_S]) for d in range(2)]

        def gla_cum_mm():
            st["b"] = []
            for d in range(2):
                tri = tri_ref[d]
                la_hi, la_lo = st["la_split"][d]
                st["b"].append(_dot(tri, la_hi) + _dot(tri, la_lo))

        def gla_out():
            gla_body(prgr_ref, st["la"], st["b"], bb)

        def ret():
            ret_body(prgr_ref, bb)

        return dict(conv=conv, gates_in=gates_in, gates_mm=gates_mm, gates_out=gates_out,
                    gla_in=gla_in, gla_z_mm=gla_z_mm, gla_la=gla_la, gla_cum_mm=gla_cum_mm,
                    gla_out=gla_out, ret=ret)

    def mlstm_gates_body(li_t, lf_t, cum_f, cum_b, k_ml, bb):
        b_t = jnp.where(row8 < HEADS, cum_f[0:NHD] + cum_f[NHD:], cum_b[0:NHD] + cum_b[NHD:])
        tot_t = jnp.zeros((NHD, TILE), F32)
        mloc_t = jnp.zeros((NHD, TILE), F32)
        for c in range(CHUNKS_PER_TILE):
            tot_c = jnp.sum(lf_t[:, c * CHUNK:(c + 1) * CHUNK], axis=-1, keepdims=True)
            tot_t = jnp.where(lane_chunk == c, tot_c, tot_t)
            mcs_ref[bb, c, 0:NHD, :] = jnp.broadcast_to(tot_c, (NHD, LANES))
        g_t = tot_t - b_t + li_t
        for c in range(CHUNKS_PER_TILE):
            mloc_c = jnp.max(g_t[:, c * CHUNK:(c + 1) * CHUNK], axis=-1, keepdims=True)
            mloc_t = jnp.where(lane_chunk == c, mloc_c, mloc_t)
            mcs_ref[bb, c, NHD:2 * NHD, :] = jnp.broadcast_to(mloc_c, (NHD, LANES))
        w_t = jnp.exp(g_t - mloc_t)
        r_t = li_t - b_t
        u_t = jnp.where(row8 < HEADS, seg_cummax(r_t, False), seg_cummax(r_t, True))
        cols = jnp.concatenate([b_t, w_t, u_t], axis=0).T
        mcol_ref[bb] = cols
        for j in range(NHD):
            d, hd = j // HEADS, j % HEADS
            mkw_ref[bb, :, d * NQK_M + hd * DK_ML:d * NQK_M + (hd + 1) * DK_ML] = (
                k_ml[:, hd * DK_ML:(hd + 1) * DK_ML] * cols[:, MC_W + j:MC_W + j + 1]).astype(BF16)
        for pp in range(NHD // 2):
            j0, j1 = 2 * pp, 2 * pp + 1
            mask = (si2 <= ti2) if pp < PAIRS else (si2 >= ti2)
            for c in range(CHUNKS_PER_TILE):
                rows = slice(c * CHUNK, (c + 1) * CHUNK)
                r2 = jnp.concatenate([r_t[j0:j0 + 1, rows], r_t[j1:j1 + 1, rows]], axis=1)
                u2 = jnp.where(lane_first, cols[rows, MC_U + j0:MC_U + j0 + 1],
                               cols[rows, MC_U + j1:MC_U + j1 + 1])
                mp0_ref[bb, pp, rows, :] = jnp.where(mask, jnp.exp(r2 - u2), 0.0).astype(BF16)

    def gla_body(prgr_ref, la, cums, bb):
        half_t = jnp.sum(la.reshape(2 * CHUNKS_PER_TILE, CHUNK // 2, 2 * NQK_S), axis=1).T
        q_gla = prgr_ref[:,GR_GLA_Q:GR_GLA_Q + NQK_S] * (DK_SMALL ** -0.5)
        k_gla = prgr_ref[:,GR_GLA_K:GR_GLA_K + NQK_S]
        for d in range(2):
            la_d = la[:, d * NQK_S:(d + 1) * NQK_S]
            b = cums[d]
            tot = _chunk_total(la_d)
            decay = jnp.exp(b)
            gqke_ref[bb, :, d * NQK_S:(d + 1) * NQK_S] = (q_gla * decay).astype(BF16)
            gqke_ref[bb, :, (2 + d) * NQK_S:(3 + d) * NQK_S] = (k_gla * (1.0 / decay)).astype(BF16)
            gqke_ref[bb, :, (4 + d) * NQK_S:(5 + d) * NQK_S] = (k_gla * jnp.exp(tot - b)).astype(BF16)
            ht = half_t[d * NQK_S:(d + 1) * NQK_S, :]
            for c in range(CHUNKS_PER_TILE):
                tot_col = ht[:, 2 * c:2 * c + 1] + ht[:, 2 * c + 1:2 * c + 2]
                gdec_ref[bb, c, d * NQK_S:(d + 1) * NQK_S, :] = jnp.broadcast_to(jnp.exp(tot_col), (NQK_S, LANES))

    def ret_body(prgr_ref, bb):
        q_ret = rope(prgr_ref[:,GR_RET_Q:GR_RET_Q + NQK_S])
        k_ret = rope(prgr_ref[:,GR_RET_K:GR_RET_K + NQK_S] * (DK_SMALL ** -0.5))
        rvs_ref[bb, :, 0:BRANCH_W] = prgr_ref[:,GR_RET_V:GR_RET_V + BRANCH_W].astype(BF16)
        rvs_ref[bb, :, BRANCH_W:] = _silu(prgr_ref[:,GR_RET_G:GR_RET_G + BRANCH_W]).astype(BF16)
        for d in range(2):
            rqke_ref[bb, :, d * NQK_S:(d + 1) * NQK_S] = (q_ret * rdec_ref[3 * d]).astype(BF16)
            rqke_ref[bb, :, (2 + d) * NQK_S:(3 + d) * NQK_S] = (k_ret * rdec_ref[3 * d + 1]).astype(BF16)
            rqke_ref[bb, :, (4 + d) * NQK_S:(5 + d) * NQK_S] = (k_ret * rdec_ref[3 * d + 2]).astype(BF16)

    next_tile = jnp.minimum(i + 1, nt - 1)
    xa = jnp.where(i == 0, ctx_ref[1], xa_ref[0])
    a_ext, a_mlr, a_gla, a_ret = project_stages(xa, xap_ref[0], xan_ref[0], i, jnp.where(i == 0, 2, 1), 1)
    b_ext, b_mlr, b_gla, b_ret = project_stages(xb_ref[0], xbp_ref[0], xbn_ref[0], next_tile, 0, 0)
    x = prepare_stages(0, 0)
    y = prepare_stages(1, 1)
    order = [a_ext, x["conv"], x["gates_in"],
             a_mlr, x["gates_mm"], x["gates_out"], x["gla_in"],
             a_gla, x["gla_z_mm"], x["gla_la"],
             a_ret, x["gla_cum_mm"],
             b_ext, x["gla_out"], x["ret"], y["conv"], y["gates_in"],
             b_mlr, y["gates_mm"], y["gates_out"], y["gla_in"],
             b_gla, y["gla_z_mm"], y["gla_la"],
             b_ret, y["gla_cum_mm"], y["gla_out"], y["ret"]]
    for stage in order:
        stage()


def _proj(ctx_src, x_src, off, mods, g1, w, layer, a2s, abias, conv_w, gate_bias, cos_t, sin_t, dl_row):
    B = x_src.shape[0]
    nt = x_src.shape[1] // TILE + off
    T = nt * TILE
    tile = lambda width: pl.BlockSpec((B, TILE, width), lambda i: (0, i, 0))
    const = lambda shape: pl.BlockSpec(shape, lambda i: (0,) * len(shape))
    per_halo = TILE // HALO
    n_halo = x_src.shape[1] // HALO
    nxt = lambda i: jnp.minimum(i + 1, nt - 1)
    blk = lambda t: jnp.maximum(t - off, 0)
    prev_halo = lambda t: jnp.maximum((t - off) * per_halo - 1, 0)
    next_halo = lambda t: jnp.clip((t - off + 1) * per_halo, 0, n_halo - 1)
    tok_shape = lambda width, dt: jax.ShapeDtypeStruct((B, T, width), dt)
    return pl.pallas_call(
        functools.partial(_proj_kernel, nt),
        out_shape=(tok_shape(6 * NQK_S, BF16), tok_shape(2 * BRANCH_W, BF16),
                   jax.ShapeDtypeStruct((B, T // CHUNK, 2 * NQK_S, LANES), F32),
                   tok_shape(2 * NQK_M, BF16), tok_shape(2 * NQK_M, BF16), tok_shape(2 * BRANCH_W, BF16),
                   jax.ShapeDtypeStruct((B, 2 * PAIRS, T, 2 * CHUNK), BF16),
                   tok_shape(MC_WIDTH, F32),
                   jax.ShapeDtypeStruct((B, T // CHUNK, 2 * NHD, LANES), F32),
                   tok_shape(6 * NQK_S, BF16), tok_shape(2 * BRANCH_W, BF16)),
        grid=(nt,),
        in_specs=[pl.BlockSpec((1, TILE, D_MODEL), lambda i: (1, blk(i), 0)),
                  pl.BlockSpec((1, HALO, D_MODEL), lambda i: (1, prev_halo(i), 0)),
                  pl.BlockSpec((1, HALO, D_MODEL), lambda i: (1, next_halo(i), 0)),
                  pl.BlockSpec((1, TILE, D_MODEL), lambda i: (0, blk(nxt(i)), 0)),
                  pl.BlockSpec((1, HALO, D_MODEL), lambda i: (0, prev_halo(nxt(i)), 0)),
                  pl.BlockSpec((1, HALO, D_MODEL), lambda i: (0, next_halo(nxt(i)), 0)),
                  pl.BlockSpec((B, TILE, D_MODEL), lambda i: (0, 0, 0)),
                  const((B + 1, 6, D_MODEL)), const((1, D_MODEL)),
                  pl.BlockSpec((1, D_MODEL, W_PROJ), lambda i: (layer, 0, 0), pipeline_mode=pl.Buffered(1)),
                  const((4 * GLA_RANK, 2 * NQK_S)), const((1, 2 * NQK_S)),
                  const((3, 2 * NQK_M)), const((1, 2 * NHD)),
                  pl.BlockSpec((TILE, NQK_S), lambda i: (i, 0)),
                  pl.BlockSpec((TILE, NQK_S), lambda i: (i, 0)),
                  const((1, 2 * NQK_S))],
        out_specs=(tile(6 * NQK_S), tile(2 * BRANCH_W),
                   pl.BlockSpec((B, CHUNKS_PER_TILE, 2 * NQK_S, LANES), lambda i: (0, i, 0, 0)),
                   tile(2 * NQK_M), tile(2 * NQK_M), tile(2 * BRANCH_W),
                   pl.BlockSpec((B, 2 * PAIRS, TILE, 2 * CHUNK), lambda i: (0, 0, i, 0)),
                   tile(MC_WIDTH),
                   pl.BlockSpec((B, CHUNKS_PER_TILE, 2 * NHD, LANES), lambda i: (0, i, 0, 0)),
                   tile(6 * NQK_S), tile(2 * BRANCH_W)),
        scratch_shapes=[pltpu.VMEM((TILE + 2 * HALO, 2 * NQK_M), F32)] * 2
                       + [pltpu.VMEM((TILE, PW_MLR_WIDTH), F32)] * 2
                       + [pltpu.VMEM((TILE, PW_GR_WIDTH), F32)] * 2
                       + [pltpu.VMEM((6, TILE, NQK_S), F32), pltpu.VMEM((2, TILE, TILE), BF16)]
                       + [pltpu.VMEM((TILE, D_MODEL), BF16)] * 2,
        compiler_params=_cparams(1, VMEM_LIMIT_PROJ), name="proj",
    )(x_src, x_src, x_src, x_src, x_src, x_src, ctx_src,
      mods, g1, w, a2s, abias, conv_w, gate_bias, cos_t, sin_t, dl_row)


def _mix_kernel(batch, *refs):
    (gq_f, gq_b, gk_f, gk_b, ge_f, ge_b, gv_f, gv_b, gd_f, gd_b,
     rq_f, rq_b, rk_f, rk_b, re_f, re_b, rv_f, rv_b, rdl,
     mq_f, mq_b, mk_f, mk_b, mw_f, mw_b, mv_f, mv_b, mp_f, mp_b, mcol_f, mcol_b, mc_f, mc_b,
     of_ref, ob_ref, sg_ref, sr_ref, sm_ref, m_ref) = refs
    j = pl.program_id(0)

    @pl.when(j == 0)
    def _():
        sg_ref[...] = jnp.zeros_like(sg_ref)
        sr_ref[...] = jnp.zeros_like(sr_ref)
        sm_ref[...] = jnp.zeros_like(sm_ref)
        m_ref[...] = jnp.full_like(m_ref, M_INIT)

    row_head = lax.broadcasted_iota(jnp.int32, (2 * CHUNK, PAIR_V), 0) // CHUNK
    diag_kv = row_head == lax.broadcasted_iota(jnp.int32, (2 * CHUNK, PAIR_V), 1) // DV
    diag_kk = (lax.broadcasted_iota(jnp.int32, (2 * CHUNK, PAIR_K), 0) // CHUNK
               == lax.broadcasted_iota(jnp.int32, (2 * CHUNK, PAIR_K), 1) // DK_SMALL)
    upper_rows = lax.broadcasted_iota(jnp.int32, (PAIR_K, DV), 0) < DK_SMALL
    t_idx = lax.broadcasted_iota(jnp.int32, (CHUNK, 2 * CHUNK), 0)
    s_idx = lax.broadcasted_iota(jnp.int32, (CHUNK, 2 * CHUNK), 1) % CHUNK
    causal = (s_idx <= t_idx, s_idx >= t_idx)
    ones_v = jnp.ones((CHUNK, DV), BF16)
    zeros_vext = jnp.zeros((CHUNK, PAIR_V), BF16)
    ret_dec = jnp.exp(float(CHUNK) * _log_sigmoid(rdl[...]))

    decay_sets = (
        ((gq_f, gk_f, ge_f, gv_f), (gq_b, gk_b, ge_b, gv_b), sg_ref, 0),
        ((rq_f, rk_f, re_f, rv_f), (rq_b, rk_b, re_b, rv_b), sr_ref, 2 * BRANCH_W),
    )
    ml_sets = ((mq_f, mk_f, mw_f, mv_f, mp_f, mcol_f, mc_f),
               (mq_b, mk_b, mw_b, mv_b, mp_b, mcol_b, mc_b))
    outs = (of_ref, ob_ref)


    def decay_phase1(bb, d, c, rows, bi, p, refs):
        q_r, k_r, e_r, v_r = refs
        ks = slice(p * PAIR_K, (p + 1) * PAIR_K)
        q2 = q_r[bb, rows, ks]
        k2 = k_r[bb, rows, ks]
        v2 = v_r[bb, rows, p * PAIR_V:(p + 1) * PAIR_V]
        k_bd = jnp.where(diag_kk, jnp.concatenate([k2, k2], axis=0), 0.0)
        dec = (gd_f if d == 0 else gd_b)[bb, c, ks, :] if bi == 0 else ret_dec[d, ks, :]
        full = _dot_tn(e_r[bb, rows, ks], v2)
        upd = jnp.where(upper_rows, full[:, 0:DV], full[:, DV:])
        return dict(q2=q2, v2=v2, dec=dec, upd=upd, att=_dot_nt(q2, k_bd))

    def decay_phase2(bb, d, p, rows, o_ref, s_ref, base, w):
        s_prev = s_ref[bb, d, p]
        s_b = s_prev.astype(BF16)
        s_bd = jnp.where(diag_kv, jnp.concatenate([s_b, s_b], axis=1), 0.0)
        att = jnp.where(causal[d], w["att"], 0.0).astype(BF16)
        v_bd = jnp.where(diag_kv, jnp.concatenate([w["v2"], w["v2"]], axis=0), 0.0)
        lhs = jnp.concatenate([att, w["q2"]], axis=1)
        rhs = jnp.concatenate([v_bd, s_bd], axis=0)
        o_ref[bb, rows, base + p * PAIR_V:base + (p + 1) * PAIR_V] = _dot(lhs, rhs).astype(BF16)
        s_ref[bb, d, p] = w["dec"] * s_prev + w["upd"]

    def ml_phase1(bb, d, c, rows, p, refs):
        q_r, k_r, w_r, v_r, p_r, col_r, c_r = refs
        cols = col_r[bb, rows, :]
        q2 = q_r[bb, rows, p * 2 * DK_ML:(p + 1) * 2 * DK_ML]
        k2 = k_r[bb, rows, p * 2 * DK_ML:(p + 1) * 2 * DK_ML]
        k_bd = jnp.where(diag_kv, jnp.concatenate([k2, k2], axis=0), 0.0)
        att = _dot_nt(q2, k_bd)
        heads = []
        for hl in range(2):
            hd = 2 * p + hl
            jj = d * HEADS + hd
            vext = jnp.concatenate([v_r[bb, rows, hd * DV:(hd + 1) * DV], ones_v], axis=1)
            st = sm_ref[bb, d, hd]
            heads.append(dict(
                hd=hd, st=st, vext=vext,
                inter=_dot(q2[:, hl * DK_ML:(hl + 1) * DK_ML], st.astype(BF16)),
                upd=_dot_tn(w_r[bb, rows, hd * DK_ML:(hd + 1) * DK_ML], vext),
                u=jnp.broadcast_to(cols[:, MC_U + jj:MC_U + jj + 1], (CHUNK, LANES)),
                b=jnp.broadcast_to(cols[:, MC_B + jj:MC_B + jj + 1], (CHUNK, LANES))))
        return dict(att=att, heads=heads, p0=p_r[bb, p, rows, :], cs=c_r[bb, c])

    def ml_phase2(bb, d, rows, o_ref, w):
        sc = (w["att"] * w["p0"]).astype(BF16)
        for hl, hw in enumerate(w["heads"]):
            hd = hw["hd"]
            jj = d * HEADS + hd
            vz = (jnp.concatenate([hw["vext"], zeros_vext], axis=0) if hl == 0
                  else jnp.concatenate([zeros_vext, hw["vext"]], axis=0))
            intra = _dot(sc, vz)
            m_prev = m_ref[bb, jj:jj + 1, :]
            u = hw["u"]
            mx = jnp.maximum(u, m_prev)
            f = jnp.exp(u - mx)
            e = jnp.exp(m_prev - mx)
            num = f * intra[:, 0:DV] + e * hw["inter"][:, 0:DV]
            den = f * intra[:, DV:] + e * hw["inter"][:, DV:]
            den = jnp.maximum(jnp.abs(den), jnp.exp(-(hw["b"] + mx)))
            o_ref[bb, rows, BRANCH_W + hd * DV:BRANCH_W + (hd + 1) * DV] = (num / den).astype(BF16)
            tot = w["cs"][jj:jj + 1, :]
            mloc = w["cs"][NHD + jj:NHD + jj + 1, :]
            m_new = jnp.maximum(tot + m_prev, mloc)
            keep = jnp.exp(tot + m_prev - m_new)
            gain = jnp.exp(mloc - m_new)
            sm_ref[bb, d, hd] = (jnp.concatenate([keep, keep], axis=1) * hw["st"]
                                 + jnp.concatenate([gain, gain], axis=1) * hw["upd"])
            m_ref[bb, jj:jj + 1, :] = m_new

    units = []
    for step in range(CHUNKS_PER_TILE):
        for bb, d in ((b_, d_) for b_ in range(batch) for d_ in range(2)):
            c = step if d == 0 else CHUNKS_PER_TILE - 1 - step
            rows = pl.ds(c * CHUNK, CHUNK)
            for bi, (fset, bset, s_ref, base) in enumerate(decay_sets):
                for p in range(PAIRS):
                    units.append((functools.partial(decay_phase1, bb, d, c, rows, bi, p, fset if d == 0 else bset),
                                  functools.partial(decay_phase2, bb, d, p, rows, outs[d], s_ref, base)))
            for p in range(PAIRS):
                units.append((functools.partial(ml_phase1, bb, d, c, rows, p, ml_sets[d]),
                              functools.partial(ml_phase2, bb, d, rows, outs[d])))

    assert MIX_LOOKAHEAD < len(units) // CHUNKS_PER_TILE
    pending = []
    for phase1, phase2 in units:
        pending.append((phase2, phase1()))
        if len(pending) > MIX_LOOKAHEAD:
            fn, w = pending.pop(0)
            fn(w)
    for fn, w in pending:
        fn(w)


def _mix(gla, ret, ml, ret_dl_col):
    g_qke, g_vs, gdec = gla
    r_qke, r_vs = ret
    m_qk, mw, m_vs, mp, mcol, mc = ml
    B, T, _ = g_vs.shape
    nt = T // TILE

    def tf(j):
        return j

    def tb(j):
        return jnp.where(j == 0, 0, nt - j)

    def pair(width, lane_block_b=0, lane_block_f=0):
        return [pl.BlockSpec((B, TILE, width), lambda j: (0, tf(j), lane_block_f)),
                pl.BlockSpec((B, TILE, width), lambda j: (0, tb(j), lane_block_b))]

    dec_pair = [pl.BlockSpec((B, CHUNKS_PER_TILE, NQK_S, LANES), lambda j: (0, tf(j), 0, 0)),
                pl.BlockSpec((B, CHUNKS_PER_TILE, NQK_S, LANES), lambda j: (0, tb(j), 1, 0))]
    p0_pair = [pl.BlockSpec((B, PAIRS, TILE, 2 * CHUNK), lambda j: (0, 0, tf(j), 0)),
               pl.BlockSpec((B, PAIRS, TILE, 2 * CHUNK), lambda j: (0, 1, tb(j), 0))]
    cs_pair = [pl.BlockSpec((B, CHUNKS_PER_TILE, 2 * NHD, LANES), lambda j: (0, tf(j), 0, 0)),
               pl.BlockSpec((B, CHUNKS_PER_TILE, 2 * NHD, LANES), lambda j: (0, tb(j), 0, 0))]
    decay_specs = pair(NQK_S, 1, 0) + pair(NQK_S, 3, 2) + pair(NQK_S, 5, 4) + pair(BRANCH_W)
    in_specs = (decay_specs + dec_pair + decay_specs
                + [pl.BlockSpec((2, NQK_S, LANES), lambda j: (0, 0, 0))]
                + pair(NQK_M) + pair(NQK_M, 1, 1) + pair(NQK_M, 1) + pair(BRANCH_W) + p0_pair
                + pair(MC_WIDTH) + cs_pair)
    args = (g_qke, g_qke, g_qke, g_qke, g_qke, g_qke, g_vs, g_vs, gdec, gdec,
            r_qke, r_qke, r_qke, r_qke, r_qke, r_qke, r_vs, r_vs, ret_dl_col,
            m_qk, m_qk, m_qk, m_qk, mw, mw, m_vs, m_vs, mp, mp, mcol, mcol, mc, mc)
    return pl.pallas_call(
        functools.partial(_mix_kernel, B),
        out_shape=(jax.ShapeDtypeStruct((B, T, 3 * BRANCH_W), BF16),) * 2,
        grid=(nt,),
        in_specs=in_specs,
        out_specs=(pl.BlockSpec((B, TILE, 3 * BRANCH_W), lambda j: (0, tf(j), 0)),
                   pl.BlockSpec((B, TILE, 3 * BRANCH_W), lambda j: (0, tb(j), 0))),
        scratch_shapes=[pltpu.VMEM((B, 2, PAIRS, PAIR_K, DV), F32),
                        pltpu.VMEM((B, 2, PAIRS, PAIR_K, DV), F32),
                        pltpu.VMEM((B, 2, HEADS, DK_ML, 2 * DV), F32),
                        pltpu.VMEM((B, NHD, LANES), F32)],
        compiler_params=_cparams(1), name="mix",
    )(*args)


def _merge_kernel(batch, first_tile, x_ref, ctx_ref, mod_ref, g_ref, wg_ref, of_ref, ob_ref,
                  sgg_ref, sgm_ref, sgr_ref, ng_ref, wb_ref, wo_ref, out_ref):
    tile_idx = pl.program_id(0) + first_tile
    side = (sgg_ref, sgm_ref, sgr_ref)
    mods = [mod_ref[jnp.where(tile_idx == 0, batch, bb)] for bb in range(batch)]
    xs = [jnp.where(tile_idx == 0, ctx_ref[bb], x_ref[bb]) for bb in range(batch)]
    hs = [_norm_mod(xs[bb], g_ref[...], mods[bb][0:1, :], mods[bb][1:2, :]).astype(BF16)
          for bb in range(batch)]
    accs = [jnp.zeros((TILE, D_MODEL), F32) for _ in range(batch)]
    for br in range(3):
        for bb in range(batch):
            parts = []
            for hd in range(HEADS):
                cols = slice(br * BRANCH_W + hd * DV, br * BRANCH_W + (hd + 1) * DV)
                oh = of_ref[bb, :, cols].astype(F32) + ob_ref[bb, :, cols].astype(F32)
                if br > 0:
                    oh = oh - jnp.mean(oh, axis=-1, keepdims=True)
                parts.append(oh * lax.rsqrt(jnp.mean(oh * oh, axis=-1, keepdims=True) + EPS))
            y = jnp.concatenate(parts, axis=1) * ng_ref[:, br * BRANCH_W:(br + 1) * BRANCH_W]
            y = (y * side[br][bb]).astype(BF16)
            gate = _sigmoid(_dot(hs[bb], wg_ref[0, :, br * D_MODEL:(br + 1) * D_MODEL]))
            accs[bb] = accs[bb] + gate * _dot(y, wb_ref[0, br])
    for bb in range(batch):
        out = _dot(accs[bb].astype(BF16), wo_ref[0])
        out_ref[bb] = xs[bb] + mods[bb][2:3, :] * out


def _merge(ctx_src, x_src, off, mods, g1, wgate, o_f, o_b, sg_gla, sz_ml, sg_ret, norm_g, wb, wo, layer, skip):
    B = x_src.shape[0]
    nt = x_src.shape[1] // TILE + off - skip
    tile = lambda width: pl.BlockSpec((B, TILE, width), lambda i: (0, i + skip, 0))
    gate_tile = pl.BlockSpec((B, TILE, BRANCH_W), lambda i: (0, i + skip, 1))
    const = lambda shape: pl.BlockSpec(shape, lambda i: (0,) * len(shape))
    per_layer = lambda shape: pl.BlockSpec((1,) + shape, lambda i: (layer,) + (0,) * len(shape),
                                           pipeline_mode=pl.Buffered(1))
    return pl.pallas_call(
        functools.partial(_merge_kernel, B, skip),
        out_shape=jax.ShapeDtypeStruct((B, nt * TILE, D_MODEL), F32),
        grid=(nt,),
        in_specs=[pl.BlockSpec((B, TILE, D_MODEL), lambda i: (0, jnp.maximum(i + skip - off, 0), 0)),
                  pl.BlockSpec((B, TILE, D_MODEL), lambda i: (0, 0, 0)),
                  const((B + 1, 6, D_MODEL)),
                  const((1, D_MODEL)), per_layer((D_MODEL, W_GATE)),
                  tile(3 * BRANCH_W), tile(3 * BRANCH_W),
                  gate_tile, gate_tile, gate_tile,
                  const((1, 3 * BRANCH_W)), per_layer((3, BRANCH_W, D_MODEL)),
                  per_layer((D_MODEL, D_MODEL))],
        out_specs=pl.BlockSpec((B, TILE, D_MODEL), lambda i: (0, i, 0)),
        compiler_params=_cparams(1), name="merge",
    )(x_src, ctx_src, mods, g1, wgate, o_f, o_b, sg_gla, sz_ml, sg_ret, norm_g, wb, wo)


def _mlp_kernel(batch, first_tile, final, x_ref, mod_ref, g_ref, wu_ref, wd_ref, fg_ref, out_ref):
    tile_idx = pl.program_id(0) + first_tile
    mods = [mod_ref[jnp.where(tile_idx == 0, batch, bb)] for bb in range(batch)]
    ups = []
    for bb in range(batch):
        h = _norm_mod(x_ref[bb], g_ref[...], mods[bb][3:4, :], mods[bb][4:5, :]).astype(BF16)
        ups.append(_dot(h, wu_ref[0]))
    for bb in range(batch):
        u = jnp.maximum(ups[bb], 0.0)
        y = x_ref[bb] + mods[bb][5:6, :] * _dot((u * u).astype(BF16), wd_ref[0])
        if final:
            y = y * lax.rsqrt(jnp.mean(y * y, axis=-1, keepdims=True) + EPS) * fg_ref[...]
        out_ref[bb] = y


def _mlp(tok, mods, g2, wu, wd, layer, final_g, first_tile, final):
    B, T, _ = tok.shape
    const = lambda shape: pl.BlockSpec(shape, lambda i: (0,) * len(shape))
    per_layer = lambda shape: pl.BlockSpec((1,) + shape, lambda i: (layer,) + (0,) * len(shape),
                                           pipeline_mode=pl.Buffered(1))
    return pl.pallas_call(
        functools.partial(_mlp_kernel, B, first_tile, final),
        out_shape=jax.ShapeDtypeStruct((B, T, D_MODEL), F32),
        grid=(T // TILE,),
        in_specs=[pl.BlockSpec((B, TILE, D_MODEL), lambda i: (0, i, 0)),
                  const((B + 1, 6, D_MODEL)),
                  const((1, D_MODEL)), per_layer((D_MODEL, D_FF)), per_layer((D_FF, D_MODEL)),
                  const((1, D_MODEL))],
        out_specs=pl.BlockSpec((B, TILE, D_MODEL), lambda i: (0, i, 0)),
        compiler_params=_cparams(1), name="mlp",
    )(tok, mods, g2, wu, wd, final_g)


def kernel(x, c, ctx, c_ctx, norm1_g, norm2_g, w_ada, b_ada, w_in, gla_a2, gla_a_bias, gla_norm_g,
           ml_conv, ml_i_bias, ml_f_bias, ml_norm_g, ret_decay_logit, ret_norm_g, w_branch, w_o,
           w_up, w_down, final_g):
    B, seq, _ = x.shape
    ctx_len = ctx.shape[1]
    assert B == 2 and ctx_len == TILE and seq % TILE == 0 and seq % GRID_W == 0
    ctx_src, x_src, off = ctx, x, 1

    c_rows = jnp.concatenate([c, c_ctx[None, :], jnp.zeros((8 - B - 1, D_MODEL), F32)], axis=0)
    mods_all = _ada(c_rows, w_ada, b_ada)[:, 0:3].reshape(DEPTH, 3, 6, D_MODEL)

    cos_t, sin_t = _rope_tables(seq, ctx_len)

    w_proj, w_gate = _wprep(w_in)
    w_branch_b = w_branch.astype(BF16)
    w_o_b = w_o.astype(BF16)
    w_up_b = w_up.astype(BF16)
    w_down_b = w_down.astype(BF16)
    for l in range(DEPTH):
        last = l == DEPTH - 1
        skip = 1 if last else 0
        mods = mods_all[l]
        g1 = norm1_g[l][None, :]
        g2 = norm2_g[l][None, :]
        zeros = jnp.zeros((GLA_RANK, NQK_S), F32)
        a2blk = jnp.concatenate([jnp.concatenate([gla_a2[l, 0], zeros], axis=1),
                                 jnp.concatenate([zeros, gla_a2[l, 1]], axis=1)], axis=0)
        a2s = jnp.concatenate([a2blk, a2blk], axis=0).astype(BF16)
        abias = gla_a_bias[l].reshape(1, 2 * NQK_S)
        gate_bias = jnp.concatenate([ml_i_bias[l].reshape(1, NHD), ml_f_bias[l].reshape(1, NHD)], axis=1)
        dl = ret_decay_logit[l]
        dl_row = jnp.repeat(dl, DK_SMALL, axis=1).reshape(1, 2 * NQK_S)
        dl_col = jnp.broadcast_to(jnp.repeat(dl, DK_SMALL, axis=1)[:, :, None], (2, NQK_S, LANES))

        (g_qke, g_vs, gdec, m_qk, mw, m_vs, mp, mcol, mc, r_qke, r_vs) = _proj(
            ctx_src, x_src, off, mods, g1, w_proj, l, a2s, abias, ml_conv[l], gate_bias, cos_t, sin_t, dl_row)
        o_f, o_b = _mix((g_qke, g_vs, gdec), (r_qke, r_vs), (m_qk, mw, m_vs, mp, mcol, mc), dl_col)
        norm_g = jnp.concatenate([gla_norm_g[l], ml_norm_g[l], ret_norm_g[l]])[None, :]
        tok = _merge(ctx_src, x_src, off, mods, g1, w_gate, o_f, o_b, g_vs, m_vs, r_vs, norm_g,
                     w_branch_b, w_o_b, l, skip)
        tok = _mlp(tok, mods, g2, w_up_b, w_down_b, l, final_g[None, :], skip, last)
        ctx_src, x_src, off = tok, tok, 0
    return tok
```
